```python
import jax, jax.numpy as jnp
from jax import lax
import numpy as np

D_MODEL = 2048
BATCH = 2
SEQ = 4096
DEPTH = 1

N_Q_HEADS = 32
N_KV_HEADS = 8
HEAD_DIM = 64
Q_PER_KV = N_Q_HEADS // N_KV_HEADS
WINDOW = 128
ATTN_BLOCK = 128
ROT_DIM = HEAD_DIM // 4
ROPE_THETA = 500000.0
SSD_HEADS = 32
SSD_HEAD_DIM = 64
SSD_INNER = SSD_HEADS * SSD_HEAD_DIM
SSD_GROUPS = 8
SSD_STATE = 128
SSD_CONV = 4
SSD_CHUNK = 128
HEADS_PER_GROUP = SSD_HEADS // SSD_GROUPS
ATTN_WIDTH = N_Q_HEADS * HEAD_DIM
KV_WIDTH = N_KV_HEADS * HEAD_DIM
MIX_WIDTH = ATTN_WIDTH + SSD_INNER
BC_WIDTH = SSD_GROUPS * SSD_STATE
CONV_CH = SSD_INNER + 2 * BC_WIDTH
IN_PROJ_WIDTH = ATTN_WIDTH + 2 * KV_WIDTH + SSD_INNER + CONV_CH + SSD_HEADS
D_FF = 5632
FFN_CONV = 3
EPS = 1e-6

kernel_name = 'hymba_swa_sink_ssd_convffn'


def _rmsnorm(x, g):
    xf = x.astype(jnp.float32)
    y = xf * lax.rsqrt(jnp.mean(xf * xf, axis=-1, keepdims=True) + EPS)
    return (y * g.astype(jnp.float32)).astype(x.dtype)


def _causal_dwconv(x, w, b):
    k_taps = w.shape[0]
    s = x.shape[1]
    xp = jnp.pad(x, ((0, 0), (k_taps - 1, 0), (0, 0)))
    out = b
    for t in range(k_taps):
        out = out + xp[:, t:t + s] * w[t]
    return out


def _partial_rope(x, pos):
    half = ROT_DIM // 2
    inv = 1.0 / (ROPE_THETA ** (jnp.arange(0, ROT_DIM, 2, dtype=jnp.float32) / ROT_DIM))
    ang = pos[:, None] * inv[None, :]
    cos = jnp.cos(ang)[None, :, None, :]
    sin = jnp.sin(ang)[None, :, None, :]
    xf = x.astype(jnp.float32)
    x1 = xf[..., :half]
    x2 = xf[..., half:ROT_DIM]
    out = jnp.concatenate([x1 * cos - x2 * sin, x2 * cos + x1 * sin, xf[..., ROT_DIM:]], axis=-1)
    return out.astype(x.dtype)


def _band_blocks(t):
    b, s, h, d = t.shape
    nb = s // ATTN_BLOCK
    tp = jnp.pad(t, ((0, 0), (ATTN_BLOCK, 0), (0, 0), (0, 0)))
    prev = tp[:, :s].reshape(b, nb, ATTN_BLOCK, h, d)
    cur = t.reshape(b, nb, ATTN_BLOCK, h, d)
    return jnp.concatenate([prev, cur], axis=2)


def _sliding_window_attention(q, k, v, sinks):
    b, s, _, d = q.shape
    nb = s // ATTN_BLOCK
    qb = q.reshape(b, nb, ATTN_BLOCK, N_KV_HEADS, Q_PER_KV, d)
    kb = _band_blocks(k)
    vb = _band_blocks(v)
    scores = jnp.einsum('bnqhgd,bnkhd->bnhgqk', qb, kb, preferred_element_type=jnp.float32) * (d ** -0.5)
    qi = jnp.arange(ATTN_BLOCK)[:, None]
    kj = jnp.arange(2 * ATTN_BLOCK)[None, :]
    rel = qi + ATTN_BLOCK - kj
    band = (rel >= 0) & (rel < WINDOW)
    blk = jnp.arange(nb)[:, None, None]
    valid = band[None] & ((blk > 0) | (kj >= ATTN_BLOCK)[None])
    scores = jnp.where(valid[None, :, None, None], scores, -jnp.inf)
    sink = sinks.astype(jnp.float32).reshape(N_KV_HEADS, Q_PER_KV)[None, None, :, :, None, None]
    m = jnp.maximum(jnp.max(scores, axis=-1, keepdims=True), sink)
    p = jnp.exp(scores - m)
    probs = p / (jnp.sum(p, axis=-1, keepdims=True) + jnp.exp(sink - m))
    out = jnp.einsum('bnhgqk,bnkhd->bnqhgd', probs.astype(v.dtype), vb)
    return out.reshape(b, s, N_Q_HEADS * d)


def _ssd_chunked(xh, dt, a, bm, cm):
    b, s, _, p = xh.shape
    nc = s // SSD_CHUNK
    g, r, n = SSD_GROUPS, HEADS_PER_GROUP, SSD_STATE
    x_c = (xh * dt[..., None]).reshape(b, nc, SSD_CHUNK, g, r, p)
    a_cs = jnp.cumsum((dt * a).reshape(b, nc, SSD_CHUNK, g, r), axis=2)
    b_c = bm.reshape(b, nc, SSD_CHUNK, g, n)
    c_c = cm.reshape(b, nc, SSD_CHUNK, g, n)
    seg = a_cs[:, :, :, None] - a_cs[:, :, None, :]
    causal = jnp.tril(jnp.ones((SSD_CHUNK, SSD_CHUNK), dtype=bool))[None, None, :, :, None, None]
    l_mat = jnp.exp(jnp.where(causal, seg, -jnp.inf))
    cb = jnp.einsum('bclgn,bcsgn->bclsg', c_c, b_c)
    y_diag = jnp.einsum('bclsgr,bcsgrp->bclgrp', cb[..., None] * l_mat, x_c)
    decay_s = jnp.exp(a_cs[:, :, -1:] - a_cs)
    states = jnp.einsum('bcsgn,bcsgrp->bcgrpn', b_c, x_c * decay_s[..., None])
    chunk_decay = jnp.exp(a_cs[:, :, -1])

    def step(h, inp):
        s_c, d_c = inp
        return h * d_c[..., None, None] + s_c, h

    h0 = jnp.zeros((b, g, r, p, n), dtype=jnp.float32)
    _, prev = lax.scan(step, h0, (jnp.moveaxis(states, 1, 0), jnp.moveaxis(chunk_decay, 1, 0)))
    prev = jnp.moveaxis(prev, 0, 1)
    y_off = jnp.einsum('bclgn,bcgrpn->bclgrp', c_c, prev) * jnp.exp(a_cs)[..., None]
    return (y_diag + y_off).reshape(b, s, SSD_HEADS, p)


def _gated_group_rmsnorm(y, z, g):
    b, s, _ = y.shape
    yg = y * jax.nn.silu(z.astype(jnp.float32))
    yr = yg.reshape(b, s, SSD_GROUPS, -1)
    yr = yr * lax.rsqrt(jnp.mean(yr * yr, axis=-1, keepdims=True) + EPS)
    return yr.reshape(b, s, SSD_INNER) * g.astype(jnp.float32)


def _hybrid_mixer(xn, w_in, sinks, attn_out_norm, ssd_conv_w, ssd_conv_b, dt_bias, a_log, ssd_d, ssd_norm, w_out):
    b, s, _ = xn.shape
    proj = xn @ w_in
    o = np.cumsum([0, ATTN_WIDTH, KV_WIDTH, KV_WIDTH, SSD_INNER, CONV_CH, SSD_HEADS])
    q = proj[..., o[0]:o[1]].reshape(b, s, N_Q_HEADS, HEAD_DIM)
    k = proj[..., o[1]:o[2]].reshape(b, s, N_KV_HEADS, HEAD_DIM)
    v = proj[..., o[2]:o[3]].reshape(b, s, N_KV_HEADS, HEAD_DIM)
    z = proj[..., o[3]:o[4]]
    xbc = proj[..., o[4]:o[5]]
    dt_raw = proj[..., o[5]:o[6]]
    pos = jnp.arange(s, dtype=jnp.float32)
    q = _partial_rope(q, pos)
    k = _partial_rope(k, pos)
    attn = _rmsnorm(_sliding_window_attention(q, k, v, sinks), attn_out_norm)
    xbc = jax.nn.silu(_causal_dwconv(xbc, ssd_conv_w, ssd_conv_b)).astype(jnp.float32)
    xs = xbc[..., :SSD_INNER].reshape(b, s, SSD_HEADS, SSD_HEAD_DIM)
    bm = xbc[..., SSD_INNER:SSD_INNER + BC_WIDTH].reshape(b, s, SSD_GROUPS, SSD_STATE)
    cm = xbc[..., SSD_INNER + BC_WIDTH:].reshape(b, s, SSD_GROUPS, SSD_STATE)
    dt = jax.nn.softplus(dt_raw.astype(jnp.float32) + dt_bias.astype(jnp.float32))
    a = -jnp.exp(a_log.astype(jnp.float32))
    y = _ssd_chunked(xs, dt, a, bm, cm) + ssd_d.astype(jnp.float32)[:, None] * xs
    y = _gated_group_rmsnorm(y.reshape(b, s, SSD_INNER), z, ssd_norm).astype(xn.dtype)
    return jnp.concatenate([attn, y], axis=-1) @ w_out


def _conv_ffn(hn, w_up, ffn_conv_w, ffn_conv_b, w_down):
    u = _causal_dwconv(hn @ w_up, ffn_conv_w, ffn_conv_b)
    gate, val = u[..., :D_FF], u[..., D_FF:]
    return (jax.nn.silu(gate) * val) @ w_down


def setup_inputs(seed: int = 0) -> dict:
    key = jax.random.key(seed)
    ks = jax.random.split(key, 20)
    f32 = jnp.float32
    nrm = lambda k, shape, scale: jax.random.normal(k, shape, f32) * scale
    x = jax.random.normal(ks[0], (BATCH, SEQ, D_MODEL), f32)
    dt0 = jnp.exp(jax.random.uniform(ks[8], (DEPTH, SSD_HEADS), f32, np.log(1e-3), np.log(1e-1)))
    return {
        'x': x,
        'norm_mix': 1.0 + nrm(ks[1], (DEPTH, D_MODEL), 0.01),
        'w_in': nrm(ks[2], (DEPTH, D_MODEL, IN_PROJ_WIDTH), D_MODEL ** -0.5),
        'sinks': nrm(ks[3], (DEPTH, N_Q_HEADS), 0.5),
        'attn_out_norm': 1.0 + nrm(ks[4], (DEPTH, ATTN_WIDTH), 0.01),
        'ssd_conv_w': nrm(ks[5], (DEPTH, SSD_CONV, CONV_CH), SSD_CONV ** -0.5),
        'ssd_conv_b': nrm(ks[6], (DEPTH, CONV_CH), 0.01),
        'dt_bias': dt0 + jnp.log(-jnp.expm1(-dt0)),
        'a_log': jnp.log(jax.random.uniform(ks[9], (DEPTH, SSD_HEADS), f32, 1.0, 16.0)),
        'ssd_d': 1.0 + nrm(ks[10], (DEPTH, SSD_HEADS), 0.01),
        'ssd_norm': 1.0 + nrm(ks[11], (DEPTH, SSD_INNER), 0.01),
        'w_out': nrm(ks[12], (DEPTH, MIX_WIDTH, D_MODEL), MIX_WIDTH ** -0.5),
        'norm_ffn': 1.0 + nrm(ks[13], (DEPTH, D_MODEL), 0.01),
        'w_up': nrm(ks[14], (DEPTH, D_MODEL, 2 * D_FF), D_MODEL ** -0.5),
        'ffn_conv_w': nrm(ks[15], (DEPTH, FFN_CONV, 2 * D_FF), FFN_CONV ** -0.5),
        'ffn_conv_b': nrm(ks[16], (DEPTH, 2 * D_FF), 0.01),
        'w_down': nrm(ks[17], (DEPTH, D_FF, D_MODEL), D_FF ** -0.5),
        'norm_final': 1.0 + nrm(ks[18], (D_MODEL,), 0.01),
    }


def reference(x, norm_mix, w_in, sinks, attn_out_norm, ssd_conv_w, ssd_conv_b, dt_bias, a_log, ssd_d, ssd_norm, w_out, norm_ffn, w_up, ffn_conv_w, ffn_conv_b, w_down, norm_final):
    h = x
    for l in range(DEPTH):
        h = h + _hybrid_mixer(_rmsnorm(h, norm_mix[l]), w_in[l], sinks[l], attn_out_norm[l], ssd_conv_w[l], ssd_conv_b[l], dt_bias[l], a_log[l], ssd_d[l], ssd_norm[l], w_out[l])
        h = h + _conv_ffn(_rmsnorm(h, norm_ffn[l]), w_up[l], ffn_conv_w[l], ffn_conv_b[l], w_down[l])
    return _rmsnorm(h, norm_final)
```

```python
import functools

import numpy as np
import jax
import jax.numpy as jnp
from jax import lax
from jax.experimental import pallas as pl
from jax.experimental.pallas import tpu as pltpu

F32 = jnp.float32
BF16 = jnp.bfloat16

D_MODEL = 2048
N_Q_HEADS = 32
N_KV_HEADS = 8
HEAD_DIM = 64
Q_PER_KV = N_Q_HEADS // N_KV_HEADS
WINDOW = 128
ATTN_BLOCK = 128
ROT_DIM = HEAD_DIM // 4
ROPE_THETA = 500000.0
SSD_HEADS = 32
SSD_HEAD_DIM = 64
SSD_INNER = SSD_HEADS * SSD_HEAD_DIM
SSD_GROUPS = 8
SSD_STATE = 128
SSD_CONV = 4
SSD_CHUNK = 128
ATTN_WIDTH = N_Q_HEADS * HEAD_DIM
KV_WIDTH = N_KV_HEADS * HEAD_DIM
BC_WIDTH = SSD_GROUPS * SSD_STATE
CONV_CH = SSD_INNER + 2 * BC_WIDTH
MAIN_PROJ = ATTN_WIDTH + 2 * KV_WIDTH + SSD_INNER + CONV_CH
D_FF = 5632
FFN_CONV = 3
EPS = 1e-6

LANES = 128
SUBLANES = 8
HALF = LANES // 2
NEG = -1e30
VMEM_LIMIT = 56 * 1024 * 1024

Q_OFF = 0
Z_OFF = ATTN_WIDTH
XBC_OFF = Z_OFF + SSD_INNER
K_OFF = XBC_OFF + CONV_CH
V_OFF = K_OFF + KV_WIDTH


def _params(n_axes):
    return pltpu.CompilerParams(dimension_semantics=("arbitrary",) * n_axes,
                                vmem_limit_bytes=VMEM_LIMIT)


def _sigmoid(x):
    return 1.0 / (1.0 + jnp.exp(-x))


def _softplus(x):
    return jnp.maximum(x, 0.0) + jnp.log1p(jnp.exp(-jnp.abs(x)))


def _cast_rows(src_ref, dst_ref, rows=256):
    k = src_ref.shape[0]
    for r in range(0, k, rows):
        dst_ref[r:r + rows, :] = src_ref[r:r + rows, :].astype(BF16)


def _rmsnorm_rows(x, g):
    ms = jnp.mean(x * x, axis=-1, keepdims=True)
    return x * lax.rsqrt(ms + EPS) * g


def _norm_dt_kernel(x_ref, g_ref, wdt_ref, wdtt_ref, xn_ref, dt_ref, dtt_ref):
    xn = _rmsnorm_rows(x_ref[...], g_ref[...]).astype(BF16)
    xn_ref[...] = xn
    dt_ref[...] = jnp.dot(xn, wdt_ref[...], preferred_element_type=F32)
    dtt_ref[...] = lax.dot_general(wdtt_ref[...], xn, (((1,), (1,)), ((), ())),
                                   preferred_element_type=F32)


def _norm_dt(x2, g, wdt, wdtt, tm):
    t, d = x2.shape
    return pl.pallas_call(
        _norm_dt_kernel,
        grid=(t // tm,),
        in_specs=[pl.BlockSpec((tm, d), lambda i: (i, 0)),
                  pl.BlockSpec((1, d), lambda i: (0, 0)),
                  pl.BlockSpec((d, LANES), lambda i: (0, 0)),
                  pl.BlockSpec((LANES, d), lambda i: (0, 0))],
        out_specs=[pl.BlockSpec((tm, d), lambda i: (i, 0)),
                   pl.BlockSpec((tm, LANES), lambda i: (i, 0)),
                   pl.BlockSpec((LANES, tm), lambda i: (0, i))],
        out_shape=[jax.ShapeDtypeStruct((t, d), BF16),
                   jax.ShapeDtypeStruct((t, LANES), F32),
                   jax.ShapeDtypeStruct((LANES, t), F32)],
        compiler_params=_params(1),
        name="norm_dt",
    )(x2, g, wdt, wdtt)


def _norm_kernel(x_ref, g_ref, o_ref):
    o_ref[...] = _rmsnorm_rows(x_ref[...], g_ref[...]).astype(o_ref.dtype)


def _norm(x2, g, out_dtype, tm, name):
    t, d = x2.shape
    return pl.pallas_call(
        _norm_kernel,
        grid=(t // tm,),
        in_specs=[pl.BlockSpec((tm, d), lambda i: (i, 0)),
                  pl.BlockSpec((1, d), lambda i: (0, 0))],
        out_specs=pl.BlockSpec((tm, d), lambda i: (i, 0)),
        out_shape=jax.ShapeDtypeStruct((t, d), out_dtype),
        compiler_params=_params(1),
        name=name,
    )(x2, g)


def _in_proj_kernel(a_ref, w_ref, o_ref, wbf_ref):
    @pl.when(pl.program_id(1) == 0)
    def _():
        _cast_rows(w_ref, wbf_ref)

    o_ref[...] = jnp.dot(a_ref[...], wbf_ref[...], preferred_element_type=F32)


def _in_proj(xn, w2, tm, tn):
    t, k = xn.shape
    u = 1024 // tn

    def wmap(j, m):
        return (0, jnp.where(j < 2 * u, j, jnp.where(j < 8 * u, j + u, j - 6 * u)))

    return pl.pallas_call(
        _in_proj_kernel,
        grid=(MAIN_PROJ // tn, t // tm),
        in_specs=[pl.BlockSpec((tm, k), lambda j, m: (m, 0)),
                  pl.BlockSpec((k, tn), wmap)],
        out_specs=pl.BlockSpec((tm, tn), lambda j, m: (m, j)),
        out_shape=jax.ShapeDtypeStruct((t, MAIN_PROJ), F32),
        scratch_shapes=[pltpu.VMEM((k, tn), BF16)],
        compiler_params=_params(2),
        name="in_proj",
    )(xn, w2)


def _rope_tables(seq):
    half = ROT_DIM // 2
    inv = 1.0 / (ROPE_THETA ** (jnp.arange(0, ROT_DIM, 2, dtype=F32) / ROT_DIM))
    ang = jnp.arange(seq, dtype=F32)[:, None] * inv[None, :]
    cos, sin = jnp.cos(ang), jnp.sin(ang)
    d = np.arange(LANES) % HEAD_DIM
    idx = d % half
    in_rot = jnp.asarray(d < ROT_DIM)
    first = jnp.asarray(d < half)
    second = jnp.asarray((d >= half) & (d < ROT_DIM))
    c = jnp.where(in_rot[None, :], cos[:, idx], 1.0)
    s1 = jnp.where(first[None, :], -sin[:, idx], 0.0)
    s2 = jnp.where(second[None, :], sin[:, idx], 0.0)
    return c.astype(F32), s1.astype(F32), s2.astype(F32)


def _rope(x, c, s1, s2):
    half = ROT_DIM // 2
    return x * c + pltpu.roll(x, LANES - half, 1) * s1 + pltpu.roll(x, half, 1) * s2


def _attn_kernel(sinks_ref, q_ref, kp_ref, kc_ref, vp_ref, vc_ref,
                 cc_ref, s1c_ref, s2c_ref, cp_ref, s1p_ref, s2p_ref, g_ref, o_ref, acc_ref):
    blk = ATTN_BLOCK
    n = pl.program_id(1)
    lo = lax.broadcasted_iota(jnp.int32, (blk, LANES), 1) < HALF
    lo2 = lax.broadcasted_iota(jnp.int32, (2 * blk, LANES), 1) < HALF
    cc, s1c, s2c = cc_ref[...], s1c_ref[...], s2c_ref[...]
    cp, s1p, s2p = cp_ref[...], s1p_ref[...], s2p_ref[...]
    scale = HEAD_DIM ** -0.5
    cq, s1q, s2q = cc * scale, s1c * scale, s2c * scale

    qi = lax.broadcasted_iota(jnp.int32, (blk, 2 * blk), 0)
    kj = lax.broadcasted_iota(jnp.int32, (blk, 2 * blk), 1)
    rel = qi + blk - kj
    kmin = jnp.where(n > 0, 0, blk)
    valid = (rel >= 0) & (rel < WINDOW) & (kj >= kmin)

    for i in range(N_KV_HEADS // 2):
        cols = slice(i * LANES, (i + 1) * LANES)
        kcat = jnp.concatenate([_rope(kp_ref[:, cols], cp, s1p, s2p),
                                _rope(kc_ref[:, cols], cc, s1c, s2c)], axis=0)
        vcat = jnp.concatenate([vp_ref[:, cols], vc_ref[:, cols]], axis=0)
        kswp = pltpu.roll(kcat, HALF, 1)
        vswp = pltpu.roll(vcat, HALF, 1)
        for e in range(2):
            h = 2 * i + e
            if e == 0:
                kb = jnp.where(lo2, kcat, kswp).astype(BF16)
                vb = jnp.where(lo2, vcat, vswp).astype(BF16)
            else:
                kb = jnp.where(lo2, kswp, kcat).astype(BF16)
                vb = jnp.where(lo2, vswp, vcat).astype(BF16)
            parts = []
            for jj in range(2):
                col = 2 * h + jj
                q2 = _rope(q_ref[:, col * LANES:(col + 1) * LANES], cq, s1q, s2q)
                parts.append(jnp.where(lo, q2, 0.0).astype(BF16))
                parts.append(jnp.where(lo, 0.0, q2).astype(BF16))
            lhs = jnp.concatenate(parts, axis=0)
            s = lax.dot_general(lhs, kb, (((1,), (1,)), ((), ())),
                                preferred_element_type=F32)
            ps, invs = [], []
            for r in range(Q_PER_KV):
                sink = sinks_ref[Q_PER_KV * h + r]
                sr = jnp.where(valid, s[r * blk:(r + 1) * blk], NEG)
                m = jnp.maximum(jnp.max(sr, axis=-1, keepdims=True), sink)
                p = jnp.exp(sr - m)
                den = jnp.sum(p, axis=-1, keepdims=True) + jnp.exp(sink - m)
                ps.append(p.astype(BF16))
                invs.append(1.0 / den)
            o = jnp.dot(jnp.concatenate(ps, axis=0), vb, preferred_element_type=F32)
            for jj in range(2):
                col = 2 * h + jj
                oa = o[(2 * jj) * blk:(2 * jj + 1) * blk] * invs[2 * jj]
                ob = o[(2 * jj + 1) * blk:(2 * jj + 2) * blk] * invs[2 * jj + 1]
                acc_ref[:, col * LANES:(col + 1) * LANES] = jnp.where(lo, oa, ob)

    o_ref[...] = _rmsnorm_rows(acc_ref[...], g_ref[...]).astype(o_ref.dtype)


def _attention(proj, sinks, g, tables, batch, seq):
    t = proj.shape[0]
    blk = ATTN_BLOCK
    nb = seq // blk
    kblk = K_OFF // KV_WIDTH
    vblk = V_OFF // KV_WIDTH
    c, s1, s2 = tables

    def cur(b, n):
        return (b * nb + n, 0)

    def tab_cur(b, n):
        return (n, 0)

    def tab_prev(b, n):
        return (jnp.maximum(n - 1, 0), 0)

    def prev_rows(b, n):
        return jnp.maximum(b * nb + n - 1, 0)

    tab = lambda f: pl.BlockSpec((blk, LANES), f)
    return pl.pallas_call(
        _attn_kernel,
        grid=(batch, nb),
        in_specs=[pl.BlockSpec(memory_space=pltpu.SMEM),
                  pl.BlockSpec((blk, ATTN_WIDTH), cur),
                  pl.BlockSpec((blk, KV_WIDTH), lambda b, n: (prev_rows(b, n), kblk)),
                  pl.BlockSpec((blk, KV_WIDTH), lambda b, n: (b * nb + n, kblk)),
                  pl.BlockSpec((blk, KV_WIDTH), lambda b, n: (prev_rows(b, n), vblk)),
                  pl.BlockSpec((blk, KV_WIDTH), lambda b, n: (b * nb + n, vblk)),
                  tab(tab_cur), tab(tab_cur), tab(tab_cur),
                  tab(tab_prev), tab(tab_prev), tab(tab_prev),
                  pl.BlockSpec((1, ATTN_WIDTH), lambda b, n: (0, 0))],
        out_specs=pl.BlockSpec((blk, ATTN_WIDTH), cur),
        out_shape=jax.ShapeDtypeStruct((t, ATTN_WIDTH), BF16),
        scratch_shapes=[pltpu.VMEM((blk, ATTN_WIDTH), F32)],
        compiler_params=_params(2),
        name="swa_attention",
    )(sinks, proj, proj, proj, proj, proj, c, s1, s2, c, s1, s2, g)


def _split3(x):
    h = x.astype(BF16)
    r = x - h.astype(F32)
    m = r.astype(BF16)
    l = (r - m.astype(F32)).astype(BF16)
    return h, m, l


def _dot3_lhs(x, w):
    h, m, l = _split3(x)
    d = lambda a: jnp.dot(a, w, preferred_element_type=F32)
    return (d(l) + d(m)) + d(h)


def _dot3_rhs(w, x):
    h, m, l = _split3(x)
    d = lambda a: jnp.dot(w, a, preferred_element_type=F32)
    return (d(l) + d(m)) + d(h)


def _ssd_kernel(xbc_ref, z_ref, dt_ref, dtt_ref, cw_ref, cb_ref, brow_ref, bcol_ref,
                alrow_ref, alcol_ref, e_ref, dfull_ref, gn_ref, o_ref,
                ext_ref, xact_ref, state_ref, xdt_ref, xdec_ref, y_ref, exp_ref, acs_ref, acst_ref):
    L = SSD_CHUNK
    c = pl.program_id(1)

    @pl.when(c == 0)
    def _():
        ext_ref[0:SUBLANES, :] = jnp.zeros((SUBLANES, CONV_CH), F32)
        state_ref[...] = jnp.zeros_like(state_ref)

    ext_ref[SUBLANES:SUBLANES + L, :] = xbc_ref[...]
    cw_chunk = 512
    for j in range(CONV_CH // cw_chunk):
        cs = slice(j * cw_chunk, (j + 1) * cw_chunk)
        acc = cb_ref[:, cs]
        for k in range(SSD_CONV):
            start = SUBLANES - (SSD_CONV - 1) + k
            acc = acc + ext_ref[start:start + L, cs] * cw_ref[k:k + 1, cs]
        xact_ref[:, cs] = acc * _sigmoid(acc)
    ext_ref[0:SUBLANES, :] = ext_ref[L:L + SUBLANES, :]

    dt = _softplus(dt_ref[...] + brow_ref[...])
    dtt = _softplus(dtt_ref[...] + bcol_ref[...])
    da = dt * (-jnp.exp(alrow_ref[...]))
    dat = dtt * (-jnp.exp(alcol_ref[...]))
    ri = lax.broadcasted_iota(jnp.int32, (L, L), 0)
    ci = lax.broadcasted_iota(jnp.int32, (L, L), 1)
    causal = ri >= ci
    tri_l = jnp.where(causal, 1.0, 0.0).astype(BF16)
    tri_u = jnp.where(ri <= ci, 1.0, 0.0).astype(BF16)
    a_cs = _dot3_rhs(tri_l, da)
    acs_ref[...] = a_cs
    acst_ref[...] = _dot3_lhs(dat, tri_u)
    a_last = a_cs[L - 1:L, :]
    stack = jnp.concatenate([dt, jnp.exp(a_last - a_cs), jnp.exp(a_cs),
                             jnp.broadcast_to(jnp.exp(a_last), (SUBLANES, LANES))], axis=0)
    sh, sm, sl = _split3(stack)
    for j in range(SSD_INNER // cw_chunk):
        cs = slice(j * cw_chunk, (j + 1) * cw_chunk)
        ej = e_ref[:, cs]
        d = lambda a: jnp.dot(a, ej, preferred_element_type=F32)
        exp_ref[:, cs] = (d(sl) + d(sm)) + d(sh)
    for j in range(SSD_INNER // cw_chunk):
        cs = slice(j * cw_chunk, (j + 1) * cw_chunk)
        xdt = xact_ref[:, cs] * exp_ref[0:L, cs]
        xdt_ref[:, cs] = xdt.astype(BF16)
        xdec_ref[:, cs] = (xdt * exp_ref[L:2 * L, cs]).astype(BF16)

    lo = ci < HALF
    gw = SSD_HEAD_DIM * (SSD_HEADS // SSD_GROUPS)
    for g in range(SSD_GROUPS):
        bg = xact_ref[:, SSD_INNER + g * SSD_STATE:SSD_INNER + (g + 1) * SSD_STATE]
        cg = xact_ref[:, SSD_INNER + BC_WIDTH + g * SSD_STATE:SSD_INNER + BC_WIDTH + (g + 1) * SSD_STATE]
        bb = bg.astype(BF16)
        cbf = cg.astype(BF16)
        cbm = lax.dot_general(cbf, bb, (((1,), (1,)), ((), ())), preferred_element_type=F32)
        gs = slice(g * gw, (g + 1) * gw)
        prev = state_ref[g]
        yoff = jnp.dot(cbf, prev.astype(BF16), preferred_element_type=F32) * exp_ref[2 * L:3 * L, gs]
        btb = bg.T.astype(BF16)
        state_ref[g] = prev * exp_ref[3 * L:3 * L + 1, gs] + jnp.dot(
            btb, xdec_ref[:, gs], preferred_element_type=F32)
        for e in range(2):
            pc = slice((2 * g + e) * LANES, (2 * g + e + 1) * LANES)
            xpair = xdt_ref[:, pc]
            yd = []
            for r in range(2):
                h = 4 * g + 2 * e + r
                seg = acs_ref[:, h:h + 1] - acst_ref[h:h + 1, :]
                lm = jnp.exp(jnp.where(causal, seg, NEG))
                yd.append(jnp.dot((cbm * lm).astype(BF16), xpair, preferred_element_type=F32))
            y_ref[:, pc] = (jnp.where(lo, yd[0], yd[1]) + yoff[:, e * LANES:(e + 1) * LANES]
                            + dfull_ref[:, pc] * xact_ref[:, pc])

    for g in range(SSD_GROUPS):
        gs = slice(g * gw, (g + 1) * gw)
        zz = z_ref[:, gs]
        yg = y_ref[:, gs] * (zz * _sigmoid(zz))
        ms = jnp.mean(yg * yg, axis=-1, keepdims=True)
        o_ref[:, gs] = (yg * lax.rsqrt(ms + EPS) * gn_ref[:, gs]).astype(o_ref.dtype)


def _ssd(proj, dt_raw, dt_rawt, cw, cb, brow, bcol, alrow, alcol, emat, dfull, gn, batch, seq):
    t = proj.shape[0]
    L = SSD_CHUNK
    nc = seq // L
    row = lambda b, c: b * nc + c
    full = lambda shape: pl.BlockSpec(shape, lambda b, c: (0, 0))
    return pl.pallas_call(
        _ssd_kernel,
        grid=(batch, nc),
        in_specs=[pl.BlockSpec((L, CONV_CH), lambda b, c: (row(b, c), XBC_OFF // CONV_CH)),
                  pl.BlockSpec((L, SSD_INNER), lambda b, c: (row(b, c), Z_OFF // SSD_INNER)),
                  pl.BlockSpec((L, LANES), lambda b, c: (row(b, c), 0)),
                  pl.BlockSpec((LANES, L), lambda b, c: (0, row(b, c))),
                  full((SSD_CONV, CONV_CH)), full((1, CONV_CH)),
                  full((1, LANES)), full((LANES, 1)), full((1, LANES)), full((LANES, 1)),
                  full((LANES, SSD_INNER)), full((1, SSD_INNER)), full((1, SSD_INNER))],
        out_specs=pl.BlockSpec((L, SSD_INNER), lambda b, c: (row(b, c), 0)),
        out_shape=jax.ShapeDtypeStruct((t, SSD_INNER), BF16),
        scratch_shapes=[pltpu.VMEM((L + SUBLANES, CONV_CH), F32),
                        pltpu.VMEM((L, CONV_CH), F32),
                        pltpu.VMEM((SSD_GROUPS, SSD_STATE, SSD_INNER // SSD_GROUPS), F32),
                        pltpu.VMEM((L, SSD_INNER), BF16),
                        pltpu.VMEM((L, SSD_INNER), BF16),
                        pltpu.VMEM((L, SSD_INNER), F32),
                        pltpu.VMEM((3 * L + SUBLANES, SSD_INNER), F32),
                        pltpu.VMEM((L, LANES), F32),
                        pltpu.VMEM((LANES, L), F32)],
        compiler_params=_params(2),
        name="ssd_scan",
    )(proj, proj, dt_raw, dt_rawt, cw, cb, brow, bcol, alrow, alcol, emat, dfull, gn)


def _out_proj_kernel(a1_ref, a2_ref, w1_ref, w2_ref, r_ref, o_ref, w1b_ref, w2b_ref):
    @pl.when(pl.program_id(1) == 0)
    def _():
        _cast_rows(w1_ref, w1b_ref)
        _cast_rows(w2_ref, w2b_ref)

    acc = jnp.dot(a1_ref[...], w1b_ref[...], preferred_element_type=F32)
    acc = acc + jnp.dot(a2_ref[...], w2b_ref[...], preferred_element_type=F32)
    o_ref[...] = r_ref[...] + acc


def _out_proj(a1, a2, w, res, tm, tn):
    t, k = a1.shape
    n = w.shape[1]
    return pl.pallas_call(
        _out_proj_kernel,
        grid=(n // tn, t // tm),
        in_specs=[pl.BlockSpec((tm, k), lambda j, m: (m, 0)),
                  pl.BlockSpec((tm, k), lambda j, m: (m, 0)),
                  pl.BlockSpec((k, tn), lambda j, m: (0, j)),
                  pl.BlockSpec((k, tn), lambda j, m: (1, j)),
                  pl.BlockSpec((tm, tn), lambda j, m: (m, j))],
        out_specs=pl.BlockSpec((tm, tn), lambda j, m: (m, j)),
        out_shape=jax.ShapeDtypeStruct((t, n), F32),
        scratch_shapes=[pltpu.VMEM((k, tn), BF16), pltpu.VMEM((k, tn), BF16)],
        compiler_params=_params(2),
        name="out_proj",
    )(a1, a2, w, w, res)


def _up_kernel(a_ref, wg_ref, wv_ref, cwg_ref, cwv_ref, cbg_ref, cbv_ref, o_ref,
               wgb_ref, wvb_ref, carry_ref, *, tm, seq):
    m = pl.program_id(1)

    @pl.when(m == 0)
    def _():
        _cast_rows(wg_ref, wgb_ref)
        _cast_rows(wv_ref, wvb_ref)

    a = a_ref[...]
    tn = o_ref.shape[1]
    seq_start = (m * tm) % seq == 0
    row8 = lax.broadcasted_iota(jnp.int32, (SUBLANES, tn), 0)
    outs = []
    for idx, (wb_ref, cw_ref, cb_ref) in enumerate(((wgb_ref, cwg_ref, cbg_ref),
                                                    (wvb_ref, cwv_ref, cbv_ref))):
        u = jnp.dot(a, wb_ref[...], preferred_element_type=F32)
        carry = jnp.where(seq_start, 0.0, carry_ref[idx])
        conv = cb_ref[...]
        for k in range(FFN_CONV - 1):
            sh = FFN_CONV - 1 - k
            us = pltpu.roll(u, sh, 0)
            top = jnp.where(row8 < sh, pltpu.roll(carry, sh, 0), us[0:SUBLANES])
            us = jnp.concatenate([top, us[SUBLANES:]], axis=0)
            conv = conv + us * cw_ref[k:k + 1, :]
        conv = conv + u * cw_ref[FFN_CONV - 1:FFN_CONV, :]
        carry_ref[idx] = u[tm - SUBLANES:tm]
        outs.append(conv)
    gate, val = outs
    o_ref[...] = ((gate * _sigmoid(gate)) * val).astype(o_ref.dtype)


def _up_proj(a, w, cw, cb, tm, tn, seq):
    t, k = a.shape
    nb = D_FF // tn
    return pl.pallas_call(
        functools.partial(_up_kernel, tm=tm, seq=seq),
        grid=(nb, t // tm),
        in_specs=[pl.BlockSpec((tm, k), lambda j, m: (m, 0)),
                  pl.BlockSpec((k, tn), lambda j, m: (0, j)),
                  pl.BlockSpec((k, tn), lambda j, m: (0, j + nb)),
                  pl.BlockSpec((FFN_CONV, tn), lambda j, m: (0, j)),
                  pl.BlockSpec((FFN_CONV, tn), lambda j, m: (0, j + nb)),
                  pl.BlockSpec((1, tn), lambda j, m: (0, j)),
                  pl.BlockSpec((1, tn), lambda j, m: (0, j + nb))],
        out_specs=pl.BlockSpec((tm, tn), lambda j, m: (m, j)),
        out_shape=jax.ShapeDtypeStruct((t, D_FF), BF16),
        scratch_shapes=[pltpu.VMEM((k, tn), BF16), pltpu.VMEM((k, tn), BF16),
                        pltpu.VMEM((2, SUBLANES, tn), F32)],
        compiler_params=_params(2),
        name="up_proj_conv_swiglu",
    )(a, w, w, cw, cw, cb, cb)


def _down_kernel(a_ref, w_ref, r_ref, o_ref, wb_ref):
    @pl.when(pl.program_id(1) == 0)
    def _():
        _cast_rows(w_ref, wb_ref)

    o_ref[...] = r_ref[...] + jnp.dot(a_ref[...], wb_ref[...], preferred_element_type=F32)


def _down_proj(a, w, res, tm, tn):
    t, k = a.shape
    n = w.shape[1]
    return pl.pallas_call(
        _down_kernel,
        grid=(n // tn, t // tm),
        in_specs=[pl.BlockSpec((tm, k), lambda j, m: (m, 0)),
                  pl.BlockSpec((k, tn), lambda j, m: (0, j)),
                  pl.BlockSpec((tm, tn), lambda j, m: (m, j))],
        out_specs=pl.BlockSpec((tm, tn), lambda j, m: (m, j)),
        out_shape=jax.ShapeDtypeStruct((t, n), F32),
        scratch_shapes=[pltpu.VMEM((k, tn), BF16)],
        compiler_params=_params(2),
        name="down_proj",
    )(a, w, res)


def _pad_lanes(v):
    return jnp.pad(v.astype(F32), (0, LANES - v.shape[0]))


def _mixer_layer(h, batch, seq, norm_mix, w_in, sinks, attn_out_norm, ssd_conv_w, ssd_conv_b, dt_bias,
                 a_log, ssd_d, ssd_norm, w_out, tables, emat):
    wdt = jnp.pad(w_in[:, MAIN_PROJ:], ((0, 0), (0, LANES - SSD_HEADS))).astype(BF16)
    xn, dt_raw, dt_rawt = _norm_dt(h, norm_mix.reshape(1, -1), wdt, wdt.T, tm=512)
    proj = _in_proj(xn, w_in, tm=1024, tn=1024)
    attn = _attention(proj, sinks.astype(F32), attn_out_norm.reshape(1, -1), tables, batch, seq)
    bias = _pad_lanes(dt_bias)
    alog = _pad_lanes(a_log)
    y = _ssd(proj, dt_raw, dt_rawt, ssd_conv_w, ssd_conv_b.reshape(1, -1),
             bias.reshape(1, -1), bias.reshape(-1, 1), alog.reshape(1, -1), alog.reshape(-1, 1),
             emat, jnp.repeat(ssd_d.astype(F32), SSD_HEAD_DIM).reshape(1, -1),
             ssd_norm.reshape(1, -1), batch, seq)
    return _out_proj(attn, y, w_out, h, tm=512, tn=512)


def kernel(x, norm_mix, w_in, sinks, attn_out_norm, ssd_conv_w, ssd_conv_b, dt_bias, a_log, ssd_d, ssd_norm,
           w_out, norm_ffn, w_up, ffn_conv_w, ffn_conv_b, w_down, norm_final):
    batch, seq, d = x.shape
    h = x.reshape(batch * seq, d)
    tables = _rope_tables(seq)
    head_of_channel = np.arange(SSD_INNER) // SSD_HEAD_DIM
    emat = jnp.asarray(np.arange(LANES)[:, None] == head_of_channel[None, :], dtype=BF16)
    for l in range(norm_mix.shape[0]):
        h = _mixer_layer(h, batch, seq, norm_mix[l], w_in[l], sinks[l], attn_out_norm[l], ssd_conv_w[l],
                         ssd_conv_b[l], dt_bias[l], a_log[l], ssd_d[l], ssd_norm[l], w_out[l], tables, emat)
        hn = _norm(h, norm_ffn[l].reshape(1, -1), BF16, 512, "norm_ffn")
        act = _up_proj(hn, w_up[l], ffn_conv_w[l], ffn_conv_b[l].reshape(1, -1), tm=512, tn=512, seq=seq)
        h = _down_proj(act, w_down[l], h, tm=512, tn=512)
    out = _norm(h, norm_final.reshape(1, -1), F32, 512, "norm_final")
    return out.reshape(batch, seq, d)
```

```python
import functools

import numpy as np
import jax
import jax.numpy as jnp
from jax import lax
from jax.experimental import pallas as pl
from jax.experimental.pallas import tpu as pltpu

F32 = jnp.float32
BF16 = jnp.bfloat16

D_MODEL = 2048
N_Q_HEADS = 32
N_KV_HEADS = 8
HEAD_DIM = 64
Q_PER_KV = N_Q_HEADS // N_KV_HEADS
WINDOW = 128
ATTN_BLOCK = 128
ROT_DIM = HEAD_DIM // 4
ROPE_THETA = 500000.0
SSD_HEADS = 32
SSD_HEAD_DIM = 64
SSD_INNER = SSD_HEADS * SSD_HEAD_DIM
SSD_GROUPS = 8
SSD_STATE = 128
SSD_CONV = 4
SSD_CHUNK = 128
ATTN_WIDTH = N_Q_HEADS * HEAD_DIM
KV_WIDTH = N_KV_HEADS * HEAD_DIM
BC_WIDTH = SSD_GROUPS * SSD_STATE
CONV_CH = SSD_INNER + 2 * BC_WIDTH
MAIN_PROJ = ATTN_WIDTH + 2 * KV_WIDTH + SSD_INNER + CONV_CH
D_FF = 5632
FFN_CONV = 3
EPS = 1e-6

LANES = 128
SUBLANES = 8
HALF = LANES // 2
NEG = -1e30
VMEM_LIMIT = 56 * 1024 * 1024

Q_OFF = 0
Z_OFF = ATTN_WIDTH
XBC_OFF = Z_OFF + SSD_INNER
K_OFF = XBC_OFF + CONV_CH
V_OFF = K_OFF + KV_WIDTH


def _params(n_axes, flags=None):
    return pltpu.CompilerParams(dimension_semantics=("arbitrary",) * n_axes,
                                vmem_limit_bytes=VMEM_LIMIT, flags=flags)


def _sigmoid(x):
    return 1.0 / (1.0 + jnp.exp(-x))


def _softplus(x):
    return jnp.maximum(x, 0.0) + jnp.log1p(jnp.exp(-jnp.abs(x)))


def _cast_rows(src_ref, dst_ref, rows=256):
    k = src_ref.shape[0]
    for r in range(0, k, rows):
        dst_ref[r:r + rows, :] = src_ref[r:r + rows, :].astype(BF16)


def _delay_rows(blk, sh):
    n, w = blk.shape[0] - SUBLANES, blk.shape[1]
    g = blk.reshape(n // SUBLANES + 1, SUBLANES, w)
    rot = pltpu.roll(g, sh, 1)
    row = lax.broadcasted_iota(jnp.int32, (SUBLANES, w), 0)
    out = jnp.where(row < sh, rot[:-1], rot[1:])
    return out.reshape(n, w)


def _rmsnorm_rows(x, g):
    ms = jnp.mean(x * x, axis=-1, keepdims=True)
    return x * lax.rsqrt(ms + EPS) * g


def _norm_dt_kernel(x_ref, g_ref, wdt_ref, wdtt_ref, xn_ref, dt_ref, dtt_ref):
    xn = _rmsnorm_rows(x_ref[...], g_ref[...]).astype(BF16)
    xn_ref[...] = xn
    dt_ref[...] = jnp.dot(xn, wdt_ref[...], preferred_element_type=F32)
    dtt_ref[...] = lax.dot_general(wdtt_ref[...], xn, (((1,), (1,)), ((), ())),
                                   preferred_element_type=F32)


def _norm_dt(x2, g, wdt, wdtt, tm):
    t, d = x2.shape
    return pl.pallas_call(
        _norm_dt_kernel,
        grid=(t // tm,),
        in_specs=[pl.BlockSpec((tm, d), lambda i: (i, 0)),
                  pl.BlockSpec((1, d), lambda i: (0, 0)),
                  pl.BlockSpec((d, LANES), lambda i: (0, 0)),
                  pl.BlockSpec((LANES, d), lambda i: (0, 0))],
        out_specs=[pl.BlockSpec((tm, d), lambda i: (i, 0)),
                   pl.BlockSpec((tm, LANES), lambda i: (i, 0)),
                   pl.BlockSpec((LANES, tm), lambda i: (0, i))],
        out_shape=[jax.ShapeDtypeStruct((t, d), BF16),
                   jax.ShapeDtypeStruct((t, LANES), F32),
                   jax.ShapeDtypeStruct((LANES, t), F32)],
        compiler_params=_params(1),
        name="norm_dt",
    )(x2, g, wdt, wdtt)


def _norm_kernel(x_ref, g_ref, o_ref):
    o_ref[...] = _rmsnorm_rows(x_ref[...], g_ref[...]).astype(o_ref.dtype)


def _norm(x2, g, out_dtype, tm, name):
    t, d = x2.shape
    return pl.pallas_call(
        _norm_kernel,
        grid=(t // tm,),
        in_specs=[pl.BlockSpec((tm, d), lambda i: (i, 0)),
                  pl.BlockSpec((1, d), lambda i: (0, 0))],
        out_specs=pl.BlockSpec((tm, d), lambda i: (i, 0)),
        out_shape=jax.ShapeDtypeStruct((t, d), out_dtype),
        compiler_params=_params(1),
        name=name,
    )(x2, g)


def _in_proj_kernel(a_ref, wt_ref, o_ref, wbf_ref):
    @pl.when(pl.program_id(1) == 0)
    def _():
        _cast_rows(wt_ref, wbf_ref)

    o_ref[...] = lax.dot_general(a_ref[...], wbf_ref[...], (((1,), (1,)), ((), ())),
                                 preferred_element_type=F32)


def _in_proj(xn, wt, tm, tn):
    t, k = xn.shape
    u = 1024 // tn

    def wmap(j, m):
        return (jnp.where(j < 2 * u, j, jnp.where(j < 8 * u, j + u, j - 6 * u)), 0)

    return pl.pallas_call(
        _in_proj_kernel,
        grid=(MAIN_PROJ // tn, t // tm),
        in_specs=[pl.BlockSpec((tm, k), lambda j, m: (m, 0)),
                  pl.BlockSpec((tn, k), wmap)],
        out_specs=pl.BlockSpec((tm, tn), lambda j, m: (m, j)),
        out_shape=jax.ShapeDtypeStruct((t, MAIN_PROJ), F32),
        scratch_shapes=[pltpu.VMEM((tn, k), BF16)],
        compiler_params=_params(2),
        name="in_proj",
    )(xn, wt)


def _rope_tables(seq):
    half = ROT_DIM // 2
    inv = 1.0 / (ROPE_THETA ** (jnp.arange(0, ROT_DIM, 2, dtype=F32) / ROT_DIM))
    ang = jnp.arange(seq, dtype=F32)[:, None] * inv[None, :]
    cos, sin = jnp.cos(ang), jnp.sin(ang)
    d = np.arange(LANES) % HEAD_DIM
    idx = d % half
    in_rot = jnp.asarray(d < ROT_DIM)
    first = jnp.asarray(d < half)
    second = jnp.asarray((d >= half) & (d < ROT_DIM))
    c = jnp.where(in_rot[None, :], cos[:, idx], 1.0)
    s1 = jnp.where(first[None, :], -sin[:, idx], 0.0)
    s2 = jnp.where(second[None, :], sin[:, idx], 0.0)
    return c.astype(F32), s1.astype(F32), s2.astype(F32)


def _rope(x, c, s1, s2):
    half = ROT_DIM // 2
    return x * c + pltpu.roll(x, LANES - half, 1) * s1 + pltpu.roll(x, half, 1) * s2


def _attn_kernel(sinks_ref, q_ref, kp_ref, kc_ref, vp_ref, vc_ref,
                 cc_ref, s1c_ref, s2c_ref, cp_ref, s1p_ref, s2p_ref, g_ref, o_ref,
                 lhs_ref, kb_ref, vb_ref, acc_ref):
    blk = ATTN_BLOCK
    n = pl.program_id(1)
    lo = lax.broadcasted_iota(jnp.int32, (blk, LANES), 1) < HALF
    lo2 = lax.broadcasted_iota(jnp.int32, (2 * blk, LANES), 1) < HALF
    cc, s1c, s2c = cc_ref[...], s1c_ref[...], s2c_ref[...]
    cp, s1p, s2p = cp_ref[...], s1p_ref[...], s2p_ref[...]
    scale = HEAD_DIM ** -0.5
    cq, s1q, s2q = cc * scale, s1c * scale, s2c * scale

    for col in range(ATTN_WIDTH // LANES):
        h, jj = col // 2, col % 2
        q2 = _rope(q_ref[:, col * LANES:(col + 1) * LANES], cq, s1q, s2q)
        lhs_ref[h, (2 * jj) * blk:(2 * jj + 1) * blk, :] = jnp.where(lo, q2, 0.0).astype(BF16)
        lhs_ref[h, (2 * jj + 1) * blk:(2 * jj + 2) * blk, :] = jnp.where(lo, 0.0, q2).astype(BF16)
    ones = jnp.ones((2 * blk, LANES), BF16)
    for i in range(N_KV_HEADS // 2):
        cols = slice(i * LANES, (i + 1) * LANES)
        kcat = jnp.concatenate([_rope(kp_ref[:, cols], cp, s1p, s2p),
                                _rope(kc_ref[:, cols], cc, s1c, s2c)], axis=0)
        vcat = jnp.concatenate([vp_ref[:, cols], vc_ref[:, cols]], axis=0)
        kswp = pltpu.roll(kcat, HALF, 1)
        vswp = pltpu.roll(vcat, HALF, 1)
        kb_ref[2 * i] = jnp.where(lo2, kcat, kswp).astype(BF16)
        kb_ref[2 * i + 1] = jnp.where(lo2, kswp, kcat).astype(BF16)
        vb_ref[2 * i, :, 0:LANES] = jnp.where(lo2, vcat, vswp).astype(BF16)
        vb_ref[2 * i + 1, :, 0:LANES] = jnp.where(lo2, vswp, vcat).astype(BF16)
        vb_ref[2 * i, :, LANES:2 * LANES] = ones
        vb_ref[2 * i + 1, :, LANES:2 * LANES] = ones

    qi = lax.broadcasted_iota(jnp.int32, (blk, 2 * blk), 0)
    kj = lax.broadcasted_iota(jnp.int32, (blk, 2 * blk), 1)
    rel = qi + blk - kj
    kmin = jnp.where(n > 0, 0, blk)
    bias = jnp.where((rel >= 0) & (rel < WINDOW) & (kj >= kmin), 0.0, NEG)

    for h in range(N_KV_HEADS):
        s = lax.dot_general(lhs_ref[h], kb_ref[h], (((1,), (1,)), ((), ())),
                            preferred_element_type=F32)
        ps, sms = [], []
        for r in range(Q_PER_KV):
            sink = sinks_ref[Q_PER_KV * h + r]
            sr = s[r * blk:(r + 1) * blk] + bias
            m = jnp.maximum(jnp.max(sr, axis=-1, keepdims=True), sink)
            ps.append(jnp.exp(sr - m).astype(BF16))
            sms.append(jnp.exp(sink - m))
        o = jnp.dot(jnp.concatenate(ps, axis=0), vb_ref[h], preferred_element_type=F32)
        outs = []
        for r in range(Q_PER_KV):
            rows = slice(r * blk, (r + 1) * blk)
            den = o[rows, LANES:2 * LANES] + sms[r]
            outs.append(o[rows, 0:LANES] * (1.0 / den))
        for jj in range(2):
            col = 2 * h + jj
            acc_ref[:, col * LANES:(col + 1) * LANES] = jnp.where(lo, outs[2 * jj], outs[2 * jj + 1])

    o_ref[...] = _rmsnorm_rows(acc_ref[...], g_ref[...]).astype(o_ref.dtype)


def _attention(proj, sinks, g, tables, batch, seq):
    t = proj.shape[0]
    blk = ATTN_BLOCK
    nb = seq // blk
    kblk = K_OFF // KV_WIDTH
    vblk = V_OFF // KV_WIDTH
    c, s1, s2 = tables

    def cur(b, n):
        return (b * nb + n, 0)

    def tab_cur(b, n):
        return (n, 0)

    def tab_prev(b, n):
        return (jnp.maximum(n - 1, 0), 0)

    def prev_rows(b, n):
        return jnp.maximum(b * nb + n - 1, 0)

    tab = lambda f: pl.BlockSpec((blk, LANES), f)
    return pl.pallas_call(
        _attn_kernel,
        grid=(batch, nb),
        in_specs=[pl.BlockSpec(memory_space=pltpu.SMEM),
                  pl.BlockSpec((blk, ATTN_WIDTH), cur),
                  pl.BlockSpec((blk, KV_WIDTH), lambda b, n: (prev_rows(b, n), kblk)),
                  pl.BlockSpec((blk, KV_WIDTH), lambda b, n: (b * nb + n, kblk)),
                  pl.BlockSpec((blk, KV_WIDTH), lambda b, n: (prev_rows(b, n), vblk)),
                  pl.BlockSpec((blk, KV_WIDTH), lambda b, n: (b * nb + n, vblk)),
                  tab(tab_cur), tab(tab_cur), tab(tab_cur),
                  tab(tab_prev), tab(tab_prev), tab(tab_prev),
                  pl.BlockSpec((1, ATTN_WIDTH), lambda b, n: (0, 0))],
        out_specs=pl.BlockSpec((blk, ATTN_WIDTH), cur),
        out_shape=jax.ShapeDtypeStruct((t, ATTN_WIDTH), BF16),
        scratch_shapes=[pltpu.VMEM((N_KV_HEADS, Q_PER_KV * blk, LANES), BF16),
                        pltpu.VMEM((N_KV_HEADS, 2 * blk, LANES), BF16),
                        pltpu.VMEM((N_KV_HEADS, 2 * blk, 2 * LANES), BF16),
                        pltpu.VMEM((blk, ATTN_WIDTH), F32)],
        compiler_params=_params(2),
        name="swa_attention",
    )(sinks, proj, proj, proj, proj, proj, c, s1, s2, c, s1, s2, g)


def _split3(x):
    h = x.astype(BF16)
    r = x - h.astype(F32)
    m = r.astype(BF16)
    l = (r - m.astype(F32)).astype(BF16)
    return h, m, l


def _dot3_lhs(x, w):
    h, m, l = _split3(x)
    d = lambda a: jnp.dot(a, w, preferred_element_type=F32)
    return (d(l) + d(m)) + d(h)


def _dot3_rhs(w, x):
    h, m, l = _split3(x)
    d = lambda a: jnp.dot(w, a, preferred_element_type=F32)
    return (d(l) + d(m)) + d(h)


def _ssd_kernel(xbc_ref, z_ref, dt_ref, dtt_ref, cw_ref, cb_ref, brow_ref, bcol_ref,
                alrow_ref, alcol_ref, e_ref, dfull_ref, gn_ref, o_ref,
                ext_ref, xact_ref, state_ref, xdt_ref, xdec_ref, y_ref, exp_ref, acs_ref, acst_ref):
    L = SSD_CHUNK
    c = pl.program_id(1)

    @pl.when(c == 0)
    def _():
        ext_ref[0:SUBLANES, :] = jnp.zeros((SUBLANES, CONV_CH), F32)
        state_ref[...] = jnp.zeros_like(state_ref)

    ext_ref[SUBLANES:SUBLANES + L, :] = xbc_ref[...]
    cw_chunk = 512
    for j in range(CONV_CH // cw_chunk):
        cs = slice(j * cw_chunk, (j + 1) * cw_chunk)
        blk = ext_ref[:, cs]
        acc = cb_ref[:, cs]
        for k in range(SSD_CONV - 1):
            acc = acc + _delay_rows(blk, SSD_CONV - 1 - k) * cw_ref[k:k + 1, cs]
        acc = acc + blk[SUBLANES:] * cw_ref[SSD_CONV - 1:SSD_CONV, cs]
        xact_ref[:, cs] = acc * _sigmoid(acc)
    ext_ref[0:SUBLANES, :] = ext_ref[L:L + SUBLANES, :]

    dt = _softplus(dt_ref[...] + brow_ref[...])
    dtt = _softplus(dtt_ref[...] + bcol_ref[...])
    da = dt * (-jnp.exp(alrow_ref[...]))
    dat = dtt * (-jnp.exp(alcol_ref[...]))
    ri = lax.broadcasted_iota(jnp.int32, (L, L), 0)
    ci = lax.broadcasted_iota(jnp.int32, (L, L), 1)
    causal = ri >= ci
    tri_l = jnp.where(causal, 1.0, 0.0).astype(BF16)
    tri_u = jnp.where(ri <= ci, 1.0, 0.0).astype(BF16)
    a_cs = _dot3_rhs(tri_l, da)
    acs_ref[...] = a_cs
    acst_ref[...] = _dot3_lhs(dat, tri_u)
    a_last = a_cs[L - 1:L, :]
    stack = jnp.concatenate([dt, jnp.exp(a_last - a_cs), jnp.exp(a_cs),
                             jnp.broadcast_to(jnp.exp(a_last), (SUBLANES, LANES))], axis=0)
    sh, sm, sl = _split3(stack)
    for j in range(SSD_INNER // cw_chunk):
        cs = slice(j * cw_chunk, (j + 1) * cw_chunk)
        ej = e_ref[:, cs]
        d = lambda a: jnp.dot(a, ej, preferred_element_type=F32)
        exp_ref[:, cs] = (d(sl) + d(sm)) + d(sh)
    for j in range(SSD_INNER // cw_chunk):
        cs = slice(j * cw_chunk, (j + 1) * cw_chunk)
        xdt = xact_ref[:, cs] * exp_ref[0:L, cs]
        xdt_ref[:, cs] = xdt.astype(BF16)
        xdec_ref[:, cs] = (xdt * exp_ref[L:2 * L, cs]).astype(BF16)

    lo = ci < HALF
    gw = SSD_HEAD_DIM * (SSD_HEADS // SSD_GROUPS)
    for g in range(SSD_GROUPS):
        bg = xact_ref[:, SSD_INNER + g * SSD_STATE:SSD_INNER + (g + 1) * SSD_STATE]
        cg = xact_ref[:, SSD_INNER + BC_WIDTH + g * SSD_STATE:SSD_INNER + BC_WIDTH + (g + 1) * SSD_STATE]
        bb = bg.astype(BF16)
        cbf = cg.astype(BF16)
        cbm = lax.dot_general(cbf, bb, (((1,), (1,)), ((), ())), preferred_element_type=F32)
        gs = slice(g * gw, (g + 1) * gw)
        prev = state_ref[g]
        yoff = jnp.dot(cbf, prev.astype(BF16), preferred_element_type=F32) * exp_ref[2 * L:3 * L, gs]
        btb = bg.T.astype(BF16)
        state_ref[g] = prev * exp_ref[3 * L:3 * L + 1, gs] + jnp.dot(
            btb, xdec_ref[:, gs], preferred_element_type=F32)
        for e in range(2):
            pc = slice((2 * g + e) * LANES, (2 * g + e + 1) * LANES)
            xpair = xdt_ref[:, pc]
            yd = []
            for r in range(2):
                h = 4 * g + 2 * e + r
                seg = acs_ref[:, h:h + 1] - acst_ref[h:h + 1, :]
                lm = jnp.exp(jnp.where(causal, seg, NEG))
                yd.append(jnp.dot((cbm * lm).astype(BF16), xpair, preferred_element_type=F32))
            y_ref[:, pc] = (jnp.where(lo, yd[0], yd[1]) + yoff[:, e * LANES:(e + 1) * LANES]
                            + dfull_ref[:, pc] * xact_ref[:, pc])

    for g in range(SSD_GROUPS):
        gs = slice(g * gw, (g + 1) * gw)
        zz = z_ref[:, gs]
        yg = y_ref[:, gs] * (zz * _sigmoid(zz))
        ms = jnp.mean(yg * yg, axis=-1, keepdims=True)
        o_ref[:, gs] = (yg * lax.rsqrt(ms + EPS) * gn_ref[:, gs]).astype(o_ref.dtype)


def _ssd(proj, dt_raw, dt_rawt, cw, cb, brow, bcol, alrow, alcol, emat, dfull, gn, batch, seq):
    t = proj.shape[0]
    L = SSD_CHUNK
    nc = seq // L
    row = lambda b, c: b * nc + c
    full = lambda shape: pl.BlockSpec(shape, lambda b, c: (0, 0))
    return pl.pallas_call(
        _ssd_kernel,
        grid=(batch, nc),
        in_specs=[pl.BlockSpec((L, CONV_CH), lambda b, c: (row(b, c), XBC_OFF // CONV_CH)),
                  pl.BlockSpec((L, SSD_INNER), lambda b, c: (row(b, c), Z_OFF // SSD_INNER)),
                  pl.BlockSpec((L, LANES), lambda b, c: (row(b, c), 0)),
                  pl.BlockSpec((LANES, L), lambda b, c: (0, row(b, c))),
                  full((SSD_CONV, CONV_CH)), full((1, CONV_CH)),
                  full((1, LANES)), full((LANES, 1)), full((1, LANES)), full((LANES, 1)),
                  full((LANES, SSD_INNER)), full((1, SSD_INNER)), full((1, SSD_INNER))],
        out_specs=pl.BlockSpec((L, SSD_INNER), lambda b, c: (row(b, c), 0)),
        out_shape=jax.ShapeDtypeStruct((t, SSD_INNER), BF16),
        scratch_shapes=[pltpu.VMEM((L + SUBLANES, CONV_CH), F32),
                        pltpu.VMEM((L, CONV_CH), F32),
                        pltpu.VMEM((SSD_GROUPS, SSD_STATE, SSD_INNER // SSD_GROUPS), F32),
                        pltpu.VMEM((L, SSD_INNER), BF16),
                        pltpu.VMEM((L, SSD_INNER), BF16),
                        pltpu.VMEM((L, SSD_INNER), F32),
                        pltpu.VMEM((3 * L + SUBLANES, SSD_INNER), F32),
                        pltpu.VMEM((L, LANES), F32),
                        pltpu.VMEM((LANES, L), F32)],
        compiler_params=_params(2),
        name="ssd_scan",
    )(proj, proj, dt_raw, dt_rawt, cw, cb, brow, bcol, alrow, alcol, emat, dfull, gn)


def _out_proj_kernel(a1_ref, a2_ref, w1_ref, w2_ref, r_ref, o_ref, w1b_ref, w2b_ref):
    @pl.when(pl.program_id(1) == 0)
    def _():
        _cast_rows(w1_ref, w1b_ref)
        _cast_rows(w2_ref, w2b_ref)

    acc = jnp.dot(a1_ref[...], w1b_ref[...], preferred_element_type=F32)
    acc = acc + jnp.dot(a2_ref[...], w2b_ref[...], preferred_element_type=F32)
    o_ref[...] = r_ref[...] + acc


def _out_proj(a1, a2, w, res, tm, tn):
    t, k = a1.shape
    n = w.shape[1]
    return pl.pallas_call(
        _out_proj_kernel,
        grid=(n // tn, t // tm),
        in_specs=[pl.BlockSpec((tm, k), lambda j, m: (m, 0)),
                  pl.BlockSpec((tm, k), lambda j, m: (m, 0)),
                  pl.BlockSpec((k, tn), lambda j, m: (0, j)),
                  pl.BlockSpec((k, tn), lambda j, m: (1, j)),
                  pl.BlockSpec((tm, tn), lambda j, m: (m, j))],
        out_specs=pl.BlockSpec((tm, tn), lambda j, m: (m, j)),
        out_shape=jax.ShapeDtypeStruct((t, n), F32),
        scratch_shapes=[pltpu.VMEM((k, tn), BF16), pltpu.VMEM((k, tn), BF16)],
        compiler_params=_params(2),
        name="out_proj",
    )(a1, a2, w, w, res)


def _up_epilogue(u_ref, cwg, cwv, cbg, cbv, o_ref, r0, r1, tn, rc):
    for r in range(r0, r1, rc):
        halves = []
        for idx, (cw, cb) in enumerate(((cwg, cbg), (cwv, cbv))):
            blk = u_ref[r:r + rc + SUBLANES, idx * tn:(idx + 1) * tn]
            conv = cb
            for k in range(FFN_CONV - 1):
                conv = conv + _delay_rows(blk, FFN_CONV - 1 - k) * cw[k:k + 1, :]
            conv = conv + blk[SUBLANES:] * cw[FFN_CONV - 1:FFN_CONV, :]
            halves.append(conv)
        gate, val = halves
        o_ref[r:r + rc, :] = ((gate * _sigmoid(gate)) * val).astype(o_ref.dtype)


def _up_kernel(a_ref, wg_ref, wv_ref, cwg_ref, cwv_ref, cbg_ref, cbv_ref, o_ref,
               wb_ref, ua_ref, ub_ref, stage_ref, abuf_ref, *, tm, mc, seq, n_tiles, rc):
    m = pl.program_id(1)
    tn = o_ref.shape[1]
    slots = (ua_ref, ub_ref)

    @pl.when(m == 0)
    def _():
        _cast_rows(wg_ref, wb_ref.at[:, 0:tn])
        _cast_rows(wv_ref, wb_ref.at[:, tn:2 * tn])

    abuf_ref[...] = a_ref[...]
    cwg, cwv, cbg, cbv = cwg_ref[...], cwv_ref[...], cbg_ref[...], cbv_ref[...]

    def matmul(u_new, c):
        u_new[SUBLANES + c * mc:SUBLANES + (c + 1) * mc, :] = jnp.dot(
            abuf_ref[c * mc:(c + 1) * mc, :], wb_ref[...], preferred_element_type=F32)

    def epilogue(u_old, c):
        _up_epilogue(u_old, cwg, cwv, cbg, cbv, stage_ref, c * mc, (c + 1) * mc, tn, rc)

    @pl.when(m == 0)
    def _():
        ua_ref[0:SUBLANES, :] = jnp.zeros((SUBLANES, 2 * tn), F32)
        for c in range(tm // mc):
            matmul(ua_ref, c)

    for par in range(2):
        @pl.when((m > 0) & (m < n_tiles) & (m % 2 == par))
        def _():
            u_new, u_old = slots[par], slots[1 - par]
            u_new[0:SUBLANES, :] = jnp.where((m * tm) % seq == 0, 0.0, u_old[tm:tm + SUBLANES, :])
            for c in range(tm // mc):
                epilogue(u_old, c)
                matmul(u_new, c)
            o_ref[...] = stage_ref[...]

    @pl.when(m == n_tiles)
    def _():
        for c in range(tm // mc):
            epilogue(slots[(n_tiles - 1) % 2], c)
        o_ref[...] = stage_ref[...]


def _up_proj(a, w, cw, cb, tm, tn, mc, rc, seq):
    t, k = a.shape
    nb = D_FF // tn
    n_tiles = t // tm
    return pl.pallas_call(
        functools.partial(_up_kernel, tm=tm, mc=mc, seq=seq, n_tiles=n_tiles, rc=rc),
        grid=(nb, n_tiles + 1),
        in_specs=[pl.BlockSpec((tm, k), lambda j, m: (jnp.minimum(m, n_tiles - 1), 0)),
                  pl.BlockSpec((k, tn), lambda j, m: (0, j)),
                  pl.BlockSpec((k, tn), lambda j, m: (0, j + nb)),
                  pl.BlockSpec((FFN_CONV, tn), lambda j, m: (0, j)),
                  pl.BlockSpec((FFN_CONV, tn), lambda j, m: (0, j + nb)),
                  pl.BlockSpec((1, tn), lambda j, m: (0, j)),
                  pl.BlockSpec((1, tn), lambda j, m: (0, j + nb))],
        out_specs=pl.BlockSpec((tm, tn), lambda j, m: (jnp.maximum(m - 1, 0), j)),
        out_shape=jax.ShapeDtypeStruct((t, D_FF), BF16),
        scratch_shapes=[pltpu.VMEM((k, 2 * tn), BF16),
                        pltpu.VMEM((tm + SUBLANES, 2 * tn), F32),
                        pltpu.VMEM((tm + SUBLANES, 2 * tn), F32),
                        pltpu.VMEM((tm, tn), BF16),
                        pltpu.VMEM((tm, k), BF16)],
        compiler_params=_params(2),
        name="up_proj_conv_swiglu",
    )(a, w, w, cw, cw, cb, cb)


def _down_kernel(a_ref, w_ref, r_ref, o_ref, wb_ref):
    @pl.when(pl.program_id(1) == 0)
    def _():
        _cast_rows(w_ref, wb_ref)

    o_ref[...] = r_ref[...] + jnp.dot(a_ref[...], wb_ref[...], preferred_element_type=F32)


def _down_proj(a, w, res, tm, tn):
    t, k = a.shape
    n = w.shape[1]
    return pl.pallas_call(
        _down_kernel,
        grid=(n // tn, t // tm),
        in_specs=[pl.BlockSpec((tm, k), lambda j, m: (m, 0)),
                  pl.BlockSpec((k, tn), lambda j, m: (0, j)),
                  pl.BlockSpec((tm, tn), lambda j, m: (m, j))],
        out_specs=pl.BlockSpec((tm, tn), lambda j, m: (m, j)),
        out_shape=jax.ShapeDtypeStruct((t, n), F32),
        scratch_shapes=[pltpu.VMEM((k, tn), BF16)],
        compiler_params=_params(2),
        name="down_proj",
    )(a, w, res)


def _pad_lanes(v):
    return jnp.pad(v.astype(F32), (0, LANES - v.shape[0]))


def _mixer_layer(h, batch, seq, norm_mix, w_in, sinks, attn_out_norm, ssd_conv_w, ssd_conv_b, dt_bias,
                 a_log, ssd_d, ssd_norm, w_out, tables, emat):
    w_in_t = jnp.swapaxes(w_in, 0, 1)
    wdt_t = jnp.pad(w_in_t[MAIN_PROJ:], ((0, LANES - SSD_HEADS), (0, 0))).astype(BF16)
    xn, dt_raw, dt_rawt = _norm_dt(h, norm_mix.reshape(1, -1), wdt_t.T, wdt_t, tm=512)
    proj = _in_proj(xn, w_in_t, tm=1024, tn=1024)
    attn = _attention(proj, sinks.astype(F32), attn_out_norm.reshape(1, -1), tables, batch, seq)
    bias = _pad_lanes(dt_bias)
    alog = _pad_lanes(a_log)
    y = _ssd(proj, dt_raw, dt_rawt, ssd_conv_w, ssd_conv_b.reshape(1, -1),
             bias.reshape(1, -1), bias.reshape(-1, 1), alog.reshape(1, -1), alog.reshape(-1, 1),
             emat, jnp.repeat(ssd_d.astype(F32), SSD_HEAD_DIM).reshape(1, -1),
             ssd_norm.reshape(1, -1), batch, seq)
    return _out_proj(attn, y, w_out, h, tm=1024, tn=512)


def kernel(x, norm_mix, w_in, sinks, attn_out_norm, ssd_conv_w, ssd_conv_b, dt_bias, a_log, ssd_d, ssd_norm,
           w_out, norm_ffn, w_up, ffn_conv_w, ffn_conv_b, w_down, norm_final):
    batch, seq, d = x.shape
    h = x.reshape(batch * seq, d)
    tables = _rope_tables(seq)
    head_of_channel = np.arange(SSD_INNER) // SSD_HEAD_DIM
    emat = jnp.asarray(np.arange(LANES)[:, None] == head_of_channel[None, :], dtype=BF16)
    for l in range(norm_mix.shape[0]):
        h = _mixer_layer(h, batch, seq, norm_mix[l], w_in[l], sinks[l], attn_out_norm[l], ssd_conv_w[l],
                         ssd_conv_b[l], dt_bias[l], a_log[l], ssd_d[l], ssd_norm[l], w_out[l], tables, emat)
        hn = _norm(h, norm_ffn[l].reshape(1, -1), BF16, 512, "norm_ffn")
        act = _up_proj(hn, w_up[l], ffn_conv_w[l], ffn_conv_b[l].reshape(1, -1),
                       tm=1024, tn=512, mc=256, rc=64, seq=seq)
        h = _down_proj(act, w_down[l], h, tm=512, tn=512)
    out = _norm(h, norm_final.reshape(1, -1), F32, 512, "norm_final")
    return out.reshape(batch, seq, d)
```

```python
import functools

import numpy as np
import jax
import jax.numpy as jnp
from jax import lax
from jax.experimental import pallas as pl
from jax.experimental.pallas import tpu as pltpu

F32 = jnp.float32
BF16 = jnp.bfloat16

D_MODEL = 2048
N_Q_HEADS = 32
N_KV_HEADS = 8
HEAD_DIM = 64
Q_PER_KV = N_Q_HEADS // N_KV_HEADS
WINDOW = 128
ATTN_BLOCK = 128
ROT_DIM = HEAD_DIM // 4
ROPE_THETA = 500000.0
SSD_HEADS = 32
SSD_HEAD_DIM = 64
SSD_INNER = SSD_HEADS * SSD_HEAD_DIM
SSD_GROUPS = 8
SSD_STATE = 128
SSD_CONV = 4
SSD_CHUNK = 128
ATTN_WIDTH = N_Q_HEADS * HEAD_DIM
KV_WIDTH = N_KV_HEADS * HEAD_DIM
BC_WIDTH = SSD_GROUPS * SSD_STATE
CONV_CH = SSD_INNER + 2 * BC_WIDTH
MAIN_PROJ = ATTN_WIDTH + 2 * KV_WIDTH + SSD_INNER + CONV_CH
D_FF = 5632
FFN_CONV = 3
EPS = 1e-6

LANES = 128
SUBLANES = 8
HALF = LANES // 2
NEG = -1e30
VMEM_LIMIT = 56 * 1024 * 1024

Q_OFF = 0
Z_OFF = ATTN_WIDTH
XBC_OFF = Z_OFF + SSD_INNER
K_OFF = XBC_OFF + CONV_CH
V_OFF = K_OFF + KV_WIDTH


def _params(n_axes, flags=None):
    return pltpu.CompilerParams(dimension_semantics=("arbitrary",) * n_axes,
                                vmem_limit_bytes=VMEM_LIMIT, flags=flags)


def _sigmoid(x):
    return 1.0 / (1.0 + jnp.exp(-x))


def _softplus(x):
    return jnp.maximum(x, 0.0) + jnp.log1p(jnp.exp(-jnp.abs(x)))


def _cast_rows(src_ref, dst_ref, rows=256):
    k = src_ref.shape[0]
    for r in range(0, k, rows):
        dst_ref[r:r + rows, :] = src_ref[r:r + rows, :].astype(BF16)


def _delay_rows(blk, sh):
    n, w = blk.shape[0] - SUBLANES, blk.shape[1]
    g = blk.reshape(n // SUBLANES + 1, SUBLANES, w)
    rot = pltpu.roll(g, sh, 1)
    row = lax.broadcasted_iota(jnp.int32, (SUBLANES, w), 0)
    out = jnp.where(row < sh, rot[:-1], rot[1:])
    return out.reshape(n, w)


def _rmsnorm_rows(x, g):
    ms = jnp.mean(x * x, axis=-1, keepdims=True)
    return x * lax.rsqrt(ms + EPS) * g


def _norm_dt_kernel(x_ref, g_ref, wdt_ref, wdtt_ref, xn_ref, dt_ref, dtt_ref):
    xn = _rmsnorm_rows(x_ref[...], g_ref[...]).astype(BF16)
    xn_ref[...] = xn
    dt_ref[...] = jnp.dot(xn, wdt_ref[...], preferred_element_type=F32)
    dtt_ref[...] = lax.dot_general(wdtt_ref[...], xn, (((1,), (1,)), ((), ())),
                                   preferred_element_type=F32)


def _norm_dt(x2, g, wdt, wdtt, tm):
    t, d = x2.shape
    return pl.pallas_call(
        _norm_dt_kernel,
        grid=(t // tm,),
        in_specs=[pl.BlockSpec((tm, d), lambda i: (i, 0)),
                  pl.BlockSpec((1, d), lambda i: (0, 0)),
                  pl.BlockSpec((d, LANES), lambda i: (0, 0)),
                  pl.BlockSpec((LANES, d), lambda i: (0, 0))],
        out_specs=[pl.BlockSpec((tm, d), lambda i: (i, 0)),
                   pl.BlockSpec((tm, LANES), lambda i: (i, 0)),
                   pl.BlockSpec((LANES, tm), lambda i: (0, i))],
        out_shape=[jax.ShapeDtypeStruct((t, d), BF16),
                   jax.ShapeDtypeStruct((t, LANES), F32),
                   jax.ShapeDtypeStruct((LANES, t), F32)],
        compiler_params=_params(1),
        name="norm_dt",
    )(x2, g, wdt, wdtt)


def _norm_kernel(x_ref, g_ref, o_ref):
    o_ref[...] = _rmsnorm_rows(x_ref[...], g_ref[...]).astype(o_ref.dtype)


def _norm(x2, g, out_dtype, tm, name):
    t, d = x2.shape
    return pl.pallas_call(
        _norm_kernel,
        grid=(t // tm,),
        in_specs=[pl.BlockSpec((tm, d), lambda i: (i, 0)),
                  pl.BlockSpec((1, d), lambda i: (0, 0))],
        out_specs=pl.BlockSpec((tm, d), lambda i: (i, 0)),
        out_shape=jax.ShapeDtypeStruct((t, d), out_dtype),
        compiler_params=_params(1),
        name=name,
    )(x2, g)


def _in_proj_kernel(a_ref, wt_ref, o_ref, wbf_ref):
    @pl.when(pl.program_id(1) == 0)
    def _():
        _cast_rows(wt_ref, wbf_ref)

    o_ref[...] = lax.dot_general(a_ref[...], wbf_ref[...], (((1,), (1,)), ((), ())),
                                 preferred_element_type=F32)


def _in_proj(xn, wt, tm, tn):
    t, k = xn.shape
    u = 1024 // tn

    def wmap(j, m):
        return (jnp.where(j < 2 * u, j, jnp.where(j < 8 * u, j + u, j - 6 * u)), 0)

    return pl.pallas_call(
        _in_proj_kernel,
        grid=(MAIN_PROJ // tn, t // tm),
        in_specs=[pl.BlockSpec((tm, k), lambda j, m: (m, 0)),
                  pl.BlockSpec((tn, k), wmap)],
        out_specs=pl.BlockSpec((tm, tn), lambda j, m: (m, j)),
        out_shape=jax.ShapeDtypeStruct((t, MAIN_PROJ), F32),
        scratch_shapes=[pltpu.VMEM((tn, k), BF16)],
        compiler_params=_params(2),
        name="in_proj",
    )(xn, wt)


def _rope_tables(seq):
    half = ROT_DIM // 2
    inv = 1.0 / (ROPE_THETA ** (jnp.arange(0, ROT_DIM, 2, dtype=F32) / ROT_DIM))
    ang = jnp.arange(seq, dtype=F32)[:, None] * inv[None, :]
    cos, sin = jnp.cos(ang), jnp.sin(ang)
    d = np.arange(LANES) % HEAD_DIM
    idx = d % half
    in_rot = jnp.asarray(d < ROT_DIM)
    first = jnp.asarray(d < half)
    second = jnp.asarray((d >= half) & (d < ROT_DIM))
    c = jnp.where(in_rot[None, :], cos[:, idx], 1.0)
    s1 = jnp.where(first[None, :], -sin[:, idx], 0.0)
    s2 = jnp.where(second[None, :], sin[:, idx], 0.0)
    return c.astype(F32), s1.astype(F32), s2.astype(F32)


def _rope(x, c, s1, s2):
    half = ROT_DIM // 2
    return x * c + pltpu.roll(x, LANES - half, 1) * s1 + pltpu.roll(x, half, 1) * s2


def _attn_kernel(sinks_ref, q_ref, kp_ref, kc_ref, vp_ref, vc_ref,
                 cc_ref, s1c_ref, s2c_ref, cp_ref, s1p_ref, s2p_ref, g_ref, o_ref,
                 lhs_ref, kb_ref, vb_ref, acc_ref):
    blk = ATTN_BLOCK
    n = pl.program_id(1)
    lo = lax.broadcasted_iota(jnp.int32, (blk, LANES), 1) < HALF
    lo2 = lax.broadcasted_iota(jnp.int32, (2 * blk, LANES), 1) < HALF
    cc, s1c, s2c = cc_ref[...], s1c_ref[...], s2c_ref[...]
    cp, s1p, s2p = cp_ref[...], s1p_ref[...], s2p_ref[...]
    scale = HEAD_DIM ** -0.5
    cq, s1q, s2q = cc * scale, s1c * scale, s2c * scale

    for col in range(ATTN_WIDTH // LANES):
        h, jj = col // 2, col % 2
        q2 = _rope(q_ref[:, col * LANES:(col + 1) * LANES], cq, s1q, s2q)
        lhs_ref[h, (2 * jj) * blk:(2 * jj + 1) * blk, :] = jnp.where(lo, q2, 0.0).astype(BF16)
        lhs_ref[h, (2 * jj + 1) * blk:(2 * jj + 2) * blk, :] = jnp.where(lo, 0.0, q2).astype(BF16)
    ones = jnp.ones((2 * blk, LANES), BF16)
    for i in range(N_KV_HEADS // 2):
        cols = slice(i * LANES, (i + 1) * LANES)
        kcat = jnp.concatenate([_rope(kp_ref[:, cols], cp, s1p, s2p),
                                _rope(kc_ref[:, cols], cc, s1c, s2c)], axis=0)
        vcat = jnp.concatenate([vp_ref[:, cols], vc_ref[:, cols]], axis=0)
        kswp = pltpu.roll(kcat, HALF, 1)
        vswp = pltpu.roll(vcat, HALF, 1)
        kb_ref[2 * i] = jnp.where(lo2, kcat, kswp).astype(BF16)
        kb_ref[2 * i + 1] = jnp.where(lo2, kswp, kcat).astype(BF16)
        vb_ref[2 * i, :, 0:LANES] = jnp.where(lo2, vcat, vswp).astype(BF16)
        vb_ref[2 * i + 1, :, 0:LANES] = jnp.where(lo2, vswp, vcat).astype(BF16)
        vb_ref[2 * i, :, LANES:2 * LANES] = ones
        vb_ref[2 * i + 1, :, LANES:2 * LANES] = ones

    qi = lax.broadcasted_iota(jnp.int32, (blk, 2 * blk), 0)
    kj = lax.broadcasted_iota(jnp.int32, (blk, 2 * blk), 1)
    rel = qi + blk - kj
    kmin = jnp.where(n > 0, 0, blk)
    bias = jnp.where((rel >= 0) & (rel < WINDOW) & (kj >= kmin), 0.0, NEG)

    for h in range(N_KV_HEADS):
        s = lax.dot_general(lhs_ref[h], kb_ref[h], (((1,), (1,)), ((), ())),
                            preferred_element_type=F32)
        ps, sms = [], []
        for r in range(Q_PER_KV):
            sink = sinks_ref[Q_PER_KV * h + r]
            sr = s[r * blk:(r + 1) * blk] + bias
            m = jnp.maximum(jnp.max(sr, axis=-1, keepdims=True), sink)
            ps.append(jnp.exp(sr - m).astype(BF16))
            sms.append(jnp.exp(sink - m))
        o = jnp.dot(jnp.concatenate(ps, axis=0), vb_ref[h], preferred_element_type=F32)
        outs = []
        for r in range(Q_PER_KV):
            rows = slice(r * blk, (r + 1) * blk)
            den = o[rows, LANES:2 * LANES] + sms[r]
            outs.append(o[rows, 0:LANES] * (1.0 / den))
        for jj in range(2):
            col = 2 * h + jj
            acc_ref[:, col * LANES:(col + 1) * LANES] = jnp.where(lo, outs[2 * jj], outs[2 * jj + 1])

    o_ref[...] = _rmsnorm_rows(acc_ref[...], g_ref[...]).astype(o_ref.dtype)


def _attention(proj, sinks, g, tables, batch, seq):
    t = proj.shape[0]
    blk = ATTN_BLOCK
    nb = seq // blk
    kblk = K_OFF // KV_WIDTH
    vblk = V_OFF // KV_WIDTH
    c, s1, s2 = tables

    def cur(b, n):
        return (b * nb + n, 0)

    def tab_cur(b, n):
        return (n, 0)

    def tab_prev(b, n):
        return (jnp.maximum(n - 1, 0), 0)

    def prev_rows(b, n):
        return jnp.maximum(b * nb + n - 1, 0)

    tab = lambda f: pl.BlockSpec((blk, LANES), f)
    return pl.pallas_call(
        _attn_kernel,
        grid=(batch, nb),
        in_specs=[pl.BlockSpec(memory_space=pltpu.SMEM),
                  pl.BlockSpec((blk, ATTN_WIDTH), cur),
                  pl.BlockSpec((blk, KV_WIDTH), lambda b, n: (prev_rows(b, n), kblk)),
                  pl.BlockSpec((blk, KV_WIDTH), lambda b, n: (b * nb + n, kblk)),
                  pl.BlockSpec((blk, KV_WIDTH), lambda b, n: (prev_rows(b, n), vblk)),
                  pl.BlockSpec((blk, KV_WIDTH), lambda b, n: (b * nb + n, vblk)),
                  tab(tab_cur), tab(tab_cur), tab(tab_cur),
                  tab(tab_prev), tab(tab_prev), tab(tab_prev),
                  pl.BlockSpec((1, ATTN_WIDTH), lambda b, n: (0, 0))],
        out_specs=pl.BlockSpec((blk, ATTN_WIDTH), cur),
        out_shape=jax.ShapeDtypeStruct((t, ATTN_WIDTH), BF16),
        scratch_shapes=[pltpu.VMEM((N_KV_HEADS, Q_PER_KV * blk, LANES), BF16),
                        pltpu.VMEM((N_KV_HEADS, 2 * blk, LANES), BF16),
                        pltpu.VMEM((N_KV_HEADS, 2 * blk, 2 * LANES), BF16),
                        pltpu.VMEM((blk, ATTN_WIDTH), F32)],
        compiler_params=_params(2),
        name="swa_attention",
    )(sinks, proj, proj, proj, proj, proj, c, s1, s2, c, s1, s2, g)


def _split3(x):
    h = x.astype(BF16)
    r = x - h.astype(F32)
    m = r.astype(BF16)
    l = (r - m.astype(F32)).astype(BF16)
    return h, m, l


def _dot3_lhs(x, w):
    h, m, l = _split3(x)
    d = lambda a: jnp.dot(a, w, preferred_element_type=F32)
    return (d(l) + d(m)) + d(h)


def _dot3_rhs(w, x):
    h, m, l = _split3(x)
    d = lambda a: jnp.dot(w, a, preferred_element_type=F32)
    return (d(l) + d(m)) + d(h)


def _ssd_kernel(xbc_ref, z_ref, dt_ref, dtt_ref, cw_ref, cb_ref, brow_ref, bcol_ref,
                alrow_ref, alcol_ref, e_ref, dfull_ref, gn_ref, o_ref,
                ext_ref, xact_ref, state_ref, xdt_ref, xdec_ref, y_ref, exp_ref, acs_ref, acst_ref):
    L = SSD_CHUNK
    c = pl.program_id(1)

    @pl.when(c == 0)
    def _():
        ext_ref[0:SUBLANES, :] = jnp.zeros((SUBLANES, CONV_CH), F32)
        state_ref[...] = jnp.zeros_like(state_ref)

    ext_ref[SUBLANES:SUBLANES + L, :] = xbc_ref[...]
    cw_chunk = 512
    for j in range(CONV_CH // cw_chunk):
        cs = slice(j * cw_chunk, (j + 1) * cw_chunk)
        blk = ext_ref[:, cs]
        acc = cb_ref[:, cs]
        for k in range(SSD_CONV - 1):
            acc = acc + _delay_rows(blk, SSD_CONV - 1 - k) * cw_ref[k:k + 1, cs]
        acc = acc + blk[SUBLANES:] * cw_ref[SSD_CONV - 1:SSD_CONV, cs]
        xact_ref[:, cs] = acc * _sigmoid(acc)
    ext_ref[0:SUBLANES, :] = ext_ref[L:L + SUBLANES, :]

    dt = _softplus(dt_ref[...] + brow_ref[...])
    dtt = _softplus(dtt_ref[...] + bcol_ref[...])
    da = dt * (-jnp.exp(alrow_ref[...]))
    dat = dtt * (-jnp.exp(alcol_ref[...]))
    ri = lax.broadcasted_iota(jnp.int32, (L, L), 0)
    ci = lax.broadcasted_iota(jnp.int32, (L, L), 1)
    causal = ri >= ci
    tri_l = jnp.where(causal, 1.0, 0.0).astype(BF16)
    tri_u = jnp.where(ri <= ci, 1.0, 0.0).astype(BF16)
    a_cs = _dot3_rhs(tri_l, da)
    acs_ref[...] = a_cs
    acst_ref[...] = _dot3_lhs(dat, tri_u)
    a_last = a_cs[L - 1:L, :]
    stack = jnp.concatenate([dt, jnp.exp(a_last - a_cs), jnp.exp(a_cs),
                             jnp.broadcast_to(jnp.exp(a_last), (SUBLANES, LANES))], axis=0)
    sh, sm, sl = _split3(stack)
    for j in range(SSD_INNER // cw_chunk):
        cs = slice(j * cw_chunk, (j + 1) * cw_chunk)
        ej = e_ref[:, cs]
        d = lambda a: jnp.dot(a, ej, preferred_element_type=F32)
        exp_ref[:, cs] = (d(sl) + d(sm)) + d(sh)
    for j in range(SSD_INNER // cw_chunk):
        cs = slice(j * cw_chunk, (j + 1) * cw_chunk)
        xdt = xact_ref[:, cs] * exp_ref[0:L, cs]
        xdt_ref[:, cs] = xdt.astype(BF16)
        xdec_ref[:, cs] = (xdt * exp_ref[L:2 * L, cs]).astype(BF16)

    lo = ci < HALF
    gw = SSD_HEAD_DIM * (SSD_HEADS // SSD_GROUPS)
    for g in range(SSD_GROUPS):
        bg = xact_ref[:, SSD_INNER + g * SSD_STATE:SSD_INNER + (g + 1) * SSD_STATE]
        cg = xact_ref[:, SSD_INNER + BC_WIDTH + g * SSD_STATE:SSD_INNER + BC_WIDTH + (g + 1) * SSD_STATE]
        bb = bg.astype(BF16)
        cbf = cg.astype(BF16)
        cbm = lax.dot_general(cbf, bb, (((1,), (1,)), ((), ())), preferred_element_type=F32)
        gs = slice(g * gw, (g + 1) * gw)
        prev = state_ref[g]
        yoff = jnp.dot(cbf, prev.astype(BF16), preferred_element_type=F32) * exp_ref[2 * L:3 * L, gs]
        btb = bg.T.astype(BF16)
        state_ref[g] = prev * exp_ref[3 * L:3 * L + 1, gs] + jnp.dot(
            btb, xdec_ref[:, gs], preferred_element_type=F32)
        for e in range(2):
            pc = slice((2 * g + e) * LANES, (2 * g + e + 1) * LANES)
            xpair = xdt_ref[:, pc]
            yd = []
            for r in range(2):
                h = 4 * g + 2 * e + r
                seg = acs_ref[:, h:h + 1] - acst_ref[h:h + 1, :]
                lm = jnp.exp(jnp.where(causal, seg, NEG))
                yd.append(jnp.dot((cbm * lm).astype(BF16), xpair, preferred_element_type=F32))
            y_ref[:, pc] = (jnp.where(lo, yd[0], yd[1]) + yoff[:, e * LANES:(e + 1) * LANES]
                            + dfull_ref[:, pc] * xact_ref[:, pc])

    for g in range(SSD_GROUPS):
        gs = slice(g * gw, (g + 1) * gw)
        zz = z_ref[:, gs]
        yg = y_ref[:, gs] * (zz * _sigmoid(zz))
        ms = jnp.mean(yg * yg, axis=-1, keepdims=True)
        o_ref[:, gs] = (yg * lax.rsqrt(ms + EPS) * gn_ref[:, gs]).astype(o_ref.dtype)


def _ssd(proj, dt_raw, dt_rawt, cw, cb, brow, bcol, alrow, alcol, emat, dfull, gn, batch, seq):
    t = proj.shape[0]
    L = SSD_CHUNK
    nc = seq // L
    row = lambda b, c: b * nc + c
    full = lambda shape: pl.BlockSpec(shape, lambda b, c: (0, 0))
    return pl.pallas_call(
        _ssd_kernel,
        grid=(batch, nc),
        in_specs=[pl.BlockSpec((L, CONV_CH), lambda b, c: (row(b, c), XBC_OFF // CONV_CH)),
                  pl.BlockSpec((L, SSD_INNER), lambda b, c: (row(b, c), Z_OFF // SSD_INNER)),
                  pl.BlockSpec((L, LANES), lambda b, c: (row(b, c), 0)),
                  pl.BlockSpec((LANES, L), lambda b, c: (0, row(b, c))),
                  full((SSD_CONV, CONV_CH)), full((1, CONV_CH)),
                  full((1, LANES)), full((LANES, 1)), full((1, LANES)), full((LANES, 1)),
                  full((LANES, SSD_INNER)), full((1, SSD_INNER)), full((1, SSD_INNER))],
        out_specs=pl.BlockSpec((L, SSD_INNER), lambda b, c: (row(b, c), 0)),
        out_shape=jax.ShapeDtypeStruct((t, SSD_INNER), BF16),
        scratch_shapes=[pltpu.VMEM((L + SUBLANES, CONV_CH), F32),
                        pltpu.VMEM((L, CONV_CH), F32),
                        pltpu.VMEM((SSD_GROUPS, SSD_STATE, SSD_INNER // SSD_GROUPS), F32),
                        pltpu.VMEM((L, SSD_INNER), BF16),
                        pltpu.VMEM((L, SSD_INNER), BF16),
                        pltpu.VMEM((L, SSD_INNER), F32),
                        pltpu.VMEM((3 * L + SUBLANES, SSD_INNER), F32),
                        pltpu.VMEM((L, LANES), F32),
                        pltpu.VMEM((LANES, L), F32)],
        compiler_params=_params(2),
        name="ssd_scan",
    )(proj, proj, dt_raw, dt_rawt, cw, cb, brow, bcol, alrow, alcol, emat, dfull, gn)


def _out_proj_kernel(a1_ref, a2_ref, w1_ref, w2_ref, r_ref, o_ref, w1b_ref, w2b_ref):
    @pl.when(pl.program_id(1) == 0)
    def _():
        _cast_rows(w1_ref, w1b_ref)
        _cast_rows(w2_ref, w2b_ref)

    acc = jnp.dot(a1_ref[...], w1b_ref[...], preferred_element_type=F32)
    acc = acc + jnp.dot(a2_ref[...], w2b_ref[...], preferred_element_type=F32)
    o_ref[...] = r_ref[...] + acc


def _out_proj(a1, a2, w, res, tm, tn):
    t, k = a1.shape
    n = w.shape[1]
    return pl.pallas_call(
        _out_proj_kernel,
        grid=(n // tn, t // tm),
        in_specs=[pl.BlockSpec((tm, k), lambda j, m: (m, 0)),
                  pl.BlockSpec((tm, k), lambda j, m: (m, 0)),
                  pl.BlockSpec((k, tn), lambda j, m: (0, j)),
                  pl.BlockSpec((k, tn), lambda j, m: (1, j)),
                  pl.BlockSpec((tm, tn), lambda j, m: (m, j))],
        out_specs=pl.BlockSpec((tm, tn), lambda j, m: (m, j)),
        out_shape=jax.ShapeDtypeStruct((t, n), F32),
        scratch_shapes=[pltpu.VMEM((k, tn), BF16), pltpu.VMEM((k, tn), BF16)],
        compiler_params=_params(2),
        name="out_proj",
    )(a1, a2, w, w, res)


def _up_kernel(a_ref, wg_ref, wv_ref, cwg_ref, cwv_ref, cbg_ref, cbv_ref, o_ref,
               wb_ref, abuf_ref, stage_ref, carry_ref, *, tm, mc, seq):
    m = pl.program_id(1)
    tn = o_ref.shape[1]

    @pl.when(m == 0)
    def _():
        _cast_rows(wg_ref, wb_ref.at[:, 0:tn])
        _cast_rows(wv_ref, wb_ref.at[:, tn:2 * tn])

    abuf_ref[...] = a_ref[...]
    cws = (cwg_ref[...], cwv_ref[...])
    cbs = (cbg_ref[...], cbv_ref[...])
    above = jnp.where((m * tm) % seq == 0, 0.0, carry_ref[...])
    for c in range(tm // mc):
        u = jnp.dot(abuf_ref[c * mc:(c + 1) * mc, :], wb_ref[...], preferred_element_type=F32)
        halves = []
        for idx in range(2):
            uh = u[:, idx * tn:(idx + 1) * tn]
            blk = jnp.concatenate([above[:, idx * tn:(idx + 1) * tn], uh], axis=0)
            conv = cbs[idx]
            for k in range(FFN_CONV - 1):
                conv = conv + _delay_rows(blk, FFN_CONV - 1 - k) * cws[idx][k:k + 1, :]
            conv = conv + uh * cws[idx][FFN_CONV - 1:FFN_CONV, :]
            halves.append(conv)
        above = u[mc - SUBLANES:mc, :]
        gate, val = halves
        stage_ref[c * mc:(c + 1) * mc, :] = ((gate * _sigmoid(gate)) * val).astype(stage_ref.dtype)
    carry_ref[...] = above
    o_ref[...] = stage_ref[...]


def _up_proj(a, w, cw, cb, tm, tn, mc, seq):
    t, k = a.shape
    nb = D_FF // tn
    return pl.pallas_call(
        functools.partial(_up_kernel, tm=tm, mc=mc, seq=seq),
        grid=(nb, t // tm),
        in_specs=[pl.BlockSpec((tm, k), lambda j, m: (m, 0)),
                  pl.BlockSpec((k, tn), lambda j, m: (0, j)),
                  pl.BlockSpec((k, tn), lambda j, m: (0, j + nb)),
                  pl.BlockSpec((FFN_CONV, tn), lambda j, m: (0, j)),
                  pl.BlockSpec((FFN_CONV, tn), lambda j, m: (0, j + nb)),
                  pl.BlockSpec((1, tn), lambda j, m: (0, j)),
                  pl.BlockSpec((1, tn), lambda j, m: (0, j + nb))],
        out_specs=pl.BlockSpec((tm, tn), lambda j, m: (m, j)),
        out_shape=jax.ShapeDtypeStruct((t, D_FF), BF16),
        scratch_shapes=[pltpu.VMEM((k, 2 * tn), BF16),
                        pltpu.VMEM((tm, k), BF16),
                        pltpu.VMEM((tm, tn), BF16),
                        pltpu.VMEM((SUBLANES, 2 * tn), F32)],
        compiler_params=_params(2),
        name="up_proj_conv_swiglu",
    )(a, w, w, cw, cw, cb, cb)


def _down_kernel(a_ref, w_ref, r_ref, o_ref, wb_ref):
    @pl.when(pl.program_id(1) == 0)
    def _():
        _cast_rows(w_ref, wb_ref)

    o_ref[...] = r_ref[...] + jnp.dot(a_ref[...], wb_ref[...], preferred_element_type=F32)


def _down_proj(a, w, res, tm, tn):
    t, k = a.shape
    n = w.shape[1]
    return pl.pallas_call(
        _down_kernel,
        grid=(n // tn, t // tm),
        in_specs=[pl.BlockSpec((tm, k), lambda j, m: (m, 0)),
                  pl.BlockSpec((k, tn), lambda j, m: (0, j)),
                  pl.BlockSpec((tm, tn), lambda j, m: (m, j))],
        out_specs=pl.BlockSpec((tm, tn), lambda j, m: (m, j)),
        out_shape=jax.ShapeDtypeStruct((t, n), F32),
        scratch_shapes=[pltpu.VMEM((k, tn), BF16)],
        compiler_params=_params(2),
        name="down_proj",
    )(a, w, res)


def _pad_lanes(v):
    return jnp.pad(v.astype(F32), (0, LANES - v.shape[0]))


def _mixer_layer(h, batch, seq, norm_mix, w_in, sinks, attn_out_norm, ssd_conv_w, ssd_conv_b, dt_bias,
                 a_log, ssd_d, ssd_norm, w_out, tables, emat):
    w_in_t = jnp.swapaxes(w_in, 0, 1)
    wdt_t = jnp.pad(w_in_t[MAIN_PROJ:], ((0, LANES - SSD_HEADS), (0, 0))).astype(BF16)
    xn, dt_raw, dt_rawt = _norm_dt(h, norm_mix.reshape(1, -1), wdt_t.T, wdt_t, tm=512)
    proj = _in_proj(xn, w_in_t, tm=1024, tn=1024)
    attn = _attention(proj, sinks.astype(F32), attn_out_norm.reshape(1, -1), tables, batch, seq)
    bias = _pad_lanes(dt_bias)
    alog = _pad_lanes(a_log)
    y = _ssd(proj, dt_raw, dt_rawt, ssd_conv_w, ssd_conv_b.reshape(1, -1),
             bias.reshape(1, -1), bias.reshape(-1, 1), alog.reshape(1, -1), alog.reshape(-1, 1),
             emat, jnp.repeat(ssd_d.astype(F32), SSD_HEAD_DIM).reshape(1, -1),
             ssd_norm.reshape(1, -1), batch, seq)
    return _out_proj(attn, y, w_out, h, tm=1024, tn=512)


def kernel(x, norm_mix, w_in, sinks, attn_out_norm, ssd_conv_w, ssd_conv_b, dt_bias, a_log, ssd_d, ssd_norm,
           w_out, norm_ffn, w_up, ffn_conv_w, ffn_conv_b, w_down, norm_final):
    batch, seq, d = x.shape
    h = x.reshape(batch * seq, d)
    tables = _rope_tables(seq)
    head_of_channel = np.arange(SSD_INNER) // SSD_HEAD_DIM
    emat = jnp.asarray(np.arange(LANES)[:, None] == head_of_channel[None, :], dtype=BF16)
    for l in range(norm_mix.shape[0]):
        h = _mixer_layer(h, batch, seq, norm_mix[l], w_in[l], sinks[l], attn_out_norm[l], ssd_conv_w[l],
                         ssd_conv_b[l], dt_bias[l], a_log[l], ssd_d[l], ssd_norm[l], w_out[l], tables, emat)
        hn = _norm(h, norm_ffn[l].reshape(1, -1), BF16, 512, "norm_ffn")
        act = _up_proj(hn, w_up[l], ffn_conv_w[l], ffn_conv_b[l].reshape(1, -1),
                       tm=1024, tn=512, mc=128, seq=seq)
        h = _down_proj(act, w_down[l], h, tm=512, tn=512)
    out = _norm(h, norm_final.reshape(1, -1), F32, 512, "norm_final")
    return out.reshape(batch, seq, d)
```

```python
import functools

import numpy as np
import jax
import jax.numpy as jnp
from jax import lax
from jax.experimental import pallas as pl
from jax.experimental.pallas import tpu as pltpu

F32 = jnp.float32
BF16 = jnp.bfloat16

D_MODEL = 2048
N_Q_HEADS = 32
N_KV_HEADS = 8
HEAD_DIM = 64
Q_PER_KV = N_Q_HEADS // N_KV_HEADS
WINDOW = 128
ATTN_BLOCK = 128
ROT_DIM = HEAD_DIM // 4
ROPE_THETA = 500000.0
SSD_HEADS = 32
SSD_HEAD_DIM = 64
SSD_INNER = SSD_HEADS * SSD_HEAD_DIM
SSD_GROUPS = 8
SSD_STATE = 128
SSD_CONV = 4
SSD_CHUNK = 128
ATTN_WIDTH = N_Q_HEADS * HEAD_DIM
KV_WIDTH = N_KV_HEADS * HEAD_DIM
BC_WIDTH = SSD_GROUPS * SSD_STATE
CONV_CH = SSD_INNER + 2 * BC_WIDTH
MAIN_PROJ = ATTN_WIDTH + 2 * KV_WIDTH + SSD_INNER + CONV_CH
D_FF = 5632
FFN_CONV = 3
EPS = 1e-6

LANES = 128
SUBLANES = 8
HALF = LANES // 2
NEG = -1e30
VMEM_LIMIT = 56 * 1024 * 1024

Q_OFF = 0
Z_OFF = ATTN_WIDTH
XBC_OFF = Z_OFF + SSD_INNER
K_OFF = XBC_OFF + CONV_CH
V_OFF = K_OFF + KV_WIDTH


def _params(n_axes, flags=None):
    return pltpu.CompilerParams(dimension_semantics=("arbitrary",) * n_axes,
                                vmem_limit_bytes=VMEM_LIMIT, flags=flags)


def _sigmoid(x):
    return 1.0 / (1.0 + jnp.exp(-x))


def _softplus(x):
    return jnp.maximum(x, 0.0) + jnp.log1p(jnp.exp(-jnp.abs(x)))


def _cast_rows(src_ref, dst_ref, rows=256):
    k = src_ref.shape[0]
    for r in range(0, k, rows):
        dst_ref[r:r + rows, :] = src_ref[r:r + rows, :].astype(BF16)


def _delay_rows(blk, sh):
    n, w = blk.shape[0] - SUBLANES, blk.shape[1]
    g = blk.reshape(n // SUBLANES + 1, SUBLANES, w)
    rot = pltpu.roll(g, sh, 1)
    row = lax.broadcasted_iota(jnp.int32, (SUBLANES, w), 0)
    out = jnp.where(row < sh, rot[:-1], rot[1:])
    return out.reshape(n, w)


def _rmsnorm_rows(x, g):
    ms = jnp.mean(x * x, axis=-1, keepdims=True)
    return x * lax.rsqrt(ms + EPS) * g


def _norm_dt_kernel(x_ref, g_ref, wdt_ref, wdtt_ref, xn_ref, dt_ref, dtt_ref):
    xn = _rmsnorm_rows(x_ref[...], g_ref[...]).astype(BF16)
    xn_ref[...] = xn
    dt_ref[...] = jnp.dot(xn, wdt_ref[...], preferred_element_type=F32)
    dtt_ref[...] = lax.dot_general(wdtt_ref[...], xn, (((1,), (1,)), ((), ())),
                                   preferred_element_type=F32)


def _norm_dt(x2, g, wdt, wdtt, tm):
    t, d = x2.shape
    return pl.pallas_call(
        _norm_dt_kernel,
        grid=(t // tm,),
        in_specs=[pl.BlockSpec((tm, d), lambda i: (i, 0)),
                  pl.BlockSpec((1, d), lambda i: (0, 0)),
                  pl.BlockSpec((d, LANES), lambda i: (0, 0)),
                  pl.BlockSpec((LANES, d), lambda i: (0, 0))],
        out_specs=[pl.BlockSpec((tm, d), lambda i: (i, 0)),
                   pl.BlockSpec((tm, LANES), lambda i: (i, 0)),
                   pl.BlockSpec((LANES, tm), lambda i: (0, i))],
        out_shape=[jax.ShapeDtypeStruct((t, d), BF16),
                   jax.ShapeDtypeStruct((t, LANES), F32),
                   jax.ShapeDtypeStruct((LANES, t), F32)],
        compiler_params=_params(1),
        name="norm_dt",
    )(x2, g, wdt, wdtt)


def _norm_kernel(x_ref, g_ref, o_ref):
    o_ref[...] = _rmsnorm_rows(x_ref[...], g_ref[...]).astype(o_ref.dtype)


def _norm(x2, g, out_dtype, tm, name):
    t, d = x2.shape
    return pl.pallas_call(
        _norm_kernel,
        grid=(t // tm,),
        in_specs=[pl.BlockSpec((tm, d), lambda i: (i, 0)),
                  pl.BlockSpec((1, d), lambda i: (0, 0))],
        out_specs=pl.BlockSpec((tm, d), lambda i: (i, 0)),
        out_shape=jax.ShapeDtypeStruct((t, d), out_dtype),
        compiler_params=_params(1),
        name=name,
    )(x2, g)


def _in_proj_kernel(a_ref, wt_ref, o_ref, wbf_ref):
    @pl.when(pl.program_id(1) == 0)
    def _():
        _cast_rows(wt_ref, wbf_ref)

    o_ref[...] = lax.dot_general(a_ref[...], wbf_ref[...], (((1,), (1,)), ((), ())),
                                 preferred_element_type=F32)


def _in_proj(xn, wt, tm, tn):
    t, k = xn.shape
    u = 1024 // tn

    def wmap(j, m):
        return (jnp.where(j < 2 * u, j, jnp.where(j < 8 * u, j + u, j - 6 * u)), 0)

    return pl.pallas_call(
        _in_proj_kernel,
        grid=(MAIN_PROJ // tn, t // tm),
        in_specs=[pl.BlockSpec((tm, k), lambda j, m: (m, 0)),
                  pl.BlockSpec((tn, k), wmap)],
        out_specs=pl.BlockSpec((tm, tn), lambda j, m: (m, j)),
        out_shape=jax.ShapeDtypeStruct((t, MAIN_PROJ), F32),
        scratch_shapes=[pltpu.VMEM((tn, k), BF16)],
        compiler_params=_params(2),
        name="in_proj",
    )(xn, wt)


def _rope_tables(seq):
    half = ROT_DIM // 2
    inv = 1.0 / (ROPE_THETA ** (jnp.arange(0, ROT_DIM, 2, dtype=F32) / ROT_DIM))
    ang = jnp.arange(seq, dtype=F32)[:, None] * inv[None, :]
    cos, sin = jnp.cos(ang), jnp.sin(ang)
    d = np.arange(LANES) % HEAD_DIM
    idx = d % half
    in_rot = jnp.asarray(d < ROT_DIM)
    first = jnp.asarray(d < half)
    second = jnp.asarray((d >= half) & (d < ROT_DIM))
    c = jnp.where(in_rot[None, :], cos[:, idx], 1.0)
    s1 = jnp.where(first[None, :], -sin[:, idx], 0.0)
    s2 = jnp.where(second[None, :], sin[:, idx], 0.0)
    return c.astype(F32), s1.astype(F32), s2.astype(F32)


def _rope(x, c, s1, s2):
    half = ROT_DIM // 2
    return x * c + pltpu.roll(x, LANES - half, 1) * s1 + pltpu.roll(x, half, 1) * s2


def _attn_kernel(sinks_ref, q_ref, kp_ref, kc_ref, vp_ref, vc_ref,
                 cc_ref, s1c_ref, s2c_ref, cp_ref, s1p_ref, s2p_ref, g_ref, o_ref,
                 lhs_ref, kb_ref, vb_ref, acc_ref):
    blk = ATTN_BLOCK
    n = pl.program_id(1)
    lo = lax.broadcasted_iota(jnp.int32, (blk, LANES), 1) < HALF
    lo2 = lax.broadcasted_iota(jnp.int32, (2 * blk, LANES), 1) < HALF
    cc, s1c, s2c = cc_ref[...], s1c_ref[...], s2c_ref[...]
    cp, s1p, s2p = cp_ref[...], s1p_ref[...], s2p_ref[...]
    scale = HEAD_DIM ** -0.5
    cq, s1q, s2q = cc * scale, s1c * scale, s2c * scale

    for col in range(ATTN_WIDTH // LANES):
        h, jj = col // 2, col % 2
        q2 = _rope(q_ref[:, col * LANES:(col + 1) * LANES], cq, s1q, s2q)
        lhs_ref[h, (2 * jj) * blk:(2 * jj + 1) * blk, :] = jnp.where(lo, q2, 0.0).astype(BF16)
        lhs_ref[h, (2 * jj + 1) * blk:(2 * jj + 2) * blk, :] = jnp.where(lo, 0.0, q2).astype(BF16)
    ones = jnp.ones((2 * blk, LANES), BF16)
    for i in range(N_KV_HEADS // 2):
        cols = slice(i * LANES, (i + 1) * LANES)
        kcat = jnp.concatenate([_rope(kp_ref[:, cols], cp, s1p, s2p),
                                _rope(kc_ref[:, cols], cc, s1c, s2c)], axis=0)
        vcat = jnp.concatenate([vp_ref[:, cols], vc_ref[:, cols]], axis=0)
        kswp = pltpu.roll(kcat, HALF, 1)
        vswp = pltpu.roll(vcat, HALF, 1)
        kb_ref[2 * i] = jnp.where(lo2, kcat, kswp).astype(BF16)
        kb_ref[2 * i + 1] = jnp.where(lo2, kswp, kcat).astype(BF16)
        vb_ref[2 * i, :, 0:LANES] = jnp.where(lo2, vcat, vswp).astype(BF16)
        vb_ref[2 * i + 1, :, 0:LANES] = jnp.where(lo2, vswp, vcat).astype(BF16)
        vb_ref[2 * i, :, LANES:2 * LANES] = ones
        vb_ref[2 * i + 1, :, LANES:2 * LANES] = ones

    qi = lax.broadcasted_iota(jnp.int32, (blk, 2 * blk), 0)
    kj = lax.broadcasted_iota(jnp.int32, (blk, 2 * blk), 1)
    rel = qi + blk - kj
    kmin = jnp.where(n > 0, 0, blk)
    bias = jnp.where((rel >= 0) & (rel < WINDOW) & (kj >= kmin), 0.0, NEG)

    for h in range(N_KV_HEADS):
        s = lax.dot_general(lhs_ref[h], kb_ref[h], (((1,), (1,)), ((), ())),
                            preferred_element_type=F32)
        ps, sms = [], []
        for r in range(Q_PER_KV):
            sink = sinks_ref[Q_PER_KV * h + r]
            sr = s[r * blk:(r + 1) * blk] + bias
            m = jnp.maximum(jnp.max(sr, axis=-1, keepdims=True), sink)
            ps.append(jnp.exp(sr - m).astype(BF16))
            sms.append(jnp.exp(sink - m))
        o = jnp.dot(jnp.concatenate(ps, axis=0), vb_ref[h], preferred_element_type=F32)
        outs = []
        for r in range(Q_PER_KV):
            rows = slice(r * blk, (r + 1) * blk)
            den = o[rows, LANES:2 * LANES] + sms[r]
            outs.append(o[rows, 0:LANES] * (1.0 / den))
        for jj in range(2):
            col = 2 * h + jj
            acc_ref[:, col * LANES:(col + 1) * LANES] = jnp.where(lo, outs[2 * jj], outs[2 * jj + 1])

    o_ref[...] = _rmsnorm_rows(acc_ref[...], g_ref[...]).astype(o_ref.dtype)


def _attention(proj, sinks, g, tables, batch, seq):
    t = proj.shape[0]
    blk = ATTN_BLOCK
    nb = seq // blk
    kblk = K_OFF // KV_WIDTH
    vblk = V_OFF // KV_WIDTH
    c, s1, s2 = tables

    def cur(b, n):
        return (b * nb + n, 0)

    def tab_cur(b, n):
        return (n, 0)

    def tab_prev(b, n):
        return (jnp.maximum(n - 1, 0), 0)

    def prev_rows(b, n):
        return jnp.maximum(b * nb + n - 1, 0)

    tab = lambda f: pl.BlockSpec((blk, LANES), f)
    return pl.pallas_call(
        _attn_kernel,
        grid=(batch, nb),
        in_specs=[pl.BlockSpec(memory_space=pltpu.SMEM),
                  pl.BlockSpec((blk, ATTN_WIDTH), cur),
                  pl.BlockSpec((blk, KV_WIDTH), lambda b, n: (prev_rows(b, n), kblk)),
                  pl.BlockSpec((blk, KV_WIDTH), lambda b, n: (b * nb + n, kblk)),
                  pl.BlockSpec((blk, KV_WIDTH), lambda b, n: (prev_rows(b, n), vblk)),
                  pl.BlockSpec((blk, KV_WIDTH), lambda b, n: (b * nb + n, vblk)),
                  tab(tab_cur), tab(tab_cur), tab(tab_cur),
                  tab(tab_prev), tab(tab_prev), tab(tab_prev),
                  pl.BlockSpec((1, ATTN_WIDTH), lambda b, n: (0, 0))],
        out_specs=pl.BlockSpec((blk, ATTN_WIDTH), cur),
        out_shape=jax.ShapeDtypeStruct((t, ATTN_WIDTH), BF16),
        scratch_shapes=[pltpu.VMEM((N_KV_HEADS, Q_PER_KV * blk, LANES), BF16),
                        pltpu.VMEM((N_KV_HEADS, 2 * blk, LANES), BF16),
                        pltpu.VMEM((N_KV_HEADS, 2 * blk, 2 * LANES), BF16),
                        pltpu.VMEM((blk, ATTN_WIDTH), F32)],
        compiler_params=_params(2),
        name="swa_attention",
    )(sinks, proj, proj, proj, proj, proj, c, s1, s2, c, s1, s2, g)


def _split3(x):
    h = x.astype(BF16)
    r = x - h.astype(F32)
    m = r.astype(BF16)
    l = (r - m.astype(F32)).astype(BF16)
    return h, m, l


def _dot3_lhs(x, w):
    h, m, l = _split3(x)
    d = lambda a: jnp.dot(a, w, preferred_element_type=F32)
    return (d(l) + d(m)) + d(h)


def _dot3_rhs(w, x):
    h, m, l = _split3(x)
    d = lambda a: jnp.dot(w, a, preferred_element_type=F32)
    return (d(l) + d(m)) + d(h)


def _ssd_kernel(xbc_ref, z_ref, dt_ref, dtt_ref, cw_ref, cb_ref, brow_ref, bcol_ref,
                alrow_ref, alcol_ref, e_ref, dfull_ref, gn_ref, o_ref,
                ext_ref, xact_ref, state_ref, xdt_ref, xdec_ref, y_ref, exp_ref, acs_ref, acst_ref):
    L = SSD_CHUNK
    c = pl.program_id(1)

    @pl.when(c == 0)
    def _():
        ext_ref[0:SUBLANES, :] = jnp.zeros((SUBLANES, CONV_CH), F32)
        state_ref[...] = jnp.zeros_like(state_ref)

    ext_ref[SUBLANES:SUBLANES + L, :] = xbc_ref[...]
    cw_chunk = 512
    for j in range(CONV_CH // cw_chunk):
        cs = slice(j * cw_chunk, (j + 1) * cw_chunk)
        blk = ext_ref[:, cs]
        acc = cb_ref[:, cs]
        for k in range(SSD_CONV - 1):
            acc = acc + _delay_rows(blk, SSD_CONV - 1 - k) * cw_ref[k:k + 1, cs]
        acc = acc + blk[SUBLANES:] * cw_ref[SSD_CONV - 1:SSD_CONV, cs]
        xact_ref[:, cs] = acc * _sigmoid(acc)
    ext_ref[0:SUBLANES, :] = ext_ref[L:L + SUBLANES, :]

    dt = _softplus(dt_ref[...] + brow_ref[...])
    dtt = _softplus(dtt_ref[...] + bcol_ref[...])
    da = dt * (-jnp.exp(alrow_ref[...]))
    dat = dtt * (-jnp.exp(alcol_ref[...]))
    ri = lax.broadcasted_iota(jnp.int32, (L, L), 0)
    ci = lax.broadcasted_iota(jnp.int32, (L, L), 1)
    causal = ri >= ci
    tri_l = jnp.where(causal, 1.0, 0.0).astype(BF16)
    tri_u = jnp.where(ri <= ci, 1.0, 0.0).astype(BF16)
    a_cs = _dot3_rhs(tri_l, da)
    acs_ref[...] = a_cs
    acst_ref[...] = _dot3_lhs(dat, tri_u)
    a_last = a_cs[L - 1:L, :]
    stack = jnp.concatenate([dt, jnp.exp(a_last - a_cs), jnp.exp(a_cs),
                             jnp.broadcast_to(jnp.exp(a_last), (SUBLANES, LANES))], axis=0)
    sh, sm, sl = _split3(stack)
    for j in range(SSD_INNER // cw_chunk):
        cs = slice(j * cw_chunk, (j + 1) * cw_chunk)
        ej = e_ref[:, cs]
        d = lambda a: jnp.dot(a, ej, preferred_element_type=F32)
        exp_ref[:, cs] = (d(sl) + d(sm)) + d(sh)
    for j in range(SSD_INNER // cw_chunk):
        cs = slice(j * cw_chunk, (j + 1) * cw_chunk)
        xdt = xact_ref[:, cs] * exp_ref[0:L, cs]
        xdt_ref[:, cs] = xdt.astype(BF16)
        xdec_ref[:, cs] = (xdt * exp_ref[L:2 * L, cs]).astype(BF16)

    lo = ci < HALF
    gw = SSD_HEAD_DIM * (SSD_HEADS // SSD_GROUPS)
    for g in range(SSD_GROUPS):
        bg = xact_ref[:, SSD_INNER + g * SSD_STATE:SSD_INNER + (g + 1) * SSD_STATE]
        cg = xact_ref[:, SSD_INNER + BC_WIDTH + g * SSD_STATE:SSD_INNER + BC_WIDTH + (g + 1) * SSD_STATE]
        bb = bg.astype(BF16)
        cbf = cg.astype(BF16)
        cbm = lax.dot_general(cbf, bb, (((1,), (1,)), ((), ())), preferred_element_type=F32)
        gs = slice(g * gw, (g + 1) * gw)
        prev = state_ref[g]
        yoff = jnp.dot(cbf, prev.astype(BF16), preferred_element_type=F32) * exp_ref[2 * L:3 * L, gs]
        btb = bg.T.astype(BF16)
        state_ref[g] = prev * exp_ref[3 * L:3 * L + 1, gs] + jnp.dot(
            btb, xdec_ref[:, gs], preferred_element_type=F32)
        for e in range(2):
            pc = slice((2 * g + e) * LANES, (2 * g + e + 1) * LANES)
            xpair = xdt_ref[:, pc]
            yd = []
            for r in range(2):
                h = 4 * g + 2 * e + r
                seg = acs_ref[:, h:h + 1] - acst_ref[h:h + 1, :]
                lm = jnp.exp(jnp.where(causal, seg, NEG))
                yd.append(jnp.dot((cbm * lm).astype(BF16), xpair, preferred_element_type=F32))
            y_ref[:, pc] = (jnp.where(lo, yd[0], yd[1]) + yoff[:, e * LANES:(e + 1) * LANES]
                            + dfull_ref[:, pc] * xact_ref[:, pc])

    for g in range(SSD_GROUPS):
        gs = slice(g * gw, (g + 1) * gw)
        zz = z_ref[:, gs]
        yg = y_ref[:, gs] * (zz * _sigmoid(zz))
        ms = jnp.mean(yg * yg, axis=-1, keepdims=True)
        o_ref[:, gs] = (yg * lax.rsqrt(ms + EPS) * gn_ref[:, gs]).astype(o_ref.dtype)


def _ssd(proj, dt_raw, dt_rawt, cw, cb, brow, bcol, alrow, alcol, emat, dfull, gn, batch, seq):
    t = proj.shape[0]
    L = SSD_CHUNK
    nc = seq // L
    row = lambda b, c: b * nc + c
    full = lambda shape: pl.BlockSpec(shape, lambda b, c: (0, 0))
    return pl.pallas_call(
        _ssd_kernel,
        grid=(batch, nc),
        in_specs=[pl.BlockSpec((L, CONV_CH), lambda b, c: (row(b, c), XBC_OFF // CONV_CH)),
                  pl.BlockSpec((L, SSD_INNER), lambda b, c: (row(b, c), Z_OFF // SSD_INNER)),
                  pl.BlockSpec((L, LANES), lambda b, c: (row(b, c), 0)),
                  pl.BlockSpec((LANES, L), lambda b, c: (0, row(b, c))),
                  full((SSD_CONV, CONV_CH)), full((1, CONV_CH)),
                  full((1, LANES)), full((LANES, 1)), full((1, LANES)), full((LANES, 1)),
                  full((LANES, SSD_INNER)), full((1, SSD_INNER)), full((1, SSD_INNER))],
        out_specs=pl.BlockSpec((L, SSD_INNER), lambda b, c: (row(b, c), 0)),
        out_shape=jax.ShapeDtypeStruct((t, SSD_INNER), BF16),
        scratch_shapes=[pltpu.VMEM((L + SUBLANES, CONV_CH), F32),
                        pltpu.VMEM((L, CONV_CH), F32),
                        pltpu.VMEM((SSD_GROUPS, SSD_STATE, SSD_INNER // SSD_GROUPS), F32),
                        pltpu.VMEM((L, SSD_INNER), BF16),
                        pltpu.VMEM((L, SSD_INNER), BF16),
                        pltpu.VMEM((L, SSD_INNER), F32),
                        pltpu.VMEM((3 * L + SUBLANES, SSD_INNER), F32),
                        pltpu.VMEM((L, LANES), F32),
                        pltpu.VMEM((LANES, L), F32)],
        compiler_params=_params(2),
        name="ssd_scan",
    )(proj, proj, dt_raw, dt_rawt, cw, cb, brow, bcol, alrow, alcol, emat, dfull, gn)


def _out_proj_kernel(a1_ref, a2_ref, w1_ref, w2_ref, r_ref, o_ref, w1b_ref, w2b_ref):
    @pl.when(pl.program_id(1) == 0)
    def _():
        _cast_rows(w1_ref, w1b_ref)
        _cast_rows(w2_ref, w2b_ref)

    acc = jnp.dot(a1_ref[...], w1b_ref[...], preferred_element_type=F32)
    acc = acc + jnp.dot(a2_ref[...], w2b_ref[...], preferred_element_type=F32)
    o_ref[...] = r_ref[...] + acc


def _out_proj(a1, a2, w, res, tm, tn):
    t, k = a1.shape
    n = w.shape[1]
    return pl.pallas_call(
        _out_proj_kernel,
        grid=(n // tn, t // tm),
        in_specs=[pl.BlockSpec((tm, k), lambda j, m: (m, 0)),
                  pl.BlockSpec((tm, k), lambda j, m: (m, 0)),
                  pl.BlockSpec((k, tn), lambda j, m: (0, j)),
                  pl.BlockSpec((k, tn), lambda j, m: (1, j)),
                  pl.BlockSpec((tm, tn), lambda j, m: (m, j))],
        out_specs=pl.BlockSpec((tm, tn), lambda j, m: (m, j)),
        out_shape=jax.ShapeDtypeStruct((t, n), F32),
        scratch_shapes=[pltpu.VMEM((k, tn), BF16), pltpu.VMEM((k, tn), BF16)],
        compiler_params=_params(2),
        name="out_proj",
    )(a1, a2, w, w, res)


def _up_epilogue(u_ref, cwg, cwv, cbg, cbv, o_ref, r0, r1, tn, rc):
    for r in range(r0, r1, rc):
        halves = []
        for idx, (cw, cb) in enumerate(((cwg, cbg), (cwv, cbv))):
            blk = u_ref[r:r + rc + SUBLANES, idx * tn:(idx + 1) * tn]
            conv = cb
            for k in range(FFN_CONV - 1):
                conv = conv + _delay_rows(blk, FFN_CONV - 1 - k) * cw[k:k + 1, :]
            conv = conv + blk[SUBLANES:] * cw[FFN_CONV - 1:FFN_CONV, :]
            halves.append(conv)
        gate, val = halves
        o_ref[r:r + rc, :] = ((gate * _sigmoid(gate)) * val).astype(o_ref.dtype)


def _up_kernel(a_ref, wg_ref, wv_ref, cwg_ref, cwv_ref, cbg_ref, cbv_ref, o_ref,
               wb_ref, ua_ref, ub_ref, stage_ref, abuf_ref, *, tm, mc, seq, n_tiles, rc):
    m = pl.program_id(1)
    tn = o_ref.shape[1]
    slots = (ua_ref, ub_ref)

    @pl.when(m == 0)
    def _():
        _cast_rows(wg_ref, wb_ref.at[:, 0:tn])
        _cast_rows(wv_ref, wb_ref.at[:, tn:2 * tn])

    abuf_ref[...] = a_ref[...]
    cwg, cwv, cbg, cbv = cwg_ref[...], cwv_ref[...], cbg_ref[...], cbv_ref[...]

    def matmul(u_new, c):
        u_new[SUBLANES + c * mc:SUBLANES + (c + 1) * mc, :] = jnp.dot(
            abuf_ref[c * mc:(c + 1) * mc, :], wb_ref[...], preferred_element_type=F32)

    def epilogue(u_old, c):
        _up_epilogue(u_old, cwg, cwv, cbg, cbv, stage_ref, c * mc, (c + 1) * mc, tn, rc)

    @pl.when(m == 0)
    def _():
        ua_ref[0:SUBLANES, :] = jnp.zeros((SUBLANES, 2 * tn), F32)
        for c in range(tm // mc):
            matmul(ua_ref, c)

    for par in range(2):
        @pl.when((m > 0) & (m < n_tiles) & (m % 2 == par))
        def _():
            u_new, u_old = slots[par], slots[1 - par]
            u_new[0:SUBLANES, :] = jnp.where((m * tm) % seq == 0, 0.0, u_old[tm:tm + SUBLANES, :])
            for c in range(tm // mc):
                epilogue(u_old, c)
                matmul(u_new, c)
            o_ref[...] = stage_ref[...]

    @pl.when(m == n_tiles)
    def _():
        for c in range(tm // mc):
            epilogue(slots[(n_tiles - 1) % 2], c)
        o_ref[...] = stage_ref[...]


def _up_proj(a, w, cw, cb, tm, tn, mc, rc, seq):
    t, k = a.shape
    nb = D_FF // tn
    n_tiles = t // tm
    return pl.pallas_call(
        functools.partial(_up_kernel, tm=tm, mc=mc, seq=seq, n_tiles=n_tiles, rc=rc),
        grid=(nb, n_tiles + 1),
        in_specs=[pl.BlockSpec((tm, k), lambda j, m: (jnp.minimum(m, n_tiles - 1), 0)),
                  pl.BlockSpec((k, tn), lambda j, m: (0, j)),
                  pl.BlockSpec((k, tn), lambda j, m: (0, j + nb)),
                  pl.BlockSpec((FFN_CONV, tn), lambda j, m: (0, j)),
                  pl.BlockSpec((FFN_CONV, tn), lambda j, m: (0, j + nb)),
                  pl.BlockSpec((1, tn), lambda j, m: (0, j)),
                  pl.BlockSpec((1, tn), lambda j, m: (0, j + nb))],
        out_specs=pl.BlockSpec((tm, tn), lambda j, m: (jnp.maximum(m - 1, 0), j)),
        out_shape=jax.ShapeDtypeStruct((t, D_FF), BF16),
        scratch_shapes=[pltpu.VMEM((k, 2 * tn), BF16),
                        pltpu.VMEM((tm + SUBLANES, 2 * tn), F32),
                        pltpu.VMEM((tm + SUBLANES, 2 * tn), F32),
                        pltpu.VMEM((tm, tn), BF16),
                        pltpu.VMEM((tm, k), BF16)],
        compiler_params=_params(2),
        name="up_proj_conv_swiglu",
    )(a, w, w, cw, cw, cb, cb)


def _down_kernel(a_ref, w_ref, r_ref, o_ref, wb_ref):
    @pl.when(pl.program_id(1) == 0)
    def _():
        _cast_rows(w_ref, wb_ref)

    o_ref[...] = r_ref[...] + jnp.dot(a_ref[...], wb_ref[...], preferred_element_type=F32)


def _down_proj(a, w, res, tm, tn):
    t, k = a.shape
    n = w.shape[1]
    return pl.pallas_call(
        _down_kernel,
        grid=(n // tn, t // tm),
        in_specs=[pl.BlockSpec((tm, k), lambda j, m: (m, 0)),
                  pl.BlockSpec((k, tn), lambda j, m: (0, j)),
                  pl.BlockSpec((tm, tn), lambda j, m: (m, j))],
        out_specs=pl.BlockSpec((tm, tn), lambda j, m: (m, j)),
        out_shape=jax.ShapeDtypeStruct((t, n), F32),
        scratch_shapes=[pltpu.VMEM((k, tn), BF16)],
        compiler_params=_params(2),
        name="down_proj",
    )(a, w, res)


def _pad_lanes(v):
    return jnp.pad(v.astype(F32), (0, LANES - v.shape[0]))


def _mixer_layer(h, batch, seq, norm_mix, w_in, sinks, attn_out_norm, ssd_conv_w, ssd_conv_b, dt_bias,
                 a_log, ssd_d, ssd_norm, w_out, tables, emat):
    w_in_t = jnp.swapaxes(w_in, 0, 1)
    wdt_t = jnp.pad(w_in_t[MAIN_PROJ:], ((0, LANES - SSD_HEADS), (0, 0))).astype(BF16)
    xn, dt_raw, dt_rawt = _norm_dt(h, norm_mix.reshape(1, -1), wdt_t.T, wdt_t, tm=512)
    proj = _in_proj(xn, w_in_t, tm=1024, tn=1024)
    attn = _attention(proj, sinks.astype(F32), attn_out_norm.reshape(1, -1), tables, batch, seq)
    bias = _pad_lanes(dt_bias)
    alog = _pad_lanes(a_log)
    y = _ssd(proj, dt_raw, dt_rawt, ssd_conv_w, ssd_conv_b.reshape(1, -1),
             bias.reshape(1, -1), bias.reshape(-1, 1), alog.reshape(1, -1), alog.reshape(-1, 1),
             emat, jnp.repeat(ssd_d.astype(F32), SSD_HEAD_DIM).reshape(1, -1),
             ssd_norm.reshape(1, -1), batch, seq)
    return _out_proj(attn, y, w_out, h, tm=1024, tn=512)


def kernel(x, norm_mix, w_in, sinks, attn_out_norm, ssd_conv_w, ssd_conv_b, dt_bias, a_log, ssd_d, ssd_norm,
           w_out, norm_ffn, w_up, ffn_conv_w, ffn_conv_b, w_down, norm_final):
    batch, seq, d = x.shape
    h = x.reshape(batch * seq, d)
    tables = _rope_tables(seq)
    head_of_channel = np.arange(SSD_INNER) // SSD_HEAD_DIM
    emat = jnp.asarray(np.arange(LANES)[:, None] == head_of_channel[None, :], dtype=BF16)
    for l in range(norm_mix.shape[0]):
        h = _mixer_layer(h, batch, seq, norm_mix[l], w_in[l], sinks[l], attn_out_norm[l], ssd_conv_w[l],
                         ssd_conv_b[l], dt_bias[l], a_log[l], ssd_d[l], ssd_norm[l], w_out[l], tables, emat)
        hn = _norm(h, norm_ffn[l].reshape(1, -1), BF16, 512, "norm_ffn")
        act = _up_proj(hn, w_up[l], ffn_conv_w[l], ffn_conv_b[l].reshape(1, -1),
                       tm=1024, tn=512, mc=1024, rc=64, seq=seq)
        h = _down_proj(act, w_down[l], h, tm=512, tn=512)
    out = _norm(h, norm_final.reshape(1, -1), F32, 512, "norm_final")
    return out.reshape(batch, seq, d)
```

```python
import functools

import numpy as np
import jax
import jax.numpy as jnp
from jax import lax
from jax.experimental import pallas as pl
from jax.experimental.pallas import tpu as pltpu

F32 = jnp.float32
BF16 = jnp.bfloat16

D_MODEL = 2048
N_Q_HEADS = 32
N_KV_HEADS = 8
HEAD_DIM = 64
Q_PER_KV = N_Q_HEADS // N_KV_HEADS
WINDOW = 128
ATTN_BLOCK = 128
ROT_DIM = HEAD_DIM // 4
ROPE_THETA = 500000.0
SSD_HEADS = 32
SSD_HEAD_DIM = 64
SSD_INNER = SSD_HEADS * SSD_HEAD_DIM
SSD_GROUPS = 8
SSD_STATE = 128
SSD_CONV = 4
SSD_CHUNK = 128
ATTN_WIDTH = N_Q_HEADS * HEAD_DIM
KV_WIDTH = N_KV_HEADS * HEAD_DIM
BC_WIDTH = SSD_GROUPS * SSD_STATE
CONV_CH = SSD_INNER + 2 * BC_WIDTH
MAIN_PROJ = ATTN_WIDTH + 2 * KV_WIDTH + SSD_INNER + CONV_CH
D_FF = 5632
FFN_CONV = 3
EPS = 1e-6

LANES = 128
SUBLANES = 8
HALF = LANES // 2
NEG = -1e30
LOG2E = 1.4426950408889634
VMEM_LIMIT = 56 * 1024 * 1024

Q_OFF = 0
Z_OFF = ATTN_WIDTH
XBC_OFF = Z_OFF + SSD_INNER
K_OFF = XBC_OFF + CONV_CH
V_OFF = K_OFF + KV_WIDTH


def _params(n_axes, flags=None):
    return pltpu.CompilerParams(dimension_semantics=("arbitrary",) * n_axes,
                                vmem_limit_bytes=VMEM_LIMIT, flags=flags)


def _sigmoid(x):
    return 1.0 / (1.0 + jnp.exp(-x))


def _softplus(x):
    return jnp.maximum(x, 0.0) + jnp.log1p(jnp.exp(-jnp.abs(x)))


def _cast_rows(src_ref, dst_ref, rows=256):
    k = src_ref.shape[0]
    for r in range(0, k, rows):
        dst_ref[r:r + rows, :] = src_ref[r:r + rows, :].astype(BF16)


def _delay_rows(blk, sh):
    n, w = blk.shape[0] - SUBLANES, blk.shape[1]
    g = blk.reshape(n // SUBLANES + 1, SUBLANES, w)
    rot = pltpu.roll(g, sh, 1)
    row = lax.broadcasted_iota(jnp.int32, (SUBLANES, w), 0)
    out = jnp.where(row < sh, rot[:-1], rot[1:])
    return out.reshape(n, w)


def _rmsnorm_rows(x, g):
    ms = jnp.mean(x * x, axis=-1, keepdims=True)
    return x * lax.rsqrt(ms + EPS) * g


def _norm_dt_kernel(x_ref, g_ref, wdt_ref, wdtt_ref, xn_ref, dt_ref, dtt_ref):
    xn = _rmsnorm_rows(x_ref[...], g_ref[...]).astype(BF16)
    xn_ref[...] = xn
    dt_ref[...] = jnp.dot(xn, wdt_ref[...], preferred_element_type=F32)
    dtt_ref[...] = lax.dot_general(wdtt_ref[...], xn, (((1,), (1,)), ((), ())),
                                   preferred_element_type=F32)


def _norm_dt(x2, g, wdt, wdtt, tm):
    t, d = x2.shape
    return pl.pallas_call(
        _norm_dt_kernel,
        grid=(t // tm,),
        in_specs=[pl.BlockSpec((tm, d), lambda i: (i, 0)),
                  pl.BlockSpec((1, d), lambda i: (0, 0)),
                  pl.BlockSpec((d, LANES), lambda i: (0, 0)),
                  pl.BlockSpec((LANES, d), lambda i: (0, 0))],
        out_specs=[pl.BlockSpec((tm, d), lambda i: (i, 0)),
                   pl.BlockSpec((tm, LANES), lambda i: (i, 0)),
                   pl.BlockSpec((LANES, tm), lambda i: (0, i))],
        out_shape=[jax.ShapeDtypeStruct((t, d), BF16),
                   jax.ShapeDtypeStruct((t, LANES), F32),
                   jax.ShapeDtypeStruct((LANES, t), F32)],
        compiler_params=_params(1),
        name="norm_dt",
    )(x2, g, wdt, wdtt)


def _norm_kernel(x_ref, g_ref, o_ref):
    o_ref[...] = _rmsnorm_rows(x_ref[...], g_ref[...]).astype(o_ref.dtype)


def _norm(x2, g, out_dtype, tm, name):
    t, d = x2.shape
    return pl.pallas_call(
        _norm_kernel,
        grid=(t // tm,),
        in_specs=[pl.BlockSpec((tm, d), lambda i: (i, 0)),
                  pl.BlockSpec((1, d), lambda i: (0, 0))],
        out_specs=pl.BlockSpec((tm, d), lambda i: (i, 0)),
        out_shape=jax.ShapeDtypeStruct((t, d), out_dtype),
        compiler_params=_params(1),
        name=name,
    )(x2, g)


def _in_proj_kernel(a_ref, wt_ref, o_ref, wbf_ref):
    @pl.when(pl.program_id(1) == 0)
    def _():
        _cast_rows(wt_ref, wbf_ref)

    o_ref[...] = lax.dot_general(a_ref[...], wbf_ref[...], (((1,), (1,)), ((), ())),
                                 preferred_element_type=F32)


def _in_proj(xn, wt, tm, tn):
    t, k = xn.shape
    u = 1024 // tn

    def wmap(j, m):
        return (jnp.where(j < 2 * u, j, jnp.where(j < 8 * u, j + u, j - 6 * u)), 0)

    return pl.pallas_call(
        _in_proj_kernel,
        grid=(MAIN_PROJ // tn, t // tm),
        in_specs=[pl.BlockSpec((tm, k), lambda j, m: (m, 0)),
                  pl.BlockSpec((tn, k), wmap)],
        out_specs=pl.BlockSpec((tm, tn), lambda j, m: (m, j)),
        out_shape=jax.ShapeDtypeStruct((t, MAIN_PROJ), F32),
        scratch_shapes=[pltpu.VMEM((tn, k), BF16)],
        compiler_params=_params(2),
        name="in_proj",
    )(xn, wt)


def _rope_tables(seq):
    half = ROT_DIM // 2
    inv = 1.0 / (ROPE_THETA ** (jnp.arange(0, ROT_DIM, 2, dtype=F32) / ROT_DIM))
    ang = jnp.arange(seq, dtype=F32)[:, None] * inv[None, :]
    cos, sin = jnp.cos(ang), jnp.sin(ang)
    d = np.arange(LANES) % HEAD_DIM
    idx = d % half
    in_rot = jnp.asarray(d < ROT_DIM)
    first = jnp.asarray(d < half)
    second = jnp.asarray((d >= half) & (d < ROT_DIM))
    c = jnp.where(in_rot[None, :], cos[:, idx], 1.0)
    s1 = jnp.where(first[None, :], -sin[:, idx], 0.0)
    s2 = jnp.where(second[None, :], sin[:, idx], 0.0)
    return c.astype(F32), s1.astype(F32), s2.astype(F32)


def _rope(x, c, s1, s2):
    half = ROT_DIM // 2
    return x * c + pltpu.roll(x, LANES - half, 1) * s1 + pltpu.roll(x, half, 1) * s2


def _attn_kernel(sinks_ref, q_ref, kp_ref, kc_ref, vp_ref, vc_ref,
                 cc_ref, s1c_ref, s2c_ref, cp_ref, s1p_ref, s2p_ref, g_ref, o_ref,
                 lhs_ref, kb_ref, vb_ref, s_ref, p_ref, o2_ref, m_ref, tq_ref, bias_ref, acc_ref):
    blk = ATTN_BLOCK
    n = pl.program_id(1)
    lo = lax.broadcasted_iota(jnp.int32, (blk, LANES), 1) < HALF
    lo2 = lax.broadcasted_iota(jnp.int32, (2 * blk, LANES), 1) < HALF

    scale = HEAD_DIM ** -0.5 * LOG2E
    for i, t_ref in enumerate((cc_ref, s1c_ref, s2c_ref)):
        tq_ref[i] = t_ref[...] * scale
    qi = lax.broadcasted_iota(jnp.int32, (blk, 2 * blk), 0)
    kj = lax.broadcasted_iota(jnp.int32, (blk, 2 * blk), 1)
    rel = qi + blk - kj
    kmin = jnp.where(n > 0, 0, blk)
    bias_ref[...] = jnp.where((rel >= 0) & (rel < WINDOW) & (kj >= kmin), 0.0, NEG)

    for col in range(ATTN_WIDTH // LANES):
        h, jj = col // 2, col % 2
        q2 = _rope(q_ref[:, col * LANES:(col + 1) * LANES], tq_ref[0], tq_ref[1], tq_ref[2])
        lhs_ref[h, (2 * jj) * blk:(2 * jj + 1) * blk, :] = jnp.where(lo, q2, 0.0).astype(BF16)
        lhs_ref[h, (2 * jj + 1) * blk:(2 * jj + 2) * blk, :] = jnp.where(lo, 0.0, q2).astype(BF16)
    ones = jnp.ones((2 * blk, LANES), BF16)
    for i in range(N_KV_HEADS // 2):
        cols = slice(i * LANES, (i + 1) * LANES)
        kcat = jnp.concatenate([_rope(kp_ref[:, cols], cp_ref[...], s1p_ref[...], s2p_ref[...]),
                                _rope(kc_ref[:, cols], cc_ref[...], s1c_ref[...], s2c_ref[...])], axis=0)
        vcat = jnp.concatenate([vp_ref[:, cols], vc_ref[:, cols]], axis=0)
        kswp = pltpu.roll(kcat, HALF, 1)
        vswp = pltpu.roll(vcat, HALF, 1)
        kb_ref[2 * i] = jnp.where(lo2, kcat, kswp).astype(BF16)
        kb_ref[2 * i + 1] = jnp.where(lo2, kswp, kcat).astype(BF16)
        vb_ref[2 * i, :, 0:LANES] = jnp.where(lo2, vcat, vswp).astype(BF16)
        vb_ref[2 * i + 1, :, 0:LANES] = jnp.where(lo2, vswp, vcat).astype(BF16)
        vb_ref[2 * i, :, LANES:2 * LANES] = ones
        vb_ref[2 * i + 1, :, LANES:2 * LANES] = ones

    def scores(h):
        s_ref[h % 2] = lax.dot_general(lhs_ref[h], kb_ref[h], (((1,), (1,)), ((), ())),
                                       preferred_element_type=F32)

    def row_max(h):
        slot = h % 2
        for r in range(Q_PER_KV):
            rows = slice(r * blk, (r + 1) * blk)
            sink = sinks_ref[Q_PER_KV * h + r] * LOG2E
            sr = s_ref[slot, rows, :] + bias_ref[...]
            s_ref[slot, rows, :] = sr
            m = jnp.maximum(jnp.max(sr, axis=-1, keepdims=True), sink)
            m_ref[slot, rows, :] = jnp.broadcast_to(m, (blk, LANES))

    def probs(h):
        slot = h % 2
        for r in range(Q_PER_KV):
            rows = slice(r * blk, (r + 1) * blk)
            m = m_ref[slot, rows, :]
            for half in range(2):
                ln = slice(half * LANES, (half + 1) * LANES)
                p_ref[slot, rows, ln] = jnp.exp2(s_ref[slot, rows, ln] - m).astype(BF16)

    def weighted(h):
        o2_ref[h % 2] = jnp.dot(p_ref[h % 2], vb_ref[h], preferred_element_type=F32)

    def finish(h):
        slot = h % 2
        outs = []
        for r in range(Q_PER_KV):
            rows = slice(r * blk, (r + 1) * blk)
            sink = sinks_ref[Q_PER_KV * h + r] * LOG2E
            den = o2_ref[slot, rows, LANES:2 * LANES] + jnp.exp2(sink - m_ref[slot, rows, :])
            outs.append(o2_ref[slot, rows, 0:LANES] * (1.0 / den))
        for jj in range(2):
            col = 2 * h + jj
            acc_ref[:, col * LANES:(col + 1) * LANES] = jnp.where(lo, outs[2 * jj], outs[2 * jj + 1])

    scores(0)
    for h in range(N_KV_HEADS + 1):
        if h + 1 < N_KV_HEADS:
            scores(h + 1)
        if h < N_KV_HEADS:
            row_max(h)
            probs(h)
            weighted(h)
        if h >= 1:
            finish(h - 1)

    o_ref[...] = _rmsnorm_rows(acc_ref[...], g_ref[...]).astype(o_ref.dtype)


def _attention(proj, sinks, g, tables, batch, seq):
    t = proj.shape[0]
    blk = ATTN_BLOCK
    nb = seq // blk
    kblk = K_OFF // KV_WIDTH
    vblk = V_OFF // KV_WIDTH
    c, s1, s2 = tables

    def cur(b, n):
        return (b * nb + n, 0)

    def tab_cur(b, n):
        return (n, 0)

    def tab_prev(b, n):
        return (jnp.maximum(n - 1, 0), 0)

    def prev_rows(b, n):
        return jnp.maximum(b * nb + n - 1, 0)

    tab = lambda f: pl.BlockSpec((blk, LANES), f)
    return pl.pallas_call(
        _attn_kernel,
        grid=(batch, nb),
        in_specs=[pl.BlockSpec(memory_space=pltpu.SMEM),
                  pl.BlockSpec((blk, ATTN_WIDTH), cur),
                  pl.BlockSpec((blk, KV_WIDTH), lambda b, n: (prev_rows(b, n), kblk)),
                  pl.BlockSpec((blk, KV_WIDTH), lambda b, n: (b * nb + n, kblk)),
                  pl.BlockSpec((blk, KV_WIDTH), lambda b, n: (prev_rows(b, n), vblk)),
                  pl.BlockSpec((blk, KV_WIDTH), lambda b, n: (b * nb + n, vblk)),
                  tab(tab_cur), tab(tab_cur), tab(tab_cur),
                  tab(tab_prev), tab(tab_prev), tab(tab_prev),
                  pl.BlockSpec((1, ATTN_WIDTH), lambda b, n: (0, 0))],
        out_specs=pl.BlockSpec((blk, ATTN_WIDTH), cur),
        out_shape=jax.ShapeDtypeStruct((t, ATTN_WIDTH), BF16),
        scratch_shapes=[pltpu.VMEM((N_KV_HEADS, Q_PER_KV * blk, LANES), BF16),
                        pltpu.VMEM((N_KV_HEADS, 2 * blk, LANES), BF16),
                        pltpu.VMEM((N_KV_HEADS, 2 * blk, 2 * LANES), BF16),
                        pltpu.VMEM((2, Q_PER_KV * blk, 2 * blk), F32),
                        pltpu.VMEM((2, Q_PER_KV * blk, 2 * blk), BF16),
                        pltpu.VMEM((2, Q_PER_KV * blk, 2 * LANES), F32),
                        pltpu.VMEM((2, Q_PER_KV * blk, LANES), F32),
                        pltpu.VMEM((3, blk, LANES), F32),
                        pltpu.VMEM((blk, 2 * blk), F32),
                        pltpu.VMEM((blk, ATTN_WIDTH), F32)],
        compiler_params=_params(2),
        name="swa_attention",
    )(sinks, proj, proj, proj, proj, proj, c, s1, s2, c, s1, s2, g)


def _split3(x):
    h = x.astype(BF16)
    r = x - h.astype(F32)
    m = r.astype(BF16)
    l = (r - m.astype(F32)).astype(BF16)
    return h, m, l


def _dot3_lhs(x, w):
    h, m, l = _split3(x)
    d = lambda a: jnp.dot(a, w, preferred_element_type=F32)
    return (d(l) + d(m)) + d(h)


def _dot3_rhs(w, x):
    h, m, l = _split3(x)
    d = lambda a: jnp.dot(w, a, preferred_element_type=F32)
    return (d(l) + d(m)) + d(h)


def _ssd_kernel(xbc_ref, z_ref, dt_ref, dtt_ref, cw_ref, cb_ref, brow_ref, bcol_ref,
                alrow_ref, alcol_ref, e_ref, dfull_ref, gn_ref, o_ref,
                ext_ref, xact_ref, state_ref, xdt_ref, xdec_ref, y_ref, exp_ref, acs_ref, acst_ref):
    L = SSD_CHUNK
    c = pl.program_id(1)

    @pl.when(c == 0)
    def _():
        ext_ref[0:SUBLANES, :] = jnp.zeros((SUBLANES, CONV_CH), F32)
        state_ref[...] = jnp.zeros_like(state_ref)

    ext_ref[SUBLANES:SUBLANES + L, :] = xbc_ref[...]
    cw_chunk = 512
    for j in range(CONV_CH // cw_chunk):
        cs = slice(j * cw_chunk, (j + 1) * cw_chunk)
        blk = ext_ref[:, cs]
        acc = cb_ref[:, cs]
        for k in range(SSD_CONV - 1):
            acc = acc + _delay_rows(blk, SSD_CONV - 1 - k) * cw_ref[k:k + 1, cs]
        acc = acc + blk[SUBLANES:] * cw_ref[SSD_CONV - 1:SSD_CONV, cs]
        xact_ref[:, cs] = acc * _sigmoid(acc)
    ext_ref[0:SUBLANES, :] = ext_ref[L:L + SUBLANES, :]

    dt = _softplus(dt_ref[...] + brow_ref[...])
    dtt = _softplus(dtt_ref[...] + bcol_ref[...])
    da = dt * (-jnp.exp(alrow_ref[...]))
    dat = dtt * (-jnp.exp(alcol_ref[...]))
    ri = lax.broadcasted_iota(jnp.int32, (L, L), 0)
    ci = lax.broadcasted_iota(jnp.int32, (L, L), 1)
    causal = ri >= ci
    tri_l = jnp.where(causal, 1.0, 0.0).astype(BF16)
    tri_u = jnp.where(ri <= ci, 1.0, 0.0).astype(BF16)
    a_cs = _dot3_rhs(tri_l, da)
    acs_ref[...] = a_cs
    acst_ref[...] = _dot3_lhs(dat, tri_u)
    a_last = a_cs[L - 1:L, :]
    stack = jnp.concatenate([dt, jnp.exp(a_last - a_cs), jnp.exp(a_cs),
                             jnp.broadcast_to(jnp.exp(a_last), (SUBLANES, LANES))], axis=0)
    sh, sm, sl = _split3(stack)
    for j in range(SSD_INNER // cw_chunk):
        cs = slice(j * cw_chunk, (j + 1) * cw_chunk)
        ej = e_ref[:, cs]
        d = lambda a: jnp.dot(a, ej, preferred_element_type=F32)
        exp_ref[:, cs] = (d(sl) + d(sm)) + d(sh)
    for j in range(SSD_INNER // cw_chunk):
        cs = slice(j * cw_chunk, (j + 1) * cw_chunk)
        xdt = xact_ref[:, cs] * exp_ref[0:L, cs]
        xdt_ref[:, cs] = xdt.astype(BF16)
        xdec_ref[:, cs] = (xdt * exp_ref[L:2 * L, cs]).astype(BF16)

    lo = ci < HALF
    gw = SSD_HEAD_DIM * (SSD_HEADS // SSD_GROUPS)
    for g in range(SSD_GROUPS):
        bg = xact_ref[:, SSD_INNER + g * SSD_STATE:SSD_INNER + (g + 1) * SSD_STATE]
        cg = xact_ref[:, SSD_INNER + BC_WIDTH + g * SSD_STATE:SSD_INNER + BC_WIDTH + (g + 1) * SSD_STATE]
        bb = bg.astype(BF16)
        cbf = cg.astype(BF16)
        cbm = lax.dot_general(cbf, bb, (((1,), (1,)), ((), ())), preferred_element_type=F32)
        gs = slice(g * gw, (g + 1) * gw)
        prev = state_ref[g]
        yoff = jnp.dot(cbf, prev.astype(BF16), preferred_element_type=F32) * exp_ref[2 * L:3 * L, gs]
        btb = bg.T.astype(BF16)
        state_ref[g] = prev * exp_ref[3 * L:3 * L + 1, gs] + jnp.dot(
            btb, xdec_ref[:, gs], preferred_element_type=F32)
        for e in range(2):
            pc = slice((2 * g + e) * LANES, (2 * g + e + 1) * LANES)
            xpair = xdt_ref[:, pc]
            yd = []
            for r in range(2):
                h = 4 * g + 2 * e + r
                seg = acs_ref[:, h:h + 1] - acst_ref[h:h + 1, :]
                lm = jnp.exp(jnp.where(causal, seg, NEG))
                yd.append(jnp.dot((cbm * lm).astype(BF16), xpair, preferred_element_type=F32))
            y_ref[:, pc] = (jnp.where(lo, yd[0], yd[1]) + yoff[:, e * LANES:(e + 1) * LANES]
                            + dfull_ref[:, pc] * xact_ref[:, pc])

    for g in range(SSD_GROUPS):
        gs = slice(g * gw, (g + 1) * gw)
        zz = z_ref[:, gs]
        yg = y_ref[:, gs] * (zz * _sigmoid(zz))
        ms = jnp.mean(yg * yg, axis=-1, keepdims=True)
        o_ref[:, gs] = (yg * lax.rsqrt(ms + EPS) * gn_ref[:, gs]).astype(o_ref.dtype)


def _ssd(proj, dt_raw, dt_rawt, cw, cb, brow, bcol, alrow, alcol, emat, dfull, gn, batch, seq):
    t = proj.shape[0]
    L = SSD_CHUNK
    nc = seq // L
    row = lambda b, c: b * nc + c
    full = lambda shape: pl.BlockSpec(shape, lambda b, c: (0, 0))
    return pl.pallas_call(
        _ssd_kernel,
        grid=(batch, nc),
        in_specs=[pl.BlockSpec((L, CONV_CH), lambda b, c: (row(b, c), XBC_OFF // CONV_CH)),
                  pl.BlockSpec((L, SSD_INNER), lambda b, c: (row(b, c), Z_OFF // SSD_INNER)),
                  pl.BlockSpec((L, LANES), lambda b, c: (row(b, c), 0)),
                  pl.BlockSpec((LANES, L), lambda b, c: (0, row(b, c))),
                  full((SSD_CONV, CONV_CH)), full((1, CONV_CH)),
                  full((1, LANES)), full((LANES, 1)), full((1, LANES)), full((LANES, 1)),
                  full((LANES, SSD_INNER)), full((1, SSD_INNER)), full((1, SSD_INNER))],
        out_specs=pl.BlockSpec((L, SSD_INNER), lambda b, c: (row(b, c), 0)),
        out_shape=jax.ShapeDtypeStruct((t, SSD_INNER), BF16),
        scratch_shapes=[pltpu.VMEM((L + SUBLANES, CONV_CH), F32),
                        pltpu.VMEM((L, CONV_CH), F32),
                        pltpu.VMEM((SSD_GROUPS, SSD_STATE, SSD_INNER // SSD_GROUPS), F32),
                        pltpu.VMEM((L, SSD_INNER), BF16),
                        pltpu.VMEM((L, SSD_INNER), BF16),
                        pltpu.VMEM((L, SSD_INNER), F32),
                        pltpu.VMEM((3 * L + SUBLANES, SSD_INNER), F32),
                        pltpu.VMEM((L, LANES), F32),
                        pltpu.VMEM((LANES, L), F32)],
        compiler_params=_params(2),
        name="ssd_scan",
    )(proj, proj, dt_raw, dt_rawt, cw, cb, brow, bcol, alrow, alcol, emat, dfull, gn)


def _out_proj_kernel(a1_ref, a2_ref, w1_ref, w2_ref, r_ref, o_ref, w1b_ref, w2b_ref):
    @pl.when(pl.program_id(1) == 0)
    def _():
        _cast_rows(w1_ref, w1b_ref)
        _cast_rows(w2_ref, w2b_ref)

    acc = jnp.dot(a1_ref[...], w1b_ref[...], preferred_element_type=F32)
    acc = acc + jnp.dot(a2_ref[...], w2b_ref[...], preferred_element_type=F32)
    o_ref[...] = r_ref[...] + acc


def _out_proj(a1, a2, w, res, tm, tn):
    t, k = a1.shape
    n = w.shape[1]
    return pl.pallas_call(
        _out_proj_kernel,
        grid=(n // tn, t // tm),
        in_specs=[pl.BlockSpec((tm, k), lambda j, m: (m, 0)),
                  pl.BlockSpec((tm, k), lambda j, m: (m, 0)),
                  pl.BlockSpec((k, tn), lambda j, m: (0, j)),
                  pl.BlockSpec((k, tn), lambda j, m: (1, j)),
                  pl.BlockSpec((tm, tn), lambda j, m: (m, j))],
        out_specs=pl.BlockSpec((tm, tn), lambda j, m: (m, j)),
        out_shape=jax.ShapeDtypeStruct((t, n), F32),
        scratch_shapes=[pltpu.VMEM((k, tn), BF16), pltpu.VMEM((k, tn), BF16)],
        compiler_params=_params(2),
        name="out_proj",
    )(a1, a2, w, w, res)


def _up_kernel(a_ref, wg_ref, wv_ref, cwg_ref, cwv_ref, cbg_ref, cbv_ref, o_ref,
               wgb_ref, wvb_ref, carry_ref, *, tm, seq):
    m = pl.program_id(1)

    @pl.when(m == 0)
    def _():
        _cast_rows(wg_ref, wgb_ref)
        _cast_rows(wv_ref, wvb_ref)

    a = a_ref[...]
    seq_start = (m * tm) % seq == 0
    outs = []
    for idx, (wb_ref, cw_ref, cb_ref) in enumerate(((wgb_ref, cwg_ref, cbg_ref),
                                                    (wvb_ref, cwv_ref, cbv_ref))):
        u = jnp.dot(a, wb_ref[...], preferred_element_type=F32)
        above = jnp.where(seq_start, 0.0, carry_ref[idx])
        blk = jnp.concatenate([above, u], axis=0)
        cw = cw_ref[...]
        conv = cb_ref[...]
        for k in range(FFN_CONV - 1):
            conv = conv + _delay_rows(blk, FFN_CONV - 1 - k) * cw[k:k + 1, :]
        conv = conv + u * cw[FFN_CONV - 1:FFN_CONV, :]
        carry_ref[idx] = u[tm - SUBLANES:tm]
        outs.append(conv)
    gate, val = outs
    o_ref[...] = ((gate * _sigmoid(gate)) * val).astype(o_ref.dtype)


def _up_proj(a, w, cw, cb, tm, tn, seq):
    t, k = a.shape
    assert seq % tm == 0 and D_FF % tn == 0
    nb = D_FF // tn
    return pl.pallas_call(
        functools.partial(_up_kernel, tm=tm, seq=seq),
        grid=(nb, t // tm),
        in_specs=[pl.BlockSpec((tm, k), lambda j, m: (m, 0)),
                  pl.BlockSpec((k, tn), lambda j, m: (0, j)),
                  pl.BlockSpec((k, tn), lambda j, m: (0, j + nb)),
                  pl.BlockSpec((FFN_CONV, tn), lambda j, m: (0, j)),
                  pl.BlockSpec((FFN_CONV, tn), lambda j, m: (0, j + nb)),
                  pl.BlockSpec((1, tn), lambda j, m: (0, j)),
                  pl.BlockSpec((1, tn), lambda j, m: (0, j + nb))],
        out_specs=pl.BlockSpec((tm, tn), lambda j, m: (m, j)),
        out_shape=jax.ShapeDtypeStruct((t, D_FF), BF16),
        scratch_shapes=[pltpu.VMEM((k, tn), BF16), pltpu.VMEM((k, tn), BF16),
                        pltpu.VMEM((2, SUBLANES, tn), F32)],
        compiler_params=_params(2),
        name="up_proj_conv_swiglu",
    )(a, w, w, cw, cw, cb, cb)


def _down_kernel(a_ref, w_ref, r_ref, o_ref, wb_ref):
    @pl.when(pl.program_id(1) == 0)
    def _():
        _cast_rows(w_ref, wb_ref)

    o_ref[...] = r_ref[...] + jnp.dot(a_ref[...], wb_ref[...], preferred_element_type=F32)


def _down_proj(a, w, res, tm, tn):
    t, k = a.shape
    n = w.shape[1]
    return pl.pallas_call(
        _down_kernel,
        grid=(n // tn, t // tm),
        in_specs=[pl.BlockSpec((tm, k), lambda j, m: (m, 0)),
                  pl.BlockSpec((k, tn), lambda j, m: (0, j)),
                  pl.BlockSpec((tm, tn), lambda j, m: (m, j))],
        out_specs=pl.BlockSpec((tm, tn), lambda j, m: (m, j)),
        out_shape=jax.ShapeDtypeStruct((t, n), F32),
        scratch_shapes=[pltpu.VMEM((k, tn), BF16)],
        compiler_params=_params(2),
        name="down_proj",
    )(a, w, res)


def _pad_lanes(v):
    return jnp.pad(v.astype(F32), (0, LANES - v.shape[0]))


def _mixer_layer(h, batch, seq, norm_mix, w_in, sinks, attn_out_norm, ssd_conv_w, ssd_conv_b, dt_bias,
                 a_log, ssd_d, ssd_norm, w_out, tables, emat):
    w_in_t = jnp.swapaxes(w_in, 0, 1)
    wdt_t = jnp.pad(w_in_t[MAIN_PROJ:], ((0, LANES - SSD_HEADS), (0, 0))).astype(BF16)
    xn, dt_raw, dt_rawt = _norm_dt(h, norm_mix.reshape(1, -1), wdt_t.T, wdt_t, tm=512)
    proj = _in_proj(xn, w_in_t, tm=1024, tn=1024)
    attn = _attention(proj, sinks.astype(F32), attn_out_norm.reshape(1, -1), tables, batch, seq)
    bias = _pad_lanes(dt_bias)
    alog = _pad_lanes(a_log)
    y = _ssd(proj, dt_raw, dt_rawt, ssd_conv_w, ssd_conv_b.reshape(1, -1),
             bias.reshape(1, -1), bias.reshape(-1, 1), alog.reshape(1, -1), alog.reshape(-1, 1),
             emat, jnp.repeat(ssd_d.astype(F32), SSD_HEAD_DIM).reshape(1, -1),
             ssd_norm.reshape(1, -1), batch, seq)
    return _out_proj(attn, y, w_out, h, tm=1024, tn=512)


def kernel(x, norm_mix, w_in, sinks, attn_out_norm, ssd_conv_w, ssd_conv_b, dt_bias, a_log, ssd_d, ssd_norm,
           w_out, norm_ffn, w_up, ffn_conv_w, ffn_conv_b, w_down, norm_final):
    batch, seq, d = x.shape
    h = x.reshape(batch * seq, d)
    tables = _rope_tables(seq)
    head_of_channel = np.arange(SSD_INNER) // SSD_HEAD_DIM
    emat = jnp.asarray(np.arange(LANES)[:, None] == head_of_channel[None, :], dtype=BF16)
    for l in range(norm_mix.shape[0]):
        h = _mixer_layer(h, batch, seq, norm_mix[l], w_in[l], sinks[l], attn_out_norm[l], ssd_conv_w[l],
                         ssd_conv_b[l], dt_bias[l], a_log[l], ssd_d[l], ssd_norm[l], w_out[l], tables, emat)
        hn = _norm(h, norm_ffn[l].reshape(1, -1), BF16, 512, "norm_ffn")
        act = _up_proj(hn, w_up[l], ffn_conv_w[l], ffn_conv_b[l].reshape(1, -1),
                       tm=512, tn=512, seq=seq)
        h = _down_proj(act, w_down[l], h, tm=512, tn=512)
    out = _norm(h, norm_final.reshape(1, -1), F32, 512, "norm_final")
    return out.reshape(batch, seq, d)
```

```python
import functools

import numpy as np
import jax
import jax.numpy as jnp
from jax import lax
from jax.experimental import pallas as pl
from jax.experimental.pallas import tpu as pltpu

F32 = jnp.float32
BF16 = jnp.bfloat16

D_MODEL = 2048
N_Q_HEADS = 32
N_KV_HEADS = 8
HEAD_DIM = 64
Q_PER_KV = N_Q_HEADS // N_KV_HEADS
WINDOW = 128
ATTN_BLOCK = 128
ROT_DIM = HEAD_DIM // 4
ROPE_THETA = 500000.0
SSD_HEADS = 32
SSD_HEAD_DIM = 64
SSD_INNER = SSD_HEADS * SSD_HEAD_DIM
SSD_GROUPS = 8
SSD_STATE = 128
SSD_CONV = 4
SSD_CHUNK = 128
ATTN_WIDTH = N_Q_HEADS * HEAD_DIM
KV_WIDTH = N_KV_HEADS * HEAD_DIM
BC_WIDTH = SSD_GROUPS * SSD_STATE
CONV_CH = SSD_INNER + 2 * BC_WIDTH
MAIN_PROJ = ATTN_WIDTH + 2 * KV_WIDTH + SSD_INNER + CONV_CH
D_FF = 5632
FFN_CONV = 3
EPS = 1e-6

LANES = 128
SUBLANES = 8
HALF = LANES // 2
NEG = -1e30
LOG2E = 1.4426950408889634
VMEM_LIMIT = 56 * 1024 * 1024

Q_OFF = 0
Z_OFF = ATTN_WIDTH
XBC_OFF = Z_OFF + SSD_INNER
K_OFF = XBC_OFF + CONV_CH
V_OFF = K_OFF + KV_WIDTH


def _params(n_axes, flags=None):
    return pltpu.CompilerParams(dimension_semantics=("arbitrary",) * n_axes,
                                vmem_limit_bytes=VMEM_LIMIT, flags=flags)


def _sigmoid(x):
    return 1.0 / (1.0 + jnp.exp(-x))


def _softplus(x):
    return jnp.maximum(x, 0.0) + jnp.log1p(jnp.exp(-jnp.abs(x)))


def _cast_rows(src_ref, dst_ref, rows=256):
    k = src_ref.shape[0]
    for r in range(0, k, rows):
        dst_ref[r:r + rows, :] = src_ref[r:r + rows, :].astype(BF16)


def _delay_rows(blk, sh):
    n, w = blk.shape[0] - SUBLANES, blk.shape[1]
    g = blk.reshape(n // SUBLANES + 1, SUBLANES, w)
    rot = pltpu.roll(g, sh, 1)
    row = lax.broadcasted_iota(jnp.int32, (SUBLANES, w), 0)
    out = jnp.where(row < sh, rot[:-1], rot[1:])
    return out.reshape(n, w)


def _rmsnorm_rows(x, g):
    ms = jnp.mean(x * x, axis=-1, keepdims=True)
    return x * lax.rsqrt(ms + EPS) * g


def _norm_dt_kernel(x_ref, g_ref, wdt_ref, wdtt_ref, xn_ref, dt_ref, dtt_ref):
    xn = _rmsnorm_rows(x_ref[...], g_ref[...]).astype(BF16)
    xn_ref[...] = xn
    dt_ref[...] = jnp.dot(xn, wdt_ref[...], preferred_element_type=F32)
    dtt_ref[...] = lax.dot_general(wdtt_ref[...], xn, (((1,), (1,)), ((), ())),
                                   preferred_element_type=F32)


def _norm_dt(x2, g, wdt, wdtt, tm):
    t, d = x2.shape
    return pl.pallas_call(
        _norm_dt_kernel,
        grid=(t // tm,),
        in_specs=[pl.BlockSpec((tm, d), lambda i: (i, 0)),
                  pl.BlockSpec((1, d), lambda i: (0, 0)),
                  pl.BlockSpec((d, LANES), lambda i: (0, 0)),
                  pl.BlockSpec((LANES, d), lambda i: (0, 0))],
        out_specs=[pl.BlockSpec((tm, d), lambda i: (i, 0)),
                   pl.BlockSpec((tm, LANES), lambda i: (i, 0)),
                   pl.BlockSpec((LANES, tm), lambda i: (0, i))],
        out_shape=[jax.ShapeDtypeStruct((t, d), BF16),
                   jax.ShapeDtypeStruct((t, LANES), F32),
                   jax.ShapeDtypeStruct((LANES, t), F32)],
        compiler_params=_params(1),
        name="norm_dt",
    )(x2, g, wdt, wdtt)


def _norm_kernel(x_ref, g_ref, o_ref):
    o_ref[...] = _rmsnorm_rows(x_ref[...], g_ref[...]).astype(o_ref.dtype)


def _norm(x2, g, out_dtype, tm, name):
    t, d = x2.shape
    return pl.pallas_call(
        _norm_kernel,
        grid=(t // tm,),
        in_specs=[pl.BlockSpec((tm, d), lambda i: (i, 0)),
                  pl.BlockSpec((1, d), lambda i: (0, 0))],
        out_specs=pl.BlockSpec((tm, d), lambda i: (i, 0)),
        out_shape=jax.ShapeDtypeStruct((t, d), out_dtype),
        compiler_params=_params(1),
        name=name,
    )(x2, g)


def _in_proj_kernel(a_ref, wt_ref, o_ref, wbf_ref):
    @pl.when(pl.program_id(1) == 0)
    def _():
        _cast_rows(wt_ref, wbf_ref)

    o_ref[...] = lax.dot_general(a_ref[...], wbf_ref[...], (((1,), (1,)), ((), ())),
                                 preferred_element_type=F32)


def _in_proj(xn, wt, tm, tn):
    t, k = xn.shape
    u = 1024 // tn

    def wmap(j, m):
        return (jnp.where(j < 2 * u, j, jnp.where(j < 8 * u, j + u, j - 6 * u)), 0)

    return pl.pallas_call(
        _in_proj_kernel,
        grid=(MAIN_PROJ // tn, t // tm),
        in_specs=[pl.BlockSpec((tm, k), lambda j, m: (m, 0)),
                  pl.BlockSpec((tn, k), wmap)],
        out_specs=pl.BlockSpec((tm, tn), lambda j, m: (m, j)),
        out_shape=jax.ShapeDtypeStruct((t, MAIN_PROJ), F32),
        scratch_shapes=[pltpu.VMEM((tn, k), BF16)],
        compiler_params=_params(2),
        name="in_proj",
    )(xn, wt)


def _rope_tables(seq):
    half = ROT_DIM // 2
    inv = 1.0 / (ROPE_THETA ** (jnp.arange(0, ROT_DIM, 2, dtype=F32) / ROT_DIM))
    ang = jnp.arange(seq, dtype=F32)[:, None] * inv[None, :]
    cos, sin = jnp.cos(ang), jnp.sin(ang)
    d = np.arange(LANES) % HEAD_DIM
    idx = d % half
    in_rot = jnp.asarray(d < ROT_DIM)
    first = jnp.asarray(d < half)
    second = jnp.asarray((d >= half) & (d < ROT_DIM))
    c = jnp.where(in_rot[None, :], cos[:, idx], 1.0)
    s1 = jnp.where(first[None, :], -sin[:, idx], 0.0)
    s2 = jnp.where(second[None, :], sin[:, idx], 0.0)
    return jnp.stack([c, s1, s2]).astype(F32)


def _rope(x, c, s1, s2):
    half = ROT_DIM // 2
    return x * c + pltpu.roll(x, LANES - half, 1) * s1 + pltpu.roll(x, half, 1) * s2


def _attn_kernel(sinks_ref, q_ref, kp_ref, kc_ref, vp_ref, vc_ref,
                 tc_ref, tp_ref, g_ref, o_ref,
                 lhs_ref, kb_ref, vb_ref, s_ref, p_ref, o2_ref, m_ref, tq_ref, bias_ref, acc_ref, *, qb):
    blk = ATTN_BLOCK
    n = pl.program_id(1)
    lo = lax.broadcasted_iota(jnp.int32, (blk, LANES), 1) < HALF

    scale = HEAD_DIM ** -0.5 * LOG2E
    for i in range(3):
        tq_ref[i] = tc_ref[i] * scale
    qi = lax.broadcasted_iota(jnp.int32, (blk, 2 * blk), 0)
    kj = lax.broadcasted_iota(jnp.int32, (blk, 2 * blk), 1)
    rel = qi + blk - kj
    band = (rel >= 0) & (rel < WINDOW)
    kmin = jnp.where(n > 0, 0, blk)
    bias_ref[0] = jnp.where(band & (kj >= kmin), 0.0, NEG)
    bias_ref[1] = jnp.where(band, 0.0, NEG)

    for j in range(qb):
        rows = slice(j * blk, (j + 1) * blk)
        for col in range(ATTN_WIDTH // LANES):
            h, jj = col // 2, col % 2
            q2 = _rope(q_ref[rows, col * LANES:(col + 1) * LANES],
                       tq_ref[0, rows, :], tq_ref[1, rows, :], tq_ref[2, rows, :])
            lhs_ref[j, h, (2 * jj) * blk:(2 * jj + 1) * blk, :] = jnp.where(lo, q2, 0.0).astype(BF16)
            lhs_ref[j, h, (2 * jj + 1) * blk:(2 * jj + 2) * blk, :] = jnp.where(lo, 0.0, q2).astype(BF16)
    nk = (qb + 1) * blk
    lok = lax.broadcasted_iota(jnp.int32, (nk, LANES), 1) < HALF
    ones = jnp.ones((nk, LANES), BF16)
    for i in range(N_KV_HEADS // 2):
        cols = slice(i * LANES, (i + 1) * LANES)
        kcat = jnp.concatenate([_rope(kp_ref[:, cols], tp_ref[0], tp_ref[1], tp_ref[2]),
                                _rope(kc_ref[:, cols], tc_ref[0], tc_ref[1], tc_ref[2])], axis=0)
        vcat = jnp.concatenate([vp_ref[:, cols], vc_ref[:, cols]], axis=0)
        kswp = pltpu.roll(kcat, HALF, 1)
        vswp = pltpu.roll(vcat, HALF, 1)
        kb_ref[2 * i] = jnp.where(lok, kcat, kswp).astype(BF16)
        kb_ref[2 * i + 1] = jnp.where(lok, kswp, kcat).astype(BF16)
        vb_ref[2 * i, :, 0:LANES] = jnp.where(lok, vcat, vswp).astype(BF16)
        vb_ref[2 * i + 1, :, 0:LANES] = jnp.where(lok, vswp, vcat).astype(BF16)
        vb_ref[2 * i, :, LANES:2 * LANES] = ones
        vb_ref[2 * i + 1, :, LANES:2 * LANES] = ones

    def keys(j):
        return slice(j * blk, (j + 2) * blk)

    def scores(j, h):
        s_ref[j, h % 2] = lax.dot_general(lhs_ref[j, h], kb_ref[h, keys(j), :], (((1,), (1,)), ((), ())),
                                          preferred_element_type=F32)

    def row_max(j, h):
        slot = h % 2
        for r in range(Q_PER_KV):
            rows = slice(r * blk, (r + 1) * blk)
            sink = sinks_ref[Q_PER_KV * h + r] * LOG2E
            sr = s_ref[j, slot, rows, :] + bias_ref[min(j, 1)]
            s_ref[j, slot, rows, :] = sr
            m = jnp.maximum(jnp.max(sr, axis=-1, keepdims=True), sink)
            m_ref[j, slot, rows, :] = jnp.broadcast_to(m, (blk, LANES))

    def probs(j, h):
        slot = h % 2
        for r in range(Q_PER_KV):
            rows = slice(r * blk, (r + 1) * blk)
            m = m_ref[j, slot, rows, :]
            for half in range(2):
                ln = slice(half * LANES, (half + 1) * LANES)
                p_ref[j, slot, rows, ln] = jnp.exp2(s_ref[j, slot, rows, ln] - m).astype(BF16)

    def weighted(j, h):
        o2_ref[j, h % 2] = jnp.dot(p_ref[j, h % 2], vb_ref[h, keys(j), :],
                                   preferred_element_type=F32)

    def finish(j, h):
        slot = h % 2
        outs = []
        for r in range(Q_PER_KV):
            rows = slice(r * blk, (r + 1) * blk)
            sink = sinks_ref[Q_PER_KV * h + r] * LOG2E
            den = o2_ref[j, slot, rows, LANES:2 * LANES] + jnp.exp2(sink - m_ref[j, slot, rows, :])
            outs.append(o2_ref[j, slot, rows, 0:LANES] * (1.0 / den))
        for jj in range(2):
            col = 2 * h + jj
            acc_ref[j * blk:(j + 1) * blk, col * LANES:(col + 1) * LANES] = jnp.where(
                lo, outs[2 * jj], outs[2 * jj + 1])

    for j in range(qb):
        scores(j, 0)
    for h in range(N_KV_HEADS + 1):
        for stage in (scores, row_max, probs, weighted, finish):
            hh = {scores: h + 1, finish: h - 1}.get(stage, h)
            if 0 <= hh < N_KV_HEADS:
                for j in range(qb):
                    stage(j, hh)

    o_ref[...] = _rmsnorm_rows(acc_ref[...], g_ref[...]).astype(o_ref.dtype)


def _attention(proj, sinks, g, tables, batch, seq, qb):
    t = proj.shape[0]
    blk = ATTN_BLOCK
    assert seq % (qb * blk) == 0
    nb = seq // blk
    ns = nb // qb
    kblk = K_OFF // KV_WIDTH
    vblk = V_OFF // KV_WIDTH

    def cur(b, n):
        return (b * ns + n, 0)

    def prev_rows(b, n):
        return jnp.maximum(b * nb + n * qb - 1, 0)

    return pl.pallas_call(
        functools.partial(_attn_kernel, qb=qb),
        grid=(batch, ns),
        in_specs=[pl.BlockSpec(memory_space=pltpu.SMEM),
                  pl.BlockSpec((qb * blk, ATTN_WIDTH), cur),
                  pl.BlockSpec((blk, KV_WIDTH), lambda b, n: (prev_rows(b, n), kblk)),
                  pl.BlockSpec((qb * blk, KV_WIDTH), lambda b, n: (b * ns + n, kblk)),
                  pl.BlockSpec((blk, KV_WIDTH), lambda b, n: (prev_rows(b, n), vblk)),
                  pl.BlockSpec((qb * blk, KV_WIDTH), lambda b, n: (b * ns + n, vblk)),
                  pl.BlockSpec((3, qb * blk, LANES), lambda b, n: (0, n, 0)),
                  pl.BlockSpec((3, blk, LANES), lambda b, n: (0, jnp.maximum(n * qb - 1, 0), 0)),
                  pl.BlockSpec((1, ATTN_WIDTH), lambda b, n: (0, 0))],
        out_specs=pl.BlockSpec((qb * blk, ATTN_WIDTH), cur),
        out_shape=jax.ShapeDtypeStruct((t, ATTN_WIDTH), BF16),
        scratch_shapes=[pltpu.VMEM((qb, N_KV_HEADS, Q_PER_KV * blk, LANES), BF16),
                        pltpu.VMEM((N_KV_HEADS, (qb + 1) * blk, LANES), BF16),
                        pltpu.VMEM((N_KV_HEADS, (qb + 1) * blk, 2 * LANES), BF16),
                        pltpu.VMEM((qb, 2, Q_PER_KV * blk, 2 * blk), F32),
                        pltpu.VMEM((qb, 2, Q_PER_KV * blk, 2 * blk), BF16),
                        pltpu.VMEM((qb, 2, Q_PER_KV * blk, 2 * LANES), F32),
                        pltpu.VMEM((qb, 2, Q_PER_KV * blk, LANES), F32),
                        pltpu.VMEM((3, qb * blk, LANES), F32),
                        pltpu.VMEM((2, blk, 2 * blk), F32),
                        pltpu.VMEM((qb * blk, ATTN_WIDTH), F32)],
        compiler_params=_params(2),
        name="swa_attention",
    )(sinks, proj, proj, proj, proj, proj, tables, tables, g)


def _split3(x):
    h = x.astype(BF16)
    r = x - h.astype(F32)
    m = r.astype(BF16)
    l = (r - m.astype(F32)).astype(BF16)
    return h, m, l


def _dot3_lhs(x, w):
    h, m, l = _split3(x)
    d = lambda a: jnp.dot(a, w, preferred_element_type=F32)
    return (d(l) + d(m)) + d(h)


def _dot3_rhs(w, x):
    h, m, l = _split3(x)
    d = lambda a: jnp.dot(w, a, preferred_element_type=F32)
    return (d(l) + d(m)) + d(h)


def _ssd_kernel(xbc_ref, z_ref, dt_ref, dtt_ref, cw_ref, cb_ref, brow_ref, bcol_ref,
                alrow_ref, alcol_ref, e_ref, dfull_ref, gn_ref, o_ref,
                ext_ref, xact_ref, state_ref, xdt_ref, xdec_ref, y_ref, exp_ref, acs_ref, acst_ref):
    L = SSD_CHUNK
    c = pl.program_id(1)

    @pl.when(c == 0)
    def _():
        ext_ref[0:SUBLANES, :] = jnp.zeros((SUBLANES, CONV_CH), F32)
        state_ref[...] = jnp.zeros_like(state_ref)

    ext_ref[SUBLANES:SUBLANES + L, :] = xbc_ref[...]
    cw_chunk = 512
    for j in range(CONV_CH // cw_chunk):
        cs = slice(j * cw_chunk, (j + 1) * cw_chunk)
        blk = ext_ref[:, cs]
        acc = cb_ref[:, cs]
        for k in range(SSD_CONV - 1):
            acc = acc + _delay_rows(blk, SSD_CONV - 1 - k) * cw_ref[k:k + 1, cs]
        acc = acc + blk[SUBLANES:] * cw_ref[SSD_CONV - 1:SSD_CONV, cs]
        xact_ref[:, cs] = acc * _sigmoid(acc)
    ext_ref[0:SUBLANES, :] = ext_ref[L:L + SUBLANES, :]

    dt = _softplus(dt_ref[...] + brow_ref[...])
    dtt = _softplus(dtt_ref[...] + bcol_ref[...])
    da = dt * (-jnp.exp(alrow_ref[...]))
    dat = dtt * (-jnp.exp(alcol_ref[...]))
    ri = lax.broadcasted_iota(jnp.int32, (L, L), 0)
    ci = lax.broadcasted_iota(jnp.int32, (L, L), 1)
    causal = ri >= ci
    tri_l = jnp.where(causal, 1.0, 0.0).astype(BF16)
    tri_u = jnp.where(ri <= ci, 1.0, 0.0).astype(BF16)
    a_cs = _dot3_rhs(tri_l, da)
    acs_ref[...] = a_cs
    acst_ref[...] = _dot3_lhs(dat, tri_u)
    a_last = a_cs[L - 1:L, :]
    stack = jnp.concatenate([dt, jnp.exp(a_last - a_cs), jnp.exp(a_cs),
                             jnp.broadcast_to(jnp.exp(a_last), (SUBLANES, LANES))], axis=0)
    sh, sm, sl = _split3(stack)
    for j in range(SSD_INNER // cw_chunk):
        cs = slice(j * cw_chunk, (j + 1) * cw_chunk)
        ej = e_ref[:, cs]
        d = lambda a: jnp.dot(a, ej, preferred_element_type=F32)
        exp_ref[:, cs] = (d(sl) + d(sm)) + d(sh)
    for j in range(SSD_INNER // cw_chunk):
        cs = slice(j * cw_chunk, (j + 1) * cw_chunk)
        xdt = xact_ref[:, cs] * exp_ref[0:L, cs]
        xdt_ref[:, cs] = xdt.astype(BF16)
        xdec_ref[:, cs] = (xdt * exp_ref[L:2 * L, cs]).astype(BF16)

    lo = ci < HALF
    gw = SSD_HEAD_DIM * (SSD_HEADS // SSD_GROUPS)
    for g in range(SSD_GROUPS):
        bg = xact_ref[:, SSD_INNER + g * SSD_STATE:SSD_INNER + (g + 1) * SSD_STATE]
        cg = xact_ref[:, SSD_INNER + BC_WIDTH + g * SSD_STATE:SSD_INNER + BC_WIDTH + (g + 1) * SSD_STATE]
        bb = bg.astype(BF16)
        cbf = cg.astype(BF16)
        cbm = lax.dot_general(cbf, bb, (((1,), (1,)), ((), ())), preferred_element_type=F32)
        gs = slice(g * gw, (g + 1) * gw)
        prev = state_ref[g]
        yoff = jnp.dot(cbf, prev.astype(BF16), preferred_element_type=F32) * exp_ref[2 * L:3 * L, gs]
        btb = bg.T.astype(BF16)
        state_ref[g] = prev * exp_ref[3 * L:3 * L + 1, gs] + jnp.dot(
            btb, xdec_ref[:, gs], preferred_element_type=F32)
        for e in range(2):
            pc = slice((2 * g + e) * LANES, (2 * g + e + 1) * LANES)
            xpair = xdt_ref[:, pc]
            yd = []
            for r in range(2):
                h = 4 * g + 2 * e + r
                seg = acs_ref[:, h:h + 1] - acst_ref[h:h + 1, :]
                lm = jnp.exp(jnp.where(causal, seg, NEG))
                yd.append(jnp.dot((cbm * lm).astype(BF16), xpair, preferred_element_type=F32))
            y_ref[:, pc] = (jnp.where(lo, yd[0], yd[1]) + yoff[:, e * LANES:(e + 1) * LANES]
                            + dfull_ref[:, pc] * xact_ref[:, pc])

    for g in range(SSD_GROUPS):
        gs = slice(g * gw, (g + 1) * gw)
        zz = z_ref[:, gs]
        yg = y_ref[:, gs] * (zz * _sigmoid(zz))
        ms = jnp.mean(yg * yg, axis=-1, keepdims=True)
        o_ref[:, gs] = (yg * lax.rsqrt(ms + EPS) * gn_ref[:, gs]).astype(o_ref.dtype)


def _ssd(proj, dt_raw, dt_rawt, cw, cb, brow, bcol, alrow, alcol, emat, dfull, gn, batch, seq):
    t = proj.shape[0]
    L = SSD_CHUNK
    nc = seq // L
    row = lambda b, c: b * nc + c
    full = lambda shape: pl.BlockSpec(shape, lambda b, c: (0, 0))
    return pl.pallas_call(
        _ssd_kernel,
        grid=(batch, nc),
        in_specs=[pl.BlockSpec((L, CONV_CH), lambda b, c: (row(b, c), XBC_OFF // CONV_CH)),
                  pl.BlockSpec((L, SSD_INNER), lambda b, c: (row(b, c), Z_OFF // SSD_INNER)),
                  pl.BlockSpec((L, LANES), lambda b, c: (row(b, c), 0)),
                  pl.BlockSpec((LANES, L), lambda b, c: (0, row(b, c))),
                  full((SSD_CONV, CONV_CH)), full((1, CONV_CH)),
                  full((1, LANES)), full((LANES, 1)), full((1, LANES)), full((LANES, 1)),
                  full((LANES, SSD_INNER)), full((1, SSD_INNER)), full((1, SSD_INNER))],
        out_specs=pl.BlockSpec((L, SSD_INNER), lambda b, c: (row(b, c), 0)),
        out_shape=jax.ShapeDtypeStruct((t, SSD_INNER), BF16),
        scratch_shapes=[pltpu.VMEM((L + SUBLANES, CONV_CH), F32),
                        pltpu.VMEM((L, CONV_CH), F32),
                        pltpu.VMEM((SSD_GROUPS, SSD_STATE, SSD_INNER // SSD_GROUPS), F32),
                        pltpu.VMEM((L, SSD_INNER), BF16),
                        pltpu.VMEM((L, SSD_INNER), BF16),
                        pltpu.VMEM((L, SSD_INNER), F32),
                        pltpu.VMEM((3 * L + SUBLANES, SSD_INNER), F32),
                        pltpu.VMEM((L, LANES), F32),
                        pltpu.VMEM((LANES, L), F32)],
        compiler_params=_params(2),
        name="ssd_scan",
    )(proj, proj, dt_raw, dt_rawt, cw, cb, brow, bcol, alrow, alcol, emat, dfull, gn)


def _out_proj_kernel(a1_ref, a2_ref, w1_ref, w2_ref, r_ref, o_ref, w1b_ref, w2b_ref):
    @pl.when(pl.program_id(1) == 0)
    def _():
        _cast_rows(w1_ref, w1b_ref)
        _cast_rows(w2_ref, w2b_ref)

    acc = jnp.dot(a1_ref[...], w1b_ref[...], preferred_element_type=F32)
    acc = acc + jnp.dot(a2_ref[...], w2b_ref[...], preferred_element_type=F32)
    o_ref[...] = r_ref[...] + acc


def _out_proj(a1, a2, w, res, tm, tn):
    t, k = a1.shape
    n = w.shape[1]
    return pl.pallas_call(
        _out_proj_kernel,
        grid=(n // tn, t // tm),
        in_specs=[pl.BlockSpec((tm, k), lambda j, m: (m, 0)),
                  pl.BlockSpec((tm, k), lambda j, m: (m, 0)),
                  pl.BlockSpec((k, tn), lambda j, m: (0, j)),
                  pl.BlockSpec((k, tn), lambda j, m: (1, j)),
                  pl.BlockSpec((tm, tn), lambda j, m: (m, j))],
        out_specs=pl.BlockSpec((tm, tn), lambda j, m: (m, j)),
        out_shape=jax.ShapeDtypeStruct((t, n), F32),
        scratch_shapes=[pltpu.VMEM((k, tn), BF16), pltpu.VMEM((k, tn), BF16)],
        compiler_params=_params(2),
        name="out_proj",
    )(a1, a2, w, w, res)


def _up_kernel(a_ref, wg_ref, wv_ref, cwg_ref, cwv_ref, cbg_ref, cbv_ref, o_ref,
               wgb_ref, wvb_ref, carry_ref, *, tm, seq):
    m = pl.program_id(1)

    @pl.when(m == 0)
    def _():
        _cast_rows(wg_ref, wgb_ref)
        _cast_rows(wv_ref, wvb_ref)

    a = a_ref[...]
    seq_start = (m * tm) % seq == 0
    outs = []
    for idx, (wb_ref, cw_ref, cb_ref) in enumerate(((wgb_ref, cwg_ref, cbg_ref),
                                                    (wvb_ref, cwv_ref, cbv_ref))):
        u = jnp.dot(a, wb_ref[...], preferred_element_type=F32)
        above = jnp.where(seq_start, 0.0, carry_ref[idx])
        blk = jnp.concatenate([above, u], axis=0)
        cw = cw_ref[...]
        conv = cb_ref[...]
        for k in range(FFN_CONV - 1):
            conv = conv + _delay_rows(blk, FFN_CONV - 1 - k) * cw[k:k + 1, :]
        conv = conv + u * cw[FFN_CONV - 1:FFN_CONV, :]
        carry_ref[idx] = u[tm - SUBLANES:tm]
        outs.append(conv)
    gate, val = outs
    o_ref[...] = ((gate * _sigmoid(gate)) * val).astype(o_ref.dtype)


def _up_proj(a, w, cw, cb, tm, tn, seq):
    t, k = a.shape
    assert seq % tm == 0 and D_FF % tn == 0
    nb = D_FF // tn
    return pl.pallas_call(
        functools.partial(_up_kernel, tm=tm, seq=seq),
        grid=(nb, t // tm),
        in_specs=[pl.BlockSpec((tm, k), lambda j, m: (m, 0)),
                  pl.BlockSpec((k, tn), lambda j, m: (0, j)),
                  pl.BlockSpec((k, tn), lambda j, m: (0, j + nb)),
                  pl.BlockSpec((FFN_CONV, tn), lambda j, m: (0, j)),
                  pl.BlockSpec((FFN_CONV, tn), lambda j, m: (0, j + nb)),
                  pl.BlockSpec((1, tn), lambda j, m: (0, j)),
                  pl.BlockSpec((1, tn), lambda j, m: (0, j + nb))],
        out_specs=pl.BlockSpec((tm, tn), lambda j, m: (m, j)),
        out_shape=jax.ShapeDtypeStruct((t, D_FF), BF16),
        scratch_shapes=[pltpu.VMEM((k, tn), BF16), pltpu.VMEM((k, tn), BF16),
                        pltpu.VMEM((2, SUBLANES, tn), F32)],
        compiler_params=_params(2),
        name="up_proj_conv_swiglu",
    )(a, w, w, cw, cw, cb, cb)


def _down_kernel(a_ref, w_ref, r_ref, o_ref, wb_ref):
    @pl.when(pl.program_id(1) == 0)
    def _():
        _cast_rows(w_ref, wb_ref)

    o_ref[...] = r_ref[...] + jnp.dot(a_ref[...], wb_ref[...], preferred_element_type=F32)


def _down_proj(a, w, res, tm, tn):
    t, k = a.shape
    n = w.shape[1]
    return pl.pallas_call(
        _down_kernel,
        grid=(n // tn, t // tm),
        in_specs=[pl.BlockSpec((tm, k), lambda j, m: (m, 0)),
                  pl.BlockSpec((k, tn), lambda j, m: (0, j)),
                  pl.BlockSpec((tm, tn), lambda j, m: (m, j))],
        out_specs=pl.BlockSpec((tm, tn), lambda j, m: (m, j)),
        out_shape=jax.ShapeDtypeStruct((t, n), F32),
        scratch_shapes=[pltpu.VMEM((k, tn), BF16)],
        compiler_params=_params(2),
        name="down_proj",
    )(a, w, res)


def _pad_lanes(v):
    return jnp.pad(v.astype(F32), (0, LANES - v.shape[0]))


def _mixer_layer(h, batch, seq, norm_mix, w_in, sinks, attn_out_norm, ssd_conv_w, ssd_conv_b, dt_bias,
                 a_log, ssd_d, ssd_norm, w_out, tables, emat):
    w_in_t = jnp.swapaxes(w_in, 0, 1)
    wdt_t = jnp.pad(w_in_t[MAIN_PROJ:], ((0, LANES - SSD_HEADS), (0, 0))).astype(BF16)
    xn, dt_raw, dt_rawt = _norm_dt(h, norm_mix.reshape(1, -1), wdt_t.T, wdt_t, tm=512)
    proj = _in_proj(xn, w_in_t, tm=1024, tn=1024)
    attn = _attention(proj, sinks.astype(F32), attn_out_norm.reshape(1, -1), tables, batch, seq, qb=4)
    bias = _pad_lanes(dt_bias)
    alog = _pad_lanes(a_log)
    y = _ssd(proj, dt_raw, dt_rawt, ssd_conv_w, ssd_conv_b.reshape(1, -1),
             bias.reshape(1, -1), bias.reshape(-1, 1), alog.reshape(1, -1), alog.reshape(-1, 1),
             emat, jnp.repeat(ssd_d.astype(F32), SSD_HEAD_DIM).reshape(1, -1),
             ssd_norm.reshape(1, -1), batch, seq)
    return _out_proj(attn, y, w_out, h, tm=1024, tn=512)


def kernel(x, norm_mix, w_in, sinks, attn_out_norm, ssd_conv_w, ssd_conv_b, dt_bias, a_log, ssd_d, ssd_norm,
           w_out, norm_ffn, w_up, ffn_conv_w, ffn_conv_b, w_down, norm_final):
    batch, seq, d = x.shape
    h = x.reshape(batch * seq, d)
    tables = _rope_tables(seq)
    head_of_channel = np.arange(SSD_INNER) // SSD_HEAD_DIM
    emat = jnp.asarray(np.arange(LANES)[:, None] == head_of_channel[None, :], dtype=BF16)
    for l in range(norm_mix.shape[0]):
        h = _mixer_layer(h, batch, seq, norm_mix[l], w_in[l], sinks[l], attn_out_norm[l], ssd_conv_w[l],
                         ssd_conv_b[l], dt_bias[l], a_log[l], ssd_d[l], ssd_norm[l], w_out[l], tables, emat)
        hn = _norm(h, norm_ffn[l].reshape(1, -1), BF16, 512, "norm_ffn")
        act = _up_proj(hn, w_up[l], ffn_conv_w[l], ffn_conv_b[l].reshape(1, -1),
                       tm=512, tn=512, seq=seq)
        h = _down_proj(act, w_down[l], h, tm=512, tn=512)
    out = _norm(h, norm_final.reshape(1, -1), F32, 512, "norm_final")
    return out.reshape(batch, seq, d)
```

```python
import functools

import numpy as np
import jax
import jax.numpy as jnp
from jax import lax
from jax.experimental import pallas as pl
from jax.experimental.pallas import tpu as pltpu

F32 = jnp.float32
BF16 = jnp.bfloat16

D_MODEL = 2048
N_Q_HEADS = 32
N_KV_HEADS = 8
HEAD_DIM = 64
Q_PER_KV = N_Q_HEADS // N_KV_HEADS
WINDOW = 128
ATTN_BLOCK = 128
ROT_DIM = HEAD_DIM // 4
ROPE_THETA = 500000.0
SSD_HEADS = 32
SSD_HEAD_DIM = 64
SSD_INNER = SSD_HEADS * SSD_HEAD_DIM
SSD_GROUPS = 8
SSD_STATE = 128
SSD_CONV = 4
SSD_CHUNK = 128
ATTN_WIDTH = N_Q_HEADS * HEAD_DIM
KV_WIDTH = N_KV_HEADS * HEAD_DIM
BC_WIDTH = SSD_GROUPS * SSD_STATE
CONV_CH = SSD_INNER + 2 * BC_WIDTH
MAIN_PROJ = ATTN_WIDTH + 2 * KV_WIDTH + SSD_INNER + CONV_CH
D_FF = 5632
FFN_CONV = 3
EPS = 1e-6

LANES = 128
SUBLANES = 8
HALF = LANES // 2
NEG = -1e30
LOG2E = 1.4426950408889634
VMEM_LIMIT = 56 * 1024 * 1024

Q_OFF = 0
Z_OFF = ATTN_WIDTH
XBC_OFF = Z_OFF + SSD_INNER
K_OFF = XBC_OFF + CONV_CH
V_OFF = K_OFF + KV_WIDTH


def _params(n_axes, flags=None):
    return pltpu.CompilerParams(dimension_semantics=("arbitrary",) * n_axes,
                                vmem_limit_bytes=VMEM_LIMIT, flags=flags)


def _sigmoid(x):
    return 1.0 / (1.0 + jnp.exp2(x * -LOG2E))


def _softplus(x):
    return jnp.maximum(x, 0.0) + jnp.log1p(jnp.exp(-jnp.abs(x)))


def _cast_rows(src_ref, dst_ref, rows=256):
    k = src_ref.shape[0]
    for r in range(0, k, rows):
        dst_ref[r:r + rows, :] = src_ref[r:r + rows, :].astype(BF16)


def _delay_rows(blk, sh):
    n, w = blk.shape[0] - SUBLANES, blk.shape[1]
    g = blk.reshape(n // SUBLANES + 1, SUBLANES, w)
    rot = pltpu.roll(g, sh, 1)
    row = lax.broadcasted_iota(jnp.int32, (SUBLANES, w), 0)
    out = jnp.where(row < sh, rot[:-1], rot[1:])
    return out.reshape(n, w)


def _rmsnorm_rows(x, g):
    ms = jnp.mean(x * x, axis=-1, keepdims=True)
    return x * lax.rsqrt(ms + EPS) * g


def _norm_dt_kernel(x_ref, g_ref, wdt_ref, wdtt_ref, xn_ref, dt_ref, dtt_ref):
    xn = _rmsnorm_rows(x_ref[...], g_ref[...]).astype(BF16)
    xn_ref[...] = xn
    dt_ref[...] = jnp.dot(xn, wdt_ref[...], preferred_element_type=F32)
    dtt_ref[...] = lax.dot_general(wdtt_ref[...], xn, (((1,), (1,)), ((), ())),
                                   preferred_element_type=F32)


def _norm_dt(x2, g, wdt, wdtt, tm, batch, seq):
    t, d = x2.shape
    assert seq % tm == 0
    per_seq = seq // tm
    return pl.pallas_call(
        _norm_dt_kernel,
        grid=(t // tm,),
        in_specs=[pl.BlockSpec((tm, d), lambda i: (i, 0)),
                  pl.BlockSpec((1, d), lambda i: (0, 0)),
                  pl.BlockSpec((d, LANES), lambda i: (0, 0)),
                  pl.BlockSpec((LANES, d), lambda i: (0, 0))],
        out_specs=[pl.BlockSpec((tm, d), lambda i: (i, 0)),
                   pl.BlockSpec((tm, LANES), lambda i: (i, 0)),
                   pl.BlockSpec((None, LANES, tm), lambda i: (i // per_seq, 0, i % per_seq))],
        out_shape=[jax.ShapeDtypeStruct((t, d), BF16),
                   jax.ShapeDtypeStruct((t, LANES), F32),
                   jax.ShapeDtypeStruct((batch, LANES, seq), F32)],
        compiler_params=_params(1),
        name="norm_dt",
    )(x2, g, wdt, wdtt)


def _norm_kernel(x_ref, g_ref, o_ref):
    o_ref[...] = _rmsnorm_rows(x_ref[...], g_ref[...]).astype(o_ref.dtype)


def _norm(x2, g, out_dtype, tm, name):
    t, d = x2.shape
    return pl.pallas_call(
        _norm_kernel,
        grid=(t // tm,),
        in_specs=[pl.BlockSpec((tm, d), lambda i: (i, 0)),
                  pl.BlockSpec((1, d), lambda i: (0, 0))],
        out_specs=pl.BlockSpec((tm, d), lambda i: (i, 0)),
        out_shape=jax.ShapeDtypeStruct((t, d), out_dtype),
        compiler_params=_params(1),
        name=name,
    )(x2, g)


def _in_proj_kernel(a_ref, wt_ref, o_ref, wbf_ref):
    @pl.when(pl.program_id(1) == 0)
    def _():
        _cast_rows(wt_ref, wbf_ref)

    o_ref[...] = lax.dot_general(a_ref[...], wbf_ref[...], (((1,), (1,)), ((), ())),
                                 preferred_element_type=F32)


def _in_proj(xn, wt, tm, tn):
    t, k = xn.shape
    u = 1024 // tn

    def wmap(j, m):
        return (jnp.where(j < 2 * u, j, jnp.where(j < 8 * u, j + u, j - 6 * u)), 0)

    return pl.pallas_call(
        _in_proj_kernel,
        grid=(MAIN_PROJ // tn, t // tm),
        in_specs=[pl.BlockSpec((tm, k), lambda j, m: (m, 0)),
                  pl.BlockSpec((tn, k), wmap)],
        out_specs=pl.BlockSpec((tm, tn), lambda j, m: (m, j)),
        out_shape=jax.ShapeDtypeStruct((t, MAIN_PROJ), F32),
        scratch_shapes=[pltpu.VMEM((tn, k), BF16)],
        compiler_params=_params(2),
        name="in_proj",
    )(xn, wt)


def _rope_tables(seq):
    half = ROT_DIM // 2
    inv = 1.0 / (ROPE_THETA ** (jnp.arange(0, ROT_DIM, 2, dtype=F32) / ROT_DIM))
    ang = jnp.arange(seq, dtype=F32)[:, None] * inv[None, :]
    cos, sin = jnp.cos(ang), jnp.sin(ang)
    d = np.arange(LANES) % HEAD_DIM
    idx = d % half
    in_rot = jnp.asarray(d < ROT_DIM)
    first = jnp.asarray(d < half)
    second = jnp.asarray((d >= half) & (d < ROT_DIM))
    c = jnp.where(in_rot[None, :], cos[:, idx], 1.0)
    s1 = jnp.where(first[None, :], -sin[:, idx], 0.0)
    s2 = jnp.where(second[None, :], sin[:, idx], 0.0)
    return jnp.stack([c, s1, s2]).astype(F32)


def _rope(x, c, s1, s2):
    half = ROT_DIM // 2
    return x * c + pltpu.roll(x, LANES - half, 1) * s1 + pltpu.roll(x, half, 1) * s2


def _attn_kernel(sinks_ref, q_ref, kp_ref, kc_ref, vp_ref, vc_ref,
                 tc_ref, tp_ref, g_ref, o_ref,
                 lhs_ref, kb_ref, vb_ref, s_ref, p_ref, o2_ref, m_ref, tq_ref, bias_ref, acc_ref, *, qb):
    blk = ATTN_BLOCK
    n = pl.program_id(1)
    lo = lax.broadcasted_iota(jnp.int32, (blk, LANES), 1) < HALF

    scale = HEAD_DIM ** -0.5 * LOG2E
    for i in range(3):
        tq_ref[i] = tc_ref[i] * scale
    qi = lax.broadcasted_iota(jnp.int32, (blk, 2 * blk), 0)
    kj = lax.broadcasted_iota(jnp.int32, (blk, 2 * blk), 1)
    rel = qi + blk - kj
    band = (rel >= 0) & (rel < WINDOW)
    kmin = jnp.where(n > 0, 0, blk)
    bias_ref[0] = jnp.where(band & (kj >= kmin), 0.0, NEG)
    bias_ref[1] = jnp.where(band, 0.0, NEG)

    for j in range(qb):
        rows = slice(j * blk, (j + 1) * blk)
        for col in range(ATTN_WIDTH // LANES):
            h, jj = col // 2, col % 2
            q2 = _rope(q_ref[rows, col * LANES:(col + 1) * LANES],
                       tq_ref[0, rows, :], tq_ref[1, rows, :], tq_ref[2, rows, :])
            lhs_ref[j, h, (2 * jj) * blk:(2 * jj + 1) * blk, :] = jnp.where(lo, q2, 0.0).astype(BF16)
            lhs_ref[j, h, (2 * jj + 1) * blk:(2 * jj + 2) * blk, :] = jnp.where(lo, 0.0, q2).astype(BF16)
    nk = (qb + 1) * blk
    lok = lax.broadcasted_iota(jnp.int32, (nk, LANES), 1) < HALF
    ones = jnp.ones((nk, LANES), BF16)
    for i in range(N_KV_HEADS // 2):
        cols = slice(i * LANES, (i + 1) * LANES)
        kcat = jnp.concatenate([_rope(kp_ref[:, cols], tp_ref[0], tp_ref[1], tp_ref[2]),
                                _rope(kc_ref[:, cols], tc_ref[0], tc_ref[1], tc_ref[2])], axis=0)
        vcat = jnp.concatenate([vp_ref[:, cols], vc_ref[:, cols]], axis=0)
        kswp = pltpu.roll(kcat, HALF, 1)
        vswp = pltpu.roll(vcat, HALF, 1)
        kb_ref[2 * i] = jnp.where(lok, kcat, kswp).astype(BF16)
        kb_ref[2 * i + 1] = jnp.where(lok, kswp, kcat).astype(BF16)
        vb_ref[2 * i, :, 0:LANES] = jnp.where(lok, vcat, vswp).astype(BF16)
        vb_ref[2 * i + 1, :, 0:LANES] = jnp.where(lok, vswp, vcat).astype(BF16)
        vb_ref[2 * i, :, LANES:2 * LANES] = ones
        vb_ref[2 * i + 1, :, LANES:2 * LANES] = ones

    def keys(j):
        return slice(j * blk, (j + 2) * blk)

    def scores(j, h):
        s_ref[j, h % 2] = lax.dot_general(lhs_ref[j, h], kb_ref[h, keys(j), :], (((1,), (1,)), ((), ())),
                                          preferred_element_type=F32)

    def row_max(j, h):
        slot = h % 2
        for r in range(Q_PER_KV):
            rows = slice(r * blk, (r + 1) * blk)
            sink = sinks_ref[Q_PER_KV * h + r] * LOG2E
            sr = s_ref[j, slot, rows, :] + bias_ref[min(j, 1)]
            s_ref[j, slot, rows, :] = sr
            m = jnp.maximum(jnp.max(sr, axis=-1, keepdims=True), sink)
            m_ref[j, slot, rows, :] = jnp.broadcast_to(m, (blk, LANES))

    def probs(j, h):
        slot = h % 2
        for r in range(Q_PER_KV):
            rows = slice(r * blk, (r + 1) * blk)
            m = m_ref[j, slot, rows, :]
            for half in range(2):
                ln = slice(half * LANES, (half + 1) * LANES)
                p_ref[j, slot, rows, ln] = jnp.exp2(s_ref[j, slot, rows, ln] - m).astype(BF16)

    def weighted(j, h):
        o2_ref[j, h % 2] = jnp.dot(p_ref[j, h % 2], vb_ref[h, keys(j), :],
                                   preferred_element_type=F32)

    def finish(j, h):
        slot = h % 2
        outs = []
        for r in range(Q_PER_KV):
            rows = slice(r * blk, (r + 1) * blk)
            sink = sinks_ref[Q_PER_KV * h + r] * LOG2E
            den = o2_ref[j, slot, rows, LANES:2 * LANES] + jnp.exp2(sink - m_ref[j, slot, rows, :])
            outs.append(o2_ref[j, slot, rows, 0:LANES] * (1.0 / den))
        for jj in range(2):
            col = 2 * h + jj
            acc_ref[j * blk:(j + 1) * blk, col * LANES:(col + 1) * LANES] = jnp.where(
                lo, outs[2 * jj], outs[2 * jj + 1])

    for j in range(qb):
        scores(j, 0)
    for h in range(N_KV_HEADS + 1):
        for stage in (scores, row_max, probs, weighted, finish):
            hh = {scores: h + 1, finish: h - 1}.get(stage, h)
            if 0 <= hh < N_KV_HEADS:
                for j in range(qb):
                    stage(j, hh)

    o_ref[...] = _rmsnorm_rows(acc_ref[...], g_ref[...]).astype(o_ref.dtype)


def _attention(proj, sinks, g, tables, batch, seq, qb):
    t = proj.shape[0]
    blk = ATTN_BLOCK
    assert seq % (qb * blk) == 0
    nb = seq // blk
    ns = nb // qb
    kblk = K_OFF // KV_WIDTH
    vblk = V_OFF // KV_WIDTH

    def cur(b, n):
        return (b * ns + n, 0)

    def prev_rows(b, n):
        return jnp.maximum(b * nb + n * qb - 1, 0)

    return pl.pallas_call(
        functools.partial(_attn_kernel, qb=qb),
        grid=(batch, ns),
        in_specs=[pl.BlockSpec(memory_space=pltpu.SMEM),
                  pl.BlockSpec((qb * blk, ATTN_WIDTH), cur),
                  pl.BlockSpec((blk, KV_WIDTH), lambda b, n: (prev_rows(b, n), kblk)),
                  pl.BlockSpec((qb * blk, KV_WIDTH), lambda b, n: (b * ns + n, kblk)),
                  pl.BlockSpec((blk, KV_WIDTH), lambda b, n: (prev_rows(b, n), vblk)),
                  pl.BlockSpec((qb * blk, KV_WIDTH), lambda b, n: (b * ns + n, vblk)),
                  pl.BlockSpec((3, qb * blk, LANES), lambda b, n: (0, n, 0)),
                  pl.BlockSpec((3, blk, LANES), lambda b, n: (0, jnp.maximum(n * qb - 1, 0), 0)),
                  pl.BlockSpec((1, ATTN_WIDTH), lambda b, n: (0, 0))],
        out_specs=pl.BlockSpec((qb * blk, ATTN_WIDTH), cur),
        out_shape=jax.ShapeDtypeStruct((t, ATTN_WIDTH), BF16),
        scratch_shapes=[pltpu.VMEM((qb, N_KV_HEADS, Q_PER_KV * blk, LANES), BF16),
                        pltpu.VMEM((N_KV_HEADS, (qb + 1) * blk, LANES), BF16),
                        pltpu.VMEM((N_KV_HEADS, (qb + 1) * blk, 2 * LANES), BF16),
                        pltpu.VMEM((qb, 2, Q_PER_KV * blk, 2 * blk), F32),
                        pltpu.VMEM((qb, 2, Q_PER_KV * blk, 2 * blk), BF16),
                        pltpu.VMEM((qb, 2, Q_PER_KV * blk, 2 * LANES), F32),
                        pltpu.VMEM((qb, 2, Q_PER_KV * blk, LANES), F32),
                        pltpu.VMEM((3, qb * blk, LANES), F32),
                        pltpu.VMEM((2, blk, 2 * blk), F32),
                        pltpu.VMEM((qb * blk, ATTN_WIDTH), F32)],
        compiler_params=_params(2),
        name="swa_attention",
    )(sinks, proj, proj, proj, proj, proj, tables, tables, g)


def _split3(x):
    h = x.astype(BF16)
    r = x - h.astype(F32)
    m = r.astype(BF16)
    l = (r - m.astype(F32)).astype(BF16)
    return h, m, l


def _dot3_lhs(x, w):
    h, m, l = _split3(x)
    d = lambda a: jnp.dot(a, w, preferred_element_type=F32)
    return (d(l) + d(m)) + d(h)


def _dot3_rhs(w, x):
    h, m, l = _split3(x)
    d = lambda a: jnp.dot(w, a, preferred_element_type=F32)
    return (d(l) + d(m)) + d(h)


def _ssd_kernel(xbc_ref, z_ref, dt_ref, dtt_ref, cw_ref, cb_ref, brow_ref, bcol_ref,
                alrow_ref, alcol_ref, e_ref, dfull_ref, gn_ref, o_ref,
                ext_ref, xact_ref, state_ref, xdt_ref, xdec_ref, y_ref, exp_ref, acs_ref, acst_ref, *, nbat):
    L = SSD_CHUNK
    c = pl.program_id(0)
    cw_chunk = 512
    gw = SSD_HEAD_DIM * (SSD_HEADS // SSD_GROUPS)
    ri = lax.broadcasted_iota(jnp.int32, (L, L), 0)
    ci = lax.broadcasted_iota(jnp.int32, (L, L), 1)
    causal = ri >= ci
    lo = ci < HALF

    @pl.when(c == 0)
    def _():
        ext_ref[:, 0:SUBLANES, :] = jnp.zeros((nbat, SUBLANES, CONV_CH), F32)
        state_ref[...] = jnp.zeros_like(state_ref)

    def conv(b):
        ext_ref[b, SUBLANES:SUBLANES + L, :] = xbc_ref[b]
        for j in range(CONV_CH // cw_chunk):
            cs = slice(j * cw_chunk, (j + 1) * cw_chunk)
            blk = ext_ref[b, :, cs]
            acc = cb_ref[:, cs]
            for k in range(SSD_CONV - 1):
                acc = acc + _delay_rows(blk, SSD_CONV - 1 - k) * cw_ref[k:k + 1, cs]
            acc = acc + blk[SUBLANES:] * cw_ref[SSD_CONV - 1:SSD_CONV, cs]
            xact_ref[b, :, cs] = acc * _sigmoid(acc)
        ext_ref[b, 0:SUBLANES, :] = ext_ref[b, L:L + SUBLANES, :]

    def decay(b):
        dt = _softplus(dt_ref[b] + brow_ref[...])
        dtt = _softplus(dtt_ref[b] + bcol_ref[...])
        da = dt * (-jnp.exp(alrow_ref[...]))
        dat = dtt * (-jnp.exp(alcol_ref[...]))
        tri_l = jnp.where(causal, 1.0, 0.0).astype(BF16)
        tri_u = jnp.where(ri <= ci, 1.0, 0.0).astype(BF16)
        a_cs = _dot3_rhs(tri_l, da)
        acs_ref[b] = a_cs
        acst_ref[b] = _dot3_lhs(dat, tri_u)
        a_last = a_cs[L - 1:L, :]
        stack = jnp.concatenate([dt, jnp.exp(a_last - a_cs), jnp.exp(a_cs),
                                 jnp.broadcast_to(jnp.exp(a_last), (SUBLANES, LANES))], axis=0)
        sh, sm, sl = _split3(stack)
        for j in range(SSD_INNER // cw_chunk):
            cs = slice(j * cw_chunk, (j + 1) * cw_chunk)
            ej = e_ref[:, cs]
            d = lambda a: jnp.dot(a, ej, preferred_element_type=F32)
            exp_ref[b, :, cs] = (d(sl) + d(sm)) + d(sh)

    def scale_x(b):
        for j in range(SSD_INNER // cw_chunk):
            cs = slice(j * cw_chunk, (j + 1) * cw_chunk)
            xdt = xact_ref[b, :, cs] * exp_ref[b, 0:L, cs]
            xdt_ref[b, :, cs] = xdt.astype(BF16)
            xdec_ref[b, :, cs] = (xdt * exp_ref[b, L:2 * L, cs]).astype(BF16)

    def group(b, g):
        bg = xact_ref[b, :, SSD_INNER + g * SSD_STATE:SSD_INNER + (g + 1) * SSD_STATE]
        cg = xact_ref[b, :, SSD_INNER + BC_WIDTH + g * SSD_STATE:SSD_INNER + BC_WIDTH + (g + 1) * SSD_STATE]
        bb = bg.astype(BF16)
        cbf = cg.astype(BF16)
        cbm = lax.dot_general(cbf, bb, (((1,), (1,)), ((), ())), preferred_element_type=F32)
        gs = slice(g * gw, (g + 1) * gw)
        prev = state_ref[b, g]
        yoff = jnp.dot(cbf, prev.astype(BF16), preferred_element_type=F32) * exp_ref[b, 2 * L:3 * L, gs]
        btb = bg.T.astype(BF16)
        state_ref[b, g] = prev * exp_ref[b, 3 * L:3 * L + 1, gs] + jnp.dot(
            btb, xdec_ref[b, :, gs], preferred_element_type=F32)
        for e in range(2):
            pc = slice((2 * g + e) * LANES, (2 * g + e + 1) * LANES)
            xpair = xdt_ref[b, :, pc]
            yd = []
            for r in range(2):
                h = 4 * g + 2 * e + r
                seg = acs_ref[b, :, h:h + 1] - acst_ref[b, h:h + 1, :]
                lm = jnp.exp(jnp.where(causal, seg, NEG))
                yd.append(jnp.dot((cbm * lm).astype(BF16), xpair, preferred_element_type=F32))
            y_ref[b, :, pc] = (jnp.where(lo, yd[0], yd[1]) + yoff[:, e * LANES:(e + 1) * LANES]
                               + dfull_ref[:, pc] * xact_ref[b, :, pc])

    def gate_norm(b, g):
        gs = slice(g * gw, (g + 1) * gw)
        zz = z_ref[b, :, gs]
        yg = y_ref[b, :, gs] * (zz * _sigmoid(zz))
        ms = jnp.mean(yg * yg, axis=-1, keepdims=True)
        o_ref[b, :, gs] = (yg * lax.rsqrt(ms + EPS) * gn_ref[:, gs]).astype(o_ref.dtype)

    for stage in (conv, decay, scale_x):
        for b in range(nbat):
            stage(b)
    for stage in (group, gate_norm):
        for g in range(SSD_GROUPS):
            for b in range(nbat):
                stage(b, g)


def _ssd(proj, dt_raw, dt_rawt, cw, cb, brow, bcol, alrow, alcol, emat, dfull, gn, batch, seq):
    L = SSD_CHUNK
    nc = seq // L
    proj3 = proj.reshape(batch, seq, MAIN_PROJ)
    full = lambda shape: pl.BlockSpec(shape, lambda c: (0, 0))
    out = pl.pallas_call(
        functools.partial(_ssd_kernel, nbat=batch),
        grid=(nc,),
        in_specs=[pl.BlockSpec((batch, L, CONV_CH), lambda c: (0, c, XBC_OFF // CONV_CH)),
                  pl.BlockSpec((batch, L, SSD_INNER), lambda c: (0, c, Z_OFF // SSD_INNER)),
                  pl.BlockSpec((batch, L, LANES), lambda c: (0, c, 0)),
                  pl.BlockSpec((batch, LANES, L), lambda c: (0, 0, c)),
                  full((SSD_CONV, CONV_CH)), full((1, CONV_CH)),
                  full((1, LANES)), full((LANES, 1)), full((1, LANES)), full((LANES, 1)),
                  full((LANES, SSD_INNER)), full((1, SSD_INNER)), full((1, SSD_INNER))],
        out_specs=pl.BlockSpec((batch, L, SSD_INNER), lambda c: (0, c, 0)),
        out_shape=jax.ShapeDtypeStruct((batch, seq, SSD_INNER), BF16),
        scratch_shapes=[pltpu.VMEM((batch, L + SUBLANES, CONV_CH), F32),
                        pltpu.VMEM((batch, L, CONV_CH), F32),
                        pltpu.VMEM((batch, SSD_GROUPS, SSD_STATE, SSD_INNER // SSD_GROUPS), F32),
                        pltpu.VMEM((batch, L, SSD_INNER), BF16),
                        pltpu.VMEM((batch, L, SSD_INNER), BF16),
                        pltpu.VMEM((batch, L, SSD_INNER), F32),
                        pltpu.VMEM((batch, 3 * L + SUBLANES, SSD_INNER), F32),
                        pltpu.VMEM((batch, L, LANES), F32),
                        pltpu.VMEM((batch, LANES, L), F32)],
        compiler_params=_params(1),
        name="ssd_scan",
    )(proj3, proj3, dt_raw.reshape(batch, seq, LANES), dt_rawt, cw, cb, brow, bcol, alrow, alcol,
      emat, dfull, gn)
    return out.reshape(batch * seq, SSD_INNER)


def _out_proj_kernel(a1_ref, a2_ref, w1_ref, w2_ref, r_ref, o_ref, w1b_ref, w2b_ref):
    @pl.when(pl.program_id(1) == 0)
    def _():
        _cast_rows(w1_ref, w1b_ref)
        _cast_rows(w2_ref, w2b_ref)

    acc = jnp.dot(a1_ref[...], w1b_ref[...], preferred_element_type=F32)
    acc = acc + jnp.dot(a2_ref[...], w2b_ref[...], preferred_element_type=F32)
    o_ref[...] = r_ref[...] + acc


def _out_proj(a1, a2, w, res, tm, tn):
    t, k = a1.shape
    n = w.shape[1]
    return pl.pallas_call(
        _out_proj_kernel,
        grid=(n // tn, t // tm),
        in_specs=[pl.BlockSpec((tm, k), lambda j, m: (m, 0)),
                  pl.BlockSpec((tm, k), lambda j, m: (m, 0)),
                  pl.BlockSpec((k, tn), lambda j, m: (0, j)),
                  pl.BlockSpec((k, tn), lambda j, m: (1, j)),
                  pl.BlockSpec((tm, tn), lambda j, m: (m, j))],
        out_specs=pl.BlockSpec((tm, tn), lambda j, m: (m, j)),
        out_shape=jax.ShapeDtypeStruct((t, n), F32),
        scratch_shapes=[pltpu.VMEM((k, tn), BF16), pltpu.VMEM((k, tn), BF16)],
        compiler_params=_params(2),
        name="out_proj",
    )(a1, a2, w, w, res)


def _up_kernel(a_ref, wg_ref, wv_ref, cwg_ref, cwv_ref, cbg_ref, cbv_ref, o_ref,
               wgb_ref, wvb_ref, carry_ref, *, tm, seq):
    m = pl.program_id(1)

    @pl.when(m == 0)
    def _():
        _cast_rows(wg_ref, wgb_ref)
        _cast_rows(wv_ref, wvb_ref)

    a = a_ref[...]
    seq_start = (m * tm) % seq == 0
    outs = []
    for idx, (wb_ref, cw_ref, cb_ref) in enumerate(((wgb_ref, cwg_ref, cbg_ref),
                                                    (wvb_ref, cwv_ref, cbv_ref))):
        u = jnp.dot(a, wb_ref[...], preferred_element_type=F32)
        above = jnp.where(seq_start, 0.0, carry_ref[idx])
        blk = jnp.concatenate([above, u], axis=0)
        cw = cw_ref[...]
        conv = cb_ref[...]
        for k in range(FFN_CONV - 1):
            conv = conv + _delay_rows(blk, FFN_CONV - 1 - k) * cw[k:k + 1, :]
        conv = conv + u * cw[FFN_CONV - 1:FFN_CONV, :]
        carry_ref[idx] = u[tm - SUBLANES:tm]
        outs.append(conv)
    gate, val = outs
    o_ref[...] = ((gate * _sigmoid(gate)) * val).astype(o_ref.dtype)


def _up_proj(a, w, cw, cb, tm, tn, seq):
    t, k = a.shape
    assert seq % tm == 0 and D_FF % tn == 0
    nb = D_FF // tn
    return pl.pallas_call(
        functools.partial(_up_kernel, tm=tm, seq=seq),
        grid=(nb, t // tm),
        in_specs=[pl.BlockSpec((tm, k), lambda j, m: (m, 0)),
                  pl.BlockSpec((k, tn), lambda j, m: (0, j)),
                  pl.BlockSpec((k, tn), lambda j, m: (0, j + nb)),
                  pl.BlockSpec((FFN_CONV, tn), lambda j, m: (0, j)),
                  pl.BlockSpec((FFN_CONV, tn), lambda j, m: (0, j + nb)),
                  pl.BlockSpec((1, tn), lambda j, m: (0, j)),
                  pl.BlockSpec((1, tn), lambda j, m: (0, j + nb))],
        out_specs=pl.BlockSpec((tm, tn), lambda j, m: (m, j)),
        out_shape=jax.ShapeDtypeStruct((t, D_FF), BF16),
        scratch_shapes=[pltpu.VMEM((k, tn), BF16), pltpu.VMEM((k, tn), BF16),
                        pltpu.VMEM((2, SUBLANES, tn), F32)],
        compiler_params=_params(2),
        name="up_proj_conv_swiglu",
    )(a, w, w, cw, cw, cb, cb)


def _down_kernel(a_ref, w_ref, r_ref, o_ref, wb_ref):
    @pl.when(pl.program_id(1) == 0)
    def _():
        _cast_rows(w_ref, wb_ref)

    o_ref[...] = r_ref[...] + jnp.dot(a_ref[...], wb_ref[...], preferred_element_type=F32)


def _down_proj(a, w, res, tm, tn):
    t, k = a.shape
    n = w.shape[1]
    return pl.pallas_call(
        _down_kernel,
        grid=(n // tn, t // tm),
        in_specs=[pl.BlockSpec((tm, k), lambda j, m: (m, 0)),
                  pl.BlockSpec((k, tn), lambda j, m: (0, j)),
                  pl.BlockSpec((tm, tn), lambda j, m: (m, j))],
        out_specs=pl.BlockSpec((tm, tn), lambda j, m: (m, j)),
        out_shape=jax.ShapeDtypeStruct((t, n), F32),
        scratch_shapes=[pltpu.VMEM((k, tn), BF16)],
        compiler_params=_params(2),
        name="down_proj",
    )(a, w, res)


def _pad_lanes(v):
    return jnp.pad(v.astype(F32), (0, LANES - v.shape[0]))


def _mixer_layer(h, batch, seq, norm_mix, w_in, sinks, attn_out_norm, ssd_conv_w, ssd_conv_b, dt_bias,
                 a_log, ssd_d, ssd_norm, w_out, tables, emat):
    w_in_t = jnp.swapaxes(w_in, 0, 1)
    wdt_t = jnp.pad(w_in_t[MAIN_PROJ:], ((0, LANES - SSD_HEADS), (0, 0))).astype(BF16)
    xn, dt_raw, dt_rawt = _norm_dt(h, norm_mix.reshape(1, -1), wdt_t.T, wdt_t, 512, batch, seq)
    proj = _in_proj(xn, w_in_t, tm=1024, tn=1024)
    attn = _attention(proj, sinks.astype(F32), attn_out_norm.reshape(1, -1), tables, batch, seq, qb=4)
    bias = _pad_lanes(dt_bias)
    alog = _pad_lanes(a_log)
    y = _ssd(proj, dt_raw, dt_rawt, ssd_conv_w, ssd_conv_b.reshape(1, -1),
             bias.reshape(1, -1), bias.reshape(-1, 1), alog.reshape(1, -1), alog.reshape(-1, 1),
             emat, jnp.repeat(ssd_d.astype(F32), SSD_HEAD_DIM).reshape(1, -1),
             ssd_norm.reshape(1, -1), batch, seq)
    return _out_proj(attn, y, w_out, h, tm=1024, tn=512)


def kernel(x, norm_mix, w_in, sinks, attn_out_norm, ssd_conv_w, ssd_conv_b, dt_bias, a_log, ssd_d, ssd_norm,
           w_out, norm_ffn, w_up, ffn_conv_w, ffn_conv_b, w_down, norm_final):
    batch, seq, d = x.shape
    h = x.reshape(batch * seq, d)
    tables = _rope_tables(seq)
    head_of_channel = np.arange(SSD_INNER) // SSD_HEAD_DIM
    emat = jnp.asarray(np.arange(LANES)[:, None] == head_of_channel[None, :], dtype=BF16)
    for l in range(norm_mix.shape[0]):
        h = _mixer_layer(h, batch, seq, norm_mix[l], w_in[l], sinks[l], attn_out_norm[l], ssd_conv_w[l],
                         ssd_conv_b[l], dt_bias[l], a_log[l], ssd_d[l], ssd_norm[l], w_out[l], tables, emat)
        hn = _norm(h, norm_ffn[l].reshape(1, -1), BF16, 512, "norm_ffn")
        act = _up_proj(hn, w_up[l], ffn_conv_w[l], ffn_conv_b[l].reshape(1, -1),
                       tm=1024, tn=512, seq=seq)
        h = _down_proj(act, w_down[l], h, tm=512, tn=512)
    out = _norm(h, norm_final.reshape(1, -1), F32, 512, "norm_final")
    return out.reshape(batch, seq, d)
```

```python
import functools

import numpy as np
import jax
import jax.numpy as jnp
from jax import lax
from jax.experimental import pallas as pl
from jax.experimental.pallas import tpu as pltpu

F32 = jnp.float32
BF16 = jnp.bfloat16

D_MODEL = 2048
N_Q_HEADS = 32
N_KV_HEADS = 8
HEAD_DIM = 64
Q_PER_KV = N_Q_HEADS // N_KV_HEADS
WINDOW = 128
ATTN_BLOCK = 128
ROT_DIM = HEAD_DIM // 4
ROPE_THETA = 500000.0
SSD_HEADS = 32
SSD_HEAD_DIM = 64
SSD_INNER = SSD_HEADS * SSD_HEAD_DIM
SSD_GROUPS = 8
SSD_STATE = 128
SSD_CONV = 4
SSD_CHUNK = 128
ATTN_WIDTH = N_Q_HEADS * HEAD_DIM
KV_WIDTH = N_KV_HEADS * HEAD_DIM
BC_WIDTH = SSD_GROUPS * SSD_STATE
CONV_CH = SSD_INNER + 2 * BC_WIDTH
MAIN_PROJ = ATTN_WIDTH + 2 * KV_WIDTH + SSD_INNER + CONV_CH
D_FF = 5632
FFN_CONV = 3
EPS = 1e-6

LANES = 128
SUBLANES = 8
HALF = LANES // 2
NEG = -1e30
LOG2E = 1.4426950408889634
VMEM_LIMIT = 56 * 1024 * 1024

Q_OFF = 0
Z_OFF = ATTN_WIDTH
XBC_OFF = Z_OFF + SSD_INNER
K_OFF = XBC_OFF + CONV_CH
V_OFF = K_OFF + KV_WIDTH


def _params(n_axes, flags=None):
    return pltpu.CompilerParams(dimension_semantics=("arbitrary",) * n_axes,
                                vmem_limit_bytes=VMEM_LIMIT, flags=flags)


def _sigmoid(x):
    return 1.0 / (1.0 + jnp.exp2(x * -LOG2E))


def _softplus(x):
    return jnp.maximum(x, 0.0) + jnp.log1p(jnp.exp(-jnp.abs(x)))


def _cast_rows(src_ref, dst_ref, rows=256):
    k = src_ref.shape[0]
    for r in range(0, k, rows):
        dst_ref[r:r + rows, :] = src_ref[r:r + rows, :].astype(BF16)


def _delay_rows(blk, sh):
    n, w = blk.shape[0] - SUBLANES, blk.shape[1]
    g = blk.reshape(n // SUBLANES + 1, SUBLANES, w)
    rot = pltpu.roll(g, sh, 1)
    row = lax.broadcasted_iota(jnp.int32, (SUBLANES, w), 0)
    out = jnp.where(row < sh, rot[:-1], rot[1:])
    return out.reshape(n, w)


def _rmsnorm_rows(x, g):
    ms = jnp.mean(x * x, axis=-1, keepdims=True)
    return x * lax.rsqrt(ms + EPS) * g


def _norm_dt_kernel(x_ref, g_ref, wdt_ref, wdtt_ref, xn_ref, dt_ref, dtt_ref):
    xn = _rmsnorm_rows(x_ref[...], g_ref[...]).astype(BF16)
    xn_ref[...] = xn
    dt_ref[...] = jnp.dot(xn, wdt_ref[...], preferred_element_type=F32)
    dtt_ref[...] = lax.dot_general(wdtt_ref[...], xn, (((1,), (1,)), ((), ())),
                                   preferred_element_type=F32)


def _norm_dt(x2, g, wdt, wdtt, tm, batch, seq):
    t, d = x2.shape
    assert seq % tm == 0
    per_seq = seq // tm
    return pl.pallas_call(
        _norm_dt_kernel,
        grid=(t // tm,),
        in_specs=[pl.BlockSpec((tm, d), lambda i: (i, 0)),
                  pl.BlockSpec((1, d), lambda i: (0, 0)),
                  pl.BlockSpec((d, LANES), lambda i: (0, 0)),
                  pl.BlockSpec((LANES, d), lambda i: (0, 0))],
        out_specs=[pl.BlockSpec((tm, d), lambda i: (i, 0)),
                   pl.BlockSpec((tm, LANES), lambda i: (i, 0)),
                   pl.BlockSpec((None, LANES, tm), lambda i: (i // per_seq, 0, i % per_seq))],
        out_shape=[jax.ShapeDtypeStruct((t, d), BF16),
                   jax.ShapeDtypeStruct((t, LANES), F32),
                   jax.ShapeDtypeStruct((batch, LANES, seq), F32)],
        compiler_params=_params(1),
        name="norm_dt",
    )(x2, g, wdt, wdtt)


def _norm_kernel(x_ref, g_ref, o_ref):
    o_ref[...] = _rmsnorm_rows(x_ref[...], g_ref[...]).astype(o_ref.dtype)


def _norm(x2, g, out_dtype, tm, name):
    t, d = x2.shape
    return pl.pallas_call(
        _norm_kernel,
        grid=(t // tm,),
        in_specs=[pl.BlockSpec((tm, d), lambda i: (i, 0)),
                  pl.BlockSpec((1, d), lambda i: (0, 0))],
        out_specs=pl.BlockSpec((tm, d), lambda i: (i, 0)),
        out_shape=jax.ShapeDtypeStruct((t, d), out_dtype),
        compiler_params=_params(1),
        name=name,
    )(x2, g)


def _in_proj_kernel(a_ref, wt_ref, o_ref, wbf_ref):
    @pl.when(pl.program_id(1) == 0)
    def _():
        _cast_rows(wt_ref, wbf_ref)

    o_ref[...] = lax.dot_general(a_ref[...], wbf_ref[...], (((1,), (1,)), ((), ())),
                                 preferred_element_type=F32)


def _in_proj(xn, wt, tm, tn):
    t, k = xn.shape
    u = 1024 // tn

    def wmap(j, m):
        return (jnp.where(j < 2 * u, j, jnp.where(j < 8 * u, j + u, j - 6 * u)), 0)

    return pl.pallas_call(
        _in_proj_kernel,
        grid=(MAIN_PROJ // tn, t // tm),
        in_specs=[pl.BlockSpec((tm, k), lambda j, m: (m, 0)),
                  pl.BlockSpec((tn, k), wmap)],
        out_specs=pl.BlockSpec((tm, tn), lambda j, m: (m, j)),
        out_shape=jax.ShapeDtypeStruct((t, MAIN_PROJ), F32),
        scratch_shapes=[pltpu.VMEM((tn, k), BF16)],
        compiler_params=_params(2),
        name="in_proj",
    )(xn, wt)


def _rope_tables(seq):
    half = ROT_DIM // 2
    inv = 1.0 / (ROPE_THETA ** (jnp.arange(0, ROT_DIM, 2, dtype=F32) / ROT_DIM))
    ang = jnp.arange(seq, dtype=F32)[:, None] * inv[None, :]
    cos, sin = jnp.cos(ang), jnp.sin(ang)
    d = np.arange(LANES) % HEAD_DIM
    idx = d % half
    in_rot = jnp.asarray(d < ROT_DIM)
    first = jnp.asarray(d < half)
    second = jnp.asarray((d >= half) & (d < ROT_DIM))
    c = jnp.where(in_rot[None, :], cos[:, idx], 1.0)
    s1 = jnp.where(first[None, :], -sin[:, idx], 0.0)
    s2 = jnp.where(second[None, :], sin[:, idx], 0.0)
    return jnp.stack([c, s1, s2]).astype(F32)


def _rope(x, c, s1, s2):
    half = ROT_DIM // 2
    return x * c + pltpu.roll(x, LANES - half, 1) * s1 + pltpu.roll(x, half, 1) * s2


def _attn_kernel(sinks_ref, q_ref, kp_ref, kc_ref, vp_ref, vc_ref,
                 tc_ref, tp_ref, g_ref, o_ref,
                 lhs_ref, kb_ref, vb_ref, s_ref, p_ref, o2_ref, m_ref, tq_ref, bias_ref, acc_ref, *, qb):
    blk = ATTN_BLOCK
    n = pl.program_id(1)
    lo = lax.broadcasted_iota(jnp.int32, (blk, LANES), 1) < HALF

    scale = HEAD_DIM ** -0.5 * LOG2E
    for i in range(3):
        tq_ref[i] = tc_ref[i] * scale
    qi = lax.broadcasted_iota(jnp.int32, (blk, 2 * blk), 0)
    kj = lax.broadcasted_iota(jnp.int32, (blk, 2 * blk), 1)
    rel = qi + blk - kj
    band = (rel >= 0) & (rel < WINDOW)
    kmin = jnp.where(n > 0, 0, blk)
    bias_ref[0] = jnp.where(band & (kj >= kmin), 0.0, NEG)
    bias_ref[1] = jnp.where(band, 0.0, NEG)

    for j in range(qb):
        rows = slice(j * blk, (j + 1) * blk)
        for col in range(ATTN_WIDTH // LANES):
            h, jj = col // 2, col % 2
            q2 = _rope(q_ref[rows, col * LANES:(col + 1) * LANES],
                       tq_ref[0, rows, :], tq_ref[1, rows, :], tq_ref[2, rows, :])
            lhs_ref[j, h, (2 * jj) * blk:(2 * jj + 1) * blk, :] = jnp.where(lo, q2, 0.0).astype(BF16)
            lhs_ref[j, h, (2 * jj + 1) * blk:(2 * jj + 2) * blk, :] = jnp.where(lo, 0.0, q2).astype(BF16)
    nk = (qb + 1) * blk
    lok = lax.broadcasted_iota(jnp.int32, (nk, LANES), 1) < HALF
    ones = jnp.ones((nk, LANES), BF16)
    for i in range(N_KV_HEADS // 2):
        cols = slice(i * LANES, (i + 1) * LANES)
        kcat = jnp.concatenate([_rope(kp_ref[:, cols], tp_ref[0], tp_ref[1], tp_ref[2]),
                                _rope(kc_ref[:, cols], tc_ref[0], tc_ref[1], tc_ref[2])], axis=0)
        vcat = jnp.concatenate([vp_ref[:, cols], vc_ref[:, cols]], axis=0)
        kswp = pltpu.roll(kcat, HALF, 1)
        vswp = pltpu.roll(vcat, HALF, 1)
        kb_ref[2 * i] = jnp.where(lok, kcat, kswp).astype(BF16)
        kb_ref[2 * i + 1] = jnp.where(lok, kswp, kcat).astype(BF16)
        vb_ref[2 * i, :, 0:LANES] = jnp.where(lok, vcat, vswp).astype(BF16)
        vb_ref[2 * i + 1, :, 0:LANES] = jnp.where(lok, vswp, vcat).astype(BF16)
        vb_ref[2 * i, :, LANES:2 * LANES] = ones
        vb_ref[2 * i + 1, :, LANES:2 * LANES] = ones

    def keys(j):
        return slice(j * blk, (j + 2) * blk)

    def scores(j, h):
        s_ref[j, h % 2] = lax.dot_general(lhs_ref[j, h], kb_ref[h, keys(j), :], (((1,), (1,)), ((), ())),
                                          preferred_element_type=F32)

    def row_max(j, h):
        slot = h % 2
        for r in range(Q_PER_KV):
            rows = slice(r * blk, (r + 1) * blk)
            sink = sinks_ref[Q_PER_KV * h + r] * LOG2E
            sr = s_ref[j, slot, rows, :] + bias_ref[min(j, 1)]
            s_ref[j, slot, rows, :] = sr
            m = jnp.maximum(jnp.max(sr, axis=-1, keepdims=True), sink)
            m_ref[j, slot, rows, :] = jnp.broadcast_to(m, (blk, LANES))

    def probs(j, h):
        slot = h % 2
        for r in range(Q_PER_KV):
            rows = slice(r * blk, (r + 1) * blk)
            m = m_ref[j, slot, rows, :]
            for half in range(2):
                ln = slice(half * LANES, (half + 1) * LANES)
                p_ref[j, slot, rows, ln] = jnp.exp2(s_ref[j, slot, rows, ln] - m).astype(BF16)

    def weighted(j, h):
        o2_ref[j, h % 2] = jnp.dot(p_ref[j, h % 2], vb_ref[h, keys(j), :],
                                   preferred_element_type=F32)

    def finish(j, h):
        slot = h % 2
        outs = []
        for r in range(Q_PER_KV):
            rows = slice(r * blk, (r + 1) * blk)
            sink = sinks_ref[Q_PER_KV * h + r] * LOG2E
            den = o2_ref[j, slot, rows, LANES:2 * LANES] + jnp.exp2(sink - m_ref[j, slot, rows, :])
            outs.append(o2_ref[j, slot, rows, 0:LANES] * (1.0 / den))
        for jj in range(2):
            col = 2 * h + jj
            acc_ref[j * blk:(j + 1) * blk, col * LANES:(col + 1) * LANES] = jnp.where(
                lo, outs[2 * jj], outs[2 * jj + 1])

    for j in range(qb):
        scores(j, 0)
    for h in range(N_KV_HEADS + 1):
        for stage in (scores, row_max, probs, weighted, finish):
            hh = {scores: h + 1, finish: h - 1}.get(stage, h)
            if 0 <= hh < N_KV_HEADS:
                for j in range(qb):
                    stage(j, hh)

    o_ref[...] = _rmsnorm_rows(acc_ref[...], g_ref[...]).astype(o_ref.dtype)


def _attention(proj, sinks, g, tables, batch, seq, qb):
    t = proj.shape[0]
    blk = ATTN_BLOCK
    assert seq % (qb * blk) == 0
    nb = seq // blk
    ns = nb // qb
    kblk = K_OFF // KV_WIDTH
    vblk = V_OFF // KV_WIDTH

    def cur(b, n):
        return (b * ns + n, 0)

    def prev_rows(b, n):
        return jnp.maximum(b * nb + n * qb - 1, 0)

    return pl.pallas_call(
        functools.partial(_attn_kernel, qb=qb),
        grid=(batch, ns),
        in_specs=[pl.BlockSpec(memory_space=pltpu.SMEM),
                  pl.BlockSpec((qb * blk, ATTN_WIDTH), cur),
                  pl.BlockSpec((blk, KV_WIDTH), lambda b, n: (prev_rows(b, n), kblk)),
                  pl.BlockSpec((qb * blk, KV_WIDTH), lambda b, n: (b * ns + n, kblk)),
                  pl.BlockSpec((blk, KV_WIDTH), lambda b, n: (prev_rows(b, n), vblk)),
                  pl.BlockSpec((qb * blk, KV_WIDTH), lambda b, n: (b * ns + n, vblk)),
                  pl.BlockSpec((3, qb * blk, LANES), lambda b, n: (0, n, 0)),
                  pl.BlockSpec((3, blk, LANES), lambda b, n: (0, jnp.maximum(n * qb - 1, 0), 0)),
                  pl.BlockSpec((1, ATTN_WIDTH), lambda b, n: (0, 0))],
        out_specs=pl.BlockSpec((qb * blk, ATTN_WIDTH), cur),
        out_shape=jax.ShapeDtypeStruct((t, ATTN_WIDTH), BF16),
        scratch_shapes=[pltpu.VMEM((qb, N_KV_HEADS, Q_PER_KV * blk, LANES), BF16),
                        pltpu.VMEM((N_KV_HEADS, (qb + 1) * blk, LANES), BF16),
                        pltpu.VMEM((N_KV_HEADS, (qb + 1) * blk, 2 * LANES), BF16),
                        pltpu.VMEM((qb, 2, Q_PER_KV * blk, 2 * blk), F32),
                        pltpu.VMEM((qb, 2, Q_PER_KV * blk, 2 * blk), BF16),
                        pltpu.VMEM((qb, 2, Q_PER_KV * blk, 2 * LANES), F32),
                        pltpu.VMEM((qb, 2, Q_PER_KV * blk, LANES), F32),
                        pltpu.VMEM((3, qb * blk, LANES), F32),
                        pltpu.VMEM((2, blk, 2 * blk), F32),
                        pltpu.VMEM((qb * blk, ATTN_WIDTH), F32)],
        compiler_params=_params(2),
        name="swa_attention",
    )(sinks, proj, proj, proj, proj, proj, tables, tables, g)


def _split3(x):
    h = x.astype(BF16)
    r = x - h.astype(F32)
    m = r.astype(BF16)
    l = (r - m.astype(F32)).astype(BF16)
    return h, m, l


def _dot3_lhs(x, w):
    h, m, l = _split3(x)
    d = lambda a: jnp.dot(a, w, preferred_element_type=F32)
    return (d(l) + d(m)) + d(h)


def _dot3_rhs(w, x):
    h, m, l = _split3(x)
    d = lambda a: jnp.dot(w, a, preferred_element_type=F32)
    return (d(l) + d(m)) + d(h)


def _ssd_kernel(xbc_ref, z_ref, dt_ref, dtt_ref, cw_ref, cb_ref, brow_ref, bcol_ref,
                alrow_ref, alcol_ref, e_ref, dfull_ref, gn_ref, o_ref,
                ext_ref, xact_ref, state_ref, xdt_ref, xdec_ref, y_ref, exp_ref, acs_ref, acst_ref, *, nbat):
    L = SSD_CHUNK
    c = pl.program_id(0)
    cw_chunk = 512
    gw = SSD_HEAD_DIM * (SSD_HEADS // SSD_GROUPS)
    ri = lax.broadcasted_iota(jnp.int32, (L, L), 0)
    ci = lax.broadcasted_iota(jnp.int32, (L, L), 1)
    causal = ri >= ci
    lo = ci < HALF

    @pl.when(c == 0)
    def _():
        ext_ref[:, 0:SUBLANES, :] = jnp.zeros((nbat, SUBLANES, CONV_CH), F32)
        state_ref[...] = jnp.zeros_like(state_ref)

    def conv(b):
        ext_ref[b, SUBLANES:SUBLANES + L, :] = xbc_ref[b]
        for j in range(CONV_CH // cw_chunk):
            cs = slice(j * cw_chunk, (j + 1) * cw_chunk)
            blk = ext_ref[b, :, cs]
            acc = cb_ref[:, cs]
            for k in range(SSD_CONV - 1):
                acc = acc + _delay_rows(blk, SSD_CONV - 1 - k) * cw_ref[k:k + 1, cs]
            acc = acc + blk[SUBLANES:] * cw_ref[SSD_CONV - 1:SSD_CONV, cs]
            xact_ref[b, :, cs] = acc * _sigmoid(acc)
        ext_ref[b, 0:SUBLANES, :] = ext_ref[b, L:L + SUBLANES, :]

    def decay(b):
        dt = _softplus(dt_ref[b] + brow_ref[...])
        dtt = _softplus(dtt_ref[b] + bcol_ref[...])
        da = dt * (-jnp.exp(alrow_ref[...]))
        dat = dtt * (-jnp.exp(alcol_ref[...]))
        tri_l = jnp.where(causal, 1.0, 0.0).astype(BF16)
        tri_u = jnp.where(ri <= ci, 1.0, 0.0).astype(BF16)
        a_cs = _dot3_rhs(tri_l, da)
        acs_ref[b] = a_cs
        acst_ref[b] = _dot3_lhs(dat, tri_u)
        a_last = a_cs[L - 1:L, :]
        stack = jnp.concatenate([dt, jnp.exp(a_last - a_cs), jnp.exp(a_cs),
                                 jnp.broadcast_to(jnp.exp(a_last), (SUBLANES, LANES))], axis=0)
        sh, sm, sl = _split3(stack)
        for j in range(SSD_INNER // cw_chunk):
            cs = slice(j * cw_chunk, (j + 1) * cw_chunk)
            ej = e_ref[:, cs]
            d = lambda a: jnp.dot(a, ej, preferred_element_type=F32)
            exp_ref[b, :, cs] = (d(sl) + d(sm)) + d(sh)

    def scale_x(b):
        for j in range(SSD_INNER // cw_chunk):
            cs = slice(j * cw_chunk, (j + 1) * cw_chunk)
            xdt = xact_ref[b, :, cs] * exp_ref[b, 0:L, cs]
            xdt_ref[b, :, cs] = xdt.astype(BF16)
            xdec_ref[b, :, cs] = (xdt * exp_ref[b, L:2 * L, cs]).astype(BF16)

    def group(b, g):
        bg = xact_ref[b, :, SSD_INNER + g * SSD_STATE:SSD_INNER + (g + 1) * SSD_STATE]
        cg = xact_ref[b, :, SSD_INNER + BC_WIDTH + g * SSD_STATE:SSD_INNER + BC_WIDTH + (g + 1) * SSD_STATE]
        bb = bg.astype(BF16)
        cbf = cg.astype(BF16)
        cbm = lax.dot_general(cbf, bb, (((1,), (1,)), ((), ())), preferred_element_type=F32)
        gs = slice(g * gw, (g + 1) * gw)
        prev = state_ref[b, g]
        yoff = jnp.dot(cbf, prev.astype(BF16), preferred_element_type=F32) * exp_ref[b, 2 * L:3 * L, gs]
        btb = bg.T.astype(BF16)
        state_ref[b, g] = prev * exp_ref[b, 3 * L:3 * L + 1, gs] + jnp.dot(
            btb, xdec_ref[b, :, gs], preferred_element_type=F32)
        for e in range(2):
            pc = slice((2 * g + e) * LANES, (2 * g + e + 1) * LANES)
            xpair = xdt_ref[b, :, pc]
            yd = []
            for r in range(2):
                h = 4 * g + 2 * e + r
                seg = acs_ref[b, :, h:h + 1] - acst_ref[b, h:h + 1, :]
                lm = jnp.exp(jnp.where(causal, seg, NEG))
                yd.append(jnp.dot((cbm * lm).astype(BF16), xpair, preferred_element_type=F32))
            y_ref[b, :, pc] = (jnp.where(lo, yd[0], yd[1]) + yoff[:, e * LANES:(e + 1) * LANES]
                               + dfull_ref[:, pc] * xact_ref[b, :, pc])

    def gate_norm(b, g):
        gs = slice(g * gw, (g + 1) * gw)
        zz = z_ref[b, :, gs]
        yg = y_ref[b, :, gs] * (zz * _sigmoid(zz))
        ms = jnp.mean(yg * yg, axis=-1, keepdims=True)
        o_ref[b, :, gs] = (yg * lax.rsqrt(ms + EPS) * gn_ref[:, gs]).astype(o_ref.dtype)

    for stage in (conv, decay, scale_x):
        for b in range(nbat):
            stage(b)
    for stage in (group, gate_norm):
        for g in range(SSD_GROUPS):
            for b in range(nbat):
                stage(b, g)


def _ssd(proj, dt_raw, dt_rawt, cw, cb, brow, bcol, alrow, alcol, emat, dfull, gn, batch, seq):
    L = SSD_CHUNK
    nc = seq // L
    proj3 = proj.reshape(batch, seq, MAIN_PROJ)
    full = lambda shape: pl.BlockSpec(shape, lambda c: (0, 0))
    out = pl.pallas_call(
        functools.partial(_ssd_kernel, nbat=batch),
        grid=(nc,),
        in_specs=[pl.BlockSpec((batch, L, CONV_CH), lambda c: (0, c, XBC_OFF // CONV_CH)),
                  pl.BlockSpec((batch, L, SSD_INNER), lambda c: (0, c, Z_OFF // SSD_INNER)),
                  pl.BlockSpec((batch, L, LANES), lambda c: (0, c, 0)),
                  pl.BlockSpec((batch, LANES, L), lambda c: (0, 0, c)),
                  full((SSD_CONV, CONV_CH)), full((1, CONV_CH)),
                  full((1, LANES)), full((LANES, 1)), full((1, LANES)), full((LANES, 1)),
                  full((LANES, SSD_INNER)), full((1, SSD_INNER)), full((1, SSD_INNER))],
        out_specs=pl.BlockSpec((batch, L, SSD_INNER), lambda c: (0, c, 0)),
        out_shape=jax.ShapeDtypeStruct((batch, seq, SSD_INNER), BF16),
        scratch_shapes=[pltpu.VMEM((batch, L + SUBLANES, CONV_CH), F32),
                        pltpu.VMEM((batch, L, CONV_CH), F32),
                        pltpu.VMEM((batch, SSD_GROUPS, SSD_STATE, SSD_INNER // SSD_GROUPS), F32),
                        pltpu.VMEM((batch, L, SSD_INNER), BF16),
                        pltpu.VMEM((batch, L, SSD_INNER), BF16),
                        pltpu.VMEM((batch, L, SSD_INNER), F32),
                        pltpu.VMEM((batch, 3 * L + SUBLANES, SSD_INNER), F32),
                        pltpu.VMEM((batch, L, LANES), F32),
                        pltpu.VMEM((batch, LANES, L), F32)],
        compiler_params=_params(1),
        name="ssd_scan",
    )(proj3, proj3, dt_raw.reshape(batch, seq, LANES), dt_rawt, cw, cb, brow, bcol, alrow, alcol,
      emat, dfull, gn)
    return out.reshape(batch * seq, SSD_INNER)


def _out_proj_kernel(a1_ref, a2_ref, w1_ref, w2_ref, r_ref, o_ref, w1b_ref, w2b_ref):
    @pl.when(pl.program_id(1) == 0)
    def _():
        _cast_rows(w1_ref, w1b_ref)
        _cast_rows(w2_ref, w2b_ref)

    acc = jnp.dot(a1_ref[...], w1b_ref[...], preferred_element_type=F32)
    acc = acc + jnp.dot(a2_ref[...], w2b_ref[...], preferred_element_type=F32)
    o_ref[...] = r_ref[...] + acc


def _out_proj(a1, a2, w, res, tm, tn):
    t, k = a1.shape
    n = w.shape[1]
    return pl.pallas_call(
        _out_proj_kernel,
        grid=(n // tn, t // tm),
        in_specs=[pl.BlockSpec((tm, k), lambda j, m: (m, 0)),
                  pl.BlockSpec((tm, k), lambda j, m: (m, 0)),
                  pl.BlockSpec((k, tn), lambda j, m: (0, j)),
                  pl.BlockSpec((k, tn), lambda j, m: (1, j)),
                  pl.BlockSpec((tm, tn), lambda j, m: (m, j))],
        out_specs=pl.BlockSpec((tm, tn), lambda j, m: (m, j)),
        out_shape=jax.ShapeDtypeStruct((t, n), F32),
        scratch_shapes=[pltpu.VMEM((k, tn), BF16), pltpu.VMEM((k, tn), BF16)],
        compiler_params=_params(2),
        name="out_proj",
    )(a1, a2, w, w, res)


def _up_kernel(a_ref, wg_ref, wv_ref, cwg_ref, cwv_ref, cbg_ref, cbv_ref, o_ref,
               wgb_ref, wvb_ref, carry_ref, *, tm, seq):
    m = pl.program_id(1)

    @pl.when(m == 0)
    def _():
        _cast_rows(wg_ref, wgb_ref)
        _cast_rows(wv_ref, wvb_ref)

    a = a_ref[...]
    seq_start = (m * tm) % seq == 0
    outs = []
    for idx, (wb_ref, cw_ref, cb_ref) in enumerate(((wgb_ref, cwg_ref, cbg_ref),
                                                    (wvb_ref, cwv_ref, cbv_ref))):
        u = jnp.dot(a, wb_ref[...], preferred_element_type=F32)
        above = jnp.where(seq_start, 0.0, carry_ref[idx])
        blk = jnp.concatenate([above, u], axis=0)
        cw = cw_ref[...]
        conv = cb_ref[...]
        for k in range(FFN_CONV - 1):
            conv = conv + _delay_rows(blk, FFN_CONV - 1 - k) * cw[k:k + 1, :]
        conv = conv + u * cw[FFN_CONV - 1:FFN_CONV, :]
        carry_ref[idx] = u[tm - SUBLANES:tm]
        outs.append(conv)
    gate, val = outs
    o_ref[...] = ((gate * _sigmoid(gate)) * val).astype(o_ref.dtype)


def _up_proj(a, w, cw, cb, tm, tn, seq):
    t, k = a.shape
    assert seq % tm == 0 and D_FF % tn == 0
    nb = D_FF // tn
    return pl.pallas_call(
        functools.partial(_up_kernel, tm=tm, seq=seq),
        grid=(nb, t // tm),
        in_specs=[pl.BlockSpec((tm, k), lambda j, m: (m, 0)),
                  pl.BlockSpec((k, tn), lambda j, m: (0, j)),
                  pl.BlockSpec((k, tn), lambda j, m: (0, j + nb)),
                  pl.BlockSpec((FFN_CONV, tn), lambda j, m: (0, j)),
                  pl.BlockSpec((FFN_CONV, tn), lambda j, m: (0, j + nb)),
                  pl.BlockSpec((1, tn), lambda j, m: (0, j)),
                  pl.BlockSpec((1, tn), lambda j, m: (0, j + nb))],
        out_specs=pl.BlockSpec((tm, tn), lambda j, m: (m, j)),
        out_shape=jax.ShapeDtypeStruct((t, D_FF), BF16),
        scratch_shapes=[pltpu.VMEM((k, tn), BF16), pltpu.VMEM((k, tn), BF16),
                        pltpu.VMEM((2, SUBLANES, tn), F32)],
        compiler_params=_params(2),
        name="up_proj_conv_swiglu",
    )(a, w, w, cw, cw, cb, cb)


def _down_kernel(a_ref, w_ref, r_ref, o_ref, wb_ref):
    @pl.when(pl.program_id(1) == 0)
    def _():
        _cast_rows(w_ref, wb_ref)

    o_ref[...] = r_ref[...] + jnp.dot(a_ref[...], wb_ref[...], preferred_element_type=F32)


def _down_proj(a, w, res, tm, tn):
    t, k = a.shape
    n = w.shape[1]
    return pl.pallas_call(
        _down_kernel,
        grid=(n // tn, t // tm),
        in_specs=[pl.BlockSpec((tm, k), lambda j, m: (m, 0)),
                  pl.BlockSpec((k, tn), lambda j, m: (0, j), pipeline_mode=pl.Buffered(1)),
                  pl.BlockSpec((tm, tn), lambda j, m: (m, j))],
        out_specs=pl.BlockSpec((tm, tn), lambda j, m: (m, j)),
        out_shape=jax.ShapeDtypeStruct((t, n), F32),
        scratch_shapes=[pltpu.VMEM((k, tn), BF16)],
        compiler_params=_params(2),
        name="down_proj",
    )(a, w, res)


def _pad_lanes(v):
    return jnp.pad(v.astype(F32), (0, LANES - v.shape[0]))


def _mixer_layer(h, batch, seq, norm_mix, w_in, sinks, attn_out_norm, ssd_conv_w, ssd_conv_b, dt_bias,
                 a_log, ssd_d, ssd_norm, w_out, tables, emat):
    w_in_t = jnp.swapaxes(w_in, 0, 1)
    wdt_t = jnp.pad(w_in_t[MAIN_PROJ:], ((0, LANES - SSD_HEADS), (0, 0))).astype(BF16)
    xn, dt_raw, dt_rawt = _norm_dt(h, norm_mix.reshape(1, -1), wdt_t.T, wdt_t, 512, batch, seq)
    proj = _in_proj(xn, w_in_t, tm=1024, tn=1024)
    attn = _attention(proj, sinks.astype(F32), attn_out_norm.reshape(1, -1), tables, batch, seq, qb=4)
    bias = _pad_lanes(dt_bias)
    alog = _pad_lanes(a_log)
    y = _ssd(proj, dt_raw, dt_rawt, ssd_conv_w, ssd_conv_b.reshape(1, -1),
             bias.reshape(1, -1), bias.reshape(-1, 1), alog.reshape(1, -1), alog.reshape(-1, 1),
             emat, jnp.repeat(ssd_d.astype(F32), SSD_HEAD_DIM).reshape(1, -1),
             ssd_norm.reshape(1, -1), batch, seq)
    return _out_proj(attn, y, w_out, h, tm=1024, tn=512)


def kernel(x, norm_mix, w_in, sinks, attn_out_norm, ssd_conv_w, ssd_conv_b, dt_bias, a_log, ssd_d, ssd_norm,
           w_out, norm_ffn, w_up, ffn_conv_w, ffn_conv_b, w_down, norm_final):
    batch, seq, d = x.shape
    h = x.reshape(batch * seq, d)
    tables = _rope_tables(seq)
    head_of_channel = np.arange(SSD_INNER) // SSD_HEAD_DIM
    emat = jnp.asarray(np.arange(LANES)[:, None] == head_of_channel[None, :], dtype=BF16)
    for l in range(norm_mix.shape[0]):
        h = _mixer_layer(h, batch, seq, norm_mix[l], w_in[l], sinks[l], attn_out_norm[l], ssd_conv_w[l],
                         ssd_conv_b[l], dt_bias[l], a_log[l], ssd_d[l], ssd_norm[l], w_out[l], tables, emat)
        hn = _norm(h, norm_ffn[l].reshape(1, -1), BF16, 512, "norm_ffn")
        act = _up_proj(hn, w_up[l], ffn_conv_w[l], ffn_conv_b[l].reshape(1, -1),
                       tm=1024, tn=512, seq=seq)
        h = _down_proj(act, w_down[l], h, tm=1024, tn=512)
    out = _norm(h, norm_final.reshape(1, -1), F32, 512, "norm_final")
    return out.reshape(batch, seq, d)
```

```python
import functools

import numpy as np
import jax
import jax.numpy as jnp
from jax import lax
from jax.experimental import pallas as pl
from jax.experimental.pallas import tpu as pltpu

F32 = jnp.float32
BF16 = jnp.bfloat16

D_MODEL = 2048
N_Q_HEADS = 32
N_KV_HEADS = 8
HEAD_DIM = 64
Q_PER_KV = N_Q_HEADS // N_KV_HEADS
WINDOW = 128
ATTN_BLOCK = 128
ROT_DIM = HEAD_DIM // 4
ROPE_THETA = 500000.0
SSD_HEADS = 32
SSD_HEAD_DIM = 64
SSD_INNER = SSD_HEADS * SSD_HEAD_DIM
SSD_GROUPS = 8
SSD_STATE = 128
SSD_CONV = 4
SSD_CHUNK = 128
ATTN_WIDTH = N_Q_HEADS * HEAD_DIM
KV_WIDTH = N_KV_HEADS * HEAD_DIM
BC_WIDTH = SSD_GROUPS * SSD_STATE
CONV_CH = SSD_INNER + 2 * BC_WIDTH
MAIN_PROJ = ATTN_WIDTH + 2 * KV_WIDTH + SSD_INNER + CONV_CH
D_FF = 5632
FFN_CONV = 3
EPS = 1e-6

LANES = 128
SUBLANES = 8
HALF = LANES // 2
NEG = -1e30
LOG2E = 1.4426950408889634
VMEM_LIMIT = 56 * 1024 * 1024

Q_OFF = 0
Z_OFF = ATTN_WIDTH
XBC_OFF = Z_OFF + SSD_INNER
K_OFF = XBC_OFF + CONV_CH
V_OFF = K_OFF + KV_WIDTH


def _params(n_axes, flags=None):
    return pltpu.CompilerParams(dimension_semantics=("arbitrary",) * n_axes,
                                vmem_limit_bytes=VMEM_LIMIT, flags=flags)


def _sigmoid(x):
    return 1.0 / (1.0 + jnp.exp2(x * -LOG2E))


def _softplus(x):
    return jnp.maximum(x, 0.0) + jnp.log1p(jnp.exp(-jnp.abs(x)))


def _cast_rows(src_ref, dst_ref, rows=256):
    k = src_ref.shape[0]
    for r in range(0, k, rows):
        dst_ref[r:r + rows, :] = src_ref[r:r + rows, :].astype(BF16)


def _delay_rows(blk, sh):
    n, w = blk.shape[0] - SUBLANES, blk.shape[1]
    g = blk.reshape(n // SUBLANES + 1, SUBLANES, w)
    rot = pltpu.roll(g, sh, 1)
    row = lax.broadcasted_iota(jnp.int32, (SUBLANES, w), 0)
    out = jnp.where(row < sh, rot[:-1], rot[1:])
    return out.reshape(n, w)


def _rmsnorm_rows(x, g):
    ms = jnp.mean(x * x, axis=-1, keepdims=True)
    return x * lax.rsqrt(ms + EPS) * g


def _norm_dt_kernel(x_ref, g_ref, wdtt_ref, xn_ref, dtt_ref):
    xn = _rmsnorm_rows(x_ref[...], g_ref[...]).astype(BF16)
    xn_ref[...] = xn
    dtt_ref[...] = lax.dot_general(wdtt_ref[...], xn, (((1,), (1,)), ((), ())),
                                   preferred_element_type=F32)


def _norm_dt(x2, g, wdtt, tm, batch, seq):
    t, d = x2.shape
    assert seq % tm == 0
    per_seq = seq // tm
    return pl.pallas_call(
        _norm_dt_kernel,
        grid=(t // tm,),
        in_specs=[pl.BlockSpec((tm, d), lambda i: (i, 0)),
                  pl.BlockSpec((1, d), lambda i: (0, 0)),
                  pl.BlockSpec((LANES, d), lambda i: (0, 0))],
        out_specs=[pl.BlockSpec((tm, d), lambda i: (i, 0)),
                   pl.BlockSpec((None, LANES, tm), lambda i: (i // per_seq, 0, i % per_seq))],
        out_shape=[jax.ShapeDtypeStruct((t, d), BF16),
                   jax.ShapeDtypeStruct((batch, LANES, seq), F32)],
        compiler_params=_params(1),
        name="norm_dt",
    )(x2, g, wdtt)


def _norm_kernel(x_ref, g_ref, o_ref):
    o_ref[...] = _rmsnorm_rows(x_ref[...], g_ref[...]).astype(o_ref.dtype)


def _norm(x2, g, out_dtype, tm, name):
    t, d = x2.shape
    return pl.pallas_call(
        _norm_kernel,
        grid=(t // tm,),
        in_specs=[pl.BlockSpec((tm, d), lambda i: (i, 0)),
                  pl.BlockSpec((1, d), lambda i: (0, 0))],
        out_specs=pl.BlockSpec((tm, d), lambda i: (i, 0)),
        out_shape=jax.ShapeDtypeStruct((t, d), out_dtype),
        compiler_params=_params(1),
        name=name,
    )(x2, g)


def _in_proj_kernel(a_ref, wt_ref, o_ref, wbf_ref):
    @pl.when(pl.program_id(1) == 0)
    def _():
        _cast_rows(wt_ref, wbf_ref)

    o_ref[...] = lax.dot_general(a_ref[...], wbf_ref[...], (((1,), (1,)), ((), ())),
                                 preferred_element_type=F32)


def _in_proj(xn, wt, tm, tn):
    t, k = xn.shape
    u = 1024 // tn

    def wmap(j, m):
        return (jnp.where(j < 2 * u, j, jnp.where(j < 8 * u, j + u, j - 6 * u)), 0)

    return pl.pallas_call(
        _in_proj_kernel,
        grid=(MAIN_PROJ // tn, t // tm),
        in_specs=[pl.BlockSpec((tm, k), lambda j, m: (m, 0)),
                  pl.BlockSpec((tn, k), wmap)],
        out_specs=pl.BlockSpec((tm, tn), lambda j, m: (m, j)),
        out_shape=jax.ShapeDtypeStruct((t, MAIN_PROJ), F32),
        scratch_shapes=[pltpu.VMEM((tn, k), BF16)],
        compiler_params=_params(2),
        name="in_proj",
    )(xn, wt)


def _rope_tables(seq):
    half = ROT_DIM // 2
    inv = 1.0 / (ROPE_THETA ** (jnp.arange(0, ROT_DIM, 2, dtype=F32) / ROT_DIM))
    ang = jnp.arange(seq, dtype=F32)[:, None] * inv[None, :]
    cos, sin = jnp.cos(ang), jnp.sin(ang)
    d = np.arange(LANES) % HEAD_DIM
    idx = d % half
    in_rot = jnp.asarray(d < ROT_DIM)
    first = jnp.asarray(d < half)
    second = jnp.asarray((d >= half) & (d < ROT_DIM))
    c = jnp.where(in_rot[None, :], cos[:, idx], 1.0)
    s1 = jnp.where(first[None, :], -sin[:, idx], 0.0)
    s2 = jnp.where(second[None, :], sin[:, idx], 0.0)
    return jnp.stack([c, s1, s2]).astype(F32)


def _rope(x, c, s1, s2):
    half = ROT_DIM // 2
    return x * c + pltpu.roll(x, LANES - half, 1) * s1 + pltpu.roll(x, half, 1) * s2


def _attn_kernel(sinks_ref, q_ref, kp_ref, kc_ref, vp_ref, vc_ref,
                 tc_ref, tp_ref, g_ref, o_ref,
                 lhs_ref, kb_ref, vb_ref, s_ref, p_ref, o2_ref, m_ref, tq_ref, bias_ref, acc_ref, *, qb):
    blk = ATTN_BLOCK
    n = pl.program_id(1)
    lo = lax.broadcasted_iota(jnp.int32, (blk, LANES), 1) < HALF

    scale = HEAD_DIM ** -0.5 * LOG2E
    for i in range(3):
        tq_ref[i] = tc_ref[i] * scale
    qi = lax.broadcasted_iota(jnp.int32, (blk, 2 * blk), 0)
    kj = lax.broadcasted_iota(jnp.int32, (blk, 2 * blk), 1)
    rel = qi + blk - kj
    band = (rel >= 0) & (rel < WINDOW)
    kmin = jnp.where(n > 0, 0, blk)
    bias_ref[0] = jnp.where(band & (kj >= kmin), 0.0, NEG)
    bias_ref[1] = jnp.where(band, 0.0, NEG)

    for j in range(qb):
        rows = slice(j * blk, (j + 1) * blk)
        for col in range(ATTN_WIDTH // LANES):
            h, jj = col // 2, col % 2
            q2 = _rope(q_ref[rows, col * LANES:(col + 1) * LANES],
                       tq_ref[0, rows, :], tq_ref[1, rows, :], tq_ref[2, rows, :])
            lhs_ref[j, h, (2 * jj) * blk:(2 * jj + 1) * blk, :] = jnp.where(lo, q2, 0.0).astype(BF16)
            lhs_ref[j, h, (2 * jj + 1) * blk:(2 * jj + 2) * blk, :] = jnp.where(lo, 0.0, q2).astype(BF16)
    nk = (qb + 1) * blk
    lok = lax.broadcasted_iota(jnp.int32, (nk, LANES), 1) < HALF
    ones = jnp.ones((nk, LANES), BF16)
    for i in range(N_KV_HEADS // 2):
        cols = slice(i * LANES, (i + 1) * LANES)
        kcat = jnp.concatenate([_rope(kp_ref[:, cols], tp_ref[0], tp_ref[1], tp_ref[2]),
                                _rope(kc_ref[:, cols], tc_ref[0], tc_ref[1], tc_ref[2])], axis=0)
        vcat = jnp.concatenate([vp_ref[:, cols], vc_ref[:, cols]], axis=0)
        kswp = pltpu.roll(kcat, HALF, 1)
        vswp = pltpu.roll(vcat, HALF, 1)
        kb_ref[2 * i] = jnp.where(lok, kcat, kswp).astype(BF16)
        kb_ref[2 * i + 1] = jnp.where(lok, kswp, kcat).astype(BF16)
        vb_ref[2 * i, :, 0:LANES] = jnp.where(lok, vcat, vswp).astype(BF16)
        vb_ref[2 * i + 1, :, 0:LANES] = jnp.where(lok, vswp, vcat).astype(BF16)
        vb_ref[2 * i, :, LANES:2 * LANES] = ones
        vb_ref[2 * i + 1, :, LANES:2 * LANES] = ones

    def keys(j):
        return slice(j * blk, (j + 2) * blk)

    def scores(j, h):
        s_ref[j, h % 2] = lax.dot_general(lhs_ref[j, h], kb_ref[h, keys(j), :], (((1,), (1,)), ((), ())),
                                          preferred_element_type=F32)

    def row_max(j, h):
        slot = h % 2
        for r in range(Q_PER_KV):
            rows = slice(r * blk, (r + 1) * blk)
            sink = sinks_ref[Q_PER_KV * h + r] * LOG2E
            sr = s_ref[j, slot, rows, :] + bias_ref[min(j, 1)]
            s_ref[j, slot, rows, :] = sr
            m = jnp.maximum(jnp.max(sr, axis=-1, keepdims=True), sink)
            m_ref[j, slot, rows, :] = jnp.broadcast_to(m, (blk, LANES))

    def probs(j, h):
        slot = h % 2
        for r in range(Q_PER_KV):
            rows = slice(r * blk, (r + 1) * blk)
            m = m_ref[j, slot, rows, :]
            for half in range(2):
                ln = slice(half * LANES, (half + 1) * LANES)
                p_ref[j, slot, rows, ln] = jnp.exp2(s_ref[j, slot, rows, ln] - m).astype(BF16)

    def weighted(j, h):
        o2_ref[j, h % 2] = jnp.dot(p_ref[j, h % 2], vb_ref[h, keys(j), :],
                                   preferred_element_type=F32)

    def finish(j, h):
        slot = h % 2
        outs = []
        for r in range(Q_PER_KV):
            rows = slice(r * blk, (r + 1) * blk)
            sink = sinks_ref[Q_PER_KV * h + r] * LOG2E
            den = o2_ref[j, slot, rows, LANES:2 * LANES] + jnp.exp2(sink - m_ref[j, slot, rows, :])
            outs.append(o2_ref[j, slot, rows, 0:LANES] * (1.0 / den))
        for jj in range(2):
            col = 2 * h + jj
            acc_ref[j * blk:(j + 1) * blk, col * LANES:(col + 1) * LANES] = jnp.where(
                lo, outs[2 * jj], outs[2 * jj + 1])

    for j in range(qb):
        scores(j, 0)
    for h in range(N_KV_HEADS + 1):
        for stage in (scores, row_max, probs, weighted, finish):
            hh = {scores: h + 1, finish: h - 1}.get(stage, h)
            if 0 <= hh < N_KV_HEADS:
                for j in range(qb):
                    stage(j, hh)

    o_ref[...] = _rmsnorm_rows(acc_ref[...], g_ref[...]).astype(o_ref.dtype)


def _attention(proj, sinks, g, tables, batch, seq, qb):
    t = proj.shape[0]
    blk = ATTN_BLOCK
    assert seq % (qb * blk) == 0
    nb = seq // blk
    ns = nb // qb
    kblk = K_OFF // KV_WIDTH
    vblk = V_OFF // KV_WIDTH

    def cur(b, n):
        return (b * ns + n, 0)

    def prev_rows(b, n):
        return jnp.maximum(b * nb + n * qb - 1, 0)

    return pl.pallas_call(
        functools.partial(_attn_kernel, qb=qb),
        grid=(batch, ns),
        in_specs=[pl.BlockSpec(memory_space=pltpu.SMEM),
                  pl.BlockSpec((qb * blk, ATTN_WIDTH), cur),
                  pl.BlockSpec((blk, KV_WIDTH), lambda b, n: (prev_rows(b, n), kblk)),
                  pl.BlockSpec((qb * blk, KV_WIDTH), lambda b, n: (b * ns + n, kblk)),
                  pl.BlockSpec((blk, KV_WIDTH), lambda b, n: (prev_rows(b, n), vblk)),
                  pl.BlockSpec((qb * blk, KV_WIDTH), lambda b, n: (b * ns + n, vblk)),
                  pl.BlockSpec((3, qb * blk, LANES), lambda b, n: (0, n, 0)),
                  pl.BlockSpec((3, blk, LANES), lambda b, n: (0, jnp.maximum(n * qb - 1, 0), 0)),
                  pl.BlockSpec((1, ATTN_WIDTH), lambda b, n: (0, 0))],
        out_specs=pl.BlockSpec((qb * blk, ATTN_WIDTH), cur),
        out_shape=jax.ShapeDtypeStruct((t, ATTN_WIDTH), BF16),
        scratch_shapes=[pltpu.VMEM((qb, N_KV_HEADS, Q_PER_KV * blk, LANES), BF16),
                        pltpu.VMEM((N_KV_HEADS, (qb + 1) * blk, LANES), BF16),
                        pltpu.VMEM((N_KV_HEADS, (qb + 1) * blk, 2 * LANES), BF16),
                        pltpu.VMEM((qb, 2, Q_PER_KV * blk, 2 * blk), F32),
                        pltpu.VMEM((qb, 2, Q_PER_KV * blk, 2 * blk), BF16),
                        pltpu.VMEM((qb, 2, Q_PER_KV * blk, 2 * LANES), F32),
                        pltpu.VMEM((qb, 2, Q_PER_KV * blk, LANES), F32),
                        pltpu.VMEM((3, qb * blk, LANES), F32),
                        pltpu.VMEM((2, blk, 2 * blk), F32),
                        pltpu.VMEM((qb * blk, ATTN_WIDTH), F32)],
        compiler_params=_params(2),
        name="swa_attention",
    )(sinks, proj, proj, proj, proj, proj, tables, tables, g)


def _split3(x):
    h = x.astype(BF16)
    r = x - h.astype(F32)
    m = r.astype(BF16)
    l = (r - m.astype(F32)).astype(BF16)
    return h, m, l


def _dot3_lhs(x, w):
    h, m, l = _split3(x)
    d = lambda a: jnp.dot(a, w, preferred_element_type=F32)
    return (d(l) + d(m)) + d(h)


def _ssd_kernel(xbc_ref, z_ref, dtt_ref, cw_ref, cb_ref, bcol_ref, alcol_ref, e_ref, dfull_ref, gn_ref, o_ref,
                ext_ref, xact_ref, state_ref, xdt_ref, xdec_ref, y_ref, exp_ref, acs_ref, acst_ref, *, nbat):
    L = SSD_CHUNK
    c = pl.program_id(0)
    cw_chunk = 512
    gw = SSD_HEAD_DIM * (SSD_HEADS // SSD_GROUPS)
    ri = lax.broadcasted_iota(jnp.int32, (L, L), 0)
    ci = lax.broadcasted_iota(jnp.int32, (L, L), 1)
    causal = ri >= ci
    lo = ci < HALF

    @pl.when(c == 0)
    def _():
        ext_ref[:, 0:SUBLANES, :] = jnp.zeros((nbat, SUBLANES, CONV_CH), F32)
        state_ref[...] = jnp.zeros_like(state_ref)

    def conv(b):
        ext_ref[b, SUBLANES:SUBLANES + L, :] = xbc_ref[b]
        for j in range(CONV_CH // cw_chunk):
            cs = slice(j * cw_chunk, (j + 1) * cw_chunk)
            blk = ext_ref[b, :, cs]
            acc = cb_ref[:, cs]
            for k in range(SSD_CONV - 1):
                acc = acc + _delay_rows(blk, SSD_CONV - 1 - k) * cw_ref[k:k + 1, cs]
            acc = acc + blk[SUBLANES:] * cw_ref[SSD_CONV - 1:SSD_CONV, cs]
            xact_ref[b, :, cs] = acc * _sigmoid(acc)
        ext_ref[b, 0:SUBLANES, :] = ext_ref[b, L:L + SUBLANES, :]

    def decay(b):
        nh = SSD_HEADS
        dtt = _softplus(dtt_ref[b, 0:nh, :] + bcol_ref[0:nh, :])
        dat = dtt * (-jnp.exp(alcol_ref[0:nh, :]))
        tri_u = jnp.where(ri <= ci, 1.0, 0.0).astype(BF16)
        acst = _dot3_lhs(dat, tri_u)
        acst_ref[b, 0:nh, :] = acst
        pad = jnp.zeros((LANES - nh, L), F32)
        a_cs = jnp.concatenate([acst, pad], axis=0).T
        dt = jnp.concatenate([dtt, pad], axis=0).T
        acs_ref[b] = a_cs
        a_last = a_cs[L - 1:L, :]
        stack = jnp.concatenate([dt, jnp.exp(a_last - a_cs), jnp.exp(a_cs),
                                 jnp.broadcast_to(jnp.exp(a_last), (SUBLANES, LANES))], axis=0)
        hi = stack.astype(BF16)
        lo = (stack - hi.astype(F32)).astype(BF16)
        split = jnp.concatenate([hi, lo], axis=1)
        for j in range(SSD_INNER // cw_chunk):
            cs = slice(j * cw_chunk, (j + 1) * cw_chunk)
            exp_ref[b, :, cs] = jnp.dot(split, e_ref[:, cs], preferred_element_type=F32)

    def scale_x(b):
        for j in range(SSD_INNER // cw_chunk):
            cs = slice(j * cw_chunk, (j + 1) * cw_chunk)
            xdt = xact_ref[b, :, cs] * exp_ref[b, 0:L, cs]
            xdt_ref[b, :, cs] = xdt.astype(BF16)
            xdec_ref[b, :, cs] = (xdt * exp_ref[b, L:2 * L, cs]).astype(BF16)

    def group(b, g):
        bg = xact_ref[b, :, SSD_INNER + g * SSD_STATE:SSD_INNER + (g + 1) * SSD_STATE]
        cg = xact_ref[b, :, SSD_INNER + BC_WIDTH + g * SSD_STATE:SSD_INNER + BC_WIDTH + (g + 1) * SSD_STATE]
        bb = bg.astype(BF16)
        cbf = cg.astype(BF16)
        cbm = lax.dot_general(cbf, bb, (((1,), (1,)), ((), ())), preferred_element_type=F32)
        gs = slice(g * gw, (g + 1) * gw)
        prev = state_ref[b, g]
        yoff = jnp.dot(cbf, prev.astype(BF16), preferred_element_type=F32) * exp_ref[b, 2 * L:3 * L, gs]
        btb = bg.T.astype(BF16)
        state_ref[b, g] = prev * exp_ref[b, 3 * L:3 * L + 1, gs] + jnp.dot(
            btb, xdec_ref[b, :, gs], preferred_element_type=F32)
        for e in range(2):
            pc = slice((2 * g + e) * LANES, (2 * g + e + 1) * LANES)
            xpair = xdt_ref[b, :, pc]
            yd = []
            for r in range(2):
                h = 4 * g + 2 * e + r
                seg = acs_ref[b, :, h:h + 1] - acst_ref[b, h:h + 1, :]
                lm = jnp.exp(jnp.where(causal, seg, NEG))
                yd.append(jnp.dot((cbm * lm).astype(BF16), xpair, preferred_element_type=F32))
            y_ref[b, :, pc] = (jnp.where(lo, yd[0], yd[1]) + yoff[:, e * LANES:(e + 1) * LANES]
                               + dfull_ref[:, pc] * xact_ref[b, :, pc])

    def gate_norm(b, g):
        gs = slice(g * gw, (g + 1) * gw)
        zz = z_ref[b, :, gs]
        yg = y_ref[b, :, gs] * (zz * _sigmoid(zz))
        ms = jnp.mean(yg * yg, axis=-1, keepdims=True)
        o_ref[b, :, gs] = (yg * lax.rsqrt(ms + EPS) * gn_ref[:, gs]).astype(o_ref.dtype)

    for stage in (decay, conv, scale_x):
        for b in range(nbat):
            stage(b)
    for stage in (group, gate_norm):
        for g in range(SSD_GROUPS):
            for b in range(nbat):
                stage(b, g)


def _ssd(proj, dt_rawt, cw, cb, bcol, alcol, emat, dfull, gn, batch, seq):
    L = SSD_CHUNK
    nc = seq // L
    proj3 = proj.reshape(batch, seq, MAIN_PROJ)
    full = lambda shape: pl.BlockSpec(shape, lambda c: (0, 0))
    out = pl.pallas_call(
        functools.partial(_ssd_kernel, nbat=batch),
        grid=(nc,),
        in_specs=[pl.BlockSpec((batch, L, CONV_CH), lambda c: (0, c, XBC_OFF // CONV_CH)),
                  pl.BlockSpec((batch, L, SSD_INNER), lambda c: (0, c, Z_OFF // SSD_INNER)),
                  pl.BlockSpec((batch, LANES, L), lambda c: (0, 0, c)),
                  full((SSD_CONV, CONV_CH)), full((1, CONV_CH)),
                  full((LANES, 1)), full((LANES, 1)),
                  full((2 * LANES, SSD_INNER)), full((1, SSD_INNER)), full((1, SSD_INNER))],
        out_specs=pl.BlockSpec((batch, L, SSD_INNER), lambda c: (0, c, 0)),
        out_shape=jax.ShapeDtypeStruct((batch, seq, SSD_INNER), BF16),
        scratch_shapes=[pltpu.VMEM((batch, L + SUBLANES, CONV_CH), F32),
                        pltpu.VMEM((batch, L, CONV_CH), F32),
                        pltpu.VMEM((batch, SSD_GROUPS, SSD_STATE, SSD_INNER // SSD_GROUPS), F32),
                        pltpu.VMEM((batch, L, SSD_INNER), BF16),
                        pltpu.VMEM((batch, L, SSD_INNER), BF16),
                        pltpu.VMEM((batch, L, SSD_INNER), F32),
                        pltpu.VMEM((batch, 3 * L + SUBLANES, SSD_INNER), F32),
                        pltpu.VMEM((batch, L, LANES), F32),
                        pltpu.VMEM((batch, LANES, L), F32)],
        compiler_params=_params(1),
        name="ssd_scan",
    )(proj3, proj3, dt_rawt, cw, cb, bcol, alcol, emat, dfull, gn)
    return out.reshape(batch * seq, SSD_INNER)


def _out_proj_kernel(a1_ref, a2_ref, w1_ref, w2_ref, r_ref, o_ref, w1b_ref, w2b_ref):
    @pl.when(pl.program_id(1) == 0)
    def _():
        _cast_rows(w1_ref, w1b_ref)
        _cast_rows(w2_ref, w2b_ref)

    acc = jnp.dot(a1_ref[...], w1b_ref[...], preferred_element_type=F32)
    acc = acc + jnp.dot(a2_ref[...], w2b_ref[...], preferred_element_type=F32)
    o_ref[...] = r_ref[...] + acc


def _out_proj(a1, a2, w, res, tm, tn):
    t, k = a1.shape
    n = w.shape[1]
    return pl.pallas_call(
        _out_proj_kernel,
        grid=(n // tn, t // tm),
        in_specs=[pl.BlockSpec((tm, k), lambda j, m: (m, 0)),
                  pl.BlockSpec((tm, k), lambda j, m: (m, 0)),
                  pl.BlockSpec((k, tn), lambda j, m: (0, j)),
                  pl.BlockSpec((k, tn), lambda j, m: (1, j)),
                  pl.BlockSpec((tm, tn), lambda j, m: (m, j))],
        out_specs=pl.BlockSpec((tm, tn), lambda j, m: (m, j)),
        out_shape=jax.ShapeDtypeStruct((t, n), F32),
        scratch_shapes=[pltpu.VMEM((k, tn), BF16), pltpu.VMEM((k, tn), BF16)],
        compiler_params=_params(2),
        name="out_proj",
    )(a1, a2, w, w, res)


def _up_kernel(a_ref, wg_ref, wv_ref, cwg_ref, cwv_ref, cbg_ref, cbv_ref, o_ref,
               wgb_ref, wvb_ref, carry_ref, *, tm, seq):
    m = pl.program_id(1)

    @pl.when(m == 0)
    def _():
        _cast_rows(wg_ref, wgb_ref)
        _cast_rows(wv_ref, wvb_ref)

    a = a_ref[...]
    seq_start = (m * tm) % seq == 0
    outs = []
    for idx, (wb_ref, cw_ref, cb_ref) in enumerate(((wgb_ref, cwg_ref, cbg_ref),
                                                    (wvb_ref, cwv_ref, cbv_ref))):
        u = jnp.dot(a, wb_ref[...], preferred_element_type=F32)
        above = jnp.where(seq_start, 0.0, carry_ref[idx])
        blk = jnp.concatenate([above, u], axis=0)
        cw = cw_ref[...]
        conv = cb_ref[...]
        for k in range(FFN_CONV - 1):
            conv = conv + _delay_rows(blk, FFN_CONV - 1 - k) * cw[k:k + 1, :]
        conv = conv + u * cw[FFN_CONV - 1:FFN_CONV, :]
        carry_ref[idx] = u[tm - SUBLANES:tm]
        outs.append(conv)
    gate, val = outs
    o_ref[...] = ((gate * _sigmoid(gate)) * val).astype(o_ref.dtype)


def _up_proj(a, w, cw, cb, tm, tn, seq):
    t, k = a.shape
    assert seq % tm == 0 and D_FF % tn == 0
    nb = D_FF // tn
    return pl.pallas_call(
        functools.partial(_up_kernel, tm=tm, seq=seq),
        grid=(nb, t // tm),
        in_specs=[pl.BlockSpec((tm, k), lambda j, m: (m, 0)),
                  pl.BlockSpec((k, tn), lambda j, m: (0, j)),
                  pl.BlockSpec((k, tn), lambda j, m: (0, j + nb)),
                  pl.BlockSpec((FFN_CONV, tn), lambda j, m: (0, j)),
                  pl.BlockSpec((FFN_CONV, tn), lambda j, m: (0, j + nb)),
                  pl.BlockSpec((1, tn), lambda j, m: (0, j)),
                  pl.BlockSpec((1, tn), lambda j, m: (0, j + nb))],
        out_specs=pl.BlockSpec((tm, tn), lambda j, m: (m, j)),
        out_shape=jax.ShapeDtypeStruct((t, D_FF), BF16),
        scratch_shapes=[pltpu.VMEM((k, tn), BF16), pltpu.VMEM((k, tn), BF16),
                        pltpu.VMEM((2, SUBLANES, tn), F32)],
        compiler_params=_params(2),
        name="up_proj_conv_swiglu",
    )(a, w, w, cw, cw, cb, cb)


def _down_kernel(a_ref, w_ref, r_ref, o_ref, wb_ref):
    @pl.when(pl.program_id(1) == 0)
    def _():
        _cast_rows(w_ref, wb_ref)

    o_ref[...] = r_ref[...] + jnp.dot(a_ref[...], wb_ref[...], preferred_element_type=F32)


def _down_proj(a, w, res, tm, tn):
    t, k = a.shape
    n = w.shape[1]
    return pl.pallas_call(
        _down_kernel,
        grid=(n // tn, t // tm),
        in_specs=[pl.BlockSpec((tm, k), lambda j, m: (m, 0)),
                  pl.BlockSpec((k, tn), lambda j, m: (0, j)),
                  pl.BlockSpec((tm, tn), lambda j, m: (m, j))],
        out_specs=pl.BlockSpec((tm, tn), lambda j, m: (m, j)),
        out_shape=jax.ShapeDtypeStruct((t, n), F32),
        scratch_shapes=[pltpu.VMEM((k, tn), BF16)],
        compiler_params=_params(2),
        name="down_proj",
    )(a, w, res)


def _pad_lanes(v):
    return jnp.pad(v.astype(F32), (0, LANES - v.shape[0]))


def _mixer_layer(h, batch, seq, norm_mix, w_in, sinks, attn_out_norm, ssd_conv_w, ssd_conv_b, dt_bias,
                 a_log, ssd_d, ssd_norm, w_out, tables, emat):
    w_in_t = jnp.swapaxes(w_in, 0, 1)
    wdt_t = jnp.pad(w_in_t[MAIN_PROJ:], ((0, LANES - SSD_HEADS), (0, 0))).astype(BF16)
    xn, dt_rawt = _norm_dt(h, norm_mix.reshape(1, -1), wdt_t, 512, batch, seq)
    proj = _in_proj(xn, w_in_t, tm=1024, tn=1024)
    attn = _attention(proj, sinks.astype(F32), attn_out_norm.reshape(1, -1), tables, batch, seq, qb=4)
    bias = _pad_lanes(dt_bias)
    alog = _pad_lanes(a_log)
    y = _ssd(proj, dt_rawt, ssd_conv_w, ssd_conv_b.reshape(1, -1), bias.reshape(-1, 1), alog.reshape(-1, 1),
             emat, jnp.repeat(ssd_d.astype(F32), SSD_HEAD_DIM).reshape(1, -1),
             ssd_norm.reshape(1, -1), batch, seq)
    return _out_proj(attn, y, w_out, h, tm=1024, tn=512)


def kernel(x, norm_mix, w_in, sinks, attn_out_norm, ssd_conv_w, ssd_conv_b, dt_bias, a_log, ssd_d, ssd_norm,
           w_out, norm_ffn, w_up, ffn_conv_w, ffn_conv_b, w_down, norm_final):
    batch, seq, d = x.shape
    h = x.reshape(batch * seq, d)
    tables = _rope_tables(seq)
    head_of_channel = np.arange(SSD_INNER) // SSD_HEAD_DIM
    emat = np.arange(LANES)[:, None] == head_of_channel[None, :]
    emat = jnp.asarray(np.concatenate([emat, emat], axis=0), dtype=BF16)
    for l in range(norm_mix.shape[0]):
        h = _mixer_layer(h, batch, seq, norm_mix[l], w_in[l], sinks[l], attn_out_norm[l], ssd_conv_w[l],
                         ssd_conv_b[l], dt_bias[l], a_log[l], ssd_d[l], ssd_norm[l], w_out[l], tables, emat)
        hn = _norm(h, norm_ffn[l].reshape(1, -1), BF16, 512, "norm_ffn")
        act = _up_proj(hn, w_up[l], ffn_conv_w[l], ffn_conv_b[l].reshape(1, -1),
                       tm=1024, tn=512, seq=seq)
        h = _down_proj(act, w_down[l], h, tm=512, tn=512)
    out = _norm(h, norm_final.reshape(1, -1), F32, 512, "norm_final")
    return out.reshape(batch, seq, d)
```

```python
import functools

import numpy as np
import jax
import jax.numpy as jnp
from jax import lax
from jax.experimental import pallas as pl
from jax.experimental.pallas import tpu as pltpu

F32 = jnp.float32
BF16 = jnp.bfloat16

D_MODEL = 2048
N_Q_HEADS = 32
N_KV_HEADS = 8
HEAD_DIM = 64
Q_PER_KV = N_Q_HEADS // N_KV_HEADS
WINDOW = 128
ATTN_BLOCK = 128
ROT_DIM = HEAD_DIM // 4
ROPE_THETA = 500000.0
SSD_HEADS = 32
SSD_HEAD_DIM = 64
SSD_INNER = SSD_HEADS * SSD_HEAD_DIM
SSD_GROUPS = 8
SSD_STATE = 128
SSD_CONV = 4
SSD_CHUNK = 128
ATTN_WIDTH = N_Q_HEADS * HEAD_DIM
KV_WIDTH = N_KV_HEADS * HEAD_DIM
BC_WIDTH = SSD_GROUPS * SSD_STATE
CONV_CH = SSD_INNER + 2 * BC_WIDTH
MAIN_PROJ = ATTN_WIDTH + 2 * KV_WIDTH + SSD_INNER + CONV_CH
D_FF = 5632
FFN_CONV = 3
EPS = 1e-6

LANES = 128
SUBLANES = 8
HALF = LANES // 2
NEG = -1e30
LOG2E = 1.4426950408889634
VMEM_LIMIT = 56 * 1024 * 1024

Q_OFF = 0
Z_OFF = ATTN_WIDTH
XBC_OFF = Z_OFF + SSD_INNER
K_OFF = XBC_OFF + CONV_CH
V_OFF = K_OFF + KV_WIDTH


def _params(n_axes, flags=None):
    return pltpu.CompilerParams(dimension_semantics=("arbitrary",) * n_axes,
                                vmem_limit_bytes=VMEM_LIMIT, flags=flags)


def _sigmoid(x):
    return 1.0 / (1.0 + jnp.exp2(x * -LOG2E))


def _softplus(x):
    return jnp.maximum(x, 0.0) + jnp.log1p(jnp.exp(-jnp.abs(x)))


def _cast_rows(src_ref, dst_ref, rows=256):
    k = src_ref.shape[0]
    for r in range(0, k, rows):
        dst_ref[r:r + rows, :] = src_ref[r:r + rows, :].astype(BF16)


def _delay_rows(blk, sh):
    n, w = blk.shape[0] - SUBLANES, blk.shape[1]
    g = blk.reshape(n // SUBLANES + 1, SUBLANES, w)
    rot = pltpu.roll(g, sh, 1)
    row = lax.broadcasted_iota(jnp.int32, (SUBLANES, w), 0)
    out = jnp.where(row < sh, rot[:-1], rot[1:])
    return out.reshape(n, w)


def _rmsnorm_rows(x, g):
    ms = jnp.mean(x * x, axis=-1, keepdims=True)
    return x * lax.rsqrt(ms + EPS) * g


def _norm_dt_kernel(x_ref, g_ref, wdtt_ref, xn_ref, dtt_ref):
    xn = _rmsnorm_rows(x_ref[...], g_ref[...]).astype(BF16)
    xn_ref[...] = xn
    dtt_ref[...] = lax.dot_general(wdtt_ref[...], xn, (((1,), (1,)), ((), ())),
                                   preferred_element_type=F32)


def _norm_dt(x2, g, wdtt, tm, batch, seq):
    t, d = x2.shape
    assert seq % tm == 0
    per_seq = seq // tm
    return pl.pallas_call(
        _norm_dt_kernel,
        grid=(t // tm,),
        in_specs=[pl.BlockSpec((tm, d), lambda i: (i, 0)),
                  pl.BlockSpec((1, d), lambda i: (0, 0)),
                  pl.BlockSpec((LANES, d), lambda i: (0, 0))],
        out_specs=[pl.BlockSpec((tm, d), lambda i: (i, 0)),
                   pl.BlockSpec((None, LANES, tm), lambda i: (i // per_seq, 0, i % per_seq))],
        out_shape=[jax.ShapeDtypeStruct((t, d), BF16),
                   jax.ShapeDtypeStruct((batch, LANES, seq), F32)],
        compiler_params=_params(1),
        name="norm_dt",
    )(x2, g, wdtt)


def _norm_kernel(x_ref, g_ref, o_ref):
    o_ref[...] = _rmsnorm_rows(x_ref[...], g_ref[...]).astype(o_ref.dtype)


def _norm(x2, g, out_dtype, tm, name):
    t, d = x2.shape
    return pl.pallas_call(
        _norm_kernel,
        grid=(t // tm,),
        in_specs=[pl.BlockSpec((tm, d), lambda i: (i, 0)),
                  pl.BlockSpec((1, d), lambda i: (0, 0))],
        out_specs=pl.BlockSpec((tm, d), lambda i: (i, 0)),
        out_shape=jax.ShapeDtypeStruct((t, d), out_dtype),
        compiler_params=_params(1),
        name=name,
    )(x2, g)


def _in_proj_kernel(a_ref, wt_ref, o_ref, wbf_ref):
    @pl.when(pl.program_id(1) == 0)
    def _():
        _cast_rows(wt_ref, wbf_ref)

    o_ref[...] = lax.dot_general(a_ref[...], wbf_ref[...], (((1,), (1,)), ((), ())),
                                 preferred_element_type=F32).astype(o_ref.dtype)


def _in_proj(xn, wt, tm, tn):
    t, k = xn.shape
    u = 1024 // tn

    def wmap(j, m):
        return (jnp.where(j < 2 * u, j, jnp.where(j < 8 * u, j + u, j - 6 * u)), 0)

    return pl.pallas_call(
        _in_proj_kernel,
        grid=(MAIN_PROJ // tn, t // tm),
        in_specs=[pl.BlockSpec((tm, k), lambda j, m: (m, 0)),
                  pl.BlockSpec((tn, k), wmap)],
        out_specs=pl.BlockSpec((tm, tn), lambda j, m: (m, j)),
        out_shape=jax.ShapeDtypeStruct((t, MAIN_PROJ), BF16),
        scratch_shapes=[pltpu.VMEM((tn, k), BF16)],
        compiler_params=_params(2),
        name="in_proj",
    )(xn, wt)


def _rope_tables(seq):
    half = ROT_DIM // 2
    inv = 1.0 / (ROPE_THETA ** (jnp.arange(0, ROT_DIM, 2, dtype=F32) / ROT_DIM))
    ang = jnp.arange(seq, dtype=F32)[:, None] * inv[None, :]
    cos, sin = jnp.cos(ang), jnp.sin(ang)
    d = np.arange(LANES) % HEAD_DIM
    idx = d % half
    in_rot = jnp.asarray(d < ROT_DIM)
    first = jnp.asarray(d < half)
    second = jnp.asarray((d >= half) & (d < ROT_DIM))
    c = jnp.where(in_rot[None, :], cos[:, idx], 1.0)
    s1 = jnp.where(first[None, :], -sin[:, idx], 0.0)
    s2 = jnp.where(second[None, :], sin[:, idx], 0.0)
    return jnp.stack([c, s1, s2]).astype(F32)


def _rope(x, c, s1, s2):
    half = ROT_DIM // 2
    return x * c + pltpu.roll(x, LANES - half, 1) * s1 + pltpu.roll(x, half, 1) * s2


def _attn_kernel(sinks_ref, q_ref, kp_ref, kc_ref, vp_ref, vc_ref,
                 tc_ref, tp_ref, g_ref, o_ref,
                 lhs_ref, kb_ref, vb_ref, s_ref, p_ref, o2_ref, m_ref, tq_ref, bias_ref, acc_ref, *, qb):
    blk = ATTN_BLOCK
    n = pl.program_id(1)
    lo = lax.broadcasted_iota(jnp.int32, (blk, LANES), 1) < HALF

    scale = HEAD_DIM ** -0.5 * LOG2E
    for i in range(3):
        tq_ref[i] = tc_ref[i] * scale
    qi = lax.broadcasted_iota(jnp.int32, (blk, 2 * blk), 0)
    kj = lax.broadcasted_iota(jnp.int32, (blk, 2 * blk), 1)
    rel = qi + blk - kj
    band = (rel >= 0) & (rel < WINDOW)
    kmin = jnp.where(n > 0, 0, blk)
    bias_ref[0] = jnp.where(band & (kj >= kmin), 0.0, NEG)
    bias_ref[1] = jnp.where(band, 0.0, NEG)

    for j in range(qb):
        rows = slice(j * blk, (j + 1) * blk)
        for col in range(ATTN_WIDTH // LANES):
            h, jj = col // 2, col % 2
            q2 = _rope(q_ref[rows, col * LANES:(col + 1) * LANES].astype(F32),
                       tq_ref[0, rows, :], tq_ref[1, rows, :], tq_ref[2, rows, :])
            lhs_ref[j, h, (2 * jj) * blk:(2 * jj + 1) * blk, :] = jnp.where(lo, q2, 0.0).astype(BF16)
            lhs_ref[j, h, (2 * jj + 1) * blk:(2 * jj + 2) * blk, :] = jnp.where(lo, 0.0, q2).astype(BF16)
    nk = (qb + 1) * blk
    lok = lax.broadcasted_iota(jnp.int32, (nk, LANES), 1) < HALF
    ones = jnp.ones((nk, LANES), BF16)
    for i in range(N_KV_HEADS // 2):
        cols = slice(i * LANES, (i + 1) * LANES)
        kcat = jnp.concatenate([_rope(kp_ref[:, cols].astype(F32), tp_ref[0], tp_ref[1], tp_ref[2]),
                                _rope(kc_ref[:, cols].astype(F32), tc_ref[0], tc_ref[1], tc_ref[2])], axis=0)
        vcat = jnp.concatenate([vp_ref[:, cols], vc_ref[:, cols]], axis=0).astype(F32)
        kswp = pltpu.roll(kcat, HALF, 1)
        vswp = pltpu.roll(vcat, HALF, 1)
        kb_ref[2 * i] = jnp.where(lok, kcat, kswp).astype(BF16)
        kb_ref[2 * i + 1] = jnp.where(lok, kswp, kcat).astype(BF16)
        vb_ref[2 * i, :, 0:LANES] = jnp.where(lok, vcat, vswp).astype(BF16)
        vb_ref[2 * i + 1, :, 0:LANES] = jnp.where(lok, vswp, vcat).astype(BF16)
        vb_ref[2 * i, :, LANES:2 * LANES] = ones
        vb_ref[2 * i + 1, :, LANES:2 * LANES] = ones

    def keys(j):
        return slice(j * blk, (j + 2) * blk)

    def scores(j, h):
        s_ref[j, h % 2] = lax.dot_general(lhs_ref[j, h], kb_ref[h, keys(j), :], (((1,), (1,)), ((), ())),
                                          preferred_element_type=F32)

    def row_max(j, h):
        slot = h % 2
        for r in range(Q_PER_KV):
            rows = slice(r * blk, (r + 1) * blk)
            sink = sinks_ref[Q_PER_KV * h + r] * LOG2E
            sr = s_ref[j, slot, rows, :] + bias_ref[min(j, 1)]
            s_ref[j, slot, rows, :] = sr
            m = jnp.maximum(jnp.max(sr, axis=-1, keepdims=True), sink)
            m_ref[j, slot, rows, :] = jnp.broadcast_to(m, (blk, LANES))

    def probs(j, h):
        slot = h % 2
        for r in range(Q_PER_KV):
            rows = slice(r * blk, (r + 1) * blk)
            m = m_ref[j, slot, rows, :]
            for half in range(2):
                ln = slice(half * LANES, (half + 1) * LANES)
                p_ref[j, slot, rows, ln] = jnp.exp2(s_ref[j, slot, rows, ln] - m).astype(BF16)

    def weighted(j, h):
        o2_ref[j, h % 2] = jnp.dot(p_ref[j, h % 2], vb_ref[h, keys(j), :],
                                   preferred_element_type=F32)

    def finish(j, h):
        slot = h % 2
        outs = []
        for r in range(Q_PER_KV):
            rows = slice(r * blk, (r + 1) * blk)
            sink = sinks_ref[Q_PER_KV * h + r] * LOG2E
            den = o2_ref[j, slot, rows, LANES:2 * LANES] + jnp.exp2(sink - m_ref[j, slot, rows, :])
            outs.append(o2_ref[j, slot, rows, 0:LANES] * (1.0 / den))
        for jj in range(2):
            col = 2 * h + jj
            acc_ref[j * blk:(j + 1) * blk, col * LANES:(col + 1) * LANES] = jnp.where(
                lo, outs[2 * jj], outs[2 * jj + 1])

    for j in range(qb):
        scores(j, 0)
    for h in range(N_KV_HEADS + 1):
        for stage in (scores, row_max, probs, weighted, finish):
            hh = {scores: h + 1, finish: h - 1}.get(stage, h)
            if 0 <= hh < N_KV_HEADS:
                for j in range(qb):
                    stage(j, hh)

    o_ref[...] = _rmsnorm_rows(acc_ref[...], g_ref[...]).astype(o_ref.dtype)


def _attention(proj, sinks, g, tables, batch, seq, qb):
    t = proj.shape[0]
    blk = ATTN_BLOCK
    assert seq % (qb * blk) == 0
    nb = seq // blk
    ns = nb // qb
    kblk = K_OFF // KV_WIDTH
    vblk = V_OFF // KV_WIDTH

    def cur(b, n):
        return (b * ns + n, 0)

    def prev_rows(b, n):
        return jnp.maximum(b * nb + n * qb - 1, 0)

    return pl.pallas_call(
        functools.partial(_attn_kernel, qb=qb),
        grid=(batch, ns),
        in_specs=[pl.BlockSpec(memory_space=pltpu.SMEM),
                  pl.BlockSpec((qb * blk, ATTN_WIDTH), cur),
                  pl.BlockSpec((blk, KV_WIDTH), lambda b, n: (prev_rows(b, n), kblk)),
                  pl.BlockSpec((qb * blk, KV_WIDTH), lambda b, n: (b * ns + n, kblk)),
                  pl.BlockSpec((blk, KV_WIDTH), lambda b, n: (prev_rows(b, n), vblk)),
                  pl.BlockSpec((qb * blk, KV_WIDTH), lambda b, n: (b * ns + n, vblk)),
                  pl.BlockSpec((3, qb * blk, LANES), lambda b, n: (0, n, 0)),
                  pl.BlockSpec((3, blk, LANES), lambda b, n: (0, jnp.maximum(n * qb - 1, 0), 0)),
                  pl.BlockSpec((1, ATTN_WIDTH), lambda b, n: (0, 0))],
        out_specs=pl.BlockSpec((qb * blk, ATTN_WIDTH), cur),
        out_shape=jax.ShapeDtypeStruct((t, ATTN_WIDTH), BF16),
        scratch_shapes=[pltpu.VMEM((qb, N_KV_HEADS, Q_PER_KV * blk, LANES), BF16),
                        pltpu.VMEM((N_KV_HEADS, (qb + 1) * blk, LANES), BF16),
                        pltpu.VMEM((N_KV_HEADS, (qb + 1) * blk, 2 * LANES), BF16),
                        pltpu.VMEM((qb, 2, Q_PER_KV * blk, 2 * blk), F32),
                        pltpu.VMEM((qb, 2, Q_PER_KV * blk, 2 * blk), BF16),
                        pltpu.VMEM((qb, 2, Q_PER_KV * blk, 2 * LANES), F32),
                        pltpu.VMEM((qb, 2, Q_PER_KV * blk, LANES), F32),
                        pltpu.VMEM((3, qb * blk, LANES), F32),
                        pltpu.VMEM((2, blk, 2 * blk), F32),
                        pltpu.VMEM((qb * blk, ATTN_WIDTH), F32)],
        compiler_params=_params(2),
        name="swa_attention",
    )(sinks, proj, proj, proj, proj, proj, tables, tables, g)


def _split3(x):
    h = x.astype(BF16)
    r = x - h.astype(F32)
    m = r.astype(BF16)
    l = (r - m.astype(F32)).astype(BF16)
    return h, m, l


def _dot3_lhs(x, w):
    h, m, l = _split3(x)
    d = lambda a: jnp.dot(a, w, preferred_element_type=F32)
    return (d(l) + d(m)) + d(h)


def _ssd_kernel(xbc_ref, z_ref, dtt_ref, cw_ref, cb_ref, bcol_ref, alcol_ref, e_ref, dfull_ref, gn_ref, o_ref,
                ext_ref, xact_ref, state_ref, xdt_ref, xdec_ref, y_ref, exp_ref, acs_ref, acst_ref, *, nbat):
    L = SSD_CHUNK
    c = pl.program_id(0)
    cw_chunk = 512
    gw = SSD_HEAD_DIM * (SSD_HEADS // SSD_GROUPS)
    ri = lax.broadcasted_iota(jnp.int32, (L, L), 0)
    ci = lax.broadcasted_iota(jnp.int32, (L, L), 1)
    causal = ri >= ci
    lo = ci < HALF

    @pl.when(c == 0)
    def _():
        ext_ref[:, 0:SUBLANES, :] = jnp.zeros((nbat, SUBLANES, CONV_CH), F32)
        state_ref[...] = jnp.zeros_like(state_ref)

    def conv(b):
        ext_ref[b, SUBLANES:SUBLANES + L, :] = xbc_ref[b].astype(F32)
        for j in range(CONV_CH // cw_chunk):
            cs = slice(j * cw_chunk, (j + 1) * cw_chunk)
            blk = ext_ref[b, :, cs]
            acc = cb_ref[:, cs]
            for k in range(SSD_CONV - 1):
                acc = acc + _delay_rows(blk, SSD_CONV - 1 - k) * cw_ref[k:k + 1, cs]
            acc = acc + blk[SUBLANES:] * cw_ref[SSD_CONV - 1:SSD_CONV, cs]
            xact_ref[b, :, cs] = acc * _sigmoid(acc)
        ext_ref[b, 0:SUBLANES, :] = ext_ref[b, L:L + SUBLANES, :]

    def decay(b):
        nh = SSD_HEADS
        dtt = _softplus(dtt_ref[b, 0:nh, :] + bcol_ref[0:nh, :])
        dat = dtt * (-jnp.exp(alcol_ref[0:nh, :]))
        tri_u = jnp.where(ri <= ci, 1.0, 0.0).astype(BF16)
        acst = _dot3_lhs(dat, tri_u) * LOG2E
        acst_ref[b, 0:nh, :] = acst
        pad = jnp.zeros((LANES - nh, L), F32)
        a_cs = jnp.concatenate([acst, pad], axis=0).T
        dt = jnp.concatenate([dtt, pad], axis=0).T
        acs_ref[b] = a_cs
        a_last = a_cs[L - 1:L, :]
        stack = jnp.concatenate([dt, jnp.exp2(a_last - a_cs), jnp.exp2(a_cs),
                                 jnp.broadcast_to(jnp.exp2(a_last), (SUBLANES, LANES))], axis=0)
        hi = stack.astype(BF16)
        lo = (stack - hi.astype(F32)).astype(BF16)
        split = jnp.concatenate([hi, lo], axis=1)
        for j in range(SSD_INNER // cw_chunk):
            cs = slice(j * cw_chunk, (j + 1) * cw_chunk)
            exp_ref[b, :, cs] = jnp.dot(split, e_ref[:, cs], preferred_element_type=F32)

    def scale_x(b):
        for j in range(SSD_INNER // cw_chunk):
            cs = slice(j * cw_chunk, (j + 1) * cw_chunk)
            xdt = xact_ref[b, :, cs] * exp_ref[b, 0:L, cs]
            xdt_ref[b, :, cs] = xdt.astype(BF16)
            xdec_ref[b, :, cs] = (xdt * exp_ref[b, L:2 * L, cs]).astype(BF16)

    def group(b, g):
        bg = xact_ref[b, :, SSD_INNER + g * SSD_STATE:SSD_INNER + (g + 1) * SSD_STATE]
        cg = xact_ref[b, :, SSD_INNER + BC_WIDTH + g * SSD_STATE:SSD_INNER + BC_WIDTH + (g + 1) * SSD_STATE]
        bb = bg.astype(BF16)
        cbf = cg.astype(BF16)
        cbm = lax.dot_general(cbf, bb, (((1,), (1,)), ((), ())), preferred_element_type=F32)
        gs = slice(g * gw, (g + 1) * gw)
        prev = state_ref[b, g]
        yoff = jnp.dot(cbf, prev.astype(BF16), preferred_element_type=F32) * exp_ref[b, 2 * L:3 * L, gs]
        btb = bg.T.astype(BF16)
        state_ref[b, g] = prev * exp_ref[b, 3 * L:3 * L + 1, gs] + jnp.dot(
            btb, xdec_ref[b, :, gs], preferred_element_type=F32)
        for e in range(2):
            pc = slice((2 * g + e) * LANES, (2 * g + e + 1) * LANES)
            xpair = xdt_ref[b, :, pc]
            yd = []
            for r in range(2):
                h = 4 * g + 2 * e + r
                seg = acs_ref[b, :, h:h + 1] - acst_ref[b, h:h + 1, :]
                lm = jnp.exp2(jnp.where(causal, seg, NEG))
                yd.append(jnp.dot((cbm * lm).astype(BF16), xpair, preferred_element_type=F32))
            y_ref[b, :, pc] = (jnp.where(lo, yd[0], yd[1]) + yoff[:, e * LANES:(e + 1) * LANES]
                               + dfull_ref[:, pc] * xact_ref[b, :, pc])

    def gate_norm(b, g):
        gs = slice(g * gw, (g + 1) * gw)
        zz = z_ref[b, :, gs].astype(F32)
        yg = y_ref[b, :, gs] * (zz * _sigmoid(zz))
        ms = jnp.mean(yg * yg, axis=-1, keepdims=True)
        o_ref[b, :, gs] = (yg * lax.rsqrt(ms + EPS) * gn_ref[:, gs]).astype(o_ref.dtype)

    for stage in (decay, conv, scale_x):
        for b in range(nbat):
            stage(b)
    for stage in (group, gate_norm):
        for g in range(SSD_GROUPS):
            for b in range(nbat):
                stage(b, g)


def _ssd(proj, dt_rawt, cw, cb, bcol, alcol, emat, dfull, gn, batch, seq):
    L = SSD_CHUNK
    nc = seq // L
    proj3 = proj.reshape(batch, seq, MAIN_PROJ)
    full = lambda shape: pl.BlockSpec(shape, lambda c: (0, 0))
    out = pl.pallas_call(
        functools.partial(_ssd_kernel, nbat=batch),
        grid=(nc,),
        in_specs=[pl.BlockSpec((batch, L, CONV_CH), lambda c: (0, c, XBC_OFF // CONV_CH)),
                  pl.BlockSpec((batch, L, SSD_INNER), lambda c: (0, c, Z_OFF // SSD_INNER)),
                  pl.BlockSpec((batch, LANES, L), lambda c: (0, 0, c)),
                  full((SSD_CONV, CONV_CH)), full((1, CONV_CH)),
                  full((LANES, 1)), full((LANES, 1)),
                  full((2 * LANES, SSD_INNER)), full((1, SSD_INNER)), full((1, SSD_INNER))],
        out_specs=pl.BlockSpec((batch, L, SSD_INNER), lambda c: (0, c, 0)),
        out_shape=jax.ShapeDtypeStruct((batch, seq, SSD_INNER), BF16),
        scratch_shapes=[pltpu.VMEM((batch, L + SUBLANES, CONV_CH), F32),
                        pltpu.VMEM((batch, L, CONV_CH), F32),
                        pltpu.VMEM((batch, SSD_GROUPS, SSD_STATE, SSD_INNER // SSD_GROUPS), F32),
                        pltpu.VMEM((batch, L, SSD_INNER), BF16),
                        pltpu.VMEM((batch, L, SSD_INNER), BF16),
                        pltpu.VMEM((batch, L, SSD_INNER), F32),
                        pltpu.VMEM((batch, 3 * L + SUBLANES, SSD_INNER), F32),
                        pltpu.VMEM((batch, L, LANES), F32),
                        pltpu.VMEM((batch, LANES, L), F32)],
        compiler_params=_params(1),
        name="ssd_scan",
    )(proj3, proj3, dt_rawt, cw, cb, bcol, alcol, emat, dfull, gn)
    return out.reshape(batch * seq, SSD_INNER)


def _out_proj_kernel(a1_ref, a2_ref, w1_ref, w2_ref, r_ref, o_ref, w1b_ref, w2b_ref):
    @pl.when(pl.program_id(1) == 0)
    def _():
        _cast_rows(w1_ref, w1b_ref)
        _cast_rows(w2_ref, w2b_ref)

    acc = jnp.dot(a1_ref[...], w1b_ref[...], preferred_element_type=F32)
    acc = acc + jnp.dot(a2_ref[...], w2b_ref[...], preferred_element_type=F32)
    o_ref[...] = r_ref[...] + acc


def _out_proj(a1, a2, w, res, tm, tn):
    t, k = a1.shape
    n = w.shape[1]
    return pl.pallas_call(
        _out_proj_kernel,
        grid=(n // tn, t // tm),
        in_specs=[pl.BlockSpec((tm, k), lambda j, m: (m, 0)),
                  pl.BlockSpec((tm, k), lambda j, m: (m, 0)),
                  pl.BlockSpec((k, tn), lambda j, m: (0, j)),
                  pl.BlockSpec((k, tn), lambda j, m: (1, j)),
                  pl.BlockSpec((tm, tn), lambda j, m: (m, j))],
        out_specs=pl.BlockSpec((tm, tn), lambda j, m: (m, j)),
        out_shape=jax.ShapeDtypeStruct((t, n), F32),
        scratch_shapes=[pltpu.VMEM((k, tn), BF16), pltpu.VMEM((k, tn), BF16)],
        compiler_params=_params(2),
        name="out_proj",
    )(a1, a2, w, w, res)


def _up_kernel(a_ref, wg_ref, wv_ref, cwg_ref, cwv_ref, cbg_ref, cbv_ref, o_ref,
               wgb_ref, wvb_ref, carry_ref, *, tm, seq):
    m = pl.program_id(1)

    @pl.when(m == 0)
    def _():
        _cast_rows(wg_ref, wgb_ref)
        _cast_rows(wv_ref, wvb_ref)

    a = a_ref[...]
    seq_start = (m * tm) % seq == 0
    outs = []
    for idx, (wb_ref, cw_ref, cb_ref) in enumerate(((wgb_ref, cwg_ref, cbg_ref),
                                                    (wvb_ref, cwv_ref, cbv_ref))):
        u = jnp.dot(a, wb_ref[...], preferred_element_type=F32)
        above = jnp.where(seq_start, 0.0, carry_ref[idx])
        blk = jnp.concatenate([above, u], axis=0)
        cw = cw_ref[...]
        conv = cb_ref[...]
        for k in range(FFN_CONV - 1):
            conv = conv + _delay_rows(blk, FFN_CONV - 1 - k) * cw[k:k + 1, :]
        conv = conv + u * cw[FFN_CONV - 1:FFN_CONV, :]
        carry_ref[idx] = u[tm - SUBLANES:tm]
        outs.append(conv)
    gate, val = outs
    o_ref[...] = ((gate * _sigmoid(gate)) * val).astype(o_ref.dtype)


def _up_proj(a, w, cw, cb, tm, tn, seq):
    t, k = a.shape
    assert seq % tm == 0 and D_FF % tn == 0
    nb = D_FF // tn
    return pl.pallas_call(
        functools.partial(_up_kernel, tm=tm, seq=seq),
        grid=(nb, t // tm),
        in_specs=[pl.BlockSpec((tm, k), lambda j, m: (m, 0)),
                  pl.BlockSpec((k, tn), lambda j, m: (0, j)),
                  pl.BlockSpec((k, tn), lambda j, m: (0, j + nb)),
                  pl.BlockSpec((FFN_CONV, tn), lambda j, m: (0, j)),
                  pl.BlockSpec((FFN_CONV, tn), lambda j, m: (0, j + nb)),
                  pl.BlockSpec((1, tn), lambda j, m: (0, j)),
                  pl.BlockSpec((1, tn), lambda j, m: (0, j + nb))],
        out_specs=pl.BlockSpec((tm, tn), lambda j, m: (m, j)),
        out_shape=jax.ShapeDtypeStruct((t, D_FF), BF16),
        scratch_shapes=[pltpu.VMEM((k, tn), BF16), pltpu.VMEM((k, tn), BF16),
                        pltpu.VMEM((2, SUBLANES, tn), F32)],
        compiler_params=_params(2),
        name="up_proj_conv_swiglu",
    )(a, w, w, cw, cw, cb, cb)


def _down_kernel(a_ref, w_ref, r_ref, o_ref, wb_ref):
    @pl.when(pl.program_id(1) == 0)
    def _():
        _cast_rows(w_ref, wb_ref)

    o_ref[...] = r_ref[...] + jnp.dot(a_ref[...], wb_ref[...], preferred_element_type=F32)


def _down_proj(a, w, res, tm, tn):
    t, k = a.shape
    n = w.shape[1]
    return pl.pallas_call(
        _down_kernel,
        grid=(n // tn, t // tm),
        in_specs=[pl.BlockSpec((tm, k), lambda j, m: (m, 0)),
                  pl.BlockSpec((k, tn), lambda j, m: (0, j)),
                  pl.BlockSpec((tm, tn), lambda j, m: (m, j))],
        out_specs=pl.BlockSpec((tm, tn), lambda j, m: (m, j)),
        out_shape=jax.ShapeDtypeStruct((t, n), F32),
        scratch_shapes=[pltpu.VMEM((k, tn), BF16)],
        compiler_params=_params(2),
        name="down_proj",
    )(a, w, res)


def _pad_lanes(v):
    return jnp.pad(v.astype(F32), (0, LANES - v.shape[0]))


def _mixer_layer(h, batch, seq, norm_mix, w_in, sinks, attn_out_norm, ssd_conv_w, ssd_conv_b, dt_bias,
                 a_log, ssd_d, ssd_norm, w_out, tables, emat):
    w_in_t = jnp.swapaxes(w_in, 0, 1)
    wdt_t = jnp.pad(w_in_t[MAIN_PROJ:], ((0, LANES - SSD_HEADS), (0, 0))).astype(BF16)
    xn, dt_rawt = _norm_dt(h, norm_mix.reshape(1, -1), wdt_t, 512, batch, seq)
    proj = _in_proj(xn, w_in_t, tm=1024, tn=1024)
    attn = _attention(proj, sinks.astype(F32), attn_out_norm.reshape(1, -1), tables, batch, seq, qb=4)
    bias = _pad_lanes(dt_bias)
    alog = _pad_lanes(a_log)
    y = _ssd(proj, dt_rawt, ssd_conv_w, ssd_conv_b.reshape(1, -1), bias.reshape(-1, 1), alog.reshape(-1, 1),
             emat, jnp.repeat(ssd_d.astype(F32), SSD_HEAD_DIM).reshape(1, -1),
             ssd_norm.reshape(1, -1), batch, seq)
    return _out_proj(attn, y, w_out, h, tm=1024, tn=512)


def kernel(x, norm_mix, w_in, sinks, attn_out_norm, ssd_conv_w, ssd_conv_b, dt_bias, a_log, ssd_d, ssd_norm,
           w_out, norm_ffn, w_up, ffn_conv_w, ffn_conv_b, w_down, norm_final):
    batch, seq, d = x.shape
    h = x.reshape(batch * seq, d)
    tables = _rope_tables(seq)
    head_of_channel = np.arange(SSD_INNER) // SSD_HEAD_DIM
    emat = np.arange(LANES)[:, None] == head_of_channel[None, :]
    emat = jnp.asarray(np.concatenate([emat, emat], axis=0), dtype=BF16)
    for l in range(norm_mix.shape[0]):
        h = _mixer_layer(h, batch, seq, norm_mix[l], w_in[l], sinks[l], attn_out_norm[l], ssd_conv_w[l],
                         ssd_conv_b[l], dt_bias[l], a_log[l], ssd_d[l], ssd_norm[l], w_out[l], tables, emat)
        hn = _norm(h, norm_ffn[l].reshape(1, -1), BF16, 512, "norm_ffn")
        act = _up_proj(hn, w_up[l], ffn_conv_w[l], ffn_conv_b[l].reshape(1, -1),
                       tm=1024, tn=512, seq=seq)
        h = _down_proj(act, w_down[l], h, tm=512, tn=512)
    out = _norm(h, norm_final.reshape(1, -1), F32, 512, "norm_final")
    return out.reshape(batch, seq, d)
```

```python
import functools

import numpy as np
import jax
import jax.numpy as jnp
from jax import lax
from jax.experimental import pallas as pl
from jax.experimental.pallas import tpu as pltpu

F32 = jnp.float32
BF16 = jnp.bfloat16

D_MODEL = 2048
N_Q_HEADS = 32
N_KV_HEADS = 8
HEAD_DIM = 64
Q_PER_KV = N_Q_HEADS // N_KV_HEADS
WINDOW = 128
ATTN_BLOCK = 128
ROT_DIM = HEAD_DIM // 4
ROPE_THETA = 500000.0
SSD_HEADS = 32
SSD_HEAD_DIM = 64
SSD_INNER = SSD_HEADS * SSD_HEAD_DIM
SSD_GROUPS = 8
SSD_STATE = 128
SSD_CONV = 4
SSD_CHUNK = 128
ATTN_WIDTH = N_Q_HEADS * HEAD_DIM
KV_WIDTH = N_KV_HEADS * HEAD_DIM
BC_WIDTH = SSD_GROUPS * SSD_STATE
CONV_CH = SSD_INNER + 2 * BC_WIDTH
MAIN_PROJ = ATTN_WIDTH + 2 * KV_WIDTH + SSD_INNER + CONV_CH
D_FF = 5632
FFN_CONV = 3
EPS = 1e-6

LANES = 128
SUBLANES = 8
HALF = LANES // 2
NEG = -1e30
LOG2E = 1.4426950408889634
VMEM_LIMIT = 56 * 1024 * 1024

Q_OFF = 0
Z_OFF = ATTN_WIDTH
XBC_OFF = Z_OFF + SSD_INNER
K_OFF = XBC_OFF + CONV_CH
V_OFF = K_OFF + KV_WIDTH


def _params(n_axes, flags=None):
    return pltpu.CompilerParams(dimension_semantics=("arbitrary",) * n_axes,
                                vmem_limit_bytes=VMEM_LIMIT, flags=flags)


def _sigmoid(x):
    return 1.0 / (1.0 + jnp.exp2(x * -LOG2E))


def _softplus(x):
    return jnp.maximum(x, 0.0) + jnp.log1p(jnp.exp(-jnp.abs(x)))


def _cast_rows(src_ref, dst_ref, rows=256):
    k = src_ref.shape[0]
    for r in range(0, k, rows):
        dst_ref[r:r + rows, :] = src_ref[r:r + rows, :].astype(BF16)


def _delay_rows(blk, sh):
    n, w = blk.shape[0] - SUBLANES, blk.shape[1]
    g = blk.reshape(n // SUBLANES + 1, SUBLANES, w)
    rot = pltpu.roll(g, sh, 1)
    row = lax.broadcasted_iota(jnp.int32, (SUBLANES, w), 0)
    out = jnp.where(row < sh, rot[:-1], rot[1:])
    return out.reshape(n, w)


def _rmsnorm_rows(x, g):
    ms = jnp.mean(x * x, axis=-1, keepdims=True)
    return x * lax.rsqrt(ms + EPS) * g


def _norm_dt_kernel(x_ref, g_ref, wdtt_ref, xn_ref, dtt_ref):
    xn = _rmsnorm_rows(x_ref[...], g_ref[...]).astype(BF16)
    xn_ref[...] = xn
    dtt_ref[...] = lax.dot_general(wdtt_ref[...], xn, (((1,), (1,)), ((), ())),
                                   preferred_element_type=F32)


def _norm_dt(x2, g, wdtt, tm, batch, seq):
    t, d = x2.shape
    assert seq % tm == 0
    per_seq = seq // tm
    return pl.pallas_call(
        _norm_dt_kernel,
        grid=(t // tm,),
        in_specs=[pl.BlockSpec((tm, d), lambda i: (i, 0)),
                  pl.BlockSpec((1, d), lambda i: (0, 0)),
                  pl.BlockSpec((LANES, d), lambda i: (0, 0))],
        out_specs=[pl.BlockSpec((tm, d), lambda i: (i, 0)),
                   pl.BlockSpec((None, LANES, tm), lambda i: (i // per_seq, 0, i % per_seq))],
        out_shape=[jax.ShapeDtypeStruct((t, d), BF16),
                   jax.ShapeDtypeStruct((batch, LANES, seq), F32)],
        compiler_params=_params(1),
        name="norm_dt",
    )(x2, g, wdtt)


def _norm_kernel(x_ref, g_ref, o_ref):
    o_ref[...] = _rmsnorm_rows(x_ref[...], g_ref[...]).astype(o_ref.dtype)


def _norm(x2, g, out_dtype, tm, name):
    t, d = x2.shape
    return pl.pallas_call(
        _norm_kernel,
        grid=(t // tm,),
        in_specs=[pl.BlockSpec((tm, d), lambda i: (i, 0)),
                  pl.BlockSpec((1, d), lambda i: (0, 0))],
        out_specs=pl.BlockSpec((tm, d), lambda i: (i, 0)),
        out_shape=jax.ShapeDtypeStruct((t, d), out_dtype),
        compiler_params=_params(1),
        name=name,
    )(x2, g)


def _in_proj_kernel(a_ref, wt_ref, o_ref, wbf_ref):
    @pl.when(pl.program_id(1) == 0)
    def _():
        _cast_rows(wt_ref, wbf_ref)

    o_ref[...] = lax.dot_general(a_ref[...], wbf_ref[...], (((1,), (1,)), ((), ())),
                                 preferred_element_type=F32).astype(o_ref.dtype)


def _in_proj(xn, wt, tm, tn):
    t, k = xn.shape
    u = 1024 // tn

    def wmap(j, m):
        return (jnp.where(j < 2 * u, j, jnp.where(j < 8 * u, j + u, j - 6 * u)), 0)

    return pl.pallas_call(
        _in_proj_kernel,
        grid=(MAIN_PROJ // tn, t // tm),
        in_specs=[pl.BlockSpec((tm, k), lambda j, m: (m, 0)),
                  pl.BlockSpec((tn, k), wmap)],
        out_specs=pl.BlockSpec((tm, tn), lambda j, m: (m, j)),
        out_shape=jax.ShapeDtypeStruct((t, MAIN_PROJ), BF16),
        scratch_shapes=[pltpu.VMEM((tn, k), BF16)],
        compiler_params=_params(2),
        name="in_proj",
    )(xn, wt)


def _rope_tables(seq):
    half = ROT_DIM // 2
    inv = 1.0 / (ROPE_THETA ** (jnp.arange(0, ROT_DIM, 2, dtype=F32) / ROT_DIM))
    ang = jnp.arange(seq, dtype=F32)[:, None] * inv[None, :]
    cos, sin = jnp.cos(ang), jnp.sin(ang)
    d = np.arange(LANES) % HEAD_DIM
    idx = d % half
    in_rot = jnp.asarray(d < ROT_DIM)
    first = jnp.asarray(d < half)
    second = jnp.asarray((d >= half) & (d < ROT_DIM))
    c = jnp.where(in_rot[None, :], cos[:, idx], 1.0)
    s1 = jnp.where(first[None, :], -sin[:, idx], 0.0)
    s2 = jnp.where(second[None, :], sin[:, idx], 0.0)
    return jnp.stack([c, s1, s2]).astype(F32)


def _rope(x, c, s1, s2):
    half = ROT_DIM // 2
    return x * c + pltpu.roll(x, LANES - half, 1) * s1 + pltpu.roll(x, half, 1) * s2


def _attn_kernel(sinks_ref, q_ref, kp_ref, kc_ref, vp_ref, vc_ref,
                 tc_ref, tp_ref, g_ref, o_ref,
                 lhs_ref, kb_ref, vb_ref, s_ref, p_ref, o2_ref, m_ref, tq_ref, bias_ref, acc_ref, *, qb):
    blk = ATTN_BLOCK
    n = pl.program_id(1)
    lo = lax.broadcasted_iota(jnp.int32, (blk, LANES), 1) < HALF

    scale = HEAD_DIM ** -0.5 * LOG2E
    for i in range(3):
        tq_ref[i] = tc_ref[i] * scale
    qi = lax.broadcasted_iota(jnp.int32, (blk, 2 * blk), 0)
    kj = lax.broadcasted_iota(jnp.int32, (blk, 2 * blk), 1)
    rel = qi + blk - kj
    band = (rel >= 0) & (rel < WINDOW)
    kmin = jnp.where(n > 0, 0, blk)
    bias_ref[0] = jnp.where(band & (kj >= kmin), 0.0, NEG)
    bias_ref[1] = jnp.where(band, 0.0, NEG)

    for j in range(qb):
        rows = slice(j * blk, (j + 1) * blk)
        for col in range(ATTN_WIDTH // LANES):
            h, jj = col // 2, col % 2
            q2 = _rope(q_ref[rows, col * LANES:(col + 1) * LANES].astype(F32),
                       tq_ref[0, rows, :], tq_ref[1, rows, :], tq_ref[2, rows, :])
            lhs_ref[j, h, (2 * jj) * blk:(2 * jj + 1) * blk, :] = jnp.where(lo, q2, 0.0).astype(BF16)
            lhs_ref[j, h, (2 * jj + 1) * blk:(2 * jj + 2) * blk, :] = jnp.where(lo, 0.0, q2).astype(BF16)
    nk = (qb + 1) * blk
    lok = lax.broadcasted_iota(jnp.int32, (nk, LANES), 1) < HALF
    ones = jnp.ones((nk, LANES), BF16)
    for i in range(N_KV_HEADS // 2):
        cols = slice(i * LANES, (i + 1) * LANES)
        kcat = jnp.concatenate([_rope(kp_ref[:, cols].astype(F32), tp_ref[0], tp_ref[1], tp_ref[2]),
                                _rope(kc_ref[:, cols].astype(F32), tc_ref[0], tc_ref[1], tc_ref[2])], axis=0)
        vcat = jnp.concatenate([vp_ref[:, cols], vc_ref[:, cols]], axis=0).astype(F32)
        kswp = pltpu.roll(kcat, HALF, 1)
        vswp = pltpu.roll(vcat, HALF, 1)
        kb_ref[2 * i] = jnp.where(lok, kcat, kswp).astype(BF16)
        kb_ref[2 * i + 1] = jnp.where(lok, kswp, kcat).astype(BF16)
        vb_ref[2 * i, :, 0:LANES] = jnp.where(lok, vcat, vswp).astype(BF16)
        vb_ref[2 * i + 1, :, 0:LANES] = jnp.where(lok, vswp, vcat).astype(BF16)
        vb_ref[2 * i, :, LANES:2 * LANES] = ones
        vb_ref[2 * i + 1, :, LANES:2 * LANES] = ones

    def keys(j):
        return slice(j * blk, (j + 2) * blk)

    def scores(j, h):
        s_ref[j, h % 2] = lax.dot_general(lhs_ref[j, h], kb_ref[h, keys(j), :], (((1,), (1,)), ((), ())),
                                          preferred_element_type=F32)

    def row_max(j, h):
        slot = h % 2
        for r in range(Q_PER_KV):
            rows = slice(r * blk, (r + 1) * blk)
            sink = sinks_ref[Q_PER_KV * h + r] * LOG2E
            sr = s_ref[j, slot, rows, :] + bias_ref[min(j, 1)]
            s_ref[j, slot, rows, :] = sr
            m = jnp.maximum(jnp.max(sr, axis=-1, keepdims=True), sink)
            m_ref[j, slot, rows, :] = jnp.broadcast_to(m, (blk, LANES))

    def probs(j, h):
        slot = h % 2
        for r in range(Q_PER_KV):
            rows = slice(r * blk, (r + 1) * blk)
            m = m_ref[j, slot, rows, :]
            for half in range(2):
                ln = slice(half * LANES, (half + 1) * LANES)
                p_ref[j, slot, rows, ln] = jnp.exp2(s_ref[j, slot, rows, ln] - m).astype(BF16)

    def weighted(j, h):
        o2_ref[j, h % 2] = jnp.dot(p_ref[j, h % 2], vb_ref[h, keys(j), :],
                                   preferred_element_type=F32)

    def finish(j, h):
        slot = h % 2
        outs = []
        for r in range(Q_PER_KV):
            rows = slice(r * blk, (r + 1) * blk)
            sink = sinks_ref[Q_PER_KV * h + r] * LOG2E
            den = o2_ref[j, slot, rows, LANES:2 * LANES] + jnp.exp2(sink - m_ref[j, slot, rows, :])
            outs.append(o2_ref[j, slot, rows, 0:LANES] * (1.0 / den))
        for jj in range(2):
            col = 2 * h + jj
            acc_ref[j * blk:(j + 1) * blk, col * LANES:(col + 1) * LANES] = jnp.where(
                lo, outs[2 * jj], outs[2 * jj + 1])

    for j in range(qb):
        scores(j, 0)
    for h in range(N_KV_HEADS + 1):
        for stage in (scores, row_max, probs, weighted, finish):
            hh = {scores: h + 1, finish: h - 1}.get(stage, h)
            if 0 <= hh < N_KV_HEADS:
                for j in range(qb):
                    stage(j, hh)

    o_ref[...] = _rmsnorm_rows(acc_ref[...], g_ref[...]).astype(o_ref.dtype)


def _attention(proj, sinks, g, tables, batch, seq, qb):
    t = proj.shape[0]
    blk = ATTN_BLOCK
    assert seq % (qb * blk) == 0
    nb = seq // blk
    ns = nb // qb
    kblk = K_OFF // KV_WIDTH
    vblk = V_OFF // KV_WIDTH

    def cur(b, n):
        return (b * ns + n, 0)

    def prev_rows(b, n):
        return jnp.maximum(b * nb + n * qb - 1, 0)

    return pl.pallas_call(
        functools.partial(_attn_kernel, qb=qb),
        grid=(batch, ns),
        in_specs=[pl.BlockSpec(memory_space=pltpu.SMEM),
                  pl.BlockSpec((qb * blk, ATTN_WIDTH), cur),
                  pl.BlockSpec((blk, KV_WIDTH), lambda b, n: (prev_rows(b, n), kblk)),
                  pl.BlockSpec((qb * blk, KV_WIDTH), lambda b, n: (b * ns + n, kblk)),
                  pl.BlockSpec((blk, KV_WIDTH), lambda b, n: (prev_rows(b, n), vblk)),
                  pl.BlockSpec((qb * blk, KV_WIDTH), lambda b, n: (b * ns + n, vblk)),
                  pl.BlockSpec((3, qb * blk, LANES), lambda b, n: (0, n, 0)),
                  pl.BlockSpec((3, blk, LANES), lambda b, n: (0, jnp.maximum(n * qb - 1, 0), 0)),
                  pl.BlockSpec((1, ATTN_WIDTH), lambda b, n: (0, 0))],
        out_specs=pl.BlockSpec((qb * blk, ATTN_WIDTH), cur),
        out_shape=jax.ShapeDtypeStruct((t, ATTN_WIDTH), BF16),
        scratch_shapes=[pltpu.VMEM((qb, N_KV_HEADS, Q_PER_KV * blk, LANES), BF16),
                        pltpu.VMEM((N_KV_HEADS, (qb + 1) * blk, LANES), BF16),
                        pltpu.VMEM((N_KV_HEADS, (qb + 1) * blk, 2 * LANES), BF16),
                        pltpu.VMEM((qb, 2, Q_PER_KV * blk, 2 * blk), F32),
                        pltpu.VMEM((qb, 2, Q_PER_KV * blk, 2 * blk), BF16),
                        pltpu.VMEM((qb, 2, Q_PER_KV * blk, 2 * LANES), F32),
                        pltpu.VMEM((qb, 2, Q_PER_KV * blk, LANES), F32),
                        pltpu.VMEM((3, qb * blk, LANES), F32),
                        pltpu.VMEM((2, blk, 2 * blk), F32),
                        pltpu.VMEM((qb * blk, ATTN_WIDTH), F32)],
        compiler_params=_params(2),
        name="swa_attention",
    )(sinks, proj, proj, proj, proj, proj, tables, tables, g)


def _split3(x):
    h = x.astype(BF16)
    r = x - h.astype(F32)
    m = r.astype(BF16)
    l = (r - m.astype(F32)).astype(BF16)
    return h, m, l


def _dot3_lhs(x, w):
    h, m, l = _split3(x)
    d = lambda a: jnp.dot(a, w, preferred_element_type=F32)
    return (d(l) + d(m)) + d(h)


def _ssd_kernel(xbc_ref, z_ref, dtt_ref, cw_ref, cb_ref, bcol_ref, alcol_ref, e_ref, dfull_ref, gn_ref, o_ref,
                ext_ref, xact_ref, state_ref, xdt_ref, xdec_ref, y_ref, exp_ref, acs_ref, acst_ref, *, nbat):
    L = SSD_CHUNK
    c = pl.program_id(0)
    cw_chunk = 512
    gw = SSD_HEAD_DIM * (SSD_HEADS // SSD_GROUPS)
    ri = lax.broadcasted_iota(jnp.int32, (L, L), 0)
    ci = lax.broadcasted_iota(jnp.int32, (L, L), 1)
    causal = ri >= ci
    lo = ci < HALF

    @pl.when(c == 0)
    def _():
        ext_ref[:, 0:SUBLANES, :] = jnp.zeros((nbat, SUBLANES, CONV_CH), F32)
        state_ref[...] = jnp.zeros_like(state_ref)

    def conv(b):
        ext_ref[b, SUBLANES:SUBLANES + L, :] = xbc_ref[b].astype(F32)
        for j in range(CONV_CH // cw_chunk):
            cs = slice(j * cw_chunk, (j + 1) * cw_chunk)
            blk = ext_ref[b, :, cs]
            acc = cb_ref[:, cs]
            for k in range(SSD_CONV - 1):
                acc = acc + _delay_rows(blk, SSD_CONV - 1 - k) * cw_ref[k:k + 1, cs]
            acc = acc + blk[SUBLANES:] * cw_ref[SSD_CONV - 1:SSD_CONV, cs]
            xact_ref[b, :, cs] = acc * _sigmoid(acc)
        ext_ref[b, 0:SUBLANES, :] = ext_ref[b, L:L + SUBLANES, :]

    def decay(b):
        nh = SSD_HEADS
        dtt = _softplus(dtt_ref[b, 0:nh, :] + bcol_ref[0:nh, :])
        dat = dtt * (-jnp.exp(alcol_ref[0:nh, :]))
        tri_u = jnp.where(ri <= ci, 1.0, 0.0).astype(BF16)
        acst = _dot3_lhs(dat, tri_u) * LOG2E
        acst_ref[b, 0:nh, :] = acst
        pad = jnp.zeros((LANES - nh, L), F32)
        a_cs = jnp.concatenate([acst, pad], axis=0).T
        dt = jnp.concatenate([dtt, pad], axis=0).T
        acs_ref[b] = a_cs
        a_last = a_cs[L - 1:L, :]
        stack = jnp.concatenate([dt, jnp.exp2(a_last - a_cs), jnp.exp2(a_cs),
                                 jnp.broadcast_to(jnp.exp2(a_last), (SUBLANES, LANES))], axis=0)
        hi = stack.astype(BF16)
        lo = (stack - hi.astype(F32)).astype(BF16)
        split = jnp.concatenate([hi, lo], axis=1)
        for j in range(SSD_INNER // cw_chunk):
            cs = slice(j * cw_chunk, (j + 1) * cw_chunk)
            exp_ref[b, :, cs] = jnp.dot(split, e_ref[:, cs], preferred_element_type=F32)

    def scale_x(b):
        for j in range(SSD_INNER // cw_chunk):
            cs = slice(j * cw_chunk, (j + 1) * cw_chunk)
            xdt = xact_ref[b, :, cs] * exp_ref[b, 0:L, cs]
            xdt_ref[b, :, cs] = xdt.astype(BF16)
            xdec_ref[b, :, cs] = (xdt * exp_ref[b, L:2 * L, cs]).astype(BF16)

    def group(b, g):
        bg = xact_ref[b, :, SSD_INNER + g * SSD_STATE:SSD_INNER + (g + 1) * SSD_STATE]
        cg = xact_ref[b, :, SSD_INNER + BC_WIDTH + g * SSD_STATE:SSD_INNER + BC_WIDTH + (g + 1) * SSD_STATE]
        bb = bg.astype(BF16)
        cbf = cg.astype(BF16)
        cbm = lax.dot_general(cbf, bb, (((1,), (1,)), ((), ())), preferred_element_type=F32)
        gs = slice(g * gw, (g + 1) * gw)
        prev = state_ref[b, g]
        yoff = jnp.dot(cbf, prev.astype(BF16), preferred_element_type=F32) * exp_ref[b, 2 * L:3 * L, gs]
        btb = bg.T.astype(BF16)
        state_ref[b, g] = prev * exp_ref[b, 3 * L:3 * L + 1, gs] + jnp.dot(
            btb, xdec_ref[b, :, gs], preferred_element_type=F32)
        for e in range(2):
            pc = slice((2 * g + e) * LANES, (2 * g + e + 1) * LANES)
            xpair = xdt_ref[b, :, pc]
            yd = []
            for r in range(2):
                h = 4 * g + 2 * e + r
                seg = acs_ref[b, :, h:h + 1] - acst_ref[b, h:h + 1, :]
                lm = jnp.exp2(jnp.where(causal, seg, NEG))
                yd.append(jnp.dot((cbm * lm).astype(BF16), xpair, preferred_element_type=F32))
            y_ref[b, :, pc] = (jnp.where(lo, yd[0], yd[1]) + yoff[:, e * LANES:(e + 1) * LANES]
                               + dfull_ref[:, pc] * xact_ref[b, :, pc])

    def gate_norm(b, g):
        gs = slice(g * gw, (g + 1) * gw)
        zz = z_ref[b, :, gs].astype(F32)
        yg = y_ref[b, :, gs] * (zz * _sigmoid(zz))
        ms = jnp.mean(yg * yg, axis=-1, keepdims=True)
        o_ref[b, :, gs] = (yg * lax.rsqrt(ms + EPS) * gn_ref[:, gs]).astype(o_ref.dtype)

    for stage in (decay, conv, scale_x):
        for b in range(nbat):
            stage(b)
    for stage in (group, gate_norm):
        for g in range(SSD_GROUPS):
            for b in range(nbat):
                stage(b, g)


def _ssd(proj, dt_rawt, cw, cb, bcol, alcol, emat, dfull, gn, batch, seq):
    L = SSD_CHUNK
    nc = seq // L
    proj3 = proj.reshape(batch, seq, MAIN_PROJ)
    full = lambda shape: pl.BlockSpec(shape, lambda c: (0, 0))
    out = pl.pallas_call(
        functools.partial(_ssd_kernel, nbat=batch),
        grid=(nc,),
        in_specs=[pl.BlockSpec((batch, L, CONV_CH), lambda c: (0, c, XBC_OFF // CONV_CH)),
                  pl.BlockSpec((batch, L, SSD_INNER), lambda c: (0, c, Z_OFF // SSD_INNER)),
                  pl.BlockSpec((batch, LANES, L), lambda c: (0, 0, c)),
                  full((SSD_CONV, CONV_CH)), full((1, CONV_CH)),
                  full((LANES, 1)), full((LANES, 1)),
                  full((2 * LANES, SSD_INNER)), full((1, SSD_INNER)), full((1, SSD_INNER))],
        out_specs=pl.BlockSpec((batch, L, SSD_INNER), lambda c: (0, c, 0)),
        out_shape=jax.ShapeDtypeStruct((batch, seq, SSD_INNER), BF16),
        scratch_shapes=[pltpu.VMEM((batch, L + SUBLANES, CONV_CH), F32),
                        pltpu.VMEM((batch, L, CONV_CH), F32),
                        pltpu.VMEM((batch, SSD_GROUPS, SSD_STATE, SSD_INNER // SSD_GROUPS), F32),
                        pltpu.VMEM((batch, L, SSD_INNER), BF16),
                        pltpu.VMEM((batch, L, SSD_INNER), BF16),
                        pltpu.VMEM((batch, L, SSD_INNER), F32),
                        pltpu.VMEM((batch, 3 * L + SUBLANES, SSD_INNER), F32),
                        pltpu.VMEM((batch, L, LANES), F32),
                        pltpu.VMEM((batch, LANES, L), F32)],
        compiler_params=_params(1),
        name="ssd_scan",
    )(proj3, proj3, dt_rawt, cw, cb, bcol, alcol, emat, dfull, gn)
    return out.reshape(batch * seq, SSD_INNER)


def _out_proj_kernel(a1_ref, a2_ref, w1_ref, w2_ref, r_ref, o_ref, w1b_ref, w2b_ref):
    @pl.when(pl.program_id(1) == 0)
    def _():
        _cast_rows(w1_ref, w1b_ref)
        _cast_rows(w2_ref, w2b_ref)

    acc = jnp.dot(a1_ref[...], w1b_ref[...], preferred_element_type=F32)
    acc = acc + jnp.dot(a2_ref[...], w2b_ref[...], preferred_element_type=F32)
    o_ref[...] = r_ref[...] + acc


def _out_proj(a1, a2, w, res, tm, tn):
    t, k = a1.shape
    n = w.shape[1]
    return pl.pallas_call(
        _out_proj_kernel,
        grid=(n // tn, t // tm),
        in_specs=[pl.BlockSpec((tm, k), lambda j, m: (m, 0)),
                  pl.BlockSpec((tm, k), lambda j, m: (m, 0)),
                  pl.BlockSpec((k, tn), lambda j, m: (0, j)),
                  pl.BlockSpec((k, tn), lambda j, m: (1, j)),
                  pl.BlockSpec((tm, tn), lambda j, m: (m, j))],
        out_specs=pl.BlockSpec((tm, tn), lambda j, m: (m, j)),
        out_shape=jax.ShapeDtypeStruct((t, n), F32),
        scratch_shapes=[pltpu.VMEM((k, tn), BF16), pltpu.VMEM((k, tn), BF16)],
        compiler_params=_params(2),
        name="out_proj",
    )(a1, a2, w, w, res)


def _up_kernel(a_ref, wg_ref, wv_ref, cwg_ref, cwv_ref, cbg_ref, cbv_ref, wd_ref, o_ref, wdb_ref,
               wgb_ref, wvb_ref, carry_ref, *, tm, seq):
    m = pl.program_id(1)
    wdb_ref[...] = wd_ref[...].astype(BF16)

    @pl.when(m == 0)
    def _():
        _cast_rows(wg_ref, wgb_ref)
        _cast_rows(wv_ref, wvb_ref)

    a = a_ref[...]
    seq_start = (m * tm) % seq == 0
    outs = []
    for idx, (wb_ref, cw_ref, cb_ref) in enumerate(((wgb_ref, cwg_ref, cbg_ref),
                                                    (wvb_ref, cwv_ref, cbv_ref))):
        u = jnp.dot(a, wb_ref[...], preferred_element_type=F32)
        above = jnp.where(seq_start, 0.0, carry_ref[idx])
        blk = jnp.concatenate([above, u], axis=0)
        cw = cw_ref[...]
        conv = cb_ref[...]
        for k in range(FFN_CONV - 1):
            conv = conv + _delay_rows(blk, FFN_CONV - 1 - k) * cw[k:k + 1, :]
        conv = conv + u * cw[FFN_CONV - 1:FFN_CONV, :]
        carry_ref[idx] = u[tm - SUBLANES:tm]
        outs.append(conv)
    gate, val = outs
    o_ref[...] = ((gate * _sigmoid(gate)) * val).astype(o_ref.dtype)


def _up_proj(a, w, cw, cb, w_down, tm, tn, seq):
    t, k = a.shape
    assert seq % tm == 0 and D_FF % tn == 0
    nb = D_FF // tn
    n_m = t // tm
    kd, nd = w_down.shape
    slab = kd // (nb * n_m)
    assert slab * nb * n_m == kd and slab % 16 == 0
    return pl.pallas_call(
        functools.partial(_up_kernel, tm=tm, seq=seq),
        grid=(nb, t // tm),
        in_specs=[pl.BlockSpec((tm, k), lambda j, m: (m, 0)),
                  pl.BlockSpec((k, tn), lambda j, m: (0, j)),
                  pl.BlockSpec((k, tn), lambda j, m: (0, j + nb)),
                  pl.BlockSpec((FFN_CONV, tn), lambda j, m: (0, j)),
                  pl.BlockSpec((FFN_CONV, tn), lambda j, m: (0, j + nb)),
                  pl.BlockSpec((1, tn), lambda j, m: (0, j)),
                  pl.BlockSpec((1, tn), lambda j, m: (0, j + nb)),
                  pl.BlockSpec((slab, nd), lambda j, m: (j * n_m + m, 0))],
        out_specs=[pl.BlockSpec((tm, tn), lambda j, m: (m, j)),
                   pl.BlockSpec((slab, nd), lambda j, m: (j * n_m + m, 0))],
        out_shape=[jax.ShapeDtypeStruct((t, D_FF), BF16),
                   jax.ShapeDtypeStruct((kd, nd), BF16)],
        scratch_shapes=[pltpu.VMEM((k, tn), BF16), pltpu.VMEM((k, tn), BF16),
                        pltpu.VMEM((2, SUBLANES, tn), F32)],
        compiler_params=_params(2),
        name="up_proj_conv_swiglu",
    )(a, w, w, cw, cw, cb, cb, w_down)


def _down_kernel(a_ref, w_ref, r_ref, o_ref):
    o_ref[...] = r_ref[...] + jnp.dot(a_ref[...], w_ref[...], preferred_element_type=F32)


def _down_proj(a, w_bf16, res, tm, tn):
    t, k = a.shape
    n = w_bf16.shape[1]
    return pl.pallas_call(
        _down_kernel,
        grid=(n // tn, t // tm),
        in_specs=[pl.BlockSpec((tm, k), lambda j, m: (m, 0)),
                  pl.BlockSpec((k, tn), lambda j, m: (0, j)),
                  pl.BlockSpec((tm, tn), lambda j, m: (m, j))],
        out_specs=pl.BlockSpec((tm, tn), lambda j, m: (m, j)),
        out_shape=jax.ShapeDtypeStruct((t, n), F32),
        compiler_params=_params(2),
        name="down_proj",
    )(a, w_bf16, res)


def _pad_lanes(v):
    return jnp.pad(v.astype(F32), (0, LANES - v.shape[0]))


def _mixer_layer(h, batch, seq, norm_mix, w_in, sinks, attn_out_norm, ssd_conv_w, ssd_conv_b, dt_bias,
                 a_log, ssd_d, ssd_norm, w_out, tables, emat):
    w_in_t = jnp.swapaxes(w_in, 0, 1)
    wdt_t = jnp.pad(w_in_t[MAIN_PROJ:], ((0, LANES - SSD_HEADS), (0, 0))).astype(BF16)
    xn, dt_rawt = _norm_dt(h, norm_mix.reshape(1, -1), wdt_t, 512, batch, seq)
    proj = _in_proj(xn, w_in_t, tm=1024, tn=1024)
    attn = _attention(proj, sinks.astype(F32), attn_out_norm.reshape(1, -1), tables, batch, seq, qb=4)
    bias = _pad_lanes(dt_bias)
    alog = _pad_lanes(a_log)
    y = _ssd(proj, dt_rawt, ssd_conv_w, ssd_conv_b.reshape(1, -1), bias.reshape(-1, 1), alog.reshape(-1, 1),
             emat, jnp.repeat(ssd_d.astype(F32), SSD_HEAD_DIM).reshape(1, -1),
             ssd_norm.reshape(1, -1), batch, seq)
    return _out_proj(attn, y, w_out, h, tm=1024, tn=512)


def kernel(x, norm_mix, w_in, sinks, attn_out_norm, ssd_conv_w, ssd_conv_b, dt_bias, a_log, ssd_d, ssd_norm,
           w_out, norm_ffn, w_up, ffn_conv_w, ffn_conv_b, w_down, norm_final):
    batch, seq, d = x.shape
    h = x.reshape(batch * seq, d)
    tables = _rope_tables(seq)
    head_of_channel = np.arange(SSD_INNER) // SSD_HEAD_DIM
    emat = np.arange(LANES)[:, None] == head_of_channel[None, :]
    emat = jnp.asarray(np.concatenate([emat, emat], axis=0), dtype=BF16)
    for l in range(norm_mix.shape[0]):
        h = _mixer_layer(h, batch, seq, norm_mix[l], w_in[l], sinks[l], attn_out_norm[l], ssd_conv_w[l],
                         ssd_conv_b[l], dt_bias[l], a_log[l], ssd_d[l], ssd_norm[l], w_out[l], tables, emat)
        hn = _norm(h, norm_ffn[l].reshape(1, -1), BF16, 512, "norm_ffn")
        act, w_down_bf16 = _up_proj(hn, w_up[l], ffn_conv_w[l], ffn_conv_b[l].reshape(1, -1), w_down[l],
                                    tm=1024, tn=512, seq=seq)
        h = _down_proj(act, w_down_bf16, h, tm=512, tn=1024)
    out = _norm(h, norm_final.reshape(1, -1), F32, 512, "norm_final")
    return out.reshape(batch, seq, d)
```

```python
import functools

import numpy as np
import jax
import jax.numpy as jnp
from jax import lax
from jax.experimental import pallas as pl
from jax.experimental.pallas import tpu as pltpu

F32 = jnp.float32
BF16 = jnp.bfloat16

D_MODEL = 2048
N_Q_HEADS = 32
N_KV_HEADS = 8
HEAD_DIM = 64
Q_PER_KV = N_Q_HEADS // N_KV_HEADS
WINDOW = 128
ATTN_BLOCK = 128
ROT_DIM = HEAD_DIM // 4
ROPE_THETA = 500000.0
SSD_HEADS = 32
SSD_HEAD_DIM = 64
SSD_INNER = SSD_HEADS * SSD_HEAD_DIM
SSD_GROUPS = 8
SSD_STATE = 128
SSD_CONV = 4
SSD_CHUNK = 128
ATTN_WIDTH = N_Q_HEADS * HEAD_DIM
KV_WIDTH = N_KV_HEADS * HEAD_DIM
BC_WIDTH = SSD_GROUPS * SSD_STATE
CONV_CH = SSD_INNER + 2 * BC_WIDTH
MAIN_PROJ = ATTN_WIDTH + 2 * KV_WIDTH + SSD_INNER + CONV_CH
D_FF = 5632
FFN_CONV = 3
EPS = 1e-6

LANES = 128
SUBLANES = 8
HALF = LANES // 2
NEG = -1e30
LOG2E = 1.4426950408889634
VMEM_LIMIT = 56 * 1024 * 1024

Q_OFF = 0
Z_OFF = ATTN_WIDTH
XBC_OFF = Z_OFF + SSD_INNER
K_OFF = XBC_OFF + CONV_CH
V_OFF = K_OFF + KV_WIDTH


def _params(n_axes, flags=None):
    return pltpu.CompilerParams(dimension_semantics=("arbitrary",) * n_axes,
                                vmem_limit_bytes=VMEM_LIMIT, flags=flags)


def _sigmoid(x):
    return 1.0 / (1.0 + jnp.exp2(x * -LOG2E))


def _softplus(x):
    return jnp.maximum(x, 0.0) + jnp.log1p(jnp.exp(-jnp.abs(x)))


def _cast_rows(src_ref, dst_ref, rows=256):
    k = src_ref.shape[0]
    for r in range(0, k, rows):
        dst_ref[r:r + rows, :] = src_ref[r:r + rows, :].astype(BF16)


def _delay_rows(blk, sh):
    n, w = blk.shape[0] - SUBLANES, blk.shape[1]
    g = blk.reshape(n // SUBLANES + 1, SUBLANES, w)
    rot = pltpu.roll(g, sh, 1)
    row = lax.broadcasted_iota(jnp.int32, (SUBLANES, w), 0)
    out = jnp.where(row < sh, rot[:-1], rot[1:])
    return out.reshape(n, w)


def _rmsnorm_rows(x, g):
    ms = jnp.mean(x * x, axis=-1, keepdims=True)
    return x * lax.rsqrt(ms + EPS) * g


def _norm_dt_kernel(x_ref, g_ref, wdtt_ref, xn_ref, dtt_ref):
    xn = _rmsnorm_rows(x_ref[...], g_ref[...]).astype(BF16)
    xn_ref[...] = xn
    dtt_ref[...] = lax.dot_general(wdtt_ref[...], xn, (((1,), (1,)), ((), ())),
                                   preferred_element_type=F32)


def _norm_dt(x2, g, wdtt, tm, batch, seq):
    t, d = x2.shape
    assert seq % tm == 0
    per_seq = seq // tm
    return pl.pallas_call(
        _norm_dt_kernel,
        grid=(t // tm,),
        in_specs=[pl.BlockSpec((tm, d), lambda i: (i, 0)),
                  pl.BlockSpec((1, d), lambda i: (0, 0)),
                  pl.BlockSpec((LANES, d), lambda i: (0, 0))],
        out_specs=[pl.BlockSpec((tm, d), lambda i: (i, 0)),
                   pl.BlockSpec((None, LANES, tm), lambda i: (i // per_seq, 0, i % per_seq))],
        out_shape=[jax.ShapeDtypeStruct((t, d), BF16),
                   jax.ShapeDtypeStruct((batch, LANES, seq), F32)],
        compiler_params=_params(1),
        name="norm_dt",
    )(x2, g, wdtt)


def _norm_kernel(x_ref, g_ref, o_ref):
    o_ref[...] = _rmsnorm_rows(x_ref[...], g_ref[...]).astype(o_ref.dtype)


def _norm(x2, g, out_dtype, tm, name):
    t, d = x2.shape
    return pl.pallas_call(
        _norm_kernel,
        grid=(t // tm,),
        in_specs=[pl.BlockSpec((tm, d), lambda i: (i, 0)),
                  pl.BlockSpec((1, d), lambda i: (0, 0))],
        out_specs=pl.BlockSpec((tm, d), lambda i: (i, 0)),
        out_shape=jax.ShapeDtypeStruct((t, d), out_dtype),
        compiler_params=_params(1),
        name=name,
    )(x2, g)


def _in_proj_kernel(a_ref, wt_ref, o_ref, wbf_ref):
    @pl.when(pl.program_id(1) == 0)
    def _():
        _cast_rows(wt_ref, wbf_ref)

    o_ref[...] = lax.dot_general(a_ref[...], wbf_ref[...], (((1,), (1,)), ((), ())),
                                 preferred_element_type=F32).astype(o_ref.dtype)


def _in_proj(xn, wt, tm, tn):
    t, k = xn.shape
    u = 1024 // tn

    def wmap(j, m):
        return (jnp.where(j < 2 * u, j, jnp.where(j < 8 * u, j + u, j - 6 * u)), 0)

    return pl.pallas_call(
        _in_proj_kernel,
        grid=(MAIN_PROJ // tn, t // tm),
        in_specs=[pl.BlockSpec((tm, k), lambda j, m: (m, 0)),
                  pl.BlockSpec((tn, k), wmap)],
        out_specs=pl.BlockSpec((tm, tn), lambda j, m: (m, j)),
        out_shape=jax.ShapeDtypeStruct((t, MAIN_PROJ), BF16),
        scratch_shapes=[pltpu.VMEM((tn, k), BF16)],
        compiler_params=_params(2),
        name="in_proj",
    )(xn, wt)


def _rope_tables(seq):
    half = ROT_DIM // 2
    inv = 1.0 / (ROPE_THETA ** (jnp.arange(0, ROT_DIM, 2, dtype=F32) / ROT_DIM))
    ang = jnp.arange(seq, dtype=F32)[:, None] * inv[None, :]
    cos, sin = jnp.cos(ang), jnp.sin(ang)
    d = np.arange(LANES) % HEAD_DIM
    idx = d % half
    in_rot = jnp.asarray(d < ROT_DIM)
    first = jnp.asarray(d < half)
    second = jnp.asarray((d >= half) & (d < ROT_DIM))
    c = jnp.where(in_rot[None, :], cos[:, idx], 1.0)
    s1 = jnp.where(first[None, :], -sin[:, idx], 0.0)
    s2 = jnp.where(second[None, :], sin[:, idx], 0.0)
    return jnp.stack([c, s1, s2]).astype(F32)


def _rope(x, c, s1, s2):
    half = ROT_DIM // 2
    return x * c + pltpu.roll(x, LANES - half, 1) * s1 + pltpu.roll(x, half, 1) * s2


def _attn_kernel(sinks_ref, q_ref, kp_ref, kc_ref, vp_ref, vc_ref,
                 tc_ref, tp_ref, g_ref, wo_ref, o_ref, wob_ref,
                 lhs_ref, kb_ref, vb_ref, s_ref, p_ref, o2_ref, m_ref, tq_ref, bias_ref, acc_ref, *, qb):
    blk = ATTN_BLOCK
    n = pl.program_id(1)
    lo = lax.broadcasted_iota(jnp.int32, (blk, LANES), 1) < HALF

    wob_ref[...] = wo_ref[...].astype(BF16)

    scale = HEAD_DIM ** -0.5 * LOG2E
    for i in range(3):
        tq_ref[i] = tc_ref[i] * scale
    qi = lax.broadcasted_iota(jnp.int32, (blk, 2 * blk), 0)
    kj = lax.broadcasted_iota(jnp.int32, (blk, 2 * blk), 1)
    rel = qi + blk - kj
    band = (rel >= 0) & (rel < WINDOW)
    kmin = jnp.where(n > 0, 0, blk)
    bias_ref[0] = jnp.where(band & (kj >= kmin), 0.0, NEG)
    bias_ref[1] = jnp.where(band, 0.0, NEG)

    for j in range(qb):
        rows = slice(j * blk, (j + 1) * blk)
        for col in range(ATTN_WIDTH // LANES):
            h, jj = col // 2, col % 2
            q2 = _rope(q_ref[rows, col * LANES:(col + 1) * LANES].astype(F32),
                       tq_ref[0, rows, :], tq_ref[1, rows, :], tq_ref[2, rows, :])
            lhs_ref[j, h, (2 * jj) * blk:(2 * jj + 1) * blk, :] = jnp.where(lo, q2, 0.0).astype(BF16)
            lhs_ref[j, h, (2 * jj + 1) * blk:(2 * jj + 2) * blk, :] = jnp.where(lo, 0.0, q2).astype(BF16)
    nk = (qb + 1) * blk
    lok = lax.broadcasted_iota(jnp.int32, (nk, LANES), 1) < HALF
    ones = jnp.ones((nk, LANES), BF16)
    for i in range(N_KV_HEADS // 2):
        cols = slice(i * LANES, (i + 1) * LANES)
        kcat = jnp.concatenate([_rope(kp_ref[:, cols].astype(F32), tp_ref[0], tp_ref[1], tp_ref[2]),
                                _rope(kc_ref[:, cols].astype(F32), tc_ref[0], tc_ref[1], tc_ref[2])], axis=0)
        vcat = jnp.concatenate([vp_ref[:, cols], vc_ref[:, cols]], axis=0).astype(F32)
        kswp = pltpu.roll(kcat, HALF, 1)
        vswp = pltpu.roll(vcat, HALF, 1)
        kb_ref[2 * i] = jnp.where(lok, kcat, kswp).astype(BF16)
        kb_ref[2 * i + 1] = jnp.where(lok, kswp, kcat).astype(BF16)
        vb_ref[2 * i, :, 0:LANES] = jnp.where(lok, vcat, vswp).astype(BF16)
        vb_ref[2 * i + 1, :, 0:LANES] = jnp.where(lok, vswp, vcat).astype(BF16)
        vb_ref[2 * i, :, LANES:2 * LANES] = ones
        vb_ref[2 * i + 1, :, LANES:2 * LANES] = ones

    def keys(j):
        return slice(j * blk, (j + 2) * blk)

    def scores(j, h):
        s_ref[j, h % 2] = lax.dot_general(lhs_ref[j, h], kb_ref[h, keys(j), :], (((1,), (1,)), ((), ())),
                                          preferred_element_type=F32)

    def row_max(j, h):
        slot = h % 2
        for r in range(Q_PER_KV):
            rows = slice(r * blk, (r + 1) * blk)
            sink = sinks_ref[Q_PER_KV * h + r] * LOG2E
            sr = s_ref[j, slot, rows, :] + bias_ref[min(j, 1)]
            s_ref[j, slot, rows, :] = sr
            m = jnp.maximum(jnp.max(sr, axis=-1, keepdims=True), sink)
            m_ref[j, slot, rows, :] = jnp.broadcast_to(m, (blk, LANES))

    def probs(j, h):
        slot = h % 2
        for r in range(Q_PER_KV):
            rows = slice(r * blk, (r + 1) * blk)
            m = m_ref[j, slot, rows, :]
            for half in range(2):
                ln = slice(half * LANES, (half + 1) * LANES)
                p_ref[j, slot, rows, ln] = jnp.exp2(s_ref[j, slot, rows, ln] - m).astype(BF16)

    def weighted(j, h):
        o2_ref[j, h % 2] = jnp.dot(p_ref[j, h % 2], vb_ref[h, keys(j), :],
                                   preferred_element_type=F32)

    def finish(j, h):
        slot = h % 2
        outs = []
        for r in range(Q_PER_KV):
            rows = slice(r * blk, (r + 1) * blk)
            sink = sinks_ref[Q_PER_KV * h + r] * LOG2E
            den = o2_ref[j, slot, rows, LANES:2 * LANES] + jnp.exp2(sink - m_ref[j, slot, rows, :])
            outs.append(o2_ref[j, slot, rows, 0:LANES] * (1.0 / den))
        for jj in range(2):
            col = 2 * h + jj
            acc_ref[j * blk:(j + 1) * blk, col * LANES:(col + 1) * LANES] = jnp.where(
                lo, outs[2 * jj], outs[2 * jj + 1])

    for j in range(qb):
        scores(j, 0)
    for h in range(N_KV_HEADS + 1):
        for stage in (scores, row_max, probs, weighted, finish):
            hh = {scores: h + 1, finish: h - 1}.get(stage, h)
            if 0 <= hh < N_KV_HEADS:
                for j in range(qb):
                    stage(j, hh)

    o_ref[...] = _rmsnorm_rows(acc_ref[...], g_ref[...]).astype(o_ref.dtype)


def _attention(proj, sinks, g, tables, w_out, batch, seq, qb):
    t = proj.shape[0]
    blk = ATTN_BLOCK
    assert seq % (qb * blk) == 0
    nb = seq // blk
    ns = nb // qb
    ko, no = w_out.shape
    slab = ko // (batch * ns)
    assert slab * batch * ns == ko and slab % 16 == 0
    kblk = K_OFF // KV_WIDTH
    vblk = V_OFF // KV_WIDTH

    def cur(b, n):
        return (b * ns + n, 0)

    def prev_rows(b, n):
        return jnp.maximum(b * nb + n * qb - 1, 0)

    return pl.pallas_call(
        functools.partial(_attn_kernel, qb=qb),
        grid=(batch, ns),
        in_specs=[pl.BlockSpec(memory_space=pltpu.SMEM),
                  pl.BlockSpec((qb * blk, ATTN_WIDTH), cur),
                  pl.BlockSpec((blk, KV_WIDTH), lambda b, n: (prev_rows(b, n), kblk)),
                  pl.BlockSpec((qb * blk, KV_WIDTH), lambda b, n: (b * ns + n, kblk)),
                  pl.BlockSpec((blk, KV_WIDTH), lambda b, n: (prev_rows(b, n), vblk)),
                  pl.BlockSpec((qb * blk, KV_WIDTH), lambda b, n: (b * ns + n, vblk)),
                  pl.BlockSpec((3, qb * blk, LANES), lambda b, n: (0, n, 0)),
                  pl.BlockSpec((3, blk, LANES), lambda b, n: (0, jnp.maximum(n * qb - 1, 0), 0)),
                  pl.BlockSpec((1, ATTN_WIDTH), lambda b, n: (0, 0)),
                  pl.BlockSpec((slab, no), cur)],
        out_specs=[pl.BlockSpec((qb * blk, ATTN_WIDTH), cur),
                   pl.BlockSpec((slab, no), cur)],
        out_shape=[jax.ShapeDtypeStruct((t, ATTN_WIDTH), BF16),
                   jax.ShapeDtypeStruct((ko, no), BF16)],
        scratch_shapes=[pltpu.VMEM((qb, N_KV_HEADS, Q_PER_KV * blk, LANES), BF16),
                        pltpu.VMEM((N_KV_HEADS, (qb + 1) * blk, LANES), BF16),
                        pltpu.VMEM((N_KV_HEADS, (qb + 1) * blk, 2 * LANES), BF16),
                        pltpu.VMEM((qb, 2, Q_PER_KV * blk, 2 * blk), F32),
                        pltpu.VMEM((qb, 2, Q_PER_KV * blk, 2 * blk), BF16),
                        pltpu.VMEM((qb, 2, Q_PER_KV * blk, 2 * LANES), F32),
                        pltpu.VMEM((qb, 2, Q_PER_KV * blk, LANES), F32),
                        pltpu.VMEM((3, qb * blk, LANES), F32),
                        pltpu.VMEM((2, blk, 2 * blk), F32),
                        pltpu.VMEM((qb * blk, ATTN_WIDTH), F32)],
        compiler_params=_params(2),
        name="swa_attention",
    )(sinks, proj, proj, proj, proj, proj, tables, tables, g, w_out)


def _split3(x):
    h = x.astype(BF16)
    r = x - h.astype(F32)
    m = r.astype(BF16)
    l = (r - m.astype(F32)).astype(BF16)
    return h, m, l


def _dot3_lhs(x, w):
    h, m, l = _split3(x)
    d = lambda a: jnp.dot(a, w, preferred_element_type=F32)
    return (d(l) + d(m)) + d(h)


def _ssd_kernel(xbc_ref, z_ref, dtt_ref, cw_ref, cb_ref, bcol_ref, alcol_ref, e_ref, dfull_ref, gn_ref, o_ref,
                ext_ref, xact_ref, state_ref, xdt_ref, xdec_ref, y_ref, exp_ref, acs_ref, acst_ref, *, nbat):
    L = SSD_CHUNK
    c = pl.program_id(0)
    cw_chunk = 512
    gw = SSD_HEAD_DIM * (SSD_HEADS // SSD_GROUPS)
    ri = lax.broadcasted_iota(jnp.int32, (L, L), 0)
    ci = lax.broadcasted_iota(jnp.int32, (L, L), 1)
    causal = ri >= ci
    lo = ci < HALF

    @pl.when(c == 0)
    def _():
        ext_ref[:, 0:SUBLANES, :] = jnp.zeros((nbat, SUBLANES, CONV_CH), F32)
        state_ref[...] = jnp.zeros_like(state_ref)

    def conv(b):
        ext_ref[b, SUBLANES:SUBLANES + L, :] = xbc_ref[b].astype(F32)
        for j in range(CONV_CH // cw_chunk):
            cs = slice(j * cw_chunk, (j + 1) * cw_chunk)
            blk = ext_ref[b, :, cs]
            acc = cb_ref[:, cs]
            for k in range(SSD_CONV - 1):
                acc = acc + _delay_rows(blk, SSD_CONV - 1 - k) * cw_ref[k:k + 1, cs]
            acc = acc + blk[SUBLANES:] * cw_ref[SSD_CONV - 1:SSD_CONV, cs]
            xact_ref[b, :, cs] = acc * _sigmoid(acc)
        ext_ref[b, 0:SUBLANES, :] = ext_ref[b, L:L + SUBLANES, :]

    def decay(b):
        nh = SSD_HEADS
        dtt = _softplus(dtt_ref[b, 0:nh, :] + bcol_ref[0:nh, :])
        dat = dtt * (-jnp.exp(alcol_ref[0:nh, :]))
        tri_u = jnp.where(ri <= ci, 1.0, 0.0).astype(BF16)
        acst = _dot3_lhs(dat, tri_u) * LOG2E
        acst_ref[b, 0:nh, :] = acst
        pad = jnp.zeros((LANES - nh, L), F32)
        a_cs = jnp.concatenate([acst, pad], axis=0).T
        dt = jnp.concatenate([dtt, pad], axis=0).T
        acs_ref[b] = a_cs
        a_last = a_cs[L - 1:L, :]
        stack = jnp.concatenate([dt, jnp.exp2(a_last - a_cs), jnp.exp2(a_cs),
                                 jnp.broadcast_to(jnp.exp2(a_last), (SUBLANES, LANES))], axis=0)
        hi = stack.astype(BF16)
        lo = (stack - hi.astype(F32)).astype(BF16)
        split = jnp.concatenate([hi, lo], axis=1)
        for j in range(SSD_INNER // cw_chunk):
            cs = slice(j * cw_chunk, (j + 1) * cw_chunk)
            exp_ref[b, :, cs] = jnp.dot(split, e_ref[:, cs], preferred_element_type=F32)

    def scale_x(b):
        for j in range(SSD_INNER // cw_chunk):
            cs = slice(j * cw_chunk, (j + 1) * cw_chunk)
            xdt = xact_ref[b, :, cs] * exp_ref[b, 0:L, cs]
            xdt_ref[b, :, cs] = xdt.astype(BF16)
            xdec_ref[b, :, cs] = (xdt * exp_ref[b, L:2 * L, cs]).astype(BF16)

    def group(b, g):
        bg = xact_ref[b, :, SSD_INNER + g * SSD_STATE:SSD_INNER + (g + 1) * SSD_STATE]
        cg = xact_ref[b, :, SSD_INNER + BC_WIDTH + g * SSD_STATE:SSD_INNER + BC_WIDTH + (g + 1) * SSD_STATE]
        bb = bg.astype(BF16)
        cbf = cg.astype(BF16)
        cbm = lax.dot_general(cbf, bb, (((1,), (1,)), ((), ())), preferred_element_type=F32)
        gs = slice(g * gw, (g + 1) * gw)
        prev = state_ref[b, g]
        yoff = jnp.dot(cbf, prev.astype(BF16), preferred_element_type=F32) * exp_ref[b, 2 * L:3 * L, gs]
        btb = bg.T.astype(BF16)
        state_ref[b, g] = prev * exp_ref[b, 3 * L:3 * L + 1, gs] + jnp.dot(
            btb, xdec_ref[b, :, gs], preferred_element_type=F32)
        for e in range(2):
            pc = slice((2 * g + e) * LANES, (2 * g + e + 1) * LANES)
            xpair = xdt_ref[b, :, pc]
            yd = []
            for r in range(2):
                h = 4 * g + 2 * e + r
                seg = acs_ref[b, :, h:h + 1] - acst_ref[b, h:h + 1, :]
                lm = jnp.exp2(jnp.where(causal, seg, NEG))
                yd.append(jnp.dot((cbm * lm).astype(BF16), xpair, preferred_element_type=F32))
            y_ref[b, :, pc] = (jnp.where(lo, yd[0], yd[1]) + yoff[:, e * LANES:(e + 1) * LANES]
                               + dfull_ref[:, pc] * xact_ref[b, :, pc])

    def gate_norm(b, g):
        gs = slice(g * gw, (g + 1) * gw)
        zz = z_ref[b, :, gs].astype(F32)
        yg = y_ref[b, :, gs] * (zz * _sigmoid(zz))
        ms = jnp.mean(yg * yg, axis=-1, keepdims=True)
        o_ref[b, :, gs] = (yg * lax.rsqrt(ms + EPS) * gn_ref[:, gs]).astype(o_ref.dtype)

    for stage in (decay, conv, scale_x):
        for b in range(nbat):
            stage(b)
    for stage in (group, gate_norm):
        for g in range(SSD_GROUPS):
            for b in range(nbat):
                stage(b, g)


def _ssd(proj, dt_rawt, cw, cb, bcol, alcol, emat, dfull, gn, batch, seq):
    L = SSD_CHUNK
    nc = seq // L
    proj3 = proj.reshape(batch, seq, MAIN_PROJ)
    full = lambda shape: pl.BlockSpec(shape, lambda c: (0, 0))
    out = pl.pallas_call(
        functools.partial(_ssd_kernel, nbat=batch),
        grid=(nc,),
        in_specs=[pl.BlockSpec((batch, L, CONV_CH), lambda c: (0, c, XBC_OFF // CONV_CH)),
                  pl.BlockSpec((batch, L, SSD_INNER), lambda c: (0, c, Z_OFF // SSD_INNER)),
                  pl.BlockSpec((batch, LANES, L), lambda c: (0, 0, c)),
                  full((SSD_CONV, CONV_CH)), full((1, CONV_CH)),
                  full((LANES, 1)), full((LANES, 1)),
                  full((2 * LANES, SSD_INNER)), full((1, SSD_INNER)), full((1, SSD_INNER))],
        out_specs=pl.BlockSpec((batch, L, SSD_INNER), lambda c: (0, c, 0)),
        out_shape=jax.ShapeDtypeStruct((batch, seq, SSD_INNER), BF16),
        scratch_shapes=[pltpu.VMEM((batch, L + SUBLANES, CONV_CH), F32),
                        pltpu.VMEM((batch, L, CONV_CH), F32),
                        pltpu.VMEM((batch, SSD_GROUPS, SSD_STATE, SSD_INNER // SSD_GROUPS), F32),
                        pltpu.VMEM((batch, L, SSD_INNER), BF16),
                        pltpu.VMEM((batch, L, SSD_INNER), BF16),
                        pltpu.VMEM((batch, L, SSD_INNER), F32),
                        pltpu.VMEM((batch, 3 * L + SUBLANES, SSD_INNER), F32),
                        pltpu.VMEM((batch, L, LANES), F32),
                        pltpu.VMEM((batch, LANES, L), F32)],
        compiler_params=_params(1),
        name="ssd_scan",
    )(proj3, proj3, dt_rawt, cw, cb, bcol, alcol, emat, dfull, gn)
    return out.reshape(batch * seq, SSD_INNER)


def _out_proj_kernel(a1_ref, a2_ref, w1_ref, w2_ref, r_ref, g_ref, h_ref, hn_ref):
    acc = jnp.dot(a1_ref[...], w1_ref[...], preferred_element_type=F32)
    acc = acc + jnp.dot(a2_ref[...], w2_ref[...], preferred_element_type=F32)
    h = r_ref[...] + acc
    h_ref[...] = h
    hn_ref[...] = _rmsnorm_rows(h, g_ref[...]).astype(hn_ref.dtype)


def _out_proj(a1, a2, w_bf16, res, g, tm):
    t, k = a1.shape
    n = w_bf16.shape[1]
    row = lambda shape: pl.BlockSpec(shape, lambda m: (m, 0))
    wspec = lambda i: pl.BlockSpec((k, n), lambda m: (i, 0), pipeline_mode=pl.Buffered(1))
    return pl.pallas_call(
        _out_proj_kernel,
        grid=(t // tm,),
        in_specs=[row((tm, k)), row((tm, k)), wspec(0), wspec(1), row((tm, n)),
                  pl.BlockSpec((1, n), lambda m: (0, 0))],
        out_specs=[row((tm, n)), row((tm, n))],
        out_shape=[jax.ShapeDtypeStruct((t, n), F32), jax.ShapeDtypeStruct((t, n), BF16)],
        compiler_params=_params(1),
        name="out_proj_norm",
    )(a1, a2, w_bf16, w_bf16, res, g)


def _up_kernel(a_ref, wg_ref, wv_ref, cwg_ref, cwv_ref, cbg_ref, cbv_ref, wd_ref, o_ref, wdb_ref,
               wgb_ref, wvb_ref, carry_ref, *, tm, seq):
    m = pl.program_id(1)
    wdb_ref[...] = wd_ref[...].astype(BF16)

    @pl.when(m == 0)
    def _():
        _cast_rows(wg_ref, wgb_ref)
        _cast_rows(wv_ref, wvb_ref)

    a = a_ref[...]
    seq_start = (m * tm) % seq == 0
    outs = []
    for idx, (wb_ref, cw_ref, cb_ref) in enumerate(((wgb_ref, cwg_ref, cbg_ref),
                                                    (wvb_ref, cwv_ref, cbv_ref))):
        u = jnp.dot(a, wb_ref[...], preferred_element_type=F32)
        above = jnp.where(seq_start, 0.0, carry_ref[idx])
        blk = jnp.concatenate([above, u], axis=0)
        cw = cw_ref[...]
        conv = cb_ref[...]
        for k in range(FFN_CONV - 1):
            conv = conv + _delay_rows(blk, FFN_CONV - 1 - k) * cw[k:k + 1, :]
        conv = conv + u * cw[FFN_CONV - 1:FFN_CONV, :]
        carry_ref[idx] = u[tm - SUBLANES:tm]
        outs.append(conv)
    gate, val = outs
    o_ref[...] = ((gate * _sigmoid(gate)) * val).astype(o_ref.dtype)


def _up_proj(a, w, cw, cb, w_down, tm, tn, seq):
    t, k = a.shape
    assert seq % tm == 0 and D_FF % tn == 0
    nb = D_FF // tn
    n_m = t // tm
    kd, nd = w_down.shape
    slab = kd // (nb * n_m)
    assert slab * nb * n_m == kd and slab % 16 == 0
    return pl.pallas_call(
        functools.partial(_up_kernel, tm=tm, seq=seq),
        grid=(nb, t // tm),
        in_specs=[pl.BlockSpec((tm, k), lambda j, m: (m, 0)),
                  pl.BlockSpec((k, tn), lambda j, m: (0, j)),
                  pl.BlockSpec((k, tn), lambda j, m: (0, j + nb)),
                  pl.BlockSpec((FFN_CONV, tn), lambda j, m: (0, j)),
                  pl.BlockSpec((FFN_CONV, tn), lambda j, m: (0, j + nb)),
                  pl.BlockSpec((1, tn), lambda j, m: (0, j)),
                  pl.BlockSpec((1, tn), lambda j, m: (0, j + nb)),
                  pl.BlockSpec((slab, nd), lambda j, m: (j * n_m + m, 0))],
        out_specs=[pl.BlockSpec((tm, tn), lambda j, m: (m, j)),
                   pl.BlockSpec((slab, nd), lambda j, m: (j * n_m + m, 0))],
        out_shape=[jax.ShapeDtypeStruct((t, D_FF), BF16),
                   jax.ShapeDtypeStruct((kd, nd), BF16)],
        scratch_shapes=[pltpu.VMEM((k, tn), BF16), pltpu.VMEM((k, tn), BF16),
                        pltpu.VMEM((2, SUBLANES, tn), F32)],
        compiler_params=_params(2),
        name="up_proj_conv_swiglu",
    )(a, w, w, cw, cw, cb, cb, w_down)


def _down_kernel(a_ref, w_ref, r_ref, o_ref):
    o_ref[...] = r_ref[...] + jnp.dot(a_ref[...], w_ref[...], preferred_element_type=F32)


def _down_proj(a, w_bf16, res, tm, tn):
    t, k = a.shape
    n = w_bf16.shape[1]
    return pl.pallas_call(
        _down_kernel,
        grid=(n // tn, t // tm),
        in_specs=[pl.BlockSpec((tm, k), lambda j, m: (m, 0)),
                  pl.BlockSpec((k, tn), lambda j, m: (0, j)),
                  pl.BlockSpec((tm, tn), lambda j, m: (m, j))],
        out_specs=pl.BlockSpec((tm, tn), lambda j, m: (m, j)),
        out_shape=jax.ShapeDtypeStruct((t, n), F32),
        compiler_params=_params(2),
        name="down_proj",
    )(a, w_bf16, res)


def _pad_lanes(v):
    return jnp.pad(v.astype(F32), (0, LANES - v.shape[0]))


def _mixer_layer(h, batch, seq, norm_mix, w_in, sinks, attn_out_norm, ssd_conv_w, ssd_conv_b, dt_bias,
                 a_log, ssd_d, ssd_norm, w_out, norm_ffn, tables, emat):
    w_in_t = jnp.swapaxes(w_in, 0, 1)
    wdt_t = jnp.pad(w_in_t[MAIN_PROJ:], ((0, LANES - SSD_HEADS), (0, 0))).astype(BF16)
    xn, dt_rawt = _norm_dt(h, norm_mix.reshape(1, -1), wdt_t, 512, batch, seq)
    proj = _in_proj(xn, w_in_t, tm=1024, tn=1024)
    attn, w_out_bf16 = _attention(proj, sinks.astype(F32), attn_out_norm.reshape(1, -1), tables, w_out,
                                  batch, seq, qb=4)
    bias = _pad_lanes(dt_bias)
    alog = _pad_lanes(a_log)
    y = _ssd(proj, dt_rawt, ssd_conv_w, ssd_conv_b.reshape(1, -1), bias.reshape(-1, 1), alog.reshape(-1, 1),
             emat, jnp.repeat(ssd_d.astype(F32), SSD_HEAD_DIM).reshape(1, -1),
             ssd_norm.reshape(1, -1), batch, seq)
    return _out_proj(attn, y, w_out_bf16, h, norm_ffn.reshape(1, -1), tm=512)


def kernel(x, norm_mix, w_in, sinks, attn_out_norm, ssd_conv_w, ssd_conv_b, dt_bias, a_log, ssd_d, ssd_norm,
           w_out, norm_ffn, w_up, ffn_conv_w, ffn_conv_b, w_down, norm_final):
    batch, seq, d = x.shape
    h = x.reshape(batch * seq, d)
    tables = _rope_tables(seq)
    head_of_channel = np.arange(SSD_INNER) // SSD_HEAD_DIM
    emat = np.arange(LANES)[:, None] == head_of_channel[None, :]
    emat = jnp.asarray(np.concatenate([emat, emat], axis=0), dtype=BF16)
    for l in range(norm_mix.shape[0]):
        h, hn = _mixer_layer(h, batch, seq, norm_mix[l], w_in[l], sinks[l], attn_out_norm[l], ssd_conv_w[l],
                             ssd_conv_b[l], dt_bias[l], a_log[l], ssd_d[l], ssd_norm[l], w_out[l], norm_ffn[l],
                             tables, emat)
        act, w_down_bf16 = _up_proj(hn, w_up[l], ffn_conv_w[l], ffn_conv_b[l].reshape(1, -1), w_down[l],
                                    tm=1024, tn=512, seq=seq)
        h = _down_proj(act, w_down_bf16, h, tm=512, tn=1024)
    out = _norm(h, norm_final.reshape(1, -1), F32, 512, "norm_final")
    return out.reshape(batch, seq, d)
```

```python
import functools

import numpy as np
import jax
import jax.numpy as jnp
from jax import lax
from jax.experimental import pallas as pl
from jax.experimental.pallas import tpu as pltpu

F32 = jnp.float32
BF16 = jnp.bfloat16

D_MODEL = 2048
N_Q_HEADS = 32
N_KV_HEADS = 8
HEAD_DIM = 64
Q_PER_KV = N_Q_HEADS // N_KV_HEADS
WINDOW = 128
ATTN_BLOCK = 128
ROT_DIM = HEAD_DIM // 4
ROPE_THETA = 500000.0
SSD_HEADS = 32
SSD_HEAD_DIM = 64
SSD_INNER = SSD_HEADS * SSD_HEAD_DIM
SSD_GROUPS = 8
SSD_STATE = 128
SSD_CONV = 4
SSD_CHUNK = 128
ATTN_WIDTH = N_Q_HEADS * HEAD_DIM
KV_WIDTH = N_KV_HEADS * HEAD_DIM
BC_WIDTH = SSD_GROUPS * SSD_STATE
CONV_CH = SSD_INNER + 2 * BC_WIDTH
MAIN_PROJ = ATTN_WIDTH + 2 * KV_WIDTH + SSD_INNER + CONV_CH
D_FF = 5632
FFN_CONV = 3
EPS = 1e-6

LANES = 128
SUBLANES = 8
HALF = LANES // 2
NEG = -1e30
LOG2E = 1.4426950408889634
VMEM_LIMIT = 56 * 1024 * 1024

Q_OFF = 0
Z_OFF = ATTN_WIDTH
XBC_OFF = Z_OFF + SSD_INNER
K_OFF = XBC_OFF + CONV_CH
V_OFF = K_OFF + KV_WIDTH


def _params(n_axes, flags=None):
    return pltpu.CompilerParams(dimension_semantics=("arbitrary",) * n_axes,
                                vmem_limit_bytes=VMEM_LIMIT, flags=flags)


def _sigmoid(x):
    return 1.0 / (1.0 + jnp.exp2(x * -LOG2E))


def _softplus(x):
    return jnp.maximum(x, 0.0) + jnp.log1p(jnp.exp(-jnp.abs(x)))


def _cast_rows(src_ref, dst_ref, rows=256):
    k = src_ref.shape[0]
    for r in range(0, k, rows):
        dst_ref[r:r + rows, :] = src_ref[r:r + rows, :].astype(BF16)


def _delay_rows(blk, sh):
    n, w = blk.shape[0] - SUBLANES, blk.shape[1]
    g = blk.reshape(n // SUBLANES + 1, SUBLANES, w)
    rot = pltpu.roll(g, sh, 1)
    row = lax.broadcasted_iota(jnp.int32, (SUBLANES, w), 0)
    out = jnp.where(row < sh, rot[:-1], rot[1:])
    return out.reshape(n, w)


def _rmsnorm_rows(x, g):
    ms = jnp.mean(x * x, axis=-1, keepdims=True)
    return x * lax.rsqrt(ms + EPS) * g


def _norm_dt_kernel(x_ref, g_ref, wdtt_ref, xn_ref, dtt_ref):
    xn = _rmsnorm_rows(x_ref[...], g_ref[...]).astype(BF16)
    xn_ref[...] = xn
    dtt_ref[...] = lax.dot_general(wdtt_ref[...], xn, (((1,), (1,)), ((), ())),
                                   preferred_element_type=F32)


def _norm_dt(x2, g, wdtt, tm, batch, seq):
    t, d = x2.shape
    assert seq % tm == 0
    per_seq = seq // tm
    return pl.pallas_call(
        _norm_dt_kernel,
        grid=(t // tm,),
        in_specs=[pl.BlockSpec((tm, d), lambda i: (i, 0)),
                  pl.BlockSpec((1, d), lambda i: (0, 0)),
                  pl.BlockSpec((LANES, d), lambda i: (0, 0))],
        out_specs=[pl.BlockSpec((tm, d), lambda i: (i, 0)),
                   pl.BlockSpec((None, LANES, tm), lambda i: (i // per_seq, 0, i % per_seq))],
        out_shape=[jax.ShapeDtypeStruct((t, d), BF16),
                   jax.ShapeDtypeStruct((batch, LANES, seq), F32)],
        compiler_params=_params(1),
        name="norm_dt",
    )(x2, g, wdtt)


def _in_proj_kernel(a_ref, wt_ref, o_ref, wbf_ref):
    @pl.when(pl.program_id(1) == 0)
    def _():
        _cast_rows(wt_ref, wbf_ref)

    o_ref[...] = lax.dot_general(a_ref[...], wbf_ref[...], (((1,), (1,)), ((), ())),
                                 preferred_element_type=F32).astype(o_ref.dtype)


def _in_proj(xn, wt, tm, tn):
    t, k = xn.shape
    u = 1024 // tn

    def wmap(j, m):
        return (jnp.where(j < 2 * u, j, jnp.where(j < 8 * u, j + u, j - 6 * u)), 0)

    return pl.pallas_call(
        _in_proj_kernel,
        grid=(MAIN_PROJ // tn, t // tm),
        in_specs=[pl.BlockSpec((tm, k), lambda j, m: (m, 0)),
                  pl.BlockSpec((tn, k), wmap)],
        out_specs=pl.BlockSpec((tm, tn), lambda j, m: (m, j)),
        out_shape=jax.ShapeDtypeStruct((t, MAIN_PROJ), BF16),
        scratch_shapes=[pltpu.VMEM((tn, k), BF16)],
        compiler_params=_params(2),
        name="in_proj",
    )(xn, wt)


def _rope_tables(seq):
    half = ROT_DIM // 2
    inv = 1.0 / (ROPE_THETA ** (jnp.arange(0, ROT_DIM, 2, dtype=F32) / ROT_DIM))
    ang = jnp.arange(seq, dtype=F32)[:, None] * inv[None, :]
    cos, sin = jnp.cos(ang), jnp.sin(ang)
    d = np.arange(LANES) % HEAD_DIM
    idx = d % half
    in_rot = jnp.asarray(d < ROT_DIM)
    first = jnp.asarray(d < half)
    second = jnp.asarray((d >= half) & (d < ROT_DIM))
    c = jnp.where(in_rot[None, :], cos[:, idx], 1.0)
    s1 = jnp.where(first[None, :], -sin[:, idx], 0.0)
    s2 = jnp.where(second[None, :], sin[:, idx], 0.0)
    return jnp.stack([c, s1, s2]).astype(F32)


def _rope(x, c, s1, s2):
    half = ROT_DIM // 2
    return x * c + pltpu.roll(x, LANES - half, 1) * s1 + pltpu.roll(x, half, 1) * s2


def _attn_kernel(sinks_ref, q_ref, kp_ref, kc_ref, vp_ref, vc_ref,
                 tc_ref, tp_ref, g_ref, wo_ref, o_ref, wob_ref,
                 lhs_ref, kb_ref, vb_ref, s_ref, p_ref, o2_ref, m_ref, tq_ref, bias_ref, acc_ref, *, qb):
    blk = ATTN_BLOCK
    n = pl.program_id(1)
    lo = lax.broadcasted_iota(jnp.int32, (blk, LANES), 1) < HALF

    wob_ref[...] = wo_ref[...].astype(BF16)

    scale = HEAD_DIM ** -0.5 * LOG2E
    for i in range(3):
        tq_ref[i] = tc_ref[i] * scale
    qi = lax.broadcasted_iota(jnp.int32, (blk, 2 * blk), 0)
    kj = lax.broadcasted_iota(jnp.int32, (blk, 2 * blk), 1)
    rel = qi + blk - kj
    band = (rel >= 0) & (rel < WINDOW)
    kmin = jnp.where(n > 0, 0, blk)
    bias_ref[0] = jnp.where(band & (kj >= kmin), 0.0, NEG)
    bias_ref[1] = jnp.where(band, 0.0, NEG)

    for j in range(qb):
        rows = slice(j * blk, (j + 1) * blk)
        for col in range(ATTN_WIDTH // LANES):
            h, jj = col // 2, col % 2
            q2 = _rope(q_ref[rows, col * LANES:(col + 1) * LANES].astype(F32),
                       tq_ref[0, rows, :], tq_ref[1, rows, :], tq_ref[2, rows, :])
            lhs_ref[j, h, (2 * jj) * blk:(2 * jj + 1) * blk, :] = jnp.where(lo, q2, 0.0).astype(BF16)
            lhs_ref[j, h, (2 * jj + 1) * blk:(2 * jj + 2) * blk, :] = jnp.where(lo, 0.0, q2).astype(BF16)
    nk = (qb + 1) * blk
    lok = lax.broadcasted_iota(jnp.int32, (nk, LANES), 1) < HALF
    ones = jnp.ones((nk, LANES), BF16)
    for i in range(N_KV_HEADS // 2):
        cols = slice(i * LANES, (i + 1) * LANES)
        kcat = jnp.concatenate([_rope(kp_ref[:, cols].astype(F32), tp_ref[0], tp_ref[1], tp_ref[2]),
                                _rope(kc_ref[:, cols].astype(F32), tc_ref[0], tc_ref[1], tc_ref[2])], axis=0)
        vcat = jnp.concatenate([vp_ref[:, cols], vc_ref[:, cols]], axis=0).astype(F32)
        kswp = pltpu.roll(kcat, HALF, 1)
        vswp = pltpu.roll(vcat, HALF, 1)
        kb_ref[2 * i] = jnp.where(lok, kcat, kswp).astype(BF16)
        kb_ref[2 * i + 1] = jnp.where(lok, kswp, kcat).astype(BF16)
        vb_ref[2 * i, :, 0:LANES] = jnp.where(lok, vcat, vswp).astype(BF16)
        vb_ref[2 * i + 1, :, 0:LANES] = jnp.where(lok, vswp, vcat).astype(BF16)
        vb_ref[2 * i, :, LANES:2 * LANES] = ones
        vb_ref[2 * i + 1, :, LANES:2 * LANES] = ones

    def keys(j):
        return slice(j * blk, (j + 2) * blk)

    def scores(j, h):
        s_ref[j, h % 2] = lax.dot_general(lhs_ref[j, h], kb_ref[h, keys(j), :], (((1,), (1,)), ((), ())),
                                          preferred_element_type=F32)

    def row_max(j, h):
        slot = h % 2
        for r in range(Q_PER_KV):
            rows = slice(r * blk, (r + 1) * blk)
            sink = sinks_ref[Q_PER_KV * h + r] * LOG2E
            sr = s_ref[j, slot, rows, :] + bias_ref[min(j, 1)]
            s_ref[j, slot, rows, :] = sr
            m = jnp.maximum(jnp.max(sr, axis=-1, keepdims=True), sink)
            m_ref[j, slot, rows, :] = jnp.broadcast_to(m, (blk, LANES))

    def probs(j, h):
        slot = h % 2
        for r in range(Q_PER_KV):
            rows = slice(r * blk, (r + 1) * blk)
            m = m_ref[j, slot, rows, :]
            for half in range(2):
                ln = slice(half * LANES, (half + 1) * LANES)
                p_ref[j, slot, rows, ln] = jnp.exp2(s_ref[j, slot, rows, ln] - m).astype(BF16)

    def weighted(j, h):
        o2_ref[j, h % 2] = jnp.dot(p_ref[j, h % 2], vb_ref[h, keys(j), :],
                                   preferred_element_type=F32)

    def finish(j, h):
        slot = h % 2
        outs = []
        for r in range(Q_PER_KV):
            rows = slice(r * blk, (r + 1) * blk)
            sink = sinks_ref[Q_PER_KV * h + r] * LOG2E
            den = o2_ref[j, slot, rows, LANES:2 * LANES] + jnp.exp2(sink - m_ref[j, slot, rows, :])
            outs.append(o2_ref[j, slot, rows, 0:LANES] * (1.0 / den))
        for jj in range(2):
            col = 2 * h + jj
            acc_ref[j * blk:(j + 1) * blk, col * LANES:(col + 1) * LANES] = jnp.where(
                lo, outs[2 * jj], outs[2 * jj + 1])

    for j in range(qb):
        scores(j, 0)
    for h in range(N_KV_HEADS + 1):
        for stage in (scores, row_max, probs, weighted, finish):
            hh = {scores: h + 1, finish: h - 1}.get(stage, h)
            if 0 <= hh < N_KV_HEADS:
                for j in range(qb):
                    stage(j, hh)

    o_ref[...] = _rmsnorm_rows(acc_ref[...], g_ref[...]).astype(o_ref.dtype)


def _attention(proj, sinks, g, tables, w_out, batch, seq, qb):
    t = proj.shape[0]
    blk = ATTN_BLOCK
    assert seq % (qb * blk) == 0
    nb = seq // blk
    ns = nb // qb
    ko, no = w_out.shape
    slab = ko // (batch * ns)
    assert slab * batch * ns == ko and slab % 16 == 0
    kblk = K_OFF // KV_WIDTH
    vblk = V_OFF // KV_WIDTH

    def cur(b, n):
        return (b * ns + n, 0)

    def prev_rows(b, n):
        return jnp.maximum(b * nb + n * qb - 1, 0)

    return pl.pallas_call(
        functools.partial(_attn_kernel, qb=qb),
        grid=(batch, ns),
        in_specs=[pl.BlockSpec(memory_space=pltpu.SMEM),
                  pl.BlockSpec((qb * blk, ATTN_WIDTH), cur),
                  pl.BlockSpec((blk, KV_WIDTH), lambda b, n: (prev_rows(b, n), kblk)),
                  pl.BlockSpec((qb * blk, KV_WIDTH), lambda b, n: (b * ns + n, kblk)),
                  pl.BlockSpec((blk, KV_WIDTH), lambda b, n: (prev_rows(b, n), vblk)),
                  pl.BlockSpec((qb * blk, KV_WIDTH), lambda b, n: (b * ns + n, vblk)),
                  pl.BlockSpec((3, qb * blk, LANES), lambda b, n: (0, n, 0)),
                  pl.BlockSpec((3, blk, LANES), lambda b, n: (0, jnp.maximum(n * qb - 1, 0), 0)),
                  pl.BlockSpec((1, ATTN_WIDTH), lambda b, n: (0, 0)),
                  pl.BlockSpec((slab, no), cur)],
        out_specs=[pl.BlockSpec((qb * blk, ATTN_WIDTH), cur),
                   pl.BlockSpec((slab, no), cur)],
        out_shape=[jax.ShapeDtypeStruct((t, ATTN_WIDTH), BF16),
                   jax.ShapeDtypeStruct((ko, no), BF16)],
        scratch_shapes=[pltpu.VMEM((qb, N_KV_HEADS, Q_PER_KV * blk, LANES), BF16),
                        pltpu.VMEM((N_KV_HEADS, (qb + 1) * blk, LANES), BF16),
                        pltpu.VMEM((N_KV_HEADS, (qb + 1) * blk, 2 * LANES), BF16),
                        pltpu.VMEM((qb, 2, Q_PER_KV * blk, 2 * blk), F32),
                        pltpu.VMEM((qb, 2, Q_PER_KV * blk, 2 * blk), BF16),
                        pltpu.VMEM((qb, 2, Q_PER_KV * blk, 2 * LANES), F32),
                        pltpu.VMEM((qb, 2, Q_PER_KV * blk, LANES), F32),
                        pltpu.VMEM((3, qb * blk, LANES), F32),
                        pltpu.VMEM((2, blk, 2 * blk), F32),
                        pltpu.VMEM((qb * blk, ATTN_WIDTH), F32)],
        compiler_params=_params(2),
        name="swa_attention",
    )(sinks, proj, proj, proj, proj, proj, tables, tables, g, w_out)


def _split3(x):
    h = x.astype(BF16)
    r = x - h.astype(F32)
    m = r.astype(BF16)
    l = (r - m.astype(F32)).astype(BF16)
    return h, m, l


def _dot3_lhs(x, w):
    h, m, l = _split3(x)
    d = lambda a: jnp.dot(a, w, preferred_element_type=F32)
    return (d(l) + d(m)) + d(h)


def _ssd_kernel(xbc_ref, z_ref, dtt_ref, cw_ref, cb_ref, bcol_ref, alcol_ref, e_ref, dfull_ref, gn_ref, o_ref,
                ext_ref, xact_ref, state_ref, xdt_ref, xdec_ref, y_ref, exp_ref, acs_ref, acst_ref, *, nbat):
    L = SSD_CHUNK
    c = pl.program_id(0)
    cw_chunk = 512
    gw = SSD_HEAD_DIM * (SSD_HEADS // SSD_GROUPS)
    ri = lax.broadcasted_iota(jnp.int32, (L, L), 0)
    ci = lax.broadcasted_iota(jnp.int32, (L, L), 1)
    causal = ri >= ci
    lo = ci < HALF

    @pl.when(c == 0)
    def _():
        ext_ref[:, 0:SUBLANES, :] = jnp.zeros((nbat, SUBLANES, CONV_CH), F32)
        state_ref[...] = jnp.zeros_like(state_ref)

    def conv(b):
        ext_ref[b, SUBLANES:SUBLANES + L, :] = xbc_ref[b].astype(F32)
        for j in range(CONV_CH // cw_chunk):
            cs = slice(j * cw_chunk, (j + 1) * cw_chunk)
            blk = ext_ref[b, :, cs]
            acc = cb_ref[:, cs]
            for k in range(SSD_CONV - 1):
                acc = acc + _delay_rows(blk, SSD_CONV - 1 - k) * cw_ref[k:k + 1, cs]
            acc = acc + blk[SUBLANES:] * cw_ref[SSD_CONV - 1:SSD_CONV, cs]
            xact_ref[b, :, cs] = acc * _sigmoid(acc)
        ext_ref[b, 0:SUBLANES, :] = ext_ref[b, L:L + SUBLANES, :]

    def decay(b):
        nh = SSD_HEADS
        dtt = _softplus(dtt_ref[b, 0:nh, :] + bcol_ref[0:nh, :])
        dat = dtt * (-jnp.exp(alcol_ref[0:nh, :]))
        tri_u = jnp.where(ri <= ci, 1.0, 0.0).astype(BF16)
        acst = _dot3_lhs(dat, tri_u) * LOG2E
        acst_ref[b, 0:nh, :] = acst
        pad = jnp.zeros((LANES - nh, L), F32)
        a_cs = jnp.concatenate([acst, pad], axis=0).T
        dt = jnp.concatenate([dtt, pad], axis=0).T
        acs_ref[b] = a_cs
        a_last = a_cs[L - 1:L, :]
        stack = jnp.concatenate([dt, jnp.exp2(a_last - a_cs), jnp.exp2(a_cs),
                                 jnp.broadcast_to(jnp.exp2(a_last), (SUBLANES, LANES))], axis=0)
        hi = stack.astype(BF16)
        lo = (stack - hi.astype(F32)).astype(BF16)
        split = jnp.concatenate([hi, lo], axis=1)
        for j in range(SSD_INNER // cw_chunk):
            cs = slice(j * cw_chunk, (j + 1) * cw_chunk)
            exp_ref[b, :, cs] = jnp.dot(split, e_ref[:, cs], preferred_element_type=F32)

    def scale_x(b):
        for j in range(SSD_INNER // cw_chunk):
            cs = slice(j * cw_chunk, (j + 1) * cw_chunk)
            xdt = xact_ref[b, :, cs] * exp_ref[b, 0:L, cs]
            xdt_ref[b, :, cs] = xdt.astype(BF16)
            xdec_ref[b, :, cs] = (xdt * exp_ref[b, L:2 * L, cs]).astype(BF16)

    def group(b, g):
        bg = xact_ref[b, :, SSD_INNER + g * SSD_STATE:SSD_INNER + (g + 1) * SSD_STATE]
        cg = xact_ref[b, :, SSD_INNER + BC_WIDTH + g * SSD_STATE:SSD_INNER + BC_WIDTH + (g + 1) * SSD_STATE]
        bb = bg.astype(BF16)
        cbf = cg.astype(BF16)
        cbm = lax.dot_general(cbf, bb, (((1,), (1,)), ((), ())), preferred_element_type=F32)
        gs = slice(g * gw, (g + 1) * gw)
        prev = state_ref[b, g]
        yoff = jnp.dot(cbf, prev.astype(BF16), preferred_element_type=F32) * exp_ref[b, 2 * L:3 * L, gs]
        btb = bg.T.astype(BF16)
        state_ref[b, g] = prev * exp_ref[b, 3 * L:3 * L + 1, gs] + jnp.dot(
            btb, xdec_ref[b, :, gs], preferred_element_type=F32)
        for e in range(2):
            pc = slice((2 * g + e) * LANES, (2 * g + e + 1) * LANES)
            xpair = xdt_ref[b, :, pc]
            yd = []
            for r in range(2):
                h = 4 * g + 2 * e + r
                seg = acs_ref[b, :, h:h + 1] - acst_ref[b, h:h + 1, :]
                lm = jnp.exp2(jnp.where(causal, seg, NEG))
                yd.append(jnp.dot((cbm * lm).astype(BF16), xpair, preferred_element_type=F32))
            y_ref[b, :, pc] = (jnp.where(lo, yd[0], yd[1]) + yoff[:, e * LANES:(e + 1) * LANES]
                               + dfull_ref[:, pc] * xact_ref[b, :, pc])

    def gate_norm(b, g):
        gs = slice(g * gw, (g + 1) * gw)
        zz = z_ref[b, :, gs].astype(F32)
        yg = y_ref[b, :, gs] * (zz * _sigmoid(zz))
        ms = jnp.mean(yg * yg, axis=-1, keepdims=True)
        o_ref[b, :, gs] = (yg * lax.rsqrt(ms + EPS) * gn_ref[:, gs]).astype(o_ref.dtype)

    for stage in (decay, conv, scale_x):
        for b in range(nbat):
            stage(b)
    for stage in (group, gate_norm):
        for g in range(SSD_GROUPS):
            for b in range(nbat):
                stage(b, g)


def _ssd(proj, dt_rawt, cw, cb, bcol, alcol, emat, dfull, gn, batch, seq):
    L = SSD_CHUNK
    nc = seq // L
    proj3 = proj.reshape(batch, seq, MAIN_PROJ)
    full = lambda shape: pl.BlockSpec(shape, lambda c: (0, 0))
    out = pl.pallas_call(
        functools.partial(_ssd_kernel, nbat=batch),
        grid=(nc,),
        in_specs=[pl.BlockSpec((batch, L, CONV_CH), lambda c: (0, c, XBC_OFF // CONV_CH)),
                  pl.BlockSpec((batch, L, SSD_INNER), lambda c: (0, c, Z_OFF // SSD_INNER)),
                  pl.BlockSpec((batch, LANES, L), lambda c: (0, 0, c)),
                  full((SSD_CONV, CONV_CH)), full((1, CONV_CH)),
                  full((LANES, 1)), full((LANES, 1)),
                  full((2 * LANES, SSD_INNER)), full((1, SSD_INNER)), full((1, SSD_INNER))],
        out_specs=pl.BlockSpec((batch, L, SSD_INNER), lambda c: (0, c, 0)),
        out_shape=jax.ShapeDtypeStruct((batch, seq, SSD_INNER), BF16),
        scratch_shapes=[pltpu.VMEM((batch, L + SUBLANES, CONV_CH), F32),
                        pltpu.VMEM((batch, L, CONV_CH), F32),
                        pltpu.VMEM((batch, SSD_GROUPS, SSD_STATE, SSD_INNER // SSD_GROUPS), F32),
                        pltpu.VMEM((batch, L, SSD_INNER), BF16),
                        pltpu.VMEM((batch, L, SSD_INNER), BF16),
                        pltpu.VMEM((batch, L, SSD_INNER), F32),
                        pltpu.VMEM((batch, 3 * L + SUBLANES, SSD_INNER), F32),
                        pltpu.VMEM((batch, L, LANES), F32),
                        pltpu.VMEM((batch, LANES, L), F32)],
        compiler_params=_params(1),
        name="ssd_scan",
    )(proj3, proj3, dt_rawt, cw, cb, bcol, alcol, emat, dfull, gn)
    return out.reshape(batch * seq, SSD_INNER)


def _out_proj_kernel(a1_ref, a2_ref, w1_ref, w2_ref, r_ref, g_ref, h_ref, hn_ref):
    acc = jnp.dot(a1_ref[...], w1_ref[...], preferred_element_type=F32)
    acc = acc + jnp.dot(a2_ref[...], w2_ref[...], preferred_element_type=F32)
    h = r_ref[...] + acc
    h_ref[...] = h
    hn_ref[...] = _rmsnorm_rows(h, g_ref[...]).astype(hn_ref.dtype)


def _out_proj(a1, a2, w_bf16, res, g, tm):
    t, k = a1.shape
    n = w_bf16.shape[1]
    row = lambda shape: pl.BlockSpec(shape, lambda m: (m, 0))
    wspec = lambda i: pl.BlockSpec((k, n), lambda m: (i, 0), pipeline_mode=pl.Buffered(1))
    return pl.pallas_call(
        _out_proj_kernel,
        grid=(t // tm,),
        in_specs=[row((tm, k)), row((tm, k)), wspec(0), wspec(1), row((tm, n)),
                  pl.BlockSpec((1, n), lambda m: (0, 0))],
        out_specs=[row((tm, n)), row((tm, n))],
        out_shape=[jax.ShapeDtypeStruct((t, n), F32), jax.ShapeDtypeStruct((t, n), BF16)],
        compiler_params=_params(1),
        name="out_proj_norm",
    )(a1, a2, w_bf16, w_bf16, res, g)


def _up_kernel(a_ref, wg_ref, wv_ref, cwg_ref, cwv_ref, cbg_ref, cbv_ref, wd_ref, o_ref, wdb_ref,
               wgb_ref, wvb_ref, carry_ref, *, tm, seq):
    m = pl.program_id(1)
    wdb_ref[...] = wd_ref[...].astype(BF16)

    @pl.when(m == 0)
    def _():
        _cast_rows(wg_ref, wgb_ref)
        _cast_rows(wv_ref, wvb_ref)

    a = a_ref[...]
    seq_start = (m * tm) % seq == 0
    outs = []
    for idx, (wb_ref, cw_ref, cb_ref) in enumerate(((wgb_ref, cwg_ref, cbg_ref),
                                                    (wvb_ref, cwv_ref, cbv_ref))):
        u = jnp.dot(a, wb_ref[...], preferred_element_type=F32)
        above = jnp.where(seq_start, 0.0, carry_ref[idx])
        blk = jnp.concatenate([above, u], axis=0)
        cw = cw_ref[...]
        conv = cb_ref[...]
        for k in range(FFN_CONV - 1):
            conv = conv + _delay_rows(blk, FFN_CONV - 1 - k) * cw[k:k + 1, :]
        conv = conv + u * cw[FFN_CONV - 1:FFN_CONV, :]
        carry_ref[idx] = u[tm - SUBLANES:tm]
        outs.append(conv)
    gate, val = outs
    o_ref[...] = ((gate * _sigmoid(gate)) * val).astype(o_ref.dtype)


def _up_proj(a, w, cw, cb, w_down, tm, tn, seq):
    t, k = a.shape
    assert seq % tm == 0 and D_FF % tn == 0
    nb = D_FF // tn
    n_m = t // tm
    kd, nd = w_down.shape
    slab = kd // (nb * n_m)
    assert slab * nb * n_m == kd and slab % 16 == 0
    return pl.pallas_call(
        functools.partial(_up_kernel, tm=tm, seq=seq),
        grid=(nb, t // tm),
        in_specs=[pl.BlockSpec((tm, k), lambda j, m: (m, 0)),
                  pl.BlockSpec((k, tn), lambda j, m: (0, j)),
                  pl.BlockSpec((k, tn), lambda j, m: (0, j + nb)),
                  pl.BlockSpec((FFN_CONV, tn), lambda j, m: (0, j)),
                  pl.BlockSpec((FFN_CONV, tn), lambda j, m: (0, j + nb)),
                  pl.BlockSpec((1, tn), lambda j, m: (0, j)),
                  pl.BlockSpec((1, tn), lambda j, m: (0, j + nb)),
                  pl.BlockSpec((slab, nd), lambda j, m: (j * n_m + m, 0))],
        out_specs=[pl.BlockSpec((tm, tn), lambda j, m: (m, j)),
                   pl.BlockSpec((slab, nd), lambda j, m: (j * n_m + m, 0))],
        out_shape=[jax.ShapeDtypeStruct((t, D_FF), BF16),
                   jax.ShapeDtypeStruct((kd, nd), BF16)],
        scratch_shapes=[pltpu.VMEM((k, tn), BF16), pltpu.VMEM((k, tn), BF16),
                        pltpu.VMEM((2, SUBLANES, tn), F32)],
        compiler_params=_params(2),
        name="up_proj_conv_swiglu",
    )(a, w, w, cw, cw, cb, cb, w_down)


def _down_kernel(a_ref, w_ref, r_ref, g_ref, o_ref, *, final_norm):
    h = r_ref[...] + jnp.dot(a_ref[...], w_ref[...], preferred_element_type=F32)
    o_ref[...] = _rmsnorm_rows(h, g_ref[...]) if final_norm else h


def _down_proj(a, w_bf16, res, g, tm, final_norm):
    t, k = a.shape
    n = w_bf16.shape[1]
    row = lambda shape: pl.BlockSpec(shape, lambda m: (m, 0))
    return pl.pallas_call(
        functools.partial(_down_kernel, final_norm=final_norm),
        grid=(t // tm,),
        in_specs=[row((tm, k)),
                  pl.BlockSpec((k, n), lambda m: (0, 0), pipeline_mode=pl.Buffered(1)),
                  row((tm, n)),
                  pl.BlockSpec((1, n), lambda m: (0, 0))],
        out_specs=row((tm, n)),
        out_shape=jax.ShapeDtypeStruct((t, n), F32),
        compiler_params=_params(1),
        name="down_proj_norm" if final_norm else "down_proj",
    )(a, w_bf16, res, g)


def _pad_lanes(v):
    return jnp.pad(v.astype(F32), (0, LANES - v.shape[0]))


def _mixer_layer(h, batch, seq, norm_mix, w_in, sinks, attn_out_norm, ssd_conv_w, ssd_conv_b, dt_bias,
                 a_log, ssd_d, ssd_norm, w_out, norm_ffn, tables, emat):
    w_in_t = jnp.swapaxes(w_in, 0, 1)
    wdt_t = jnp.pad(w_in_t[MAIN_PROJ:], ((0, LANES - SSD_HEADS), (0, 0))).astype(BF16)
    xn, dt_rawt = _norm_dt(h, norm_mix.reshape(1, -1), wdt_t, 512, batch, seq)
    proj = _in_proj(xn, w_in_t, tm=1024, tn=1024)
    attn, w_out_bf16 = _attention(proj, sinks.astype(F32), attn_out_norm.reshape(1, -1), tables, w_out,
                                  batch, seq, qb=4)
    bias = _pad_lanes(dt_bias)
    alog = _pad_lanes(a_log)
    y = _ssd(proj, dt_rawt, ssd_conv_w, ssd_conv_b.reshape(1, -1), bias.reshape(-1, 1), alog.reshape(-1, 1),
             emat, jnp.repeat(ssd_d.astype(F32), SSD_HEAD_DIM).reshape(1, -1),
             ssd_norm.reshape(1, -1), batch, seq)
    return _out_proj(attn, y, w_out_bf16, h, norm_ffn.reshape(1, -1), tm=512)


def kernel(x, norm_mix, w_in, sinks, attn_out_norm, ssd_conv_w, ssd_conv_b, dt_bias, a_log, ssd_d, ssd_norm,
           w_out, norm_ffn, w_up, ffn_conv_w, ffn_conv_b, w_down, norm_final):
    batch, seq, d = x.shape
    h = x.reshape(batch * seq, d)
    tables = _rope_tables(seq)
    head_of_channel = np.arange(SSD_INNER) // SSD_HEAD_DIM
    emat = np.arange(LANES)[:, None] == head_of_channel[None, :]
    emat = jnp.asarray(np.concatenate([emat, emat], axis=0), dtype=BF16)
    for l in range(norm_mix.shape[0]):
        h, hn = _mixer_layer(h, batch, seq, norm_mix[l], w_in[l], sinks[l], attn_out_norm[l], ssd_conv_w[l],
                             ssd_conv_b[l], dt_bias[l], a_log[l], ssd_d[l], ssd_norm[l], w_out[l], norm_ffn[l],
                             tables, emat)
        act, w_down_bf16 = _up_proj(hn, w_up[l], ffn_conv_w[l], ffn_conv_b[l].reshape(1, -1), w_down[l],
                                    tm=1024, tn=512, seq=seq)
        h = _down_proj(act, w_down_bf16, h, norm_final.reshape(1, -1), tm=512,
                       final_norm=(l == norm_mix.shape[0] - 1))
    return h.reshape(batch, seq, d)
```

```python
import functools

import numpy as np
import jax
import jax.numpy as jnp
from jax import lax
from jax.experimental import pallas as pl
from jax.experimental.pallas import tpu as pltpu

F32 = jnp.float32
BF16 = jnp.bfloat16

D_MODEL = 2048
N_Q_HEADS = 32
N_KV_HEADS = 8
HEAD_DIM = 64
Q_PER_KV = N_Q_HEADS // N_KV_HEADS
WINDOW = 128
ATTN_BLOCK = 128
ROT_DIM = HEAD_DIM // 4
ROPE_THETA = 500000.0
SSD_HEADS = 32
SSD_HEAD_DIM = 64
SSD_INNER = SSD_HEADS * SSD_HEAD_DIM
SSD_GROUPS = 8
SSD_STATE = 128
SSD_CONV = 4
SSD_CHUNK = 128
ATTN_WIDTH = N_Q_HEADS * HEAD_DIM
KV_WIDTH = N_KV_HEADS * HEAD_DIM
BC_WIDTH = SSD_GROUPS * SSD_STATE
CONV_CH = SSD_INNER + 2 * BC_WIDTH
MAIN_PROJ = ATTN_WIDTH + 2 * KV_WIDTH + SSD_INNER + CONV_CH
D_FF = 5632
FFN_CONV = 3
EPS = 1e-6

LANES = 128
SUBLANES = 8
HALF = LANES // 2
NEG = -1e30
LOG2E = 1.4426950408889634
VMEM_LIMIT = 56 * 1024 * 1024

Q_OFF = 0
Z_OFF = ATTN_WIDTH
XBC_OFF = Z_OFF + SSD_INNER
K_OFF = XBC_OFF + CONV_CH
V_OFF = K_OFF + KV_WIDTH


def _params(n_axes, flags=None):
    return pltpu.CompilerParams(dimension_semantics=("arbitrary",) * n_axes,
                                vmem_limit_bytes=VMEM_LIMIT, flags=flags)


def _sigmoid(x):
    return 1.0 / (1.0 + jnp.exp2(x * -LOG2E))


def _softplus(x):
    return jnp.maximum(x, 0.0) + jnp.log1p(jnp.exp(-jnp.abs(x)))


def _cast_rows(src_ref, dst_ref, rows=256):
    k = src_ref.shape[0]
    for r in range(0, k, rows):
        dst_ref[r:r + rows, :] = src_ref[r:r + rows, :].astype(BF16)


def _delay_rows(blk, sh):
    n, w = blk.shape[0] - SUBLANES, blk.shape[1]
    g = blk.reshape(n // SUBLANES + 1, SUBLANES, w)
    rot = pltpu.roll(g, sh, 1)
    row = lax.broadcasted_iota(jnp.int32, (SUBLANES, w), 0)
    out = jnp.where(row < sh, rot[:-1], rot[1:])
    return out.reshape(n, w)


def _rmsnorm_rows(x, g):
    ms = jnp.mean(x * x, axis=-1, keepdims=True)
    return x * lax.rsqrt(ms + EPS) * g


def _norm_dt_kernel(x_ref, g_ref, wdtt_ref, xn_ref, dtt_ref):
    xn = _rmsnorm_rows(x_ref[...], g_ref[...]).astype(BF16)
    xn_ref[...] = xn
    dtt_ref[...] = lax.dot_general(wdtt_ref[...], xn, (((1,), (1,)), ((), ())),
                                   preferred_element_type=F32)


def _norm_dt(x2, g, wdtt, tm, batch, seq):
    t, d = x2.shape
    assert seq % tm == 0
    per_seq = seq // tm
    return pl.pallas_call(
        _norm_dt_kernel,
        grid=(t // tm,),
        in_specs=[pl.BlockSpec((tm, d), lambda i: (i, 0)),
                  pl.BlockSpec((1, d), lambda i: (0, 0)),
                  pl.BlockSpec((LANES, d), lambda i: (0, 0))],
        out_specs=[pl.BlockSpec((tm, d), lambda i: (i, 0)),
                   pl.BlockSpec((None, LANES, tm), lambda i: (i // per_seq, 0, i % per_seq))],
        out_shape=[jax.ShapeDtypeStruct((t, d), BF16),
                   jax.ShapeDtypeStruct((batch, LANES, seq), F32)],
        compiler_params=_params(1),
        name="norm_dt",
    )(x2, g, wdtt)


def _in_proj_kernel(a_ref, wt_ref, o_ref, wbf_ref):
    @pl.when(pl.program_id(1) == 0)
    def _():
        _cast_rows(wt_ref, wbf_ref)

    o_ref[...] = lax.dot_general(a_ref[...], wbf_ref[...], (((1,), (1,)), ((), ())),
                                 preferred_element_type=F32).astype(o_ref.dtype)


def _in_proj(xn, wt, tm, tn):
    t, k = xn.shape
    u = 1024 // tn

    def wmap(j, m):
        return (jnp.where(j < 2 * u, j, jnp.where(j < 8 * u, j + u, j - 6 * u)), 0)

    return pl.pallas_call(
        _in_proj_kernel,
        grid=(MAIN_PROJ // tn, t // tm),
        in_specs=[pl.BlockSpec((tm, k), lambda j, m: (m, 0)),
                  pl.BlockSpec((tn, k), wmap)],
        out_specs=pl.BlockSpec((tm, tn), lambda j, m: (m, j)),
        out_shape=jax.ShapeDtypeStruct((t, MAIN_PROJ), BF16),
        scratch_shapes=[pltpu.VMEM((tn, k), BF16)],
        compiler_params=_params(2),
        name="in_proj",
    )(xn, wt)


def _rope_tables(seq):
    half = ROT_DIM // 2
    inv = 1.0 / (ROPE_THETA ** (np.arange(0, ROT_DIM, 2, dtype=np.float64) / ROT_DIM))
    ang = np.arange(seq, dtype=np.float64)[:, None] * inv[None, :]
    cos, sin = np.cos(ang), np.sin(ang)
    d = np.arange(LANES) % HEAD_DIM
    idx = d % half
    c = np.where((d < ROT_DIM)[None, :], cos[:, idx], 1.0)
    s1 = np.where((d < half)[None, :], -sin[:, idx], 0.0)
    s2 = np.where(((d >= half) & (d < ROT_DIM))[None, :], sin[:, idx], 0.0)
    return jnp.asarray(np.stack([c, s1, s2]).astype(np.float32))


def _rope(x, c, s1, s2):
    half = ROT_DIM // 2
    return x * c + pltpu.roll(x, LANES - half, 1) * s1 + pltpu.roll(x, half, 1) * s2


def _attn_kernel(sinks_ref, q_ref, kp_ref, kc_ref, vp_ref, vc_ref,
                 tc_ref, tp_ref, g_ref, wo_ref, o_ref, wob_ref,
                 lhs_ref, kb_ref, vb_ref, s_ref, p_ref, o2_ref, m_ref, tq_ref, bias_ref, acc_ref, *, qb):
    blk = ATTN_BLOCK
    n = pl.program_id(1)
    lo = lax.broadcasted_iota(jnp.int32, (blk, LANES), 1) < HALF

    wob_ref[...] = wo_ref[...].astype(BF16)

    scale = HEAD_DIM ** -0.5 * LOG2E
    for i in range(3):
        tq_ref[i] = tc_ref[i] * scale
    qi = lax.broadcasted_iota(jnp.int32, (blk, 2 * blk), 0)
    kj = lax.broadcasted_iota(jnp.int32, (blk, 2 * blk), 1)
    rel = qi + blk - kj
    band = (rel >= 0) & (rel < WINDOW)
    kmin = jnp.where(n > 0, 0, blk)
    bias_ref[0] = jnp.where(band & (kj >= kmin), 0.0, NEG)
    bias_ref[1] = jnp.where(band, 0.0, NEG)

    for j in range(qb):
        rows = slice(j * blk, (j + 1) * blk)
        for col in range(ATTN_WIDTH // LANES):
            h, jj = col // 2, col % 2
            q2 = _rope(q_ref[rows, col * LANES:(col + 1) * LANES].astype(F32),
                       tq_ref[0, rows, :], tq_ref[1, rows, :], tq_ref[2, rows, :])
            lhs_ref[j, h, (2 * jj) * blk:(2 * jj + 1) * blk, :] = jnp.where(lo, q2, 0.0).astype(BF16)
            lhs_ref[j, h, (2 * jj + 1) * blk:(2 * jj + 2) * blk, :] = jnp.where(lo, 0.0, q2).astype(BF16)
    nk = (qb + 1) * blk
    lok = lax.broadcasted_iota(jnp.int32, (nk, LANES), 1) < HALF
    ones = jnp.ones((nk, LANES), BF16)
    for i in range(N_KV_HEADS // 2):
        cols = slice(i * LANES, (i + 1) * LANES)
        kcat = jnp.concatenate([_rope(kp_ref[:, cols].astype(F32), tp_ref[0], tp_ref[1], tp_ref[2]),
                                _rope(kc_ref[:, cols].astype(F32), tc_ref[0], tc_ref[1], tc_ref[2])], axis=0)
        vcat = jnp.concatenate([vp_ref[:, cols], vc_ref[:, cols]], axis=0).astype(F32)
        kswp = pltpu.roll(kcat, HALF, 1)
        vswp = pltpu.roll(vcat, HALF, 1)
        kb_ref[2 * i] = jnp.where(lok, kcat, kswp).astype(BF16)
        kb_ref[2 * i + 1] = jnp.where(lok, kswp, kcat).astype(BF16)
        vb_ref[2 * i, :, 0:LANES] = jnp.where(lok, vcat, vswp).astype(BF16)
        vb_ref[2 * i + 1, :, 0:LANES] = jnp.where(lok, vswp, vcat).astype(BF16)
        vb_ref[2 * i, :, LANES:2 * LANES] = ones
        vb_ref[2 * i + 1, :, LANES:2 * LANES] = ones

    def keys(j):
        return slice(j * blk, (j + 2) * blk)

    def scores(j, h):
        s_ref[j, h % 2] = lax.dot_general(lhs_ref[j, h], kb_ref[h, keys(j), :], (((1,), (1,)), ((), ())),
                                          preferred_element_type=F32)

    def row_max(j, h):
        slot = h % 2
        for r in range(Q_PER_KV):
            rows = slice(r * blk, (r + 1) * blk)
            sink = sinks_ref[Q_PER_KV * h + r] * LOG2E
            sr = s_ref[j, slot, rows, :] + bias_ref[min(j, 1)]
            s_ref[j, slot, rows, :] = sr
            m = jnp.maximum(jnp.max(sr, axis=-1, keepdims=True), sink)
            m_ref[j, slot, rows, :] = jnp.broadcast_to(m, (blk, LANES))

    def probs(j, h):
        slot = h % 2
        for r in range(Q_PER_KV):
            rows = slice(r * blk, (r + 1) * blk)
            m = m_ref[j, slot, rows, :]
            for half in range(2):
                ln = slice(half * LANES, (half + 1) * LANES)
                p_ref[j, slot, rows, ln] = jnp.exp2(s_ref[j, slot, rows, ln] - m).astype(BF16)

    def weighted(j, h):
        o2_ref[j, h % 2] = jnp.dot(p_ref[j, h % 2], vb_ref[h, keys(j), :],
                                   preferred_element_type=F32)

    def finish(j, h):
        slot = h % 2
        outs = []
        for r in range(Q_PER_KV):
            rows = slice(r * blk, (r + 1) * blk)
            sink = sinks_ref[Q_PER_KV * h + r] * LOG2E
            den = o2_ref[j, slot, rows, LANES:2 * LANES] + jnp.exp2(sink - m_ref[j, slot, rows, :])
            outs.append(o2_ref[j, slot, rows, 0:LANES] * (1.0 / den))
        for jj in range(2):
            col = 2 * h + jj
            acc_ref[j * blk:(j + 1) * blk, col * LANES:(col + 1) * LANES] = jnp.where(
                lo, outs[2 * jj], outs[2 * jj + 1])

    for j in range(qb):
        scores(j, 0)
    for h in range(N_KV_HEADS + 1):
        for stage in (scores, row_max, probs, weighted, finish):
            hh = {scores: h + 1, finish: h - 1}.get(stage, h)
            if 0 <= hh < N_KV_HEADS:
                for j in range(qb):
                    stage(j, hh)

    o_ref[...] = _rmsnorm_rows(acc_ref[...], g_ref[...]).astype(o_ref.dtype)


def _attention(proj, sinks, g, tables, w_out, batch, seq, qb):
    t = proj.shape[0]
    blk = ATTN_BLOCK
    assert seq % (qb * blk) == 0
    nb = seq // blk
    ns = nb // qb
    ko, no = w_out.shape
    slab = ko // (batch * ns)
    assert slab * batch * ns == ko and slab % 16 == 0
    kblk = K_OFF // KV_WIDTH
    vblk = V_OFF // KV_WIDTH

    def cur(b, n):
        return (b * ns + n, 0)

    def prev_rows(b, n):
        return jnp.maximum(b * nb + n * qb - 1, 0)

    return pl.pallas_call(
        functools.partial(_attn_kernel, qb=qb),
        grid=(batch, ns),
        in_specs=[pl.BlockSpec(memory_space=pltpu.SMEM),
                  pl.BlockSpec((qb * blk, ATTN_WIDTH), cur),
                  pl.BlockSpec((blk, KV_WIDTH), lambda b, n: (prev_rows(b, n), kblk)),
                  pl.BlockSpec((qb * blk, KV_WIDTH), lambda b, n: (b * ns + n, kblk)),
                  pl.BlockSpec((blk, KV_WIDTH), lambda b, n: (prev_rows(b, n), vblk)),
                  pl.BlockSpec((qb * blk, KV_WIDTH), lambda b, n: (b * ns + n, vblk)),
                  pl.BlockSpec((3, qb * blk, LANES), lambda b, n: (0, n, 0)),
                  pl.BlockSpec((3, blk, LANES), lambda b, n: (0, jnp.maximum(n * qb - 1, 0), 0)),
                  pl.BlockSpec((1, ATTN_WIDTH), lambda b, n: (0, 0)),
                  pl.BlockSpec((slab, no), cur)],
        out_specs=[pl.BlockSpec((qb * blk, ATTN_WIDTH), cur),
                   pl.BlockSpec((slab, no), cur)],
        out_shape=[jax.ShapeDtypeStruct((t, ATTN_WIDTH), BF16),
                   jax.ShapeDtypeStruct((ko, no), BF16)],
        scratch_shapes=[pltpu.VMEM((qb, N_KV_HEADS, Q_PER_KV * blk, LANES), BF16),
                        pltpu.VMEM((N_KV_HEADS, (qb + 1) * blk, LANES), BF16),
                        pltpu.VMEM((N_KV_HEADS, (qb + 1) * blk, 2 * LANES), BF16),
                        pltpu.VMEM((qb, 2, Q_PER_KV * blk, 2 * blk), F32),
                        pltpu.VMEM((qb, 2, Q_PER_KV * blk, 2 * blk), BF16),
                        pltpu.VMEM((qb, 2, Q_PER_KV * blk, 2 * LANES), F32),
                        pltpu.VMEM((qb, 2, Q_PER_KV * blk, LANES), F32),
                        pltpu.VMEM((3, qb * blk, LANES), F32),
                        pltpu.VMEM((2, blk, 2 * blk), F32),
                        pltpu.VMEM((qb * blk, ATTN_WIDTH), F32)],
        compiler_params=_params(2),
        name="swa_attention",
    )(sinks, proj, proj, proj, proj, proj, tables, tables, g, w_out)


def _split3(x):
    h = x.astype(BF16)
    r = x - h.astype(F32)
    m = r.astype(BF16)
    l = (r - m.astype(F32)).astype(BF16)
    return h, m, l


def _dot3_lhs(x, w):
    h, m, l = _split3(x)
    d = lambda a: jnp.dot(a, w, preferred_element_type=F32)
    return (d(l) + d(m)) + d(h)


def _ssd_kernel(xbc_ref, z_ref, dtt_ref, cw_ref, cb_ref, bcol_ref, alcol_ref, e_ref, dfull_ref, gn_ref, o_ref,
                ext_ref, xact_ref, state_ref, xdt_ref, xdec_ref, y_ref, exp_ref, acs_ref, acst_ref, *, nbat):
    L = SSD_CHUNK
    c = pl.program_id(0)
    cw_chunk = 512
    gw = SSD_HEAD_DIM * (SSD_HEADS // SSD_GROUPS)
    ri = lax.broadcasted_iota(jnp.int32, (L, L), 0)
    ci = lax.broadcasted_iota(jnp.int32, (L, L), 1)
    causal = ri >= ci
    lo = ci < HALF

    @pl.when(c == 0)
    def _():
        ext_ref[:, 0:SUBLANES, :] = jnp.zeros((nbat, SUBLANES, CONV_CH), F32)
        state_ref[...] = jnp.zeros_like(state_ref)

    def conv(b):
        ext_ref[b, SUBLANES:SUBLANES + L, :] = xbc_ref[b].astype(F32)
        for j in range(CONV_CH // cw_chunk):
            cs = slice(j * cw_chunk, (j + 1) * cw_chunk)
            blk = ext_ref[b, :, cs]
            acc = cb_ref[:, cs]
            for k in range(SSD_CONV - 1):
                acc = acc + _delay_rows(blk, SSD_CONV - 1 - k) * cw_ref[k:k + 1, cs]
            acc = acc + blk[SUBLANES:] * cw_ref[SSD_CONV - 1:SSD_CONV, cs]
            xact_ref[b, :, cs] = acc * _sigmoid(acc)
        ext_ref[b, 0:SUBLANES, :] = ext_ref[b, L:L + SUBLANES, :]

    def decay(b):
        nh = SSD_HEADS
        dtt = _softplus(dtt_ref[b, 0:nh, :] + bcol_ref[0:nh, :])
        dat = dtt * (-jnp.exp(alcol_ref[0:nh, :]))
        tri_u = jnp.where(ri <= ci, 1.0, 0.0).astype(BF16)
        acst = _dot3_lhs(dat, tri_u) * LOG2E
        acst_ref[b, 0:nh, :] = acst
        pad = jnp.zeros((LANES - nh, L), F32)
        a_cs = jnp.concatenate([acst, pad], axis=0).T
        dt = jnp.concatenate([dtt, pad], axis=0).T
        acs_ref[b] = a_cs
        a_last = a_cs[L - 1:L, :]
        stack = jnp.concatenate([dt, jnp.exp2(a_last - a_cs), jnp.exp2(a_cs),
                                 jnp.broadcast_to(jnp.exp2(a_last), (SUBLANES, LANES))], axis=0)
        hi = stack.astype(BF16)
        lo = (stack - hi.astype(F32)).astype(BF16)
        split = jnp.concatenate([hi, lo], axis=1)
        for j in range(SSD_INNER // cw_chunk):
            cs = slice(j * cw_chunk, (j + 1) * cw_chunk)
            exp_ref[b, :, cs] = jnp.dot(split, e_ref[:, cs], preferred_element_type=F32)

    def scale_x(b):
        for j in range(SSD_INNER // cw_chunk):
            cs = slice(j * cw_chunk, (j + 1) * cw_chunk)
            xdt = xact_ref[b, :, cs] * exp_ref[b, 0:L, cs]
            xdt_ref[b, :, cs] = xdt.astype(BF16)
            xdec_ref[b, :, cs] = (xdt * exp_ref[b, L:2 * L, cs]).astype(BF16)

    def group(b, g):
        bg = xact_ref[b, :, SSD_INNER + g * SSD_STATE:SSD_INNER + (g + 1) * SSD_STATE]
        cg = xact_ref[b, :, SSD_INNER + BC_WIDTH + g * SSD_STATE:SSD_INNER + BC_WIDTH + (g + 1) * SSD_STATE]
        bb = bg.astype(BF16)
        cbf = cg.astype(BF16)
        cbm = lax.dot_general(cbf, bb, (((1,), (1,)), ((), ())), preferred_element_type=F32)
        gs = slice(g * gw, (g + 1) * gw)
        prev = state_ref[b, g]
        yoff = jnp.dot(cbf, prev.astype(BF16), preferred_element_type=F32) * exp_ref[b, 2 * L:3 * L, gs]
        btb = bg.T.astype(BF16)
        state_ref[b, g] = prev * exp_ref[b, 3 * L:3 * L + 1, gs] + jnp.dot(
            btb, xdec_ref[b, :, gs], preferred_element_type=F32)
        for e in range(2):
            pc = slice((2 * g + e) * LANES, (2 * g + e + 1) * LANES)
            xpair = xdt_ref[b, :, pc]
            yd = []
            for r in range(2):
                h = 4 * g + 2 * e + r
                seg = acs_ref[b, :, h:h + 1] - acst_ref[b, h:h + 1, :]
                lm = jnp.exp2(jnp.where(causal, seg, NEG))
                yd.append(jnp.dot((cbm * lm).astype(BF16), xpair, preferred_element_type=F32))
            y_ref[b, :, pc] = (jnp.where(lo, yd[0], yd[1]) + yoff[:, e * LANES:(e + 1) * LANES]
                               + dfull_ref[:, pc] * xact_ref[b, :, pc])

    def gate_norm(b, g):
        gs = slice(g * gw, (g + 1) * gw)
        zz = z_ref[b, :, gs].astype(F32)
        yg = y_ref[b, :, gs] * (zz * _sigmoid(zz))
        ms = jnp.mean(yg * yg, axis=-1, keepdims=True)
        o_ref[b, :, gs] = (yg * lax.rsqrt(ms + EPS) * gn_ref[:, gs]).astype(o_ref.dtype)

    for stage in (decay, conv, scale_x):
        for b in range(nbat):
            stage(b)
    for stage in (group, gate_norm):
        for g in range(SSD_GROUPS):
            for b in range(nbat):
                stage(b, g)


def _ssd(proj, dt_rawt, cw, cb, bcol, alcol, emat, dfull, gn, batch, seq):
    L = SSD_CHUNK
    nc = seq // L
    proj3 = proj.reshape(batch, seq, MAIN_PROJ)
    full = lambda shape: pl.BlockSpec(shape, lambda c: (0, 0))
    out = pl.pallas_call(
        functools.partial(_ssd_kernel, nbat=batch),
        grid=(nc,),
        in_specs=[pl.BlockSpec((batch, L, CONV_CH), lambda c: (0, c, XBC_OFF // CONV_CH)),
                  pl.BlockSpec((batch, L, SSD_INNER), lambda c: (0, c, Z_OFF // SSD_INNER)),
                  pl.BlockSpec((batch, LANES, L), lambda c: (0, 0, c)),
                  full((SSD_CONV, CONV_CH)), full((1, CONV_CH)),
                  full((LANES, 1)), full((LANES, 1)),
                  full((2 * LANES, SSD_INNER)), full((1, SSD_INNER)), full((1, SSD_INNER))],
        out_specs=pl.BlockSpec((batch, L, SSD_INNER), lambda c: (0, c, 0)),
        out_shape=jax.ShapeDtypeStruct((batch, seq, SSD_INNER), BF16),
        scratch_shapes=[pltpu.VMEM((batch, L + SUBLANES, CONV_CH), F32),
                        pltpu.VMEM((batch, L, CONV_CH), F32),
                        pltpu.VMEM((batch, SSD_GROUPS, SSD_STATE, SSD_INNER // SSD_GROUPS), F32),
                        pltpu.VMEM((batch, L, SSD_INNER), BF16),
                        pltpu.VMEM((batch, L, SSD_INNER), BF16),
                        pltpu.VMEM((batch, L, SSD_INNER), F32),
                        pltpu.VMEM((batch, 3 * L + SUBLANES, SSD_INNER), F32),
                        pltpu.VMEM((batch, L, LANES), F32),
                        pltpu.VMEM((batch, LANES, L), F32)],
        compiler_params=_params(1),
        name="ssd_scan",
    )(proj3, proj3, dt_rawt, cw, cb, bcol, alcol, emat, dfull, gn)
    return out.reshape(batch * seq, SSD_INNER)


def _out_proj_kernel(a1_ref, a2_ref, w1_ref, w2_ref, r_ref, g_ref, h_ref, hn_ref):
    acc = jnp.dot(a1_ref[...], w1_ref[...], preferred_element_type=F32)
    acc = acc + jnp.dot(a2_ref[...], w2_ref[...], preferred_element_type=F32)
    h = r_ref[...] + acc
    h_ref[...] = h
    hn_ref[...] = _rmsnorm_rows(h, g_ref[...]).astype(hn_ref.dtype)


def _out_proj(a1, a2, w_bf16, res, g, tm):
    t, k = a1.shape
    n = w_bf16.shape[1]
    row = lambda shape: pl.BlockSpec(shape, lambda m: (m, 0))
    wspec = lambda i: pl.BlockSpec((k, n), lambda m: (i, 0), pipeline_mode=pl.Buffered(1))
    return pl.pallas_call(
        _out_proj_kernel,
        grid=(t // tm,),
        in_specs=[row((tm, k)), row((tm, k)), wspec(0), wspec(1), row((tm, n)),
                  pl.BlockSpec((1, n), lambda m: (0, 0))],
        out_specs=[row((tm, n)), row((tm, n))],
        out_shape=[jax.ShapeDtypeStruct((t, n), F32), jax.ShapeDtypeStruct((t, n), BF16)],
        compiler_params=_params(1),
        name="out_proj_norm",
    )(a1, a2, w_bf16, w_bf16, res, g)


def _up_kernel(a_ref, wg_ref, wv_ref, cwg_ref, cwv_ref, cbg_ref, cbv_ref, wd_ref, o_ref, wdb_ref,
               wgb_ref, wvb_ref, carry_ref, *, tm, seq):
    m = pl.program_id(1)
    wdb_ref[...] = wd_ref[...].astype(BF16)

    @pl.when(m == 0)
    def _():
        _cast_rows(wg_ref, wgb_ref)
        _cast_rows(wv_ref, wvb_ref)

    a = a_ref[...]
    seq_start = (m * tm) % seq == 0
    outs = []
    for idx, (wb_ref, cw_ref, cb_ref) in enumerate(((wgb_ref, cwg_ref, cbg_ref),
                                                    (wvb_ref, cwv_ref, cbv_ref))):
        u = jnp.dot(a, wb_ref[...], preferred_element_type=F32)
        above = jnp.where(seq_start, 0.0, carry_ref[idx])
        blk = jnp.concatenate([above, u], axis=0)
        cw = cw_ref[...]
        conv = cb_ref[...]
        for k in range(FFN_CONV - 1):
            conv = conv + _delay_rows(blk, FFN_CONV - 1 - k) * cw[k:k + 1, :]
        conv = conv + u * cw[FFN_CONV - 1:FFN_CONV, :]
        carry_ref[idx] = u[tm - SUBLANES:tm]
        outs.append(conv)
    gate, val = outs
    o_ref[...] = ((gate * _sigmoid(gate)) * val).astype(o_ref.dtype)


def _up_proj(a, w, cw, cb, w_down, tm, tn, seq):
    t, k = a.shape
    assert seq % tm == 0 and D_FF % tn == 0
    nb = D_FF // tn
    n_m = t // tm
    kd, nd = w_down.shape
    slab = kd // (nb * n_m)
    assert slab * nb * n_m == kd and slab % 16 == 0
    return pl.pallas_call(
        functools.partial(_up_kernel, tm=tm, seq=seq),
        grid=(nb, t // tm),
        in_specs=[pl.BlockSpec((tm, k), lambda j, m: (m, 0)),
                  pl.BlockSpec((k, tn), lambda j, m: (0, j)),
                  pl.BlockSpec((k, tn), lambda j, m: (0, j + nb)),
                  pl.BlockSpec((FFN_CONV, tn), lambda j, m: (0, j)),
                  pl.BlockSpec((FFN_CONV, tn), lambda j, m: (0, j + nb)),
                  pl.BlockSpec((1, tn), lambda j, m: (0, j)),
                  pl.BlockSpec((1, tn), lambda j, m: (0, j + nb)),
                  pl.BlockSpec((slab, nd), lambda j, m: (j * n_m + m, 0))],
        out_specs=[pl.BlockSpec((tm, tn), lambda j, m: (m, j)),
                   pl.BlockSpec((slab, nd), lambda j, m: (j * n_m + m, 0))],
        out_shape=[jax.ShapeDtypeStruct((t, D_FF), BF16),
                   jax.ShapeDtypeStruct((kd, nd), BF16)],
        scratch_shapes=[pltpu.VMEM((k, tn), BF16), pltpu.VMEM((k, tn), BF16),
                        pltpu.VMEM((2, SUBLANES, tn), F32)],
        compiler_params=_params(2),
        name="up_proj_conv_swiglu",
    )(a, w, w, cw, cw, cb, cb, w_down)


def _down_kernel(a_ref, w_ref, r_ref, g_ref, o_ref, *, final_norm):
    h = r_ref[...] + jnp.dot(a_ref[...], w_ref[...], preferred_element_type=F32)
    o_ref[...] = _rmsnorm_rows(h, g_ref[...]) if final_norm else h


def _down_proj(a, w_bf16, res, g, tm, final_norm):
    t, k = a.shape
    n = w_bf16.shape[1]
    row = lambda shape: pl.BlockSpec(shape, lambda m: (m, 0))
    return pl.pallas_call(
        functools.partial(_down_kernel, final_norm=final_norm),
        grid=(t // tm,),
        in_specs=[row((tm, k)),
                  pl.BlockSpec((k, n), lambda m: (0, 0), pipeline_mode=pl.Buffered(1)),
                  row((tm, n)),
                  pl.BlockSpec((1, n), lambda m: (0, 0))],
        out_specs=row((tm, n)),
        out_shape=jax.ShapeDtypeStruct((t, n), F32),
        compiler_params=_params(1),
        name="down_proj_norm" if final_norm else "down_proj",
    )(a, w_bf16, res, g)


def _pad_lanes(v):
    return jnp.pad(v.astype(F32), (0, LANES - v.shape[0]))


def _mixer_layer(h, batch, seq, norm_mix, w_in, sinks, attn_out_norm, ssd_conv_w, ssd_conv_b, dt_bias,
                 a_log, ssd_d, ssd_norm, w_out, norm_ffn, tables, emat):
    w_in_t = jnp.swapaxes(w_in, 0, 1)
    wdt_t = jnp.pad(w_in_t[MAIN_PROJ:], ((0, LANES - SSD_HEADS), (0, 0))).astype(BF16)
    xn, dt_rawt = _norm_dt(h, norm_mix.reshape(1, -1), wdt_t, 512, batch, seq)
    proj = _in_proj(xn, w_in_t, tm=1024, tn=1024)
    attn, w_out_bf16 = _attention(proj, sinks.astype(F32), attn_out_norm.reshape(1, -1), tables, w_out,
                                  batch, seq, qb=4)
    bias = _pad_lanes(dt_bias)
    alog = _pad_lanes(a_log)
    y = _ssd(proj, dt_rawt, ssd_conv_w, ssd_conv_b.reshape(1, -1), bias.reshape(-1, 1), alog.reshape(-1, 1),
             emat, jnp.repeat(ssd_d.astype(F32), SSD_HEAD_DIM).reshape(1, -1),
             ssd_norm.reshape(1, -1), batch, seq)
    return _out_proj(attn, y, w_out_bf16, h, norm_ffn.reshape(1, -1), tm=512)


def kernel(x, norm_mix, w_in, sinks, attn_out_norm, ssd_conv_w, ssd_conv_b, dt_bias, a_log, ssd_d, ssd_norm,
           w_out, norm_ffn, w_up, ffn_conv_w, ffn_conv_b, w_down, norm_final):
    batch, seq, d = x.shape
    h = x.reshape(batch * seq, d)
    tables = _rope_tables(seq)
    head_of_channel = np.arange(SSD_INNER) // SSD_HEAD_DIM
    emat = np.arange(LANES)[:, None] == head_of_channel[None, :]
    emat = jnp.asarray(np.concatenate([emat, emat], axis=0), dtype=BF16)
    for l in range(norm_mix.shape[0]):
        h, hn = _mixer_layer(h, batch, seq, norm_mix[l], w_in[l], sinks[l], attn_out_norm[l], ssd_conv_w[l],
                             ssd_conv_b[l], dt_bias[l], a_log[l], ssd_d[l], ssd_norm[l], w_out[l], norm_ffn[l],
                             tables, emat)
        act, w_down_bf16 = _up_proj(hn, w_up[l], ffn_conv_w[l], ffn_conv_b[l].reshape(1, -1), w_down[l],
                                    tm=1024, tn=512, seq=seq)
        h = _down_proj(act, w_down_bf16, h, norm_final.reshape(1, -1), tm=512,
                       final_norm=(l == norm_mix.shape[0] - 1))
    return h.reshape(batch, seq, d)
```

```python
import functools

import numpy as np
import jax
import jax.numpy as jnp
from jax import lax
from jax.experimental import pallas as pl
from jax.experimental.pallas import tpu as pltpu

F32 = jnp.float32
BF16 = jnp.bfloat16

D_MODEL = 2048
N_Q_HEADS = 32
N_KV_HEADS = 8
HEAD_DIM = 64
Q_PER_KV = N_Q_HEADS // N_KV_HEADS
WINDOW = 128
ATTN_BLOCK = 128
ROT_DIM = HEAD_DIM // 4
ROPE_THETA = 500000.0
SSD_HEADS = 32
SSD_HEAD_DIM = 64
SSD_INNER = SSD_HEADS * SSD_HEAD_DIM
SSD_GROUPS = 8
SSD_STATE = 128
SSD_CONV = 4
SSD_CHUNK = 128
ATTN_WIDTH = N_Q_HEADS * HEAD_DIM
KV_WIDTH = N_KV_HEADS * HEAD_DIM
BC_WIDTH = SSD_GROUPS * SSD_STATE
CONV_CH = SSD_INNER + 2 * BC_WIDTH
MAIN_PROJ = ATTN_WIDTH + 2 * KV_WIDTH + SSD_INNER + CONV_CH
D_FF = 5632
FFN_CONV = 3
EPS = 1e-6

LANES = 128
SUBLANES = 8
HALF = LANES // 2
NEG = -1e30
LOG2E = 1.4426950408889634
VMEM_LIMIT = 56 * 1024 * 1024

Q_OFF = 0
Z_OFF = ATTN_WIDTH
XBC_OFF = Z_OFF + SSD_INNER
K_OFF = XBC_OFF + CONV_CH
V_OFF = K_OFF + KV_WIDTH


def _params(n_axes, flags=None):
    return pltpu.CompilerParams(dimension_semantics=("arbitrary",) * n_axes,
                                vmem_limit_bytes=VMEM_LIMIT, flags=flags)


def _sigmoid(x):
    return 1.0 / (1.0 + jnp.exp2(x * -LOG2E))


def _softplus(x):
    return jnp.maximum(x, 0.0) + jnp.log1p(jnp.exp(-jnp.abs(x)))


def _cast_rows(src_ref, dst_ref, rows=256):
    k = src_ref.shape[0]
    for r in range(0, k, rows):
        dst_ref[r:r + rows, :] = src_ref[r:r + rows, :].astype(BF16)


def _delay_rows(blk, sh):
    n, w = blk.shape[0] - SUBLANES, blk.shape[1]
    g = blk.reshape(n // SUBLANES + 1, SUBLANES, w)
    rot = pltpu.roll(g, sh, 1)
    row = lax.broadcasted_iota(jnp.int32, (SUBLANES, w), 0)
    out = jnp.where(row < sh, rot[:-1], rot[1:])
    return out.reshape(n, w)


def _rmsnorm_rows(x, g):
    ms = jnp.mean(x * x, axis=-1, keepdims=True)
    return x * lax.rsqrt(ms + EPS) * g


def _in_proj_head_kernel(x_ref, g_ref, wdtt_ref, wt_ref, xn_ref, dtt_ref, o_ref, wbf_ref):
    @pl.when(pl.program_id(0) == 0)
    def _():
        _cast_rows(wt_ref, wbf_ref)

    xn = _rmsnorm_rows(x_ref[...], g_ref[...]).astype(BF16)
    xn_ref[...] = xn
    dtt_ref[...] = lax.dot_general(wdtt_ref[...], xn, (((1,), (1,)), ((), ())),
                                   preferred_element_type=F32)
    o_ref[...] = lax.dot_general(xn, wbf_ref[...], (((1,), (1,)), ((), ())),
                                 preferred_element_type=F32).astype(o_ref.dtype)


def _in_proj_kernel(a_ref, wt_ref, prev_ref, o_ref, wbf_ref):
    del prev_ref
    @pl.when(pl.program_id(1) == 0)
    def _():
        _cast_rows(wt_ref, wbf_ref)

    o_ref[...] = lax.dot_general(a_ref[...], wbf_ref[...], (((1,), (1,)), ((), ())),
                                 preferred_element_type=F32).astype(o_ref.dtype)


def _in_proj(x2, g, wdtt, wt, tm, tn, batch, seq):
    t, k = x2.shape
    assert seq % tm == 0 and tn == 1024
    per_seq = seq // tm
    nj = MAIN_PROJ // tn

    def wblock(j):
        return jnp.where(j < 2, j, jnp.where(j < 8, j + 1, j - 6))

    xn, dtt, proj = pl.pallas_call(
        _in_proj_head_kernel,
        grid=(t // tm,),
        in_specs=[pl.BlockSpec((tm, k), lambda m: (m, 0)),
                  pl.BlockSpec((1, k), lambda m: (0, 0)),
                  pl.BlockSpec((LANES, k), lambda m: (0, 0)),
                  pl.BlockSpec((tn, k), lambda m: (0, 0))],
        out_specs=[pl.BlockSpec((tm, k), lambda m: (m, 0)),
                   pl.BlockSpec((None, LANES, tm), lambda m: (m // per_seq, 0, m % per_seq)),
                   pl.BlockSpec((tm, tn), lambda m: (m, 0))],
        out_shape=[jax.ShapeDtypeStruct((t, k), BF16),
                   jax.ShapeDtypeStruct((batch, LANES, seq), F32),
                   jax.ShapeDtypeStruct((t, MAIN_PROJ), BF16)],
        scratch_shapes=[pltpu.VMEM((tn, k), BF16)],
        compiler_params=_params(1),
        name="norm_in_proj_head",
    )(x2, g, wdtt, wt)
    proj = pl.pallas_call(
        _in_proj_kernel,
        grid=(nj - 1, t // tm),
        in_specs=[pl.BlockSpec((tm, k), lambda j, m: (m, 0)),
                  pl.BlockSpec((tn, k), lambda j, m: (wblock(j + 1), 0)),
                  pl.BlockSpec(memory_space=pl.ANY)],
        out_specs=pl.BlockSpec((tm, tn), lambda j, m: (m, j + 1)),
        out_shape=jax.ShapeDtypeStruct((t, MAIN_PROJ), BF16),
        scratch_shapes=[pltpu.VMEM((tn, k), BF16)],
        input_output_aliases={2: 0},
        compiler_params=_params(2),
        name="in_proj",
    )(xn, wt, proj)
    return proj, dtt


def _rope_tables(seq):
    half = ROT_DIM // 2
    inv = 1.0 / (ROPE_THETA ** (np.arange(0, ROT_DIM, 2, dtype=np.float64) / ROT_DIM))
    ang = np.arange(seq, dtype=np.float64)[:, None] * inv[None, :]
    cos, sin = np.cos(ang), np.sin(ang)
    d = np.arange(LANES) % HEAD_DIM
    idx = d % half
    c = np.where((d < ROT_DIM)[None, :], cos[:, idx], 1.0)
    s1 = np.where((d < half)[None, :], -sin[:, idx], 0.0)
    s2 = np.where(((d >= half) & (d < ROT_DIM))[None, :], sin[:, idx], 0.0)
    return jnp.asarray(np.stack([c, s1, s2]).astype(np.float32))


def _rope(x, c, s1, s2):
    half = ROT_DIM // 2
    return x * c + pltpu.roll(x, LANES - half, 1) * s1 + pltpu.roll(x, half, 1) * s2


def _attn_kernel(sinks_ref, q_ref, kp_ref, kc_ref, vp_ref, vc_ref,
                 tc_ref, tp_ref, g_ref, wo_ref, o_ref, wob_ref,
                 lhs_ref, kb_ref, vb_ref, s_ref, p_ref, o2_ref, m_ref, tq_ref, bias_ref, acc_ref, *, qb):
    blk = ATTN_BLOCK
    n = pl.program_id(1)
    lo = lax.broadcasted_iota(jnp.int32, (blk, LANES), 1) < HALF

    wob_ref[...] = wo_ref[...].astype(BF16)

    scale = HEAD_DIM ** -0.5 * LOG2E
    for i in range(3):
        tq_ref[i] = tc_ref[i] * scale
    qi = lax.broadcasted_iota(jnp.int32, (blk, 2 * blk), 0)
    kj = lax.broadcasted_iota(jnp.int32, (blk, 2 * blk), 1)
    rel = qi + blk - kj
    band = (rel >= 0) & (rel < WINDOW)
    kmin = jnp.where(n > 0, 0, blk)
    bias_ref[0] = jnp.where(band & (kj >= kmin), 0.0, NEG)
    bias_ref[1] = jnp.where(band, 0.0, NEG)

    for j in range(qb):
        rows = slice(j * blk, (j + 1) * blk)
        for col in range(ATTN_WIDTH // LANES):
            h, jj = col // 2, col % 2
            q2 = _rope(q_ref[rows, col * LANES:(col + 1) * LANES].astype(F32),
                       tq_ref[0, rows, :], tq_ref[1, rows, :], tq_ref[2, rows, :])
            lhs_ref[j, h, (2 * jj) * blk:(2 * jj + 1) * blk, :] = jnp.where(lo, q2, 0.0).astype(BF16)
            lhs_ref[j, h, (2 * jj + 1) * blk:(2 * jj + 2) * blk, :] = jnp.where(lo, 0.0, q2).astype(BF16)
    nk = (qb + 1) * blk
    lok = lax.broadcasted_iota(jnp.int32, (nk, LANES), 1) < HALF
    ones = jnp.ones((nk, LANES), BF16)
    for i in range(N_KV_HEADS // 2):
        cols = slice(i * LANES, (i + 1) * LANES)
        kcat = jnp.concatenate([_rope(kp_ref[:, cols].astype(F32), tp_ref[0], tp_ref[1], tp_ref[2]),
                                _rope(kc_ref[:, cols].astype(F32), tc_ref[0], tc_ref[1], tc_ref[2])], axis=0)
        vcat = jnp.concatenate([vp_ref[:, cols], vc_ref[:, cols]], axis=0).astype(F32)
        kswp = pltpu.roll(kcat, HALF, 1)
        vswp = pltpu.roll(vcat, HALF, 1)
        kb_ref[2 * i] = jnp.where(lok, kcat, kswp).astype(BF16)
        kb_ref[2 * i + 1] = jnp.where(lok, kswp, kcat).astype(BF16)
        vb_ref[2 * i, :, 0:LANES] = jnp.where(lok, vcat, vswp).astype(BF16)
        vb_ref[2 * i + 1, :, 0:LANES] = jnp.where(lok, vswp, vcat).astype(BF16)
        vb_ref[2 * i, :, LANES:2 * LANES] = ones
        vb_ref[2 * i + 1, :, LANES:2 * LANES] = ones

    def keys(j):
        return slice(j * blk, (j + 2) * blk)

    def scores(j, h):
        s_ref[j, h % 2] = lax.dot_general(lhs_ref[j, h], kb_ref[h, keys(j), :], (((1,), (1,)), ((), ())),
                                          preferred_element_type=F32)

    def row_max(j, h):
        slot = h % 2
        for r in range(Q_PER_KV):
            rows = slice(r * blk, (r + 1) * blk)
            sink = sinks_ref[Q_PER_KV * h + r] * LOG2E
            sr = s_ref[j, slot, rows, :] + bias_ref[min(j, 1)]
            s_ref[j, slot, rows, :] = sr
            m = jnp.maximum(jnp.max(sr, axis=-1, keepdims=True), sink)
            m_ref[j, slot, rows, :] = jnp.broadcast_to(m, (blk, LANES))

    def probs(j, h):
        slot = h % 2
        for r in range(Q_PER_KV):
            rows = slice(r * blk, (r + 1) * blk)
            m = m_ref[j, slot, rows, :]
            for half in range(2):
                ln = slice(half * LANES, (half + 1) * LANES)
                p_ref[j, slot, rows, ln] = jnp.exp2(s_ref[j, slot, rows, ln] - m).astype(BF16)

    def weighted(j, h):
        o2_ref[j, h % 2] = jnp.dot(p_ref[j, h % 2], vb_ref[h, keys(j), :],
                                   preferred_element_type=F32)

    def finish(j, h):
        slot = h % 2
        outs = []
        for r in range(Q_PER_KV):
            rows = slice(r * blk, (r + 1) * blk)
            sink = sinks_ref[Q_PER_KV * h + r] * LOG2E
            den = o2_ref[j, slot, rows, LANES:2 * LANES] + jnp.exp2(sink - m_ref[j, slot, rows, :])
            outs.append(o2_ref[j, slot, rows, 0:LANES] * (1.0 / den))
        for jj in range(2):
            col = 2 * h + jj
            acc_ref[j * blk:(j + 1) * blk, col * LANES:(col + 1) * LANES] = jnp.where(
                lo, outs[2 * jj], outs[2 * jj + 1])

    for j in range(qb):
        scores(j, 0)
    for h in range(N_KV_HEADS + 1):
        for stage in (scores, row_max, probs, weighted, finish):
            hh = {scores: h + 1, finish: h - 1}.get(stage, h)
            if 0 <= hh < N_KV_HEADS:
                for j in range(qb):
                    stage(j, hh)

    o_ref[...] = _rmsnorm_rows(acc_ref[...], g_ref[...]).astype(o_ref.dtype)


def _attention(proj, sinks, g, tables, w_out, batch, seq, qb):
    t = proj.shape[0]
    blk = ATTN_BLOCK
    assert seq % (qb * blk) == 0
    nb = seq // blk
    ns = nb // qb
    ko, no = w_out.shape
    slab = ko // (batch * ns)
    assert slab * batch * ns == ko and slab % 16 == 0
    kblk = K_OFF // KV_WIDTH
    vblk = V_OFF // KV_WIDTH

    def cur(b, n):
        return (b * ns + n, 0)

    def prev_rows(b, n):
        return jnp.maximum(b * nb + n * qb - 1, 0)

    return pl.pallas_call(
        functools.partial(_attn_kernel, qb=qb),
        grid=(batch, ns),
        in_specs=[pl.BlockSpec(memory_space=pltpu.SMEM),
                  pl.BlockSpec((qb * blk, ATTN_WIDTH), cur),
                  pl.BlockSpec((blk, KV_WIDTH), lambda b, n: (prev_rows(b, n), kblk)),
                  pl.BlockSpec((qb * blk, KV_WIDTH), lambda b, n: (b * ns + n, kblk)),
                  pl.BlockSpec((blk, KV_WIDTH), lambda b, n: (prev_rows(b, n), vblk)),
                  pl.BlockSpec((qb * blk, KV_WIDTH), lambda b, n: (b * ns + n, vblk)),
                  pl.BlockSpec((3, qb * blk, LANES), lambda b, n: (0, n, 0)),
                  pl.BlockSpec((3, blk, LANES), lambda b, n: (0, jnp.maximum(n * qb - 1, 0), 0)),
                  pl.BlockSpec((1, ATTN_WIDTH), lambda b, n: (0, 0)),
                  pl.BlockSpec((slab, no), cur)],
        out_specs=[pl.BlockSpec((qb * blk, ATTN_WIDTH), cur),
                   pl.BlockSpec((slab, no), cur)],
        out_shape=[jax.ShapeDtypeStruct((t, ATTN_WIDTH), BF16),
                   jax.ShapeDtypeStruct((ko, no), BF16)],
        scratch_shapes=[pltpu.VMEM((qb, N_KV_HEADS, Q_PER_KV * blk, LANES), BF16),
                        pltpu.VMEM((N_KV_HEADS, (qb + 1) * blk, LANES), BF16),
                        pltpu.VMEM((N_KV_HEADS, (qb + 1) * blk, 2 * LANES), BF16),
                        pltpu.VMEM((qb, 2, Q_PER_KV * blk, 2 * blk), F32),
                        pltpu.VMEM((qb, 2, Q_PER_KV * blk, 2 * blk), BF16),
                        pltpu.VMEM((qb, 2, Q_PER_KV * blk, 2 * LANES), F32),
                        pltpu.VMEM((qb, 2, Q_PER_KV * blk, LANES), F32),
                        pltpu.VMEM((3, qb * blk, LANES), F32),
                        pltpu.VMEM((2, blk, 2 * blk), F32),
                        pltpu.VMEM((qb * blk, ATTN_WIDTH), F32)],
        compiler_params=_params(2),
        name="swa_attention",
    )(sinks, proj, proj, proj, proj, proj, tables, tables, g, w_out)


def _split3(x):
    h = x.astype(BF16)
    r = x - h.astype(F32)
    m = r.astype(BF16)
    l = (r - m.astype(F32)).astype(BF16)
    return h, m, l


def _dot3_lhs(x, w):
    h, m, l = _split3(x)
    d = lambda a: jnp.dot(a, w, preferred_element_type=F32)
    return (d(l) + d(m)) + d(h)


def _ssd_kernel(xbc_ref, z_ref, dtt_ref, cw_ref, cb_ref, bcol_ref, alcol_ref, e_ref, dfull_ref, gn_ref, o_ref,
                ext_ref, xact_ref, state_ref, xdt_ref, xdec_ref, y_ref, exp_ref, acs_ref, acst_ref, *, nbat):
    L = SSD_CHUNK
    c = pl.program_id(0)
    cw_chunk = 512
    gw = SSD_HEAD_DIM * (SSD_HEADS // SSD_GROUPS)
    ri = lax.broadcasted_iota(jnp.int32, (L, L), 0)
    ci = lax.broadcasted_iota(jnp.int32, (L, L), 1)
    causal = ri >= ci
    lo = ci < HALF

    @pl.when(c == 0)
    def _():
        ext_ref[:, 0:SUBLANES, :] = jnp.zeros((nbat, SUBLANES, CONV_CH), F32)
        state_ref[...] = jnp.zeros_like(state_ref)

    def conv(b):
        ext_ref[b, SUBLANES:SUBLANES + L, :] = xbc_ref[b].astype(F32)
        for j in range(CONV_CH // cw_chunk):
            cs = slice(j * cw_chunk, (j + 1) * cw_chunk)
            blk = ext_ref[b, :, cs]
            acc = cb_ref[:, cs]
            for k in range(SSD_CONV - 1):
                acc = acc + _delay_rows(blk, SSD_CONV - 1 - k) * cw_ref[k:k + 1, cs]
            acc = acc + blk[SUBLANES:] * cw_ref[SSD_CONV - 1:SSD_CONV, cs]
            xact_ref[b, :, cs] = acc * _sigmoid(acc)
        ext_ref[b, 0:SUBLANES, :] = ext_ref[b, L:L + SUBLANES, :]

    def decay(b):
        nh = SSD_HEADS
        dtt = _softplus(dtt_ref[b, 0:nh, :] + bcol_ref[0:nh, :])
        dat = dtt * (-jnp.exp(alcol_ref[0:nh, :]))
        tri_u = jnp.where(ri <= ci, 1.0, 0.0).astype(BF16)
        acst = _dot3_lhs(dat, tri_u) * LOG2E
        acst_ref[b, 0:nh, :] = acst
        pad = jnp.zeros((LANES - nh, L), F32)
        a_cs = jnp.concatenate([acst, pad], axis=0).T
        dt = jnp.concatenate([dtt, pad], axis=0).T
        acs_ref[b] = a_cs
        a_last = a_cs[L - 1:L, :]
        stack = jnp.concatenate([dt, jnp.exp2(a_last - a_cs), jnp.exp2(a_cs),
                                 jnp.broadcast_to(jnp.exp2(a_last), (SUBLANES, LANES))], axis=0)
        hi = stack.astype(BF16)
        lo = (stack - hi.astype(F32)).astype(BF16)
        split = jnp.concatenate([hi, lo], axis=1)
        for j in range(SSD_INNER // cw_chunk):
            cs = slice(j * cw_chunk, (j + 1) * cw_chunk)
            exp_ref[b, :, cs] = jnp.dot(split, e_ref[:, cs], preferred_element_type=F32)

    def scale_x(b):
        for j in range(SSD_INNER // cw_chunk):
            cs = slice(j * cw_chunk, (j + 1) * cw_chunk)
            xdt = xact_ref[b, :, cs] * exp_ref[b, 0:L, cs]
            xdt_ref[b, :, cs] = xdt.astype(BF16)
            xdec_ref[b, :, cs] = (xdt * exp_ref[b, L:2 * L, cs]).astype(BF16)

    def group(b, g):
        bg = xact_ref[b, :, SSD_INNER + g * SSD_STATE:SSD_INNER + (g + 1) * SSD_STATE]
        cg = xact_ref[b, :, SSD_INNER + BC_WIDTH + g * SSD_STATE:SSD_INNER + BC_WIDTH + (g + 1) * SSD_STATE]
        bb = bg.astype(BF16)
        cbf = cg.astype(BF16)
        cbm = lax.dot_general(cbf, bb, (((1,), (1,)), ((), ())), preferred_element_type=F32)
        gs = slice(g * gw, (g + 1) * gw)
        prev = state_ref[b, g]
        yoff = jnp.dot(cbf, prev.astype(BF16), preferred_element_type=F32) * exp_ref[b, 2 * L:3 * L, gs]
        btb = bg.T.astype(BF16)
        state_ref[b, g] = prev * exp_ref[b, 3 * L:3 * L + 1, gs] + jnp.dot(
            btb, xdec_ref[b, :, gs], preferred_element_type=F32)
        for e in range(2):
            pc = slice((2 * g + e) * LANES, (2 * g + e + 1) * LANES)
            xpair = xdt_ref[b, :, pc]
            yd = []
            for r in range(2):
                h = 4 * g + 2 * e + r
                seg = acs_ref[b, :, h:h + 1] - acst_ref[b, h:h + 1, :]
                lm = jnp.exp2(jnp.where(causal, seg, NEG))
                yd.append(jnp.dot((cbm * lm).astype(BF16), xpair, preferred_element_type=F32))
            y_ref[b, :, pc] = (jnp.where(lo, yd[0], yd[1]) + yoff[:, e * LANES:(e + 1) * LANES]
                               + dfull_ref[:, pc] * xact_ref[b, :, pc])

    def gate_norm(b, g):
        gs = slice(g * gw, (g + 1) * gw)
        zz = z_ref[b, :, gs].astype(F32)
        yg = y_ref[b, :, gs] * (zz * _sigmoid(zz))
        ms = jnp.mean(yg * yg, axis=-1, keepdims=True)
        o_ref[b, :, gs] = (yg * lax.rsqrt(ms + EPS) * gn_ref[:, gs]).astype(o_ref.dtype)

    for stage in (decay, conv, scale_x):
        for b in range(nbat):
            stage(b)
    for stage in (group, gate_norm):
        for g in range(SSD_GROUPS):
            for b in range(nbat):
                stage(b, g)


def _ssd(proj, dt_rawt, cw, cb, bcol, alcol, emat, dfull, gn, batch, seq):
    L = SSD_CHUNK
    nc = seq // L
    proj3 = proj.reshape(batch, seq, MAIN_PROJ)
    full = lambda shape: pl.BlockSpec(shape, lambda c: (0, 0))
    out = pl.pallas_call(
        functools.partial(_ssd_kernel, nbat=batch),
        grid=(nc,),
        in_specs=[pl.BlockSpec((batch, L, CONV_CH), lambda c: (0, c, XBC_OFF // CONV_CH)),
                  pl.BlockSpec((batch, L, SSD_INNER), lambda c: (0, c, Z_OFF // SSD_INNER)),
                  pl.BlockSpec((batch, LANES, L), lambda c: (0, 0, c)),
                  full((SSD_CONV, CONV_CH)), full((1, CONV_CH)),
                  full((LANES, 1)), full((LANES, 1)),
                  full((2 * LANES, SSD_INNER)), full((1, SSD_INNER)), full((1, SSD_INNER))],
        out_specs=pl.BlockSpec((batch, L, SSD_INNER), lambda c: (0, c, 0)),
        out_shape=jax.ShapeDtypeStruct((batch, seq, SSD_INNER), BF16),
        scratch_shapes=[pltpu.VMEM((batch, L + SUBLANES, CONV_CH), F32),
                        pltpu.VMEM((batch, L, CONV_CH), F32),
                        pltpu.VMEM((batch, SSD_GROUPS, SSD_STATE, SSD_INNER // SSD_GROUPS), F32),
                        pltpu.VMEM((batch, L, SSD_INNER), BF16),
                        pltpu.VMEM((batch, L, SSD_INNER), BF16),
                        pltpu.VMEM((batch, L, SSD_INNER), F32),
                        pltpu.VMEM((batch, 3 * L + SUBLANES, SSD_INNER), F32),
                        pltpu.VMEM((batch, L, LANES), F32),
                        pltpu.VMEM((batch, LANES, L), F32)],
        compiler_params=_params(1),
        name="ssd_scan",
    )(proj3, proj3, dt_rawt, cw, cb, bcol, alcol, emat, dfull, gn)
    return out.reshape(batch * seq, SSD_INNER)


def _out_proj_kernel(a1_ref, a2_ref, w1_ref, w2_ref, r_ref, g_ref, h_ref, hn_ref):
    acc = jnp.dot(a1_ref[...], w1_ref[...], preferred_element_type=F32)
    acc = acc + jnp.dot(a2_ref[...], w2_ref[...], preferred_element_type=F32)
    h = r_ref[...] + acc
    h_ref[...] = h
    hn_ref[...] = _rmsnorm_rows(h, g_ref[...]).astype(hn_ref.dtype)


def _out_proj(a1, a2, w_bf16, res, g, tm):
    t, k = a1.shape
    n = w_bf16.shape[1]
    row = lambda shape: pl.BlockSpec(shape, lambda m: (m, 0))
    wspec = lambda i: pl.BlockSpec((k, n), lambda m: (i, 0), pipeline_mode=pl.Buffered(1))
    return pl.pallas_call(
        _out_proj_kernel,
        grid=(t // tm,),
        in_specs=[row((tm, k)), row((tm, k)), wspec(0), wspec(1), row((tm, n)),
                  pl.BlockSpec((1, n), lambda m: (0, 0))],
        out_specs=[row((tm, n)), row((tm, n))],
        out_shape=[jax.ShapeDtypeStruct((t, n), F32), jax.ShapeDtypeStruct((t, n), BF16)],
        compiler_params=_params(1),
        name="out_proj_norm",
    )(a1, a2, w_bf16, w_bf16, res, g)


def _up_kernel(a_ref, wg_ref, wv_ref, cwg_ref, cwv_ref, cbg_ref, cbv_ref, wd_ref, o_ref, wdb_ref,
               wgb_ref, wvb_ref, carry_ref, *, tm, seq):
    m = pl.program_id(1)
    wdb_ref[...] = wd_ref[...].astype(BF16)

    @pl.when(m == 0)
    def _():
        _cast_rows(wg_ref, wgb_ref)
        _cast_rows(wv_ref, wvb_ref)

    a = a_ref[...]
    seq_start = (m * tm) % seq == 0
    outs = []
    for idx, (wb_ref, cw_ref, cb_ref) in enumerate(((wgb_ref, cwg_ref, cbg_ref),
                                                    (wvb_ref, cwv_ref, cbv_ref))):
        u = jnp.dot(a, wb_ref[...], preferred_element_type=F32)
        above = jnp.where(seq_start, 0.0, carry_ref[idx])
        blk = jnp.concatenate([above, u], axis=0)
        cw = cw_ref[...]
        conv = cb_ref[...]
        for k in range(FFN_CONV - 1):
            conv = conv + _delay_rows(blk, FFN_CONV - 1 - k) * cw[k:k + 1, :]
        conv = conv + u * cw[FFN_CONV - 1:FFN_CONV, :]
        carry_ref[idx] = u[tm - SUBLANES:tm]
        outs.append(conv)
    gate, val = outs
    o_ref[...] = ((gate * _sigmoid(gate)) * val).astype(o_ref.dtype)


def _up_proj(a, w, cw, cb, w_down, tm, tn, seq):
    t, k = a.shape
    assert seq % tm == 0 and D_FF % tn == 0
    nb = D_FF // tn
    n_m = t // tm
    kd, nd = w_down.shape
    slab = kd // (nb * n_m)
    assert slab * nb * n_m == kd and slab % 16 == 0
    return pl.pallas_call(
        functools.partial(_up_kernel, tm=tm, seq=seq),
        grid=(nb, t // tm),
        in_specs=[pl.BlockSpec((tm, k), lambda j, m: (m, 0)),
                  pl.BlockSpec((k, tn), lambda j, m: (0, j)),
                  pl.BlockSpec((k, tn), lambda j, m: (0, j + nb)),
                  pl.BlockSpec((FFN_CONV, tn), lambda j, m: (0, j)),
                  pl.BlockSpec((FFN_CONV, tn), lambda j, m: (0, j + nb)),
                  pl.BlockSpec((1, tn), lambda j, m: (0, j)),
                  pl.BlockSpec((1, tn), lambda j, m: (0, j + nb)),
                  pl.BlockSpec((slab, nd), lambda j, m: (j * n_m + m, 0))],
        out_specs=[pl.BlockSpec((tm, tn), lambda j, m: (m, j)),
                   pl.BlockSpec((slab, nd), lambda j, m: (j * n_m + m, 0))],
        out_shape=[jax.ShapeDtypeStruct((t, D_FF), BF16),
                   jax.ShapeDtypeStruct((kd, nd), BF16)],
        scratch_shapes=[pltpu.VMEM((k, tn), BF16), pltpu.VMEM((k, tn), BF16),
                        pltpu.VMEM((2, SUBLANES, tn), F32)],
        compiler_params=_params(2),
        name="up_proj_conv_swiglu",
    )(a, w, w, cw, cw, cb, cb, w_down)


def _down_kernel(a_ref, w_ref, r_ref, g_ref, o_ref, *, final_norm):
    h = r_ref[...] + jnp.dot(a_ref[...], w_ref[...], preferred_element_type=F32)
    o_ref[...] = _rmsnorm_rows(h, g_ref[...]) if final_norm else h


def _down_proj(a, w_bf16, res, g, tm, final_norm):
    t, k = a.shape
    n = w_bf16.shape[1]
    row = lambda shape: pl.BlockSpec(shape, lambda m: (m, 0))
    return pl.pallas_call(
        functools.partial(_down_kernel, final_norm=final_norm),
        grid=(t // tm,),
        in_specs=[row((tm, k)),
                  pl.BlockSpec((k, n), lambda m: (0, 0), pipeline_mode=pl.Buffered(1)),
                  row((tm, n)),
                  pl.BlockSpec((1, n), lambda m: (0, 0))],
        out_specs=row((tm, n)),
        out_shape=jax.ShapeDtypeStruct((t, n), F32),
        compiler_params=_params(1),
        name="down_proj_norm" if final_norm else "down_proj",
    )(a, w_bf16, res, g)


def _pad_lanes(v):
    return jnp.pad(v.astype(F32), (0, LANES - v.shape[0]))


def _mixer_layer(h, batch, seq, norm_mix, w_in, sinks, attn_out_norm, ssd_conv_w, ssd_conv_b, dt_bias,
                 a_log, ssd_d, ssd_norm, w_out, norm_ffn, tables, emat):
    w_in_t = jnp.swapaxes(w_in, 0, 1)
    wdt_t = jnp.pad(w_in_t[MAIN_PROJ:], ((0, LANES - SSD_HEADS), (0, 0))).astype(BF16)
    proj, dt_rawt = _in_proj(h, norm_mix.reshape(1, -1), wdt_t, w_in_t, 1024, 1024, batch, seq)
    attn, w_out_bf16 = _attention(proj, sinks.astype(F32), attn_out_norm.reshape(1, -1), tables, w_out,
                                  batch, seq, qb=4)
    bias = _pad_lanes(dt_bias)
    alog = _pad_lanes(a_log)
    y = _ssd(proj, dt_rawt, ssd_conv_w, ssd_conv_b.reshape(1, -1), bias.reshape(-1, 1), alog.reshape(-1, 1),
             emat, jnp.repeat(ssd_d.astype(F32), SSD_HEAD_DIM).reshape(1, -1),
             ssd_norm.reshape(1, -1), batch, seq)
    return _out_proj(attn, y, w_out_bf16, h, norm_ffn.reshape(1, -1), tm=512)


def kernel(x, norm_mix, w_in, sinks, attn_out_norm, ssd_conv_w, ssd_conv_b, dt_bias, a_log, ssd_d, ssd_norm,
           w_out, norm_ffn, w_up, ffn_conv_w, ffn_conv_b, w_down, norm_final):
    batch, seq, d = x.shape
    h = x.reshape(batch * seq, d)
    tables = _rope_tables(seq)
    head_of_channel = np.arange(SSD_INNER) // SSD_HEAD_DIM
    emat = np.arange(LANES)[:, None] == head_of_channel[None, :]
    emat = jnp.asarray(np.concatenate([emat, emat], axis=0), dtype=BF16)
    for l in range(norm_mix.shape[0]):
        h, hn = _mixer_layer(h, batch, seq, norm_mix[l], w_in[l], sinks[l], attn_out_norm[l], ssd_conv_w[l],
                             ssd_conv_b[l], dt_bias[l], a_log[l], ssd_d[l], ssd_norm[l], w_out[l], norm_ffn[l],
                             tables, emat)
        act, w_down_bf16 = _up_proj(hn, w_up[l], ffn_conv_w[l], ffn_conv_b[l].reshape(1, -1), w_down[l],
                                    tm=1024, tn=512, seq=seq)
        h = _down_proj(act, w_down_bf16, h, norm_final.reshape(1, -1), tm=512,
                       final_norm=(l == norm_mix.shape[0] - 1))
    return h.reshape(batch, seq, d)
```

```python
import functools

import numpy as np
import jax
import jax.numpy as jnp
from jax import lax
from jax.experimental import pallas as pl
from jax.experimental.pallas import tpu as pltpu

F32 = jnp.float32
BF16 = jnp.bfloat16

D_MODEL = 2048
N_Q_HEADS = 32
N_KV_HEADS = 8
HEAD_DIM = 64
Q_PER_KV = N_Q_HEADS // N_KV_HEADS
WINDOW = 128
ATTN_BLOCK = 128
ROT_DIM = HEAD_DIM // 4
ROPE_THETA = 500000.0
SSD_HEADS = 32
SSD_HEAD_DIM = 64
SSD_INNER = SSD_HEADS * SSD_HEAD_DIM
SSD_GROUPS = 8
SSD_STATE = 128
SSD_CONV = 4
SSD_CHUNK = 128
ATTN_WIDTH = N_Q_HEADS * HEAD_DIM
KV_WIDTH = N_KV_HEADS * HEAD_DIM
BC_WIDTH = SSD_GROUPS * SSD_STATE
CONV_CH = SSD_INNER + 2 * BC_WIDTH
MAIN_PROJ = ATTN_WIDTH + 2 * KV_WIDTH + SSD_INNER + CONV_CH
D_FF = 5632
FFN_CONV = 3
EPS = 1e-6

LANES = 128
SUBLANES = 8
HALF = LANES // 2
NEG = -1e30
LOG2E = 1.4426950408889634
VMEM_LIMIT = 56 * 1024 * 1024

Q_OFF = 0
Z_OFF = ATTN_WIDTH
XBC_OFF = Z_OFF + SSD_INNER
K_OFF = XBC_OFF + CONV_CH
V_OFF = K_OFF + KV_WIDTH


def _params(n_axes, flags=None):
    return pltpu.CompilerParams(dimension_semantics=("arbitrary",) * n_axes,
                                vmem_limit_bytes=VMEM_LIMIT, flags=flags)


def _sigmoid(x):
    return 1.0 / (1.0 + jnp.exp2(x * -LOG2E))


def _softplus(x):
    return jnp.maximum(x, 0.0) + jnp.log1p(jnp.exp(-jnp.abs(x)))


def _cast_rows(src_ref, dst_ref, rows=256):
    k = src_ref.shape[0]
    for r in range(0, k, rows):
        dst_ref[r:r + rows, :] = src_ref[r:r + rows, :].astype(BF16)


def _delay_rows(blk, sh):
    n, w = blk.shape[0] - SUBLANES, blk.shape[1]
    g = blk.reshape(n // SUBLANES + 1, SUBLANES, w)
    rot = pltpu.roll(g, sh, 1)
    row = lax.broadcasted_iota(jnp.int32, (SUBLANES, w), 0)
    out = jnp.where(row < sh, rot[:-1], rot[1:])
    return out.reshape(n, w)


def _rmsnorm_rows(x, g):
    ms = jnp.mean(x * x, axis=-1, keepdims=True)
    return x * lax.rsqrt(ms + EPS) * g


def _in_proj_head_kernel(x_ref, g_ref, wdtt_ref, wt_ref, xn_ref, dtt_ref, o_ref, wbf_ref):
    @pl.when(pl.program_id(0) == 0)
    def _():
        _cast_rows(wt_ref, wbf_ref)

    xn = _rmsnorm_rows(x_ref[...], g_ref[...]).astype(BF16)
    xn_ref[...] = xn
    dtt_ref[...] = lax.dot_general(wdtt_ref[...], xn, (((1,), (1,)), ((), ())),
                                   preferred_element_type=F32)
    o_ref[...] = lax.dot_general(xn, wbf_ref[...], (((1,), (1,)), ((), ())),
                                 preferred_element_type=F32).astype(o_ref.dtype)


def _in_proj_kernel(a_ref, wt_ref, prev_ref, o_ref, wbf_ref):
    del prev_ref
    @pl.when(pl.program_id(1) == 0)
    def _():
        _cast_rows(wt_ref, wbf_ref)

    o_ref[...] = lax.dot_general(a_ref[...], wbf_ref[...], (((1,), (1,)), ((), ())),
                                 preferred_element_type=F32).astype(o_ref.dtype)


def _in_proj(x2, g, wdtt, wt, tm, tn, batch, seq):
    t, k = x2.shape
    assert seq % tm == 0 and tn == 1024
    per_seq = seq // tm
    nj = MAIN_PROJ // tn

    def wblock(j):
        return jnp.where(j < 2, j, jnp.where(j < 8, j + 1, j - 6))

    xn, dtt, proj = pl.pallas_call(
        _in_proj_head_kernel,
        grid=(t // tm,),
        in_specs=[pl.BlockSpec((tm, k), lambda m: (m, 0)),
                  pl.BlockSpec((1, k), lambda m: (0, 0)),
                  pl.BlockSpec((LANES, k), lambda m: (0, 0)),
                  pl.BlockSpec((tn, k), lambda m: (0, 0))],
        out_specs=[pl.BlockSpec((tm, k), lambda m: (m, 0)),
                   pl.BlockSpec((None, LANES, tm), lambda m: (m // per_seq, 0, m % per_seq)),
                   pl.BlockSpec((tm, tn), lambda m: (m, 0))],
        out_shape=[jax.ShapeDtypeStruct((t, k), BF16),
                   jax.ShapeDtypeStruct((batch, LANES, seq), F32),
                   jax.ShapeDtypeStruct((t, MAIN_PROJ), BF16)],
        scratch_shapes=[pltpu.VMEM((tn, k), BF16)],
        compiler_params=_params(1),
        name="norm_in_proj_head",
    )(x2, g, wdtt, wt)
    proj = pl.pallas_call(
        _in_proj_kernel,
        grid=(nj - 1, t // tm),
        in_specs=[pl.BlockSpec((tm, k), lambda j, m: (m, 0)),
                  pl.BlockSpec((tn, k), lambda j, m: (wblock(j + 1), 0)),
                  pl.BlockSpec(memory_space=pl.ANY)],
        out_specs=pl.BlockSpec((tm, tn), lambda j, m: (m, j + 1)),
        out_shape=jax.ShapeDtypeStruct((t, MAIN_PROJ), BF16),
        scratch_shapes=[pltpu.VMEM((tn, k), BF16)],
        input_output_aliases={2: 0},
        compiler_params=_params(2),
        name="in_proj",
    )(xn, wt, proj)
    return proj, dtt


def _rope_tables(seq):
    half = ROT_DIM // 2
    inv = 1.0 / (ROPE_THETA ** (np.arange(0, ROT_DIM, 2, dtype=np.float64) / ROT_DIM))
    ang = np.arange(seq, dtype=np.float64)[:, None] * inv[None, :]
    cos, sin = np.cos(ang), np.sin(ang)
    d = np.arange(LANES) % HEAD_DIM
    idx = d % half
    c = np.where((d < ROT_DIM)[None, :], cos[:, idx], 1.0)
    s1 = np.where((d < half)[None, :], -sin[:, idx], 0.0)
    s2 = np.where(((d >= half) & (d < ROT_DIM))[None, :], sin[:, idx], 0.0)
    return jnp.asarray(np.stack([c, s1, s2]).astype(np.float32))


def _rope(x, c, s1, s2):
    half = ROT_DIM // 2
    return x * c + pltpu.roll(x, LANES - half, 1) * s1 + pltpu.roll(x, half, 1) * s2


def _attn_kernel(sinks_ref, q_ref, kp_ref, kc_ref, vp_ref, vc_ref,
                 tc_ref, tp_ref, g_ref, wo_ref, o_ref, wob_ref,
                 lhs_ref, kb_ref, vb_ref, s_ref, p_ref, o2_ref, m_ref, tq_ref, bias_ref, acc_ref, *, qb):
    blk = ATTN_BLOCK
    n = pl.program_id(1)
    lo = lax.broadcasted_iota(jnp.int32, (blk, LANES), 1) < HALF

    wob_ref[...] = wo_ref[...].astype(BF16)

    scale = HEAD_DIM ** -0.5 * LOG2E
    for i in range(3):
        tq_ref[i] = tc_ref[i] * scale
    qi = lax.broadcasted_iota(jnp.int32, (blk, 2 * blk), 0)
    kj = lax.broadcasted_iota(jnp.int32, (blk, 2 * blk), 1)
    rel = qi + blk - kj
    band = (rel >= 0) & (rel < WINDOW)
    kmin = jnp.where(n > 0, 0, blk)
    bias_ref[0] = jnp.where(band & (kj >= kmin), 0.0, NEG)
    bias_ref[1] = jnp.where(band, 0.0, NEG)

    for j in range(qb):
        rows = slice(j * blk, (j + 1) * blk)
        for col in range(ATTN_WIDTH // LANES):
            h, jj = col // 2, col % 2
            q2 = _rope(q_ref[rows, col * LANES:(col + 1) * LANES].astype(F32),
                       tq_ref[0, rows, :], tq_ref[1, rows, :], tq_ref[2, rows, :])
            lhs_ref[j, h, (2 * jj) * blk:(2 * jj + 1) * blk, :] = jnp.where(lo, q2, 0.0).astype(BF16)
            lhs_ref[j, h, (2 * jj + 1) * blk:(2 * jj + 2) * blk, :] = jnp.where(lo, 0.0, q2).astype(BF16)
    nk = (qb + 1) * blk
    lok = lax.broadcasted_iota(jnp.int32, (nk, LANES), 1) < HALF
    ones = jnp.ones((nk, LANES), BF16)
    for i in range(N_KV_HEADS // 2):
        cols = slice(i * LANES, (i + 1) * LANES)
        kcat = jnp.concatenate([_rope(kp_ref[:, cols].astype(F32), tp_ref[0], tp_ref[1], tp_ref[2]),
                                _rope(kc_ref[:, cols].astype(F32), tc_ref[0], tc_ref[1], tc_ref[2])], axis=0)
        vcat = jnp.concatenate([vp_ref[:, cols], vc_ref[:, cols]], axis=0).astype(F32)
        kswp = pltpu.roll(kcat, HALF, 1)
        vswp = pltpu.roll(vcat, HALF, 1)
        kb_ref[2 * i] = jnp.where(lok, kcat, kswp).astype(BF16)
        kb_ref[2 * i + 1] = jnp.where(lok, kswp, kcat).astype(BF16)
        vb_ref[2 * i, :, 0:LANES] = jnp.where(lok, vcat, vswp).astype(BF16)
        vb_ref[2 * i + 1, :, 0:LANES] = jnp.where(lok, vswp, vcat).astype(BF16)
        vb_ref[2 * i, :, LANES:2 * LANES] = ones
        vb_ref[2 * i + 1, :, LANES:2 * LANES] = ones

    def keys(j):
        return slice(j * blk, (j + 2) * blk)

    def scores(j, h):
        s_ref[j, h % 2] = lax.dot_general(lhs_ref[j, h], kb_ref[h, keys(j), :], (((1,), (1,)), ((), ())),
                                          preferred_element_type=F32)

    def row_max(j, h):
        slot = h % 2
        for r in range(Q_PER_KV):
            rows = slice(r * blk, (r + 1) * blk)
            sink = sinks_ref[Q_PER_KV * h + r] * LOG2E
            sr = s_ref[j, slot, rows, :] + bias_ref[min(j, 1)]
            s_ref[j, slot, rows, :] = sr
            m = jnp.maximum(jnp.max(sr, axis=-1, keepdims=True), sink)
            m_ref[j, slot, rows, :] = jnp.broadcast_to(m, (blk, LANES))

    def probs(j, h):
        slot = h % 2
        for r in range(Q_PER_KV):
            rows = slice(r * blk, (r + 1) * blk)
            m = m_ref[j, slot, rows, :]
            for half in range(2):
                ln = slice(half * LANES, (half + 1) * LANES)
                p_ref[j, slot, rows, ln] = jnp.exp2(s_ref[j, slot, rows, ln] - m).astype(BF16)

    def weighted(j, h):
        o2_ref[j, h % 2] = jnp.dot(p_ref[j, h % 2], vb_ref[h, keys(j), :],
                                   preferred_element_type=F32)

    def finish(j, h):
        slot = h % 2
        outs = []
        for r in range(Q_PER_KV):
            rows = slice(r * blk, (r + 1) * blk)
            sink = sinks_ref[Q_PER_KV * h + r] * LOG2E
            den = o2_ref[j, slot, rows, LANES:2 * LANES] + jnp.exp2(sink - m_ref[j, slot, rows, :])
            outs.append(o2_ref[j, slot, rows, 0:LANES] * (1.0 / den))
        for jj in range(2):
            col = 2 * h + jj
            acc_ref[j * blk:(j + 1) * blk, col * LANES:(col + 1) * LANES] = jnp.where(
                lo, outs[2 * jj], outs[2 * jj + 1])

    for j in range(qb):
        scores(j, 0)
    for h in range(N_KV_HEADS + 1):
        for stage in (scores, row_max, probs, weighted, finish):
            hh = {scores: h + 1, finish: h - 1}.get(stage, h)
            if 0 <= hh < N_KV_HEADS:
                for j in range(qb):
                    stage(j, hh)

    o_ref[...] = _rmsnorm_rows(acc_ref[...], g_ref[...]).astype(o_ref.dtype)


def _attention(proj, sinks, g, tables, w_out, batch, seq, qb):
    t = proj.shape[0]
    blk = ATTN_BLOCK
    assert seq % (qb * blk) == 0
    nb = seq // blk
    ns = nb // qb
    ko, no = w_out.shape
    slab = ko // (batch * ns)
    assert slab * batch * ns == ko and slab % 16 == 0
    kblk = K_OFF // KV_WIDTH
    vblk = V_OFF // KV_WIDTH

    def cur(b, n):
        return (b * ns + n, 0)

    def prev_rows(b, n):
        return jnp.maximum(b * nb + n * qb - 1, 0)

    return pl.pallas_call(
        functools.partial(_attn_kernel, qb=qb),
        grid=(batch, ns),
        in_specs=[pl.BlockSpec(memory_space=pltpu.SMEM),
                  pl.BlockSpec((qb * blk, ATTN_WIDTH), cur),
                  pl.BlockSpec((blk, KV_WIDTH), lambda b, n: (prev_rows(b, n), kblk)),
                  pl.BlockSpec((qb * blk, KV_WIDTH), lambda b, n: (b * ns + n, kblk)),
                  pl.BlockSpec((blk, KV_WIDTH), lambda b, n: (prev_rows(b, n), vblk)),
                  pl.BlockSpec((qb * blk, KV_WIDTH), lambda b, n: (b * ns + n, vblk)),
                  pl.BlockSpec((3, qb * blk, LANES), lambda b, n: (0, n, 0)),
                  pl.BlockSpec((3, blk, LANES), lambda b, n: (0, jnp.maximum(n * qb - 1, 0), 0)),
                  pl.BlockSpec((1, ATTN_WIDTH), lambda b, n: (0, 0)),
                  pl.BlockSpec((slab, no), cur)],
        out_specs=[pl.BlockSpec((qb * blk, ATTN_WIDTH), cur),
                   pl.BlockSpec((slab, no), cur)],
        out_shape=[jax.ShapeDtypeStruct((t, ATTN_WIDTH), BF16),
                   jax.ShapeDtypeStruct((ko, no), BF16)],
        scratch_shapes=[pltpu.VMEM((qb, N_KV_HEADS, Q_PER_KV * blk, LANES), BF16),
                        pltpu.VMEM((N_KV_HEADS, (qb + 1) * blk, LANES), BF16),
                        pltpu.VMEM((N_KV_HEADS, (qb + 1) * blk, 2 * LANES), BF16),
                        pltpu.VMEM((qb, 2, Q_PER_KV * blk, 2 * blk), F32),
                        pltpu.VMEM((qb, 2, Q_PER_KV * blk, 2 * blk), BF16),
                        pltpu.VMEM((qb, 2, Q_PER_KV * blk, 2 * LANES), F32),
                        pltpu.VMEM((qb, 2, Q_PER_KV * blk, LANES), F32),
                        pltpu.VMEM((3, qb * blk, LANES), F32),
                        pltpu.VMEM((2, blk, 2 * blk), F32),
                        pltpu.VMEM((qb * blk, ATTN_WIDTH), F32)],
        compiler_params=_params(2),
        name="swa_attention",
    )(sinks, proj, proj, proj, proj, proj, tables, tables, g, w_out)


def _split3(x):
    h = x.astype(BF16)
    r = x - h.astype(F32)
    m = r.astype(BF16)
    l = (r - m.astype(F32)).astype(BF16)
    return h, m, l


def _dot3_lhs(x, w):
    h, m, l = _split3(x)
    d = lambda a: jnp.dot(a, w, preferred_element_type=F32)
    return (d(l) + d(m)) + d(h)


def _ssd_kernel(xbc_ref, z_ref, dtt_ref, cw_ref, cb_ref, bcol_ref, alcol_ref, e_ref, dfull_ref, gn_ref, o_ref,
                ext_ref, xact_ref, state_ref, xdt_ref, xdec_ref, y_ref, exp_ref, acs_ref, acst_ref, split_ref,
                *, nbat):
    L = SSD_CHUNK
    c = pl.program_id(0)
    cw_chunk = 512
    gw = SSD_HEAD_DIM * (SSD_HEADS // SSD_GROUPS)
    ri = lax.broadcasted_iota(jnp.int32, (L, L), 0)
    ci = lax.broadcasted_iota(jnp.int32, (L, L), 1)
    causal = ri >= ci
    lo = ci < HALF

    @pl.when(c == 0)
    def _():
        ext_ref[:, 0:SUBLANES, :] = jnp.zeros((nbat, SUBLANES, CONV_CH), F32)
        state_ref[...] = jnp.zeros_like(state_ref)

    def load_x(b):
        ext_ref[b, SUBLANES:SUBLANES + L, :] = xbc_ref[b].astype(F32)

    def conv(b, j):
        cs = slice(j * cw_chunk, (j + 1) * cw_chunk)
        blk = ext_ref[b, :, cs]
        acc = cb_ref[:, cs]
        for k in range(SSD_CONV - 1):
            acc = acc + _delay_rows(blk, SSD_CONV - 1 - k) * cw_ref[k:k + 1, cs]
        acc = acc + blk[SUBLANES:] * cw_ref[SSD_CONV - 1:SSD_CONV, cs]
        xact_ref[b, :, cs] = acc * _sigmoid(acc)

    def keep_tail(b):
        ext_ref[b, 0:SUBLANES, :] = ext_ref[b, L:L + SUBLANES, :]

    def factors(b):
        nh = SSD_HEADS
        dtt = _softplus(dtt_ref[b, 0:nh, :] + bcol_ref[0:nh, :])
        dat = dtt * (-jnp.exp(alcol_ref[0:nh, :]))
        tri_u = jnp.where(ri <= ci, 1.0, 0.0).astype(BF16)
        acst = _dot3_lhs(dat, tri_u) * LOG2E
        acst_ref[b, 0:nh, :] = acst
        pad = jnp.zeros((LANES - nh, L), F32)
        a_cs = jnp.concatenate([acst, pad], axis=0).T
        dt = jnp.concatenate([dtt, pad], axis=0).T
        acs_ref[b] = a_cs
        a_last = a_cs[L - 1:L, :]
        stack = jnp.concatenate([dt, jnp.exp2(a_last - a_cs), jnp.exp2(a_cs),
                                 jnp.broadcast_to(jnp.exp2(a_last), (SUBLANES, LANES))], axis=0)
        hi = stack.astype(BF16)
        split_ref[b, :, 0:LANES] = hi
        split_ref[b, :, LANES:2 * LANES] = (stack - hi.astype(F32)).astype(BF16)

    def expand(b, j):
        cs = slice(j * cw_chunk, (j + 1) * cw_chunk)
        exp_ref[b, :, cs] = jnp.dot(split_ref[b], e_ref[:, cs], preferred_element_type=F32)

    def scale_x(b):
        for j in range(SSD_INNER // cw_chunk):
            cs = slice(j * cw_chunk, (j + 1) * cw_chunk)
            xdt = xact_ref[b, :, cs] * exp_ref[b, 0:L, cs]
            xdt_ref[b, :, cs] = xdt.astype(BF16)
            xdec_ref[b, :, cs] = (xdt * exp_ref[b, L:2 * L, cs]).astype(BF16)

    def group(b, g):
        bg = xact_ref[b, :, SSD_INNER + g * SSD_STATE:SSD_INNER + (g + 1) * SSD_STATE]
        cg = xact_ref[b, :, SSD_INNER + BC_WIDTH + g * SSD_STATE:SSD_INNER + BC_WIDTH + (g + 1) * SSD_STATE]
        bb = bg.astype(BF16)
        cbf = cg.astype(BF16)
        cbm = lax.dot_general(cbf, bb, (((1,), (1,)), ((), ())), preferred_element_type=F32)
        gs = slice(g * gw, (g + 1) * gw)
        prev = state_ref[b, g]
        yoff = jnp.dot(cbf, prev.astype(BF16), preferred_element_type=F32) * exp_ref[b, 2 * L:3 * L, gs]
        btb = bg.T.astype(BF16)
        state_ref[b, g] = prev * exp_ref[b, 3 * L:3 * L + 1, gs] + jnp.dot(
            btb, xdec_ref[b, :, gs], preferred_element_type=F32)
        for e in range(2):
            pc = slice((2 * g + e) * LANES, (2 * g + e + 1) * LANES)
            xpair = xdt_ref[b, :, pc]
            yd = []
            for r in range(2):
                h = 4 * g + 2 * e + r
                seg = acs_ref[b, :, h:h + 1] - acst_ref[b, h:h + 1, :]
                lm = jnp.exp2(jnp.where(causal, seg, NEG))
                yd.append(jnp.dot((cbm * lm).astype(BF16), xpair, preferred_element_type=F32))
            y_ref[b, :, pc] = (jnp.where(lo, yd[0], yd[1]) + yoff[:, e * LANES:(e + 1) * LANES]
                               + dfull_ref[:, pc] * xact_ref[b, :, pc])

    def gate_norm(b, g):
        gs = slice(g * gw, (g + 1) * gw)
        zz = z_ref[b, :, gs].astype(F32)
        yg = y_ref[b, :, gs] * (zz * _sigmoid(zz))
        ms = jnp.mean(yg * yg, axis=-1, keepdims=True)
        o_ref[b, :, gs] = (yg * lax.rsqrt(ms + EPS) * gn_ref[:, gs]).astype(o_ref.dtype)

    for b in range(nbat):
        load_x(b)
    pieces = [functools.partial(factors, b) for b in range(nbat)]
    pieces += [functools.partial(expand, b, j) for b in range(nbat) for j in range(SSD_INNER // cw_chunk)]
    chunks = [functools.partial(conv, b, j) for b in range(nbat) for j in range(CONV_CH // cw_chunk)]
    for idx in range(max(len(pieces), len(chunks))):
        if idx < len(pieces):
            pieces[idx]()
        if idx < len(chunks):
            chunks[idx]()
    for b in range(nbat):
        keep_tail(b)
        scale_x(b)
    for stage in (group, gate_norm):
        for g in range(SSD_GROUPS):
            for b in range(nbat):
                stage(b, g)


def _ssd(proj, dt_rawt, cw, cb, bcol, alcol, emat, dfull, gn, batch, seq):
    L = SSD_CHUNK
    nc = seq // L
    proj3 = proj.reshape(batch, seq, MAIN_PROJ)
    full = lambda shape: pl.BlockSpec(shape, lambda c: (0, 0))
    out = pl.pallas_call(
        functools.partial(_ssd_kernel, nbat=batch),
        grid=(nc,),
        in_specs=[pl.BlockSpec((batch, L, CONV_CH), lambda c: (0, c, XBC_OFF // CONV_CH)),
                  pl.BlockSpec((batch, L, SSD_INNER), lambda c: (0, c, Z_OFF // SSD_INNER)),
                  pl.BlockSpec((batch, LANES, L), lambda c: (0, 0, c)),
                  full((SSD_CONV, CONV_CH)), full((1, CONV_CH)),
                  full((LANES, 1)), full((LANES, 1)),
                  full((2 * LANES, SSD_INNER)), full((1, SSD_INNER)), full((1, SSD_INNER))],
        out_specs=pl.BlockSpec((batch, L, SSD_INNER), lambda c: (0, c, 0)),
        out_shape=jax.ShapeDtypeStruct((batch, seq, SSD_INNER), BF16),
        scratch_shapes=[pltpu.VMEM((batch, L + SUBLANES, CONV_CH), F32),
                        pltpu.VMEM((batch, L, CONV_CH), F32),
                        pltpu.VMEM((batch, SSD_GROUPS, SSD_STATE, SSD_INNER // SSD_GROUPS), F32),
                        pltpu.VMEM((batch, L, SSD_INNER), BF16),
                        pltpu.VMEM((batch, L, SSD_INNER), BF16),
                        pltpu.VMEM((batch, L, SSD_INNER), F32),
                        pltpu.VMEM((batch, 3 * L + SUBLANES, SSD_INNER), F32),
                        pltpu.VMEM((batch, L, LANES), F32),
                        pltpu.VMEM((batch, LANES, L), F32),
                        pltpu.VMEM((batch, 3 * L + SUBLANES, 2 * LANES), BF16)],
        compiler_params=_params(1),
        name="ssd_scan",
    )(proj3, proj3, dt_rawt, cw, cb, bcol, alcol, emat, dfull, gn)
    return out.reshape(batch * seq, SSD_INNER)


def _out_proj_kernel(a1_ref, a2_ref, w1_ref, w2_ref, r_ref, g_ref, h_ref, hn_ref):
    acc = jnp.dot(a1_ref[...], w1_ref[...], preferred_element_type=F32)
    acc = acc + jnp.dot(a2_ref[...], w2_ref[...], preferred_element_type=F32)
    h = r_ref[...] + acc
    h_ref[...] = h
    hn_ref[...] = _rmsnorm_rows(h, g_ref[...]).astype(hn_ref.dtype)


def _out_proj(a1, a2, w_bf16, res, g, tm):
    t, k = a1.shape
    n = w_bf16.shape[1]
    row = lambda shape: pl.BlockSpec(shape, lambda m: (m, 0))
    wspec = lambda i: pl.BlockSpec((k, n), lambda m: (i, 0), pipeline_mode=pl.Buffered(1))
    return pl.pallas_call(
        _out_proj_kernel,
        grid=(t // tm,),
        in_specs=[row((tm, k)), row((tm, k)), wspec(0), wspec(1), row((tm, n)),
                  pl.BlockSpec((1, n), lambda m: (0, 0))],
        out_specs=[row((tm, n)), row((tm, n))],
        out_shape=[jax.ShapeDtypeStruct((t, n), F32), jax.ShapeDtypeStruct((t, n), BF16)],
        compiler_params=_params(1),
        name="out_proj_norm",
    )(a1, a2, w_bf16, w_bf16, res, g)


def _up_kernel(a_ref, wg_ref, wv_ref, cwg_ref, cwv_ref, cbg_ref, cbv_ref, wd_ref, o_ref, wdb_ref,
               wgb_ref, wvb_ref, carry_ref, *, tm, seq):
    m = pl.program_id(1)
    wdb_ref[...] = wd_ref[...].astype(BF16)

    @pl.when(m == 0)
    def _():
        _cast_rows(wg_ref, wgb_ref)
        _cast_rows(wv_ref, wvb_ref)

    a = a_ref[...]
    seq_start = (m * tm) % seq == 0
    outs = []
    for idx, (wb_ref, cw_ref, cb_ref) in enumerate(((wgb_ref, cwg_ref, cbg_ref),
                                                    (wvb_ref, cwv_ref, cbv_ref))):
        u = jnp.dot(a, wb_ref[...], preferred_element_type=F32)
        above = jnp.where(seq_start, 0.0, carry_ref[idx])
        blk = jnp.concatenate([above, u], axis=0)
        cw = cw_ref[...]
        conv = cb_ref[...]
        for k in range(FFN_CONV - 1):
            conv = conv + _delay_rows(blk, FFN_CONV - 1 - k) * cw[k:k + 1, :]
        conv = conv + u * cw[FFN_CONV - 1:FFN_CONV, :]
        carry_ref[idx] = u[tm - SUBLANES:tm]
        outs.append(conv)
    gate, val = outs
    o_ref[...] = ((gate * _sigmoid(gate)) * val).astype(o_ref.dtype)


def _up_proj(a, w, cw, cb, w_down, tm, tn, seq):
    t, k = a.shape
    assert seq % tm == 0 and D_FF % tn == 0
    nb = D_FF // tn
    n_m = t // tm
    kd, nd = w_down.shape
    slab = kd // (nb * n_m)
    assert slab * nb * n_m == kd and slab % 16 == 0
    return pl.pallas_call(
        functools.partial(_up_kernel, tm=tm, seq=seq),
        grid=(nb, t // tm),
        in_specs=[pl.BlockSpec((tm, k), lambda j, m: (m, 0)),
                  pl.BlockSpec((k, tn), lambda j, m: (0, j)),
                  pl.BlockSpec((k, tn), lambda j, m: (0, j + nb)),
                  pl.BlockSpec((FFN_CONV, tn), lambda j, m: (0, j)),
                  pl.BlockSpec((FFN_CONV, tn), lambda j, m: (0, j + nb)),
                  pl.BlockSpec((1, tn), lambda j, m: (0, j)),
                  pl.BlockSpec((1, tn), lambda j, m: (0, j + nb)),
                  pl.BlockSpec((slab, nd), lambda j, m: (j * n_m + m, 0))],
        out_specs=[pl.BlockSpec((tm, tn), lambda j, m: (m, j)),
                   pl.BlockSpec((slab, nd), lambda j, m: (j * n_m + m, 0))],
        out_shape=[jax.ShapeDtypeStruct((t, D_FF), BF16),
                   jax.ShapeDtypeStruct((kd, nd), BF16)],
        scratch_shapes=[pltpu.VMEM((k, tn), BF16), pltpu.VMEM((k, tn), BF16),
                        pltpu.VMEM((2, SUBLANES, tn), F32)],
        compiler_params=_params(2),
        name="up_proj_conv_swiglu",
    )(a, w, w, cw, cw, cb, cb, w_down)


def _down_kernel(a_ref, w_ref, r_ref, g_ref, o_ref, *, final_norm):
    h = r_ref[...] + jnp.dot(a_ref[...], w_ref[...], preferred_element_type=F32)
    o_ref[...] = _rmsnorm_rows(h, g_ref[...]) if final_norm else h


def _down_proj(a, w_bf16, res, g, tm, final_norm):
    t, k = a.shape
    n = w_bf16.shape[1]
    row = lambda shape: pl.BlockSpec(shape, lambda m: (m, 0))
    return pl.pallas_call(
        functools.partial(_down_kernel, final_norm=final_norm),
        grid=(t // tm,),
        in_specs=[row((tm, k)),
                  pl.BlockSpec((k, n), lambda m: (0, 0), pipeline_mode=pl.Buffered(1)),
                  row((tm, n)),
                  pl.BlockSpec((1, n), lambda m: (0, 0))],
        out_specs=row((tm, n)),
        out_shape=jax.ShapeDtypeStruct((t, n), F32),
        compiler_params=_params(1),
        name="down_proj_norm" if final_norm else "down_proj",
    )(a, w_bf16, res, g)


def _pad_lanes(v):
    return jnp.pad(v.astype(F32), (0, LANES - v.shape[0]))


def _mixer_layer(h, batch, seq, norm_mix, w_in, sinks, attn_out_norm, ssd_conv_w, ssd_conv_b, dt_bias,
                 a_log, ssd_d, ssd_norm, w_out, norm_ffn, tables, emat):
    w_in_t = jnp.swapaxes(w_in, 0, 1)
    wdt_t = jnp.pad(w_in_t[MAIN_PROJ:], ((0, LANES - SSD_HEADS), (0, 0))).astype(BF16)
    proj, dt_rawt = _in_proj(h, norm_mix.reshape(1, -1), wdt_t, w_in_t, 1024, 1024, batch, seq)
    attn, w_out_bf16 = _attention(proj, sinks.astype(F32), attn_out_norm.reshape(1, -1), tables, w_out,
                                  batch, seq, qb=4)
    bias = _pad_lanes(dt_bias)
    alog = _pad_lanes(a_log)
    y = _ssd(proj, dt_rawt, ssd_conv_w, ssd_conv_b.reshape(1, -1), bias.reshape(-1, 1), alog.reshape(-1, 1),
             emat, jnp.repeat(ssd_d.astype(F32), SSD_HEAD_DIM).reshape(1, -1),
             ssd_norm.reshape(1, -1), batch, seq)
    return _out_proj(attn, y, w_out_bf16, h, norm_ffn.reshape(1, -1), tm=512)


def kernel(x, norm_mix, w_in, sinks, attn_out_norm, ssd_conv_w, ssd_conv_b, dt_bias, a_log, ssd_d, ssd_norm,
           w_out, norm_ffn, w_up, ffn_conv_w, ffn_conv_b, w_down, norm_final):
    batch, seq, d = x.shape
    h = x.reshape(batch * seq, d)
    tables = _rope_tables(seq)
    head_of_channel = np.arange(SSD_INNER) // SSD_HEAD_DIM
    emat = np.arange(LANES)[:, None] == head_of_channel[None, :]
    emat = jnp.asarray(np.concatenate([emat, emat], axis=0), dtype=BF16)
    for l in range(norm_mix.shape[0]):
        h, hn = _mixer_layer(h, batch, seq, norm_mix[l], w_in[l], sinks[l], attn_out_norm[l], ssd_conv_w[l],
                             ssd_conv_b[l], dt_bias[l], a_log[l], ssd_d[l], ssd_norm[l], w_out[l], norm_ffn[l],
                             tables, emat)
        act, w_down_bf16 = _up_proj(hn, w_up[l], ffn_conv_w[l], ffn_conv_b[l].reshape(1, -1), w_down[l],
                                    tm=1024, tn=512, seq=seq)
        h = _down_proj(act, w_down_bf16, h, norm_final.reshape(1, -1), tm=512,
                       final_norm=(l == norm_mix.shape[0] - 1))
    return h.reshape(batch, seq, d)
```

```python
import functools

import numpy as np
import jax
import jax.numpy as jnp
from jax import lax
from jax.experimental import pallas as pl
from jax.experimental.pallas import tpu as pltpu

F32 = jnp.float32
BF16 = jnp.bfloat16

D_MODEL = 2048
N_Q_HEADS = 32
N_KV_HEADS = 8
HEAD_DIM = 64
Q_PER_KV = N_Q_HEADS // N_KV_HEADS
WINDOW = 128
ATTN_BLOCK = 128
ROT_DIM = HEAD_DIM // 4
ROPE_THETA = 500000.0
SSD_HEADS = 32
SSD_HEAD_DIM = 64
SSD_INNER = SSD_HEADS * SSD_HEAD_DIM
SSD_GROUPS = 8
SSD_STATE = 128
SSD_CONV = 4
SSD_CHUNK = 128
ATTN_WIDTH = N_Q_HEADS * HEAD_DIM
KV_WIDTH = N_KV_HEADS * HEAD_DIM
BC_WIDTH = SSD_GROUPS * SSD_STATE
CONV_CH = SSD_INNER + 2 * BC_WIDTH
MAIN_PROJ = ATTN_WIDTH + 2 * KV_WIDTH + SSD_INNER + CONV_CH
D_FF = 5632
FFN_CONV = 3
EPS = 1e-6

LANES = 128
SUBLANES = 8
HALF = LANES // 2
NEG = -1e30
LOG2E = 1.4426950408889634
VMEM_LIMIT = 56 * 1024 * 1024

Q_HEAD = 1024
REST_PROJ = MAIN_PROJ - Q_HEAD
XBC_OFF = 0
Z_OFF = XBC_OFF + CONV_CH
Q_REST_OFF = Z_OFF + SSD_INNER
K_OFF = Q_REST_OFF + (ATTN_WIDTH - Q_HEAD)
V_OFF = K_OFF + KV_WIDTH


def _params(n_axes, flags=None):
    return pltpu.CompilerParams(dimension_semantics=("arbitrary",) * n_axes,
                                vmem_limit_bytes=VMEM_LIMIT, flags=flags)


def _sigmoid(x):
    return 1.0 / (1.0 + jnp.exp2(x * -LOG2E))


def _softplus(x):
    return jnp.maximum(x, 0.0) + jnp.log1p(jnp.exp(-jnp.abs(x)))


def _cast_rows(src_ref, dst_ref, rows=256):
    k = src_ref.shape[0]
    for r in range(0, k, rows):
        dst_ref[r:r + rows, :] = src_ref[r:r + rows, :].astype(BF16)


def _delay_rows(blk, sh):
    n, w = blk.shape[0] - SUBLANES, blk.shape[1]
    g = blk.reshape(n // SUBLANES + 1, SUBLANES, w)
    rot = pltpu.roll(g, sh, 1)
    row = lax.broadcasted_iota(jnp.int32, (SUBLANES, w), 0)
    out = jnp.where(row < sh, rot[:-1], rot[1:])
    return out.reshape(n, w)


def _rmsnorm_rows(x, g):
    ms = jnp.mean(x * x, axis=-1, keepdims=True)
    return x * lax.rsqrt(ms + EPS) * g


def _in_proj_head_kernel(x_ref, g_ref, wdtt_ref, wt_ref, xn_ref, dtt_ref, o_ref, wbf_ref):
    @pl.when(pl.program_id(0) == 0)
    def _():
        _cast_rows(wt_ref, wbf_ref)

    xn = _rmsnorm_rows(x_ref[...], g_ref[...]).astype(BF16)
    xn_ref[...] = xn
    dtt_ref[...] = lax.dot_general(wdtt_ref[...], xn, (((1,), (1,)), ((), ())),
                                   preferred_element_type=F32)
    o_ref[...] = lax.dot_general(xn, wbf_ref[...], (((1,), (1,)), ((), ())),
                                 preferred_element_type=F32).astype(o_ref.dtype)


def _in_proj_kernel(a_ref, wt_ref, o_ref, wbf_ref):
    @pl.when(pl.program_id(1) == 0)
    def _():
        _cast_rows(wt_ref, wbf_ref)

    o_ref[...] = lax.dot_general(a_ref[...], wbf_ref[...], (((1,), (1,)), ((), ())),
                                 preferred_element_type=F32).astype(o_ref.dtype)


def _in_proj(x2, g, wdtt, wt, tm, batch, seq):
    t, k = x2.shape
    tn = Q_HEAD
    assert seq % tm == 0
    per_seq = seq // tm

    xn, dtt, q_head = pl.pallas_call(
        _in_proj_head_kernel,
        grid=(t // tm,),
        in_specs=[pl.BlockSpec((tm, k), lambda m: (m, 0)),
                  pl.BlockSpec((1, k), lambda m: (0, 0)),
                  pl.BlockSpec((LANES, k), lambda m: (0, 0)),
                  pl.BlockSpec((tn, k), lambda m: (0, 0))],
        out_specs=[pl.BlockSpec((tm, k), lambda m: (m, 0)),
                   pl.BlockSpec((None, LANES, tm), lambda m: (m // per_seq, 0, m % per_seq)),
                   pl.BlockSpec((tm, tn), lambda m: (m, 0))],
        out_shape=[jax.ShapeDtypeStruct((t, k), BF16),
                   jax.ShapeDtypeStruct((batch, LANES, seq), F32),
                   jax.ShapeDtypeStruct((t, tn), BF16)],
        scratch_shapes=[pltpu.VMEM((tn, k), BF16)],
        compiler_params=_params(1),
        name="norm_in_proj_head",
    )(x2, g, wdtt, wt)

    def wblock(j):
        return jnp.where(j < 4, j + 5, jnp.where(j < 6, j - 1, j - 5))

    rest = pl.pallas_call(
        _in_proj_kernel,
        grid=(REST_PROJ // tn, t // tm),
        in_specs=[pl.BlockSpec((tm, k), lambda j, m: (m, 0)),
                  pl.BlockSpec((tn, k), lambda j, m: (wblock(j), 0))],
        out_specs=pl.BlockSpec((tm, tn), lambda j, m: (m, j)),
        out_shape=jax.ShapeDtypeStruct((t, REST_PROJ), BF16),
        scratch_shapes=[pltpu.VMEM((tn, k), BF16)],
        compiler_params=_params(2),
        name="in_proj",
    )(xn, wt)
    return q_head, rest, dtt


def _rope_tables(seq):
    half = ROT_DIM // 2
    inv = 1.0 / (ROPE_THETA ** (np.arange(0, ROT_DIM, 2, dtype=np.float64) / ROT_DIM))
    ang = np.arange(seq, dtype=np.float64)[:, None] * inv[None, :]
    cos, sin = np.cos(ang), np.sin(ang)
    d = np.arange(LANES) % HEAD_DIM
    idx = d % half
    c = np.where((d < ROT_DIM)[None, :], cos[:, idx], 1.0)
    s1 = np.where((d < half)[None, :], -sin[:, idx], 0.0)
    s2 = np.where(((d >= half) & (d < ROT_DIM))[None, :], sin[:, idx], 0.0)
    return jnp.asarray(np.stack([c, s1, s2]).astype(np.float32))


def _rope(x, c, s1, s2):
    half = ROT_DIM // 2
    return x * c + pltpu.roll(x, LANES - half, 1) * s1 + pltpu.roll(x, half, 1) * s2


def _attn_kernel(sinks_ref, qa_ref, qb_ref, kp_ref, kc_ref, vp_ref, vc_ref,
                 tc_ref, tp_ref, g_ref, wo_ref, o_ref, wob_ref,
                 lhs_ref, kb_ref, vb_ref, s_ref, p_ref, o2_ref, m_ref, tq_ref, bias_ref, acc_ref, *, qb):
    blk = ATTN_BLOCK
    n = pl.program_id(1)
    lo = lax.broadcasted_iota(jnp.int32, (blk, LANES), 1) < HALF

    wob_ref[...] = wo_ref[...].astype(BF16)

    scale = HEAD_DIM ** -0.5 * LOG2E
    for i in range(3):
        tq_ref[i] = tc_ref[i] * scale
    qi = lax.broadcasted_iota(jnp.int32, (blk, 2 * blk), 0)
    kj = lax.broadcasted_iota(jnp.int32, (blk, 2 * blk), 1)
    rel = qi + blk - kj
    band = (rel >= 0) & (rel < WINDOW)
    kmin = jnp.where(n > 0, 0, blk)
    bias_ref[0] = jnp.where(band & (kj >= kmin), 0.0, NEG)
    bias_ref[1] = jnp.where(band, 0.0, NEG)

    for j in range(qb):
        rows = slice(j * blk, (j + 1) * blk)
        for col in range(ATTN_WIDTH // LANES):
            h, jj = col // 2, col % 2
            q_ref, qc = (qa_ref, col) if col < Q_HEAD // LANES else (qb_ref, col - Q_HEAD // LANES)
            q2 = _rope(q_ref[rows, qc * LANES:(qc + 1) * LANES].astype(F32),
                       tq_ref[0, rows, :], tq_ref[1, rows, :], tq_ref[2, rows, :])
            lhs_ref[j, h, (2 * jj) * blk:(2 * jj + 1) * blk, :] = jnp.where(lo, q2, 0.0).astype(BF16)
            lhs_ref[j, h, (2 * jj + 1) * blk:(2 * jj + 2) * blk, :] = jnp.where(lo, 0.0, q2).astype(BF16)
    nk = (qb + 1) * blk
    lok = lax.broadcasted_iota(jnp.int32, (nk, LANES), 1) < HALF
    ones = jnp.ones((nk, LANES), BF16)
    for i in range(N_KV_HEADS // 2):
        cols = slice(i * LANES, (i + 1) * LANES)
        kcat = jnp.concatenate([_rope(kp_ref[:, cols].astype(F32), tp_ref[0], tp_ref[1], tp_ref[2]),
                                _rope(kc_ref[:, cols].astype(F32), tc_ref[0], tc_ref[1], tc_ref[2])], axis=0)
        vcat = jnp.concatenate([vp_ref[:, cols], vc_ref[:, cols]], axis=0).astype(F32)
        kswp = pltpu.roll(kcat, HALF, 1)
        vswp = pltpu.roll(vcat, HALF, 1)
        kb_ref[2 * i] = jnp.where(lok, kcat, kswp).astype(BF16)
        kb_ref[2 * i + 1] = jnp.where(lok, kswp, kcat).astype(BF16)
        vb_ref[2 * i, :, 0:LANES] = jnp.where(lok, vcat, vswp).astype(BF16)
        vb_ref[2 * i + 1, :, 0:LANES] = jnp.where(lok, vswp, vcat).astype(BF16)
        vb_ref[2 * i, :, LANES:2 * LANES] = ones
        vb_ref[2 * i + 1, :, LANES:2 * LANES] = ones

    def keys(j):
        return slice(j * blk, (j + 2) * blk)

    def scores(j, h):
        s_ref[j, h % 2] = lax.dot_general(lhs_ref[j, h], kb_ref[h, keys(j), :], (((1,), (1,)), ((), ())),
                                          preferred_element_type=F32)

    def row_max(j, h):
        slot = h % 2
        for r in range(Q_PER_KV):
            rows = slice(r * blk, (r + 1) * blk)
            sink = sinks_ref[Q_PER_KV * h + r] * LOG2E
            sr = s_ref[j, slot, rows, :] + bias_ref[min(j, 1)]
            s_ref[j, slot, rows, :] = sr
            m = jnp.maximum(jnp.max(sr, axis=-1, keepdims=True), sink)
            m_ref[j, slot, rows, :] = jnp.broadcast_to(m, (blk, LANES))

    def probs(j, h):
        slot = h % 2
        for r in range(Q_PER_KV):
            rows = slice(r * blk, (r + 1) * blk)
            m = m_ref[j, slot, rows, :]
            for half in range(2):
                ln = slice(half * LANES, (half + 1) * LANES)
                p_ref[j, slot, rows, ln] = jnp.exp2(s_ref[j, slot, rows, ln] - m).astype(BF16)

    def weighted(j, h):
        o2_ref[j, h % 2] = jnp.dot(p_ref[j, h % 2], vb_ref[h, keys(j), :],
                                   preferred_element_type=F32)

    def finish(j, h):
        slot = h % 2
        outs = []
        for r in range(Q_PER_KV):
            rows = slice(r * blk, (r + 1) * blk)
            sink = sinks_ref[Q_PER_KV * h + r] * LOG2E
            den = o2_ref[j, slot, rows, LANES:2 * LANES] + jnp.exp2(sink - m_ref[j, slot, rows, :])
            outs.append(o2_ref[j, slot, rows, 0:LANES] * (1.0 / den))
        for jj in range(2):
            col = 2 * h + jj
            acc_ref[j * blk:(j + 1) * blk, col * LANES:(col + 1) * LANES] = jnp.where(
                lo, outs[2 * jj], outs[2 * jj + 1])

    for j in range(qb):
        scores(j, 0)
    for h in range(N_KV_HEADS + 1):
        for stage in (scores, row_max, probs, weighted, finish):
            hh = {scores: h + 1, finish: h - 1}.get(stage, h)
            if 0 <= hh < N_KV_HEADS:
                for j in range(qb):
                    stage(j, hh)

    o_ref[...] = _rmsnorm_rows(acc_ref[...], g_ref[...]).astype(o_ref.dtype)


def _attention(q_head, proj, sinks, g, tables, w_out, batch, seq, qb):
    t = proj.shape[0]
    blk = ATTN_BLOCK
    assert seq % (qb * blk) == 0
    nb = seq // blk
    ns = nb // qb
    ko, no = w_out.shape
    slab = ko // (batch * ns)
    assert slab * batch * ns == ko and slab % 16 == 0
    kblk = K_OFF // KV_WIDTH
    vblk = V_OFF // KV_WIDTH

    def cur(b, n):
        return (b * ns + n, 0)

    def prev_rows(b, n):
        return jnp.maximum(b * nb + n * qb - 1, 0)

    return pl.pallas_call(
        functools.partial(_attn_kernel, qb=qb),
        grid=(batch, ns),
        in_specs=[pl.BlockSpec(memory_space=pltpu.SMEM),
                  pl.BlockSpec((qb * blk, Q_HEAD), cur),
                  pl.BlockSpec((qb * blk, ATTN_WIDTH - Q_HEAD),
                               lambda b, n: (b * ns + n, Q_REST_OFF // (ATTN_WIDTH - Q_HEAD))),
                  pl.BlockSpec((blk, KV_WIDTH), lambda b, n: (prev_rows(b, n), kblk)),
                  pl.BlockSpec((qb * blk, KV_WIDTH), lambda b, n: (b * ns + n, kblk)),
                  pl.BlockSpec((blk, KV_WIDTH), lambda b, n: (prev_rows(b, n), vblk)),
                  pl.BlockSpec((qb * blk, KV_WIDTH), lambda b, n: (b * ns + n, vblk)),
                  pl.BlockSpec((3, qb * blk, LANES), lambda b, n: (0, n, 0)),
                  pl.BlockSpec((3, blk, LANES), lambda b, n: (0, jnp.maximum(n * qb - 1, 0), 0)),
                  pl.BlockSpec((1, ATTN_WIDTH), lambda b, n: (0, 0)),
                  pl.BlockSpec((slab, no), cur)],
        out_specs=[pl.BlockSpec((qb * blk, ATTN_WIDTH), cur),
                   pl.BlockSpec((slab, no), cur)],
        out_shape=[jax.ShapeDtypeStruct((t, ATTN_WIDTH), BF16),
                   jax.ShapeDtypeStruct((ko, no), BF16)],
        scratch_shapes=[pltpu.VMEM((qb, N_KV_HEADS, Q_PER_KV * blk, LANES), BF16),
                        pltpu.VMEM((N_KV_HEADS, (qb + 1) * blk, LANES), BF16),
                        pltpu.VMEM((N_KV_HEADS, (qb + 1) * blk, 2 * LANES), BF16),
                        pltpu.VMEM((qb, 2, Q_PER_KV * blk, 2 * blk), F32),
                        pltpu.VMEM((qb, 2, Q_PER_KV * blk, 2 * blk), BF16),
                        pltpu.VMEM((qb, 2, Q_PER_KV * blk, 2 * LANES), F32),
                        pltpu.VMEM((qb, 2, Q_PER_KV * blk, LANES), F32),
                        pltpu.VMEM((3, qb * blk, LANES), F32),
                        pltpu.VMEM((2, blk, 2 * blk), F32),
                        pltpu.VMEM((qb * blk, ATTN_WIDTH), F32)],
        compiler_params=_params(2),
        name="swa_attention",
    )(sinks, q_head, proj, proj, proj, proj, proj, tables, tables, g, w_out)


def _split3(x):
    h = x.astype(BF16)
    r = x - h.astype(F32)
    m = r.astype(BF16)
    l = (r - m.astype(F32)).astype(BF16)
    return h, m, l


def _dot3_lhs(x, w):
    h, m, l = _split3(x)
    d = lambda a: jnp.dot(a, w, preferred_element_type=F32)
    return (d(l) + d(m)) + d(h)


def _ssd_kernel(xbc_ref, z_ref, dtt_ref, cw_ref, cb_ref, bcol_ref, alcol_ref, e_ref, dfull_ref, gn_ref, o_ref,
                ext_ref, xact_ref, state_ref, xdt_ref, xdec_ref, y_ref, exp_ref, acs_ref, acst_ref, split_ref,
                *, nbat):
    L = SSD_CHUNK
    c = pl.program_id(0)
    cw_chunk = 512
    gw = SSD_HEAD_DIM * (SSD_HEADS // SSD_GROUPS)
    ri = lax.broadcasted_iota(jnp.int32, (L, L), 0)
    ci = lax.broadcasted_iota(jnp.int32, (L, L), 1)
    causal = ri >= ci
    lo = ci < HALF

    @pl.when(c == 0)
    def _():
        ext_ref[:, 0:SUBLANES, :] = jnp.zeros((nbat, SUBLANES, CONV_CH), F32)
        state_ref[...] = jnp.zeros_like(state_ref)

    def load_x(b):
        ext_ref[b, SUBLANES:SUBLANES + L, :] = xbc_ref[b].astype(F32)

    def conv(b, j):
        cs = slice(j * cw_chunk, (j + 1) * cw_chunk)
        blk = ext_ref[b, :, cs]
        acc = cb_ref[:, cs]
        for k in range(SSD_CONV - 1):
            acc = acc + _delay_rows(blk, SSD_CONV - 1 - k) * cw_ref[k:k + 1, cs]
        acc = acc + blk[SUBLANES:] * cw_ref[SSD_CONV - 1:SSD_CONV, cs]
        xact_ref[b, :, cs] = acc * _sigmoid(acc)

    def keep_tail(b):
        ext_ref[b, 0:SUBLANES, :] = ext_ref[b, L:L + SUBLANES, :]

    def factors(b):
        nh = SSD_HEADS
        dtt = _softplus(dtt_ref[b, 0:nh, :] + bcol_ref[0:nh, :])
        dat = dtt * (-jnp.exp(alcol_ref[0:nh, :]))
        tri_u = jnp.where(ri <= ci, 1.0, 0.0).astype(BF16)
        acst = _dot3_lhs(dat, tri_u) * LOG2E
        acst_ref[b, 0:nh, :] = acst
        pad = jnp.zeros((LANES - nh, L), F32)
        a_cs = jnp.concatenate([acst, pad], axis=0).T
        dt = jnp.concatenate([dtt, pad], axis=0).T
        acs_ref[b] = a_cs
        a_last = a_cs[L - 1:L, :]
        stack = jnp.concatenate([dt, jnp.exp2(a_last - a_cs), jnp.exp2(a_cs),
                                 jnp.broadcast_to(jnp.exp2(a_last), (SUBLANES, LANES))], axis=0)
        hi = stack.astype(BF16)
        split_ref[b, :, 0:LANES] = hi
        split_ref[b, :, LANES:2 * LANES] = (stack - hi.astype(F32)).astype(BF16)

    def expand(b, j):
        cs = slice(j * cw_chunk, (j + 1) * cw_chunk)
        exp_ref[b, :, cs] = jnp.dot(split_ref[b], e_ref[:, cs], preferred_element_type=F32)

    def scale_x(b):
        for j in range(SSD_INNER // cw_chunk):
            cs = slice(j * cw_chunk, (j + 1) * cw_chunk)
            xdt = xact_ref[b, :, cs] * exp_ref[b, 0:L, cs]
            xdt_ref[b, :, cs] = xdt.astype(BF16)
            xdec_ref[b, :, cs] = (xdt * exp_ref[b, L:2 * L, cs]).astype(BF16)

    def group(b, g):
        bg = xact_ref[b, :, SSD_INNER + g * SSD_STATE:SSD_INNER + (g + 1) * SSD_STATE]
        cg = xact_ref[b, :, SSD_INNER + BC_WIDTH + g * SSD_STATE:SSD_INNER + BC_WIDTH + (g + 1) * SSD_STATE]
        bb = bg.astype(BF16)
        cbf = cg.astype(BF16)
        cbm = lax.dot_general(cbf, bb, (((1,), (1,)), ((), ())), preferred_element_type=F32)
        gs = slice(g * gw, (g + 1) * gw)
        prev = state_ref[b, g]
        yoff = jnp.dot(cbf, prev.astype(BF16), preferred_element_type=F32) * exp_ref[b, 2 * L:3 * L, gs]
        btb = bg.T.astype(BF16)
        state_ref[b, g] = prev * exp_ref[b, 3 * L:3 * L + 1, gs] + jnp.dot(
            btb, xdec_ref[b, :, gs], preferred_element_type=F32)
        for e in range(2):
            pc = slice((2 * g + e) * LANES, (2 * g + e + 1) * LANES)
            xpair = xdt_ref[b, :, pc]
            yd = []
            for r in range(2):
                h = 4 * g + 2 * e + r
                seg = acs_ref[b, :, h:h + 1] - acst_ref[b, h:h + 1, :]
                lm = jnp.exp2(jnp.where(causal, seg, NEG))
                yd.append(jnp.dot((cbm * lm).astype(BF16), xpair, preferred_element_type=F32))
            y_ref[b, :, pc] = (jnp.where(lo, yd[0], yd[1]) + yoff[:, e * LANES:(e + 1) * LANES]
                               + dfull_ref[:, pc] * xact_ref[b, :, pc])

    def gate_norm(b, g):
        gs = slice(g * gw, (g + 1) * gw)
        zz = z_ref[b, :, gs].astype(F32)
        yg = y_ref[b, :, gs] * (zz * _sigmoid(zz))
        ms = jnp.mean(yg * yg, axis=-1, keepdims=True)
        o_ref[b, :, gs] = (yg * lax.rsqrt(ms + EPS) * gn_ref[:, gs]).astype(o_ref.dtype)

    for b in range(nbat):
        load_x(b)
    pieces = [functools.partial(factors, b) for b in range(nbat)]
    pieces += [functools.partial(expand, b, j) for b in range(nbat) for j in range(SSD_INNER // cw_chunk)]
    chunks = [functools.partial(conv, b, j) for b in range(nbat) for j in range(CONV_CH // cw_chunk)]
    for idx in range(max(len(pieces), len(chunks))):
        if idx < len(pieces):
            pieces[idx]()
        if idx < len(chunks):
            chunks[idx]()
    for b in range(nbat):
        keep_tail(b)
        scale_x(b)
    for stage in (group, gate_norm):
        for g in range(SSD_GROUPS):
            for b in range(nbat):
                stage(b, g)


def _ssd(proj, dt_rawt, cw, cb, bcol, alcol, emat, dfull, gn, batch, seq):
    L = SSD_CHUNK
    nc = seq // L
    proj3 = proj.reshape(batch, seq, REST_PROJ)
    full = lambda shape: pl.BlockSpec(shape, lambda c: (0, 0))
    out = pl.pallas_call(
        functools.partial(_ssd_kernel, nbat=batch),
        grid=(nc,),
        in_specs=[pl.BlockSpec((batch, L, CONV_CH), lambda c: (0, c, XBC_OFF // CONV_CH)),
                  pl.BlockSpec((batch, L, SSD_INNER), lambda c: (0, c, Z_OFF // SSD_INNER)),
                  pl.BlockSpec((batch, LANES, L), lambda c: (0, 0, c)),
                  full((SSD_CONV, CONV_CH)), full((1, CONV_CH)),
                  full((LANES, 1)), full((LANES, 1)),
                  full((2 * LANES, SSD_INNER)), full((1, SSD_INNER)), full((1, SSD_INNER))],
        out_specs=pl.BlockSpec((batch, L, SSD_INNER), lambda c: (0, c, 0)),
        out_shape=jax.ShapeDtypeStruct((batch, seq, SSD_INNER), BF16),
        scratch_shapes=[pltpu.VMEM((batch, L + SUBLANES, CONV_CH), F32),
                        pltpu.VMEM((batch, L, CONV_CH), F32),
                        pltpu.VMEM((batch, SSD_GROUPS, SSD_STATE, SSD_INNER // SSD_GROUPS), F32),
                        pltpu.VMEM((batch, L, SSD_INNER), BF16),
                        pltpu.VMEM((batch, L, SSD_INNER), BF16),
                        pltpu.VMEM((batch, L, SSD_INNER), F32),
                        pltpu.VMEM((batch, 3 * L + SUBLANES, SSD_INNER), F32),
                        pltpu.VMEM((batch, L, LANES), F32),
                        pltpu.VMEM((batch, LANES, L), F32),
                        pltpu.VMEM((batch, 3 * L + SUBLANES, 2 * LANES), BF16)],
        compiler_params=_params(1),
        name="ssd_scan",
    )(proj3, proj3, dt_rawt, cw, cb, bcol, alcol, emat, dfull, gn)
    return out.reshape(batch * seq, SSD_INNER)


def _out_proj_kernel(a1_ref, a2_ref, w1_ref, w2_ref, r_ref, g_ref, h_ref, hn_ref):
    acc = jnp.dot(a1_ref[...], w1_ref[...], preferred_element_type=F32)
    acc = acc + jnp.dot(a2_ref[...], w2_ref[...], preferred_element_type=F32)
    h = r_ref[...] + acc
    h_ref[...] = h
    hn_ref[...] = _rmsnorm_rows(h, g_ref[...]).astype(hn_ref.dtype)


def _out_proj(a1, a2, w_bf16, res, g, tm):
    t, k = a1.shape
    n = w_bf16.shape[1]
    row = lambda shape: pl.BlockSpec(shape, lambda m: (m, 0))
    wspec = lambda i: pl.BlockSpec((k, n), lambda m: (i, 0), pipeline_mode=pl.Buffered(1))
    return pl.pallas_call(
        _out_proj_kernel,
        grid=(t // tm,),
        in_specs=[row((tm, k)), row((tm, k)), wspec(0), wspec(1), row((tm, n)),
                  pl.BlockSpec((1, n), lambda m: (0, 0))],
        out_specs=[row((tm, n)), row((tm, n))],
        out_shape=[jax.ShapeDtypeStruct((t, n), F32), jax.ShapeDtypeStruct((t, n), BF16)],
        compiler_params=_params(1),
        name="out_proj_norm",
    )(a1, a2, w_bf16, w_bf16, res, g)


def _up_kernel(a_ref, wg_ref, wv_ref, cwg_ref, cwv_ref, cbg_ref, cbv_ref, wd_ref, o_ref, wdb_ref,
               wgb_ref, wvb_ref, carry_ref, *, tm, seq):
    m = pl.program_id(1)
    wdb_ref[...] = wd_ref[...].astype(BF16)

    @pl.when(m == 0)
    def _():
        _cast_rows(wg_ref, wgb_ref)
        _cast_rows(wv_ref, wvb_ref)

    a = a_ref[...]
    seq_start = (m * tm) % seq == 0
    outs = []
    for idx, (wb_ref, cw_ref, cb_ref) in enumerate(((wgb_ref, cwg_ref, cbg_ref),
                                                    (wvb_ref, cwv_ref, cbv_ref))):
        u = jnp.dot(a, wb_ref[...], preferred_element_type=F32)
        above = jnp.where(seq_start, 0.0, carry_ref[idx])
        blk = jnp.concatenate([above, u], axis=0)
        cw = cw_ref[...]
        conv = cb_ref[...]
        for k in range(FFN_CONV - 1):
            conv = conv + _delay_rows(blk, FFN_CONV - 1 - k) * cw[k:k + 1, :]
        conv = conv + u * cw[FFN_CONV - 1:FFN_CONV, :]
        carry_ref[idx] = u[tm - SUBLANES:tm]
        outs.append(conv)
    gate, val = outs
    o_ref[...] = ((gate * _sigmoid(gate)) * val).astype(o_ref.dtype)


def _up_proj(a, w, cw, cb, w_down, tm, tn, seq):
    t, k = a.shape
    assert seq % tm == 0 and D_FF % tn == 0
    nb = D_FF // tn
    n_m = t // tm
    kd, nd = w_down.shape
    slab = kd // (nb * n_m)
    assert slab * nb * n_m == kd and slab % 16 == 0
    return pl.pallas_call(
        functools.partial(_up_kernel, tm=tm, seq=seq),
        grid=(nb, t // tm),
        in_specs=[pl.BlockSpec((tm, k), lambda j, m: (m, 0)),
                  pl.BlockSpec((k, tn), lambda j, m: (0, j)),
                  pl.BlockSpec((k, tn), lambda j, m: (0, j + nb)),
                  pl.BlockSpec((FFN_CONV, tn), lambda j, m: (0, j)),
                  pl.BlockSpec((FFN_CONV, tn), lambda j, m: (0, j + nb)),
                  pl.BlockSpec((1, tn), lambda j, m: (0, j)),
                  pl.BlockSpec((1, tn), lambda j, m: (0, j + nb)),
                  pl.BlockSpec((slab, nd), lambda j, m: (j * n_m + m, 0))],
        out_specs=[pl.BlockSpec((tm, tn), lambda j, m: (m, j)),
                   pl.BlockSpec((slab, nd), lambda j, m: (j * n_m + m, 0))],
        out_shape=[jax.ShapeDtypeStruct((t, D_FF), BF16),
                   jax.ShapeDtypeStruct((kd, nd), BF16)],
        scratch_shapes=[pltpu.VMEM((k, tn), BF16), pltpu.VMEM((k, tn), BF16),
                        pltpu.VMEM((2, SUBLANES, tn), F32)],
        compiler_params=_params(2),
        name="up_proj_conv_swiglu",
    )(a, w, w, cw, cw, cb, cb, w_down)


def _down_kernel(a_ref, w_ref, r_ref, g_ref, o_ref, *, final_norm):
    h = r_ref[...] + jnp.dot(a_ref[...], w_ref[...], preferred_element_type=F32)
    o_ref[...] = _rmsnorm_rows(h, g_ref[...]) if final_norm else h


def _down_proj(a, w_bf16, res, g, tm, final_norm):
    t, k = a.shape
    n = w_bf16.shape[1]
    row = lambda shape: pl.BlockSpec(shape, lambda m: (m, 0))
    return pl.pallas_call(
        functools.partial(_down_kernel, final_norm=final_norm),
        grid=(t // tm,),
        in_specs=[row((tm, k)),
                  pl.BlockSpec((k, n), lambda m: (0, 0), pipeline_mode=pl.Buffered(1)),
                  row((tm, n)),
                  pl.BlockSpec((1, n), lambda m: (0, 0))],
        out_specs=row((tm, n)),
        out_shape=jax.ShapeDtypeStruct((t, n), F32),
        compiler_params=_params(1),
        name="down_proj_norm" if final_norm else "down_proj",
    )(a, w_bf16, res, g)


def _pad_lanes(v):
    return jnp.pad(v.astype(F32), (0, LANES - v.shape[0]))


def _mixer_layer(h, batch, seq, norm_mix, w_in, sinks, attn_out_norm, ssd_conv_w, ssd_conv_b, dt_bias,
                 a_log, ssd_d, ssd_norm, w_out, norm_ffn, tables, emat):
    w_in_t = jnp.swapaxes(w_in, 0, 1)
    wdt_t = jnp.pad(w_in_t[MAIN_PROJ:], ((0, LANES - SSD_HEADS), (0, 0))).astype(BF16)
    q_head, proj, dt_rawt = _in_proj(h, norm_mix.reshape(1, -1), wdt_t, w_in_t, 1024, batch, seq)
    attn, w_out_bf16 = _attention(q_head, proj, sinks.astype(F32), attn_out_norm.reshape(1, -1), tables, w_out,
                                  batch, seq, qb=4)
    bias = _pad_lanes(dt_bias)
    alog = _pad_lanes(a_log)
    y = _ssd(proj, dt_rawt, ssd_conv_w, ssd_conv_b.reshape(1, -1), bias.reshape(-1, 1), alog.reshape(-1, 1),
             emat, jnp.repeat(ssd_d.astype(F32), SSD_HEAD_DIM).reshape(1, -1),
             ssd_norm.reshape(1, -1), batch, seq)
    return _out_proj(attn, y, w_out_bf16, h, norm_ffn.reshape(1, -1), tm=512)


def kernel(x, norm_mix, w_in, sinks, attn_out_norm, ssd_conv_w, ssd_conv_b, dt_bias, a_log, ssd_d, ssd_norm,
           w_out, norm_ffn, w_up, ffn_conv_w, ffn_conv_b, w_down, norm_final):
    batch, seq, d = x.shape
    h = x.reshape(batch * seq, d)
    tables = _rope_tables(seq)
    head_of_channel = np.arange(SSD_INNER) // SSD_HEAD_DIM
    emat = np.arange(LANES)[:, None] == head_of_channel[None, :]
    emat = jnp.asarray(np.concatenate([emat, emat], axis=0), dtype=BF16)
    for l in range(norm_mix.shape[0]):
        h, hn = _mixer_layer(h, batch, seq, norm_mix[l], w_in[l], sinks[l], attn_out_norm[l], ssd_conv_w[l],
                             ssd_conv_b[l], dt_bias[l], a_log[l], ssd_d[l], ssd_norm[l], w_out[l], norm_ffn[l],
                             tables, emat)
        act, w_down_bf16 = _up_proj(hn, w_up[l], ffn_conv_w[l], ffn_conv_b[l].reshape(1, -1), w_down[l],
                                    tm=1024, tn=512, seq=seq)
        h = _down_proj(act, w_down_bf16, h, norm_final.reshape(1, -1), tm=512,
                       final_norm=(l == norm_mix.shape[0] - 1))
    return h.reshape(batch, seq, d)
```

```python
import functools

import numpy as np
import jax
import jax.numpy as jnp
from jax import lax
from jax.experimental import pallas as pl
from jax.experimental.pallas import tpu as pltpu

F32 = jnp.float32
BF16 = jnp.bfloat16

D_MODEL = 2048
N_Q_HEADS = 32
N_KV_HEADS = 8
HEAD_DIM = 64
Q_PER_KV = N_Q_HEADS // N_KV_HEADS
WINDOW = 128
ATTN_BLOCK = 128
ROT_DIM = HEAD_DIM // 4
ROPE_THETA = 500000.0
SSD_HEADS = 32
SSD_HEAD_DIM = 64
SSD_INNER = SSD_HEADS * SSD_HEAD_DIM
SSD_GROUPS = 8
SSD_STATE = 128
SSD_CONV = 4
SSD_CHUNK = 128
ATTN_WIDTH = N_Q_HEADS * HEAD_DIM
KV_WIDTH = N_KV_HEADS * HEAD_DIM
BC_WIDTH = SSD_GROUPS * SSD_STATE
CONV_CH = SSD_INNER + 2 * BC_WIDTH
MAIN_PROJ = ATTN_WIDTH + 2 * KV_WIDTH + SSD_INNER + CONV_CH
D_FF = 5632
FFN_CONV = 3
EPS = 1e-6

LANES = 128
SUBLANES = 8
HALF = LANES // 2
NEG = -1e30
LOG2E = 1.4426950408889634
VMEM_LIMIT = 56 * 1024 * 1024

Q_HEAD = 1024
REST_PROJ = MAIN_PROJ - Q_HEAD
XBC_OFF = 0
Z_OFF = XBC_OFF + CONV_CH
Q_REST_OFF = Z_OFF + SSD_INNER
K_OFF = Q_REST_OFF + (ATTN_WIDTH - Q_HEAD)
V_OFF = K_OFF + KV_WIDTH


def _params(n_axes, flags=None):
    return pltpu.CompilerParams(dimension_semantics=("arbitrary",) * n_axes,
                                vmem_limit_bytes=VMEM_LIMIT, flags=flags)


def _sigmoid(x):
    return 1.0 / (1.0 + jnp.exp2(x * -LOG2E))


def _softplus(x):
    return jnp.maximum(x, 0.0) + jnp.log1p(jnp.exp(-jnp.abs(x)))


def _cast_rows(src_ref, dst_ref, rows=256):
    k = src_ref.shape[0]
    for r in range(0, k, rows):
        dst_ref[r:r + rows, :] = src_ref[r:r + rows, :].astype(BF16)


def _delay_rows(blk, sh):
    n, w = blk.shape[0] - SUBLANES, blk.shape[1]
    g = blk.reshape(n // SUBLANES + 1, SUBLANES, w)
    rot = pltpu.roll(g, sh, 1)
    row = lax.broadcasted_iota(jnp.int32, (SUBLANES, w), 0)
    out = jnp.where(row < sh, rot[:-1], rot[1:])
    return out.reshape(n, w)


def _rmsnorm_rows(x, g):
    ms = jnp.mean(x * x, axis=-1, keepdims=True)
    return x * lax.rsqrt(ms + EPS) * g


def _in_proj_head_kernel(x_ref, g_ref, wdtt_ref, wt_ref, xn_ref, dtt_ref, o_ref, wbf_ref):
    @pl.when(pl.program_id(0) == 0)
    def _():
        _cast_rows(wt_ref, wbf_ref)

    xn = _rmsnorm_rows(x_ref[...], g_ref[...]).astype(BF16)
    xn_ref[...] = xn
    dtt_ref[...] = lax.dot_general(wdtt_ref[...], xn, (((1,), (1,)), ((), ())),
                                   preferred_element_type=F32)
    o_ref[...] = lax.dot_general(xn, wbf_ref[...], (((1,), (1,)), ((), ())),
                                 preferred_element_type=F32).astype(o_ref.dtype)


def _in_proj_kernel(a_ref, wt_ref, o_ref, wbf_ref):
    @pl.when(pl.program_id(1) == 0)
    def _():
        _cast_rows(wt_ref, wbf_ref)

    o_ref[...] = lax.dot_general(a_ref[...], wbf_ref[...], (((1,), (1,)), ((), ())),
                                 preferred_element_type=F32).astype(o_ref.dtype)


def _in_proj(x2, g, wdtt, wt, tm, batch, seq):
    t, k = x2.shape
    tn = Q_HEAD
    assert seq % tm == 0
    per_seq = seq // tm

    xn, dtt, q_head = pl.pallas_call(
        _in_proj_head_kernel,
        grid=(t // tm,),
        in_specs=[pl.BlockSpec((tm, k), lambda m: (m, 0)),
                  pl.BlockSpec((1, k), lambda m: (0, 0)),
                  pl.BlockSpec((LANES, k), lambda m: (0, 0)),
                  pl.BlockSpec((tn, k), lambda m: (0, 0))],
        out_specs=[pl.BlockSpec((tm, k), lambda m: (m, 0)),
                   pl.BlockSpec((None, LANES, tm), lambda m: (m // per_seq, 0, m % per_seq)),
                   pl.BlockSpec((tm, tn), lambda m: (m, 0))],
        out_shape=[jax.ShapeDtypeStruct((t, k), BF16),
                   jax.ShapeDtypeStruct((batch, LANES, seq), F32),
                   jax.ShapeDtypeStruct((t, tn), BF16)],
        scratch_shapes=[pltpu.VMEM((tn, k), BF16)],
        compiler_params=_params(1),
        name="norm_in_proj_head",
    )(x2, g, wdtt, wt)

    def wblock(j):
        return jnp.where(j < 4, j + 5, jnp.where(j < 6, j - 1, j - 5))

    rest = pl.pallas_call(
        _in_proj_kernel,
        grid=(REST_PROJ // tn, t // tm),
        in_specs=[pl.BlockSpec((tm, k), lambda j, m: (m, 0)),
                  pl.BlockSpec((tn, k), lambda j, m: (wblock(j), 0))],
        out_specs=pl.BlockSpec((tm, tn), lambda j, m: (m, j)),
        out_shape=jax.ShapeDtypeStruct((t, REST_PROJ), BF16),
        scratch_shapes=[pltpu.VMEM((tn, k), BF16)],
        compiler_params=_params(2),
        name="in_proj",
    )(xn, wt)
    return q_head, rest, dtt


def _rope_tables(seq):
    half = ROT_DIM // 2
    inv = 1.0 / (ROPE_THETA ** (np.arange(0, ROT_DIM, 2, dtype=np.float64) / ROT_DIM))
    ang = np.arange(seq, dtype=np.float64)[:, None] * inv[None, :]
    cos, sin = np.cos(ang), np.sin(ang)
    d = np.arange(LANES) % HEAD_DIM
    idx = d % half
    c = np.where((d < ROT_DIM)[None, :], cos[:, idx], 1.0)
    s1 = np.where((d < half)[None, :], -sin[:, idx], 0.0)
    s2 = np.where(((d >= half) & (d < ROT_DIM))[None, :], sin[:, idx], 0.0)
    return jnp.asarray(np.stack([c, s1, s2]).astype(np.float32))


def _rope(x, c, s1, s2):
    half = ROT_DIM // 2
    return x * c + pltpu.roll(x, LANES - half, 1) * s1 + pltpu.roll(x, half, 1) * s2


def _attn_kernel(sinks_ref, qa_ref, qb_ref, kp_ref, kc_ref, vp_ref, vc_ref,
                 tc_ref, tp_ref, g_ref, wo_ref, wu_ref, o_ref, wob_ref, wub_ref,
                 lhs_ref, kb_ref, vb_ref, s_ref, p_ref, o2_ref, m_ref, tq_ref, bias_ref, acc_ref, *, qb):
    blk = ATTN_BLOCK
    n = pl.program_id(1)
    lo = lax.broadcasted_iota(jnp.int32, (blk, LANES), 1) < HALF

    wob_ref[...] = wo_ref[...].astype(BF16)
    wub_ref[...] = wu_ref[...].astype(BF16)

    scale = HEAD_DIM ** -0.5 * LOG2E
    for i in range(3):
        tq_ref[i] = tc_ref[i] * scale
    qi = lax.broadcasted_iota(jnp.int32, (blk, 2 * blk), 0)
    kj = lax.broadcasted_iota(jnp.int32, (blk, 2 * blk), 1)
    rel = qi + blk - kj
    band = (rel >= 0) & (rel < WINDOW)
    kmin = jnp.where(n > 0, 0, blk)
    bias_ref[0] = jnp.where(band & (kj >= kmin), 0.0, NEG)
    bias_ref[1] = jnp.where(band, 0.0, NEG)

    for j in range(qb):
        rows = slice(j * blk, (j + 1) * blk)
        for col in range(ATTN_WIDTH // LANES):
            h, jj = col // 2, col % 2
            q_ref, qc = (qa_ref, col) if col < Q_HEAD // LANES else (qb_ref, col - Q_HEAD // LANES)
            q2 = _rope(q_ref[rows, qc * LANES:(qc + 1) * LANES].astype(F32),
                       tq_ref[0, rows, :], tq_ref[1, rows, :], tq_ref[2, rows, :])
            lhs_ref[j, h, (2 * jj) * blk:(2 * jj + 1) * blk, :] = jnp.where(lo, q2, 0.0).astype(BF16)
            lhs_ref[j, h, (2 * jj + 1) * blk:(2 * jj + 2) * blk, :] = jnp.where(lo, 0.0, q2).astype(BF16)
    nk = (qb + 1) * blk
    lok = lax.broadcasted_iota(jnp.int32, (nk, LANES), 1) < HALF
    ones = jnp.ones((nk, LANES), BF16)
    for i in range(N_KV_HEADS // 2):
        cols = slice(i * LANES, (i + 1) * LANES)
        kcat = jnp.concatenate([_rope(kp_ref[:, cols].astype(F32), tp_ref[0], tp_ref[1], tp_ref[2]),
                                _rope(kc_ref[:, cols].astype(F32), tc_ref[0], tc_ref[1], tc_ref[2])], axis=0)
        vcat = jnp.concatenate([vp_ref[:, cols], vc_ref[:, cols]], axis=0).astype(F32)
        kswp = pltpu.roll(kcat, HALF, 1)
        vswp = pltpu.roll(vcat, HALF, 1)
        kb_ref[2 * i] = jnp.where(lok, kcat, kswp).astype(BF16)
        kb_ref[2 * i + 1] = jnp.where(lok, kswp, kcat).astype(BF16)
        vb_ref[2 * i, :, 0:LANES] = jnp.where(lok, vcat, vswp).astype(BF16)
        vb_ref[2 * i + 1, :, 0:LANES] = jnp.where(lok, vswp, vcat).astype(BF16)
        vb_ref[2 * i, :, LANES:2 * LANES] = ones
        vb_ref[2 * i + 1, :, LANES:2 * LANES] = ones

    def keys(j):
        return slice(j * blk, (j + 2) * blk)

    def scores(j, h):
        s_ref[j, h % 2] = lax.dot_general(lhs_ref[j, h], kb_ref[h, keys(j), :], (((1,), (1,)), ((), ())),
                                          preferred_element_type=F32)

    def row_max(j, h):
        slot = h % 2
        for r in range(Q_PER_KV):
            rows = slice(r * blk, (r + 1) * blk)
            sink = sinks_ref[Q_PER_KV * h + r] * LOG2E
            sr = s_ref[j, slot, rows, :] + bias_ref[min(j, 1)]
            s_ref[j, slot, rows, :] = sr
            m = jnp.maximum(jnp.max(sr, axis=-1, keepdims=True), sink)
            m_ref[j, slot, rows, :] = jnp.broadcast_to(m, (blk, LANES))

    def probs(j, h):
        slot = h % 2
        for r in range(Q_PER_KV):
            rows = slice(r * blk, (r + 1) * blk)
            m = m_ref[j, slot, rows, :]
            for half in range(2):
                ln = slice(half * LANES, (half + 1) * LANES)
                p_ref[j, slot, rows, ln] = jnp.exp2(s_ref[j, slot, rows, ln] - m).astype(BF16)

    def weighted(j, h):
        o2_ref[j, h % 2] = jnp.dot(p_ref[j, h % 2], vb_ref[h, keys(j), :],
                                   preferred_element_type=F32)

    def finish(j, h):
        slot = h % 2
        outs = []
        for r in range(Q_PER_KV):
            rows = slice(r * blk, (r + 1) * blk)
            sink = sinks_ref[Q_PER_KV * h + r] * LOG2E
            den = o2_ref[j, slot, rows, LANES:2 * LANES] + jnp.exp2(sink - m_ref[j, slot, rows, :])
            outs.append(o2_ref[j, slot, rows, 0:LANES] * (1.0 / den))
        for jj in range(2):
            col = 2 * h + jj
            acc_ref[j * blk:(j + 1) * blk, col * LANES:(col + 1) * LANES] = jnp.where(
                lo, outs[2 * jj], outs[2 * jj + 1])

    for j in range(qb):
        scores(j, 0)
    for h in range(N_KV_HEADS + 1):
        for stage in (scores, row_max, probs, weighted, finish):
            hh = {scores: h + 1, finish: h - 1}.get(stage, h)
            if 0 <= hh < N_KV_HEADS:
                for j in range(qb):
                    stage(j, hh)

    o_ref[...] = _rmsnorm_rows(acc_ref[...], g_ref[...]).astype(o_ref.dtype)


def _attention(q_head, proj, sinks, g, tables, w_out, w_up, batch, seq, qb):
    t = proj.shape[0]
    blk = ATTN_BLOCK
    assert seq % (qb * blk) == 0
    nb = seq // blk
    ns = nb // qb
    ko, no = w_out.shape
    slab = ko // (batch * ns)
    assert slab * batch * ns == ko and slab % 16 == 0
    ku, nu = w_up.shape
    slab_u = ku // (batch * ns)
    assert slab_u * batch * ns == ku and slab_u % 16 == 0
    kblk = K_OFF // KV_WIDTH
    vblk = V_OFF // KV_WIDTH

    def cur(b, n):
        return (b * ns + n, 0)

    def prev_rows(b, n):
        return jnp.maximum(b * nb + n * qb - 1, 0)

    return pl.pallas_call(
        functools.partial(_attn_kernel, qb=qb),
        grid=(batch, ns),
        in_specs=[pl.BlockSpec(memory_space=pltpu.SMEM),
                  pl.BlockSpec((qb * blk, Q_HEAD), cur),
                  pl.BlockSpec((qb * blk, ATTN_WIDTH - Q_HEAD),
                               lambda b, n: (b * ns + n, Q_REST_OFF // (ATTN_WIDTH - Q_HEAD))),
                  pl.BlockSpec((blk, KV_WIDTH), lambda b, n: (prev_rows(b, n), kblk)),
                  pl.BlockSpec((qb * blk, KV_WIDTH), lambda b, n: (b * ns + n, kblk)),
                  pl.BlockSpec((blk, KV_WIDTH), lambda b, n: (prev_rows(b, n), vblk)),
                  pl.BlockSpec((qb * blk, KV_WIDTH), lambda b, n: (b * ns + n, vblk)),
                  pl.BlockSpec((3, qb * blk, LANES), lambda b, n: (0, n, 0)),
                  pl.BlockSpec((3, blk, LANES), lambda b, n: (0, jnp.maximum(n * qb - 1, 0), 0)),
                  pl.BlockSpec((1, ATTN_WIDTH), lambda b, n: (0, 0)),
                  pl.BlockSpec((slab, no), cur),
                  pl.BlockSpec((slab_u, nu), cur)],
        out_specs=[pl.BlockSpec((qb * blk, ATTN_WIDTH), cur),
                   pl.BlockSpec((slab, no), cur),
                   pl.BlockSpec((slab_u, nu), cur)],
        out_shape=[jax.ShapeDtypeStruct((t, ATTN_WIDTH), BF16),
                   jax.ShapeDtypeStruct((ko, no), BF16),
                   jax.ShapeDtypeStruct((ku, nu), BF16)],
        scratch_shapes=[pltpu.VMEM((qb, N_KV_HEADS, Q_PER_KV * blk, LANES), BF16),
                        pltpu.VMEM((N_KV_HEADS, (qb + 1) * blk, LANES), BF16),
                        pltpu.VMEM((N_KV_HEADS, (qb + 1) * blk, 2 * LANES), BF16),
                        pltpu.VMEM((qb, 2, Q_PER_KV * blk, 2 * blk), F32),
                        pltpu.VMEM((qb, 2, Q_PER_KV * blk, 2 * blk), BF16),
                        pltpu.VMEM((qb, 2, Q_PER_KV * blk, 2 * LANES), F32),
                        pltpu.VMEM((qb, 2, Q_PER_KV * blk, LANES), F32),
                        pltpu.VMEM((3, qb * blk, LANES), F32),
                        pltpu.VMEM((2, blk, 2 * blk), F32),
                        pltpu.VMEM((qb * blk, ATTN_WIDTH), F32)],
        compiler_params=_params(2),
        name="swa_attention",
    )(sinks, q_head, proj, proj, proj, proj, proj, tables, tables, g, w_out, w_up)


def _split3(x):
    h = x.astype(BF16)
    r = x - h.astype(F32)
    m = r.astype(BF16)
    l = (r - m.astype(F32)).astype(BF16)
    return h, m, l


def _dot3_lhs(x, w):
    h, m, l = _split3(x)
    d = lambda a: jnp.dot(a, w, preferred_element_type=F32)
    return (d(l) + d(m)) + d(h)


def _ssd_kernel(xbc_ref, z_ref, dtt_ref, cw_ref, cb_ref, bcol_ref, alcol_ref, e_ref, dfull_ref, gn_ref, o_ref,
                ext_ref, xact_ref, state_ref, xdt_ref, xdec_ref, y_ref, exp_ref, acs_ref, acst_ref, split_ref,
                *, nbat):
    L = SSD_CHUNK
    c = pl.program_id(0)
    cw_chunk = 512
    gw = SSD_HEAD_DIM * (SSD_HEADS // SSD_GROUPS)
    ri = lax.broadcasted_iota(jnp.int32, (L, L), 0)
    ci = lax.broadcasted_iota(jnp.int32, (L, L), 1)
    causal = ri >= ci
    lo = ci < HALF

    @pl.when(c == 0)
    def _():
        ext_ref[:, 0:SUBLANES, :] = jnp.zeros((nbat, SUBLANES, CONV_CH), F32)
        state_ref[...] = jnp.zeros_like(state_ref)

    def load_x(b):
        ext_ref[b, SUBLANES:SUBLANES + L, :] = xbc_ref[b].astype(F32)

    def conv(b, j):
        cs = slice(j * cw_chunk, (j + 1) * cw_chunk)
        blk = ext_ref[b, :, cs]
        acc = cb_ref[:, cs]
        for k in range(SSD_CONV - 1):
            acc = acc + _delay_rows(blk, SSD_CONV - 1 - k) * cw_ref[k:k + 1, cs]
        acc = acc + blk[SUBLANES:] * cw_ref[SSD_CONV - 1:SSD_CONV, cs]
        xact_ref[b, :, cs] = acc * _sigmoid(acc)

    def keep_tail(b):
        ext_ref[b, 0:SUBLANES, :] = ext_ref[b, L:L + SUBLANES, :]

    def factors(b):
        nh = SSD_HEADS
        dtt = _softplus(dtt_ref[b, 0:nh, :] + bcol_ref[0:nh, :])
        dat = dtt * (-jnp.exp(alcol_ref[0:nh, :]))
        tri_u = jnp.where(ri <= ci, 1.0, 0.0).astype(BF16)
        acst = _dot3_lhs(dat, tri_u) * LOG2E
        acst_ref[b, 0:nh, :] = acst
        pad = jnp.zeros((LANES - nh, L), F32)
        a_cs = jnp.concatenate([acst, pad], axis=0).T
        dt = jnp.concatenate([dtt, pad], axis=0).T
        acs_ref[b] = a_cs
        a_last = a_cs[L - 1:L, :]
        stack = jnp.concatenate([dt, jnp.exp2(a_last - a_cs), jnp.exp2(a_cs),
                                 jnp.broadcast_to(jnp.exp2(a_last), (SUBLANES, LANES))], axis=0)
        hi = stack.astype(BF16)
        split_ref[b, :, 0:LANES] = hi
        split_ref[b, :, LANES:2 * LANES] = (stack - hi.astype(F32)).astype(BF16)

    def expand(b, j):
        cs = slice(j * cw_chunk, (j + 1) * cw_chunk)
        exp_ref[b, :, cs] = jnp.dot(split_ref[b], e_ref[:, cs], preferred_element_type=F32)

    def scale_x(b):
        for j in range(SSD_INNER // cw_chunk):
            cs = slice(j * cw_chunk, (j + 1) * cw_chunk)
            xdt = xact_ref[b, :, cs] * exp_ref[b, 0:L, cs]
            xdt_ref[b, :, cs] = xdt.astype(BF16)
            xdec_ref[b, :, cs] = (xdt * exp_ref[b, L:2 * L, cs]).astype(BF16)

    def group(b, g):
        bg = xact_ref[b, :, SSD_INNER + g * SSD_STATE:SSD_INNER + (g + 1) * SSD_STATE]
        cg = xact_ref[b, :, SSD_INNER + BC_WIDTH + g * SSD_STATE:SSD_INNER + BC_WIDTH + (g + 1) * SSD_STATE]
        bb = bg.astype(BF16)
        cbf = cg.astype(BF16)
        cbm = lax.dot_general(cbf, bb, (((1,), (1,)), ((), ())), preferred_element_type=F32)
        gs = slice(g * gw, (g + 1) * gw)
        prev = state_ref[b, g]
        yoff = jnp.dot(cbf, prev.astype(BF16), preferred_element_type=F32) * exp_ref[b, 2 * L:3 * L, gs]
        btb = bg.T.astype(BF16)
        state_ref[b, g] = prev * exp_ref[b, 3 * L:3 * L + 1, gs] + jnp.dot(
            btb, xdec_ref[b, :, gs], preferred_element_type=F32)
        for e in range(2):
            pc = slice((2 * g + e) * LANES, (2 * g + e + 1) * LANES)
            xpair = xdt_ref[b, :, pc]
            yd = []
            for r in range(2):
                h = 4 * g + 2 * e + r
                seg = acs_ref[b, :, h:h + 1] - acst_ref[b, h:h + 1, :]
                lm = jnp.exp2(jnp.where(causal, seg, NEG))
                yd.append(jnp.dot((cbm * lm).astype(BF16), xpair, preferred_element_type=F32))
            y_ref[b, :, pc] = (jnp.where(lo, yd[0], yd[1]) + yoff[:, e * LANES:(e + 1) * LANES]
                               + dfull_ref[:, pc] * xact_ref[b, :, pc])

    def gate_norm(b, g):
        gs = slice(g * gw, (g + 1) * gw)
        zz = z_ref[b, :, gs].astype(F32)
        yg = y_ref[b, :, gs] * (zz * _sigmoid(zz))
        ms = jnp.mean(yg * yg, axis=-1, keepdims=True)
        o_ref[b, :, gs] = (yg * lax.rsqrt(ms + EPS) * gn_ref[:, gs]).astype(o_ref.dtype)

    for b in range(nbat):
        load_x(b)
    pieces = [functools.partial(factors, b) for b in range(nbat)]
    pieces += [functools.partial(expand, b, j) for b in range(nbat) for j in range(SSD_INNER // cw_chunk)]
    chunks = [functools.partial(conv, b, j) for b in range(nbat) for j in range(CONV_CH // cw_chunk)]
    for idx in range(max(len(pieces), len(chunks))):
        if idx < len(pieces):
            pieces[idx]()
        if idx < len(chunks):
            chunks[idx]()
    for b in range(nbat):
        keep_tail(b)
        scale_x(b)
    for stage in (group, gate_norm):
        for g in range(SSD_GROUPS):
            for b in range(nbat):
                stage(b, g)


def _ssd(proj, dt_rawt, cw, cb, bcol, alcol, emat, dfull, gn, batch, seq):
    L = SSD_CHUNK
    nc = seq // L
    proj3 = proj.reshape(batch, seq, REST_PROJ)
    full = lambda shape: pl.BlockSpec(shape, lambda c: (0, 0))
    out = pl.pallas_call(
        functools.partial(_ssd_kernel, nbat=batch),
        grid=(nc,),
        in_specs=[pl.BlockSpec((batch, L, CONV_CH), lambda c: (0, c, XBC_OFF // CONV_CH)),
                  pl.BlockSpec((batch, L, SSD_INNER), lambda c: (0, c, Z_OFF // SSD_INNER)),
                  pl.BlockSpec((batch, LANES, L), lambda c: (0, 0, c)),
                  full((SSD_CONV, CONV_CH)), full((1, CONV_CH)),
                  full((LANES, 1)), full((LANES, 1)),
                  full((2 * LANES, SSD_INNER)), full((1, SSD_INNER)), full((1, SSD_INNER))],
        out_specs=pl.BlockSpec((batch, L, SSD_INNER), lambda c: (0, c, 0)),
        out_shape=jax.ShapeDtypeStruct((batch, seq, SSD_INNER), BF16),
        scratch_shapes=[pltpu.VMEM((batch, L + SUBLANES, CONV_CH), F32),
                        pltpu.VMEM((batch, L, CONV_CH), F32),
                        pltpu.VMEM((batch, SSD_GROUPS, SSD_STATE, SSD_INNER // SSD_GROUPS), F32),
                        pltpu.VMEM((batch, L, SSD_INNER), BF16),
                        pltpu.VMEM((batch, L, SSD_INNER), BF16),
                        pltpu.VMEM((batch, L, SSD_INNER), F32),
                        pltpu.VMEM((batch, 3 * L + SUBLANES, SSD_INNER), F32),
                        pltpu.VMEM((batch, L, LANES), F32),
                        pltpu.VMEM((batch, LANES, L), F32),
                        pltpu.VMEM((batch, 3 * L + SUBLANES, 2 * LANES), BF16)],
        compiler_params=_params(1),
        name="ssd_scan",
    )(proj3, proj3, dt_rawt, cw, cb, bcol, alcol, emat, dfull, gn)
    return out.reshape(batch * seq, SSD_INNER)


def _out_proj_kernel(a1_ref, a2_ref, w1_ref, w2_ref, r_ref, g_ref, h_ref, hn_ref):
    acc = jnp.dot(a1_ref[...], w1_ref[...], preferred_element_type=F32)
    acc = acc + jnp.dot(a2_ref[...], w2_ref[...], preferred_element_type=F32)
    h = r_ref[...] + acc
    h_ref[...] = h
    hn_ref[...] = _rmsnorm_rows(h, g_ref[...]).astype(hn_ref.dtype)


def _out_proj(a1, a2, w_bf16, res, g, tm):
    t, k = a1.shape
    n = w_bf16.shape[1]
    row = lambda shape: pl.BlockSpec(shape, lambda m: (m, 0))
    wspec = lambda i: pl.BlockSpec((k, n), lambda m: (i, 0), pipeline_mode=pl.Buffered(1))
    return pl.pallas_call(
        _out_proj_kernel,
        grid=(t // tm,),
        in_specs=[row((tm, k)), row((tm, k)), wspec(0), wspec(1), row((tm, n)),
                  pl.BlockSpec((1, n), lambda m: (0, 0))],
        out_specs=[row((tm, n)), row((tm, n))],
        out_shape=[jax.ShapeDtypeStruct((t, n), F32), jax.ShapeDtypeStruct((t, n), BF16)],
        compiler_params=_params(1),
        name="out_proj_norm",
    )(a1, a2, w_bf16, w_bf16, res, g)


def _up_kernel(a_ref, wgb_ref, wvb_ref, cwg_ref, cwv_ref, cbg_ref, cbv_ref, wd_ref, o_ref, wdb_ref,
               carry_ref, *, tm, seq):
    m = pl.program_id(1)
    wdb_ref[...] = wd_ref[...].astype(BF16)

    a = a_ref[...]
    seq_start = (m * tm) % seq == 0
    outs = []
    for idx, (wb_ref, cw_ref, cb_ref) in enumerate(((wgb_ref, cwg_ref, cbg_ref),
                                                    (wvb_ref, cwv_ref, cbv_ref))):
        u = jnp.dot(a, wb_ref[...], preferred_element_type=F32)
        above = jnp.where(seq_start, 0.0, carry_ref[idx])
        blk = jnp.concatenate([above, u], axis=0)
        cw = cw_ref[...]
        conv = cb_ref[...]
        for k in range(FFN_CONV - 1):
            conv = conv + _delay_rows(blk, FFN_CONV - 1 - k) * cw[k:k + 1, :]
        conv = conv + u * cw[FFN_CONV - 1:FFN_CONV, :]
        carry_ref[idx] = u[tm - SUBLANES:tm]
        outs.append(conv)
    gate, val = outs
    o_ref[...] = ((gate * _sigmoid(gate)) * val).astype(o_ref.dtype)


def _up_proj(a, w, cw, cb, w_down, tm, tn, seq):
    t, k = a.shape
    assert seq % tm == 0 and D_FF % tn == 0
    nb = D_FF // tn
    n_m = t // tm
    kd, nd = w_down.shape
    slab = kd // (nb * n_m)
    assert slab * nb * n_m == kd and slab % 16 == 0
    return pl.pallas_call(
        functools.partial(_up_kernel, tm=tm, seq=seq),
        grid=(nb, t // tm),
        in_specs=[pl.BlockSpec((tm, k), lambda j, m: (m, 0)),
                  pl.BlockSpec((k, tn), lambda j, m: (0, j)),
                  pl.BlockSpec((k, tn), lambda j, m: (0, j + nb)),
                  pl.BlockSpec((FFN_CONV, tn), lambda j, m: (0, j)),
                  pl.BlockSpec((FFN_CONV, tn), lambda j, m: (0, j + nb)),
                  pl.BlockSpec((1, tn), lambda j, m: (0, j)),
                  pl.BlockSpec((1, tn), lambda j, m: (0, j + nb)),
                  pl.BlockSpec((slab, nd), lambda j, m: (j * n_m + m, 0))],
        out_specs=[pl.BlockSpec((tm, tn), lambda j, m: (m, j)),
                   pl.BlockSpec((slab, nd), lambda j, m: (j * n_m + m, 0))],
        out_shape=[jax.ShapeDtypeStruct((t, D_FF), BF16),
                   jax.ShapeDtypeStruct((kd, nd), BF16)],
        scratch_shapes=[pltpu.VMEM((2, SUBLANES, tn), F32)],
        compiler_params=_params(2),
        name="up_proj_conv_swiglu",
    )(a, w, w, cw, cw, cb, cb, w_down)


def _down_kernel(a_ref, w_ref, r_ref, g_ref, o_ref, *, final_norm):
    h = r_ref[...] + jnp.dot(a_ref[...], w_ref[...], preferred_element_type=F32)
    o_ref[...] = _rmsnorm_rows(h, g_ref[...]) if final_norm else h


def _down_proj(a, w_bf16, res, g, tm, final_norm):
    t, k = a.shape
    n = w_bf16.shape[1]
    row = lambda shape: pl.BlockSpec(shape, lambda m: (m, 0))
    return pl.pallas_call(
        functools.partial(_down_kernel, final_norm=final_norm),
        grid=(t // tm,),
        in_specs=[row((tm, k)),
                  pl.BlockSpec((k, n), lambda m: (0, 0), pipeline_mode=pl.Buffered(1)),
                  row((tm, n)),
                  pl.BlockSpec((1, n), lambda m: (0, 0))],
        out_specs=row((tm, n)),
        out_shape=jax.ShapeDtypeStruct((t, n), F32),
        compiler_params=_params(1),
        name="down_proj_norm" if final_norm else "down_proj",
    )(a, w_bf16, res, g)


def _pad_lanes(v):
    return jnp.pad(v.astype(F32), (0, LANES - v.shape[0]))


def _mixer_layer(h, batch, seq, norm_mix, w_in, sinks, attn_out_norm, ssd_conv_w, ssd_conv_b, dt_bias,
                 a_log, ssd_d, ssd_norm, w_out, norm_ffn, w_up, tables, emat):
    w_in_t = jnp.swapaxes(w_in, 0, 1)
    wdt_t = jnp.pad(w_in_t[MAIN_PROJ:], ((0, LANES - SSD_HEADS), (0, 0))).astype(BF16)
    q_head, proj, dt_rawt = _in_proj(h, norm_mix.reshape(1, -1), wdt_t, w_in_t, 1024, batch, seq)
    attn, w_out_bf16, w_up_bf16 = _attention(q_head, proj, sinks.astype(F32), attn_out_norm.reshape(1, -1),
                                             tables, w_out, w_up, batch, seq, qb=4)
    bias = _pad_lanes(dt_bias)
    alog = _pad_lanes(a_log)
    y = _ssd(proj, dt_rawt, ssd_conv_w, ssd_conv_b.reshape(1, -1), bias.reshape(-1, 1), alog.reshape(-1, 1),
             emat, jnp.repeat(ssd_d.astype(F32), SSD_HEAD_DIM).reshape(1, -1),
             ssd_norm.reshape(1, -1), batch, seq)
    return _out_proj(attn, y, w_out_bf16, h, norm_ffn.reshape(1, -1), tm=512) + (w_up_bf16,)


def kernel(x, norm_mix, w_in, sinks, attn_out_norm, ssd_conv_w, ssd_conv_b, dt_bias, a_log, ssd_d, ssd_norm,
           w_out, norm_ffn, w_up, ffn_conv_w, ffn_conv_b, w_down, norm_final):
    batch, seq, d = x.shape
    h = x.reshape(batch * seq, d)
    tables = _rope_tables(seq)
    head_of_channel = np.arange(SSD_INNER) // SSD_HEAD_DIM
    emat = np.arange(LANES)[:, None] == head_of_channel[None, :]
    emat = jnp.asarray(np.concatenate([emat, emat], axis=0), dtype=BF16)
    for l in range(norm_mix.shape[0]):
        h, hn, w_up_bf16 = _mixer_layer(h, batch, seq, norm_mix[l], w_in[l], sinks[l], attn_out_norm[l],
                                        ssd_conv_w[l], ssd_conv_b[l], dt_bias[l], a_log[l], ssd_d[l], ssd_norm[l],
                                        w_out[l], norm_ffn[l], w_up[l], tables, emat)
        act, w_down_bf16 = _up_proj(hn, w_up_bf16, ffn_conv_w[l], ffn_conv_b[l].reshape(1, -1), w_down[l],
                                    tm=1024, tn=512, seq=seq)
        h = _down_proj(act, w_down_bf16, h, norm_final.reshape(1, -1), tm=512,
                       final_norm=(l == norm_mix.shape[0] - 1))
    return h.reshape(batch, seq, d)
```

```python
import functools

import numpy as np
import jax
import jax.numpy as jnp
from jax import lax
from jax.experimental import pallas as pl
from jax.experimental.pallas import tpu as pltpu

F32 = jnp.float32
BF16 = jnp.bfloat16

D_MODEL = 2048
N_Q_HEADS = 32
N_KV_HEADS = 8
HEAD_DIM = 64
Q_PER_KV = N_Q_HEADS // N_KV_HEADS
WINDOW = 128
ATTN_BLOCK = 128
ROT_DIM = HEAD_DIM // 4
ROPE_THETA = 500000.0
SSD_HEADS = 32
SSD_HEAD_DIM = 64
SSD_INNER = SSD_HEADS * SSD_HEAD_DIM
SSD_GROUPS = 8
SSD_STATE = 128
SSD_CONV = 4
SSD_CHUNK = 128
ATTN_WIDTH = N_Q_HEADS * HEAD_DIM
KV_WIDTH = N_KV_HEADS * HEAD_DIM
BC_WIDTH = SSD_GROUPS * SSD_STATE
CONV_CH = SSD_INNER + 2 * BC_WIDTH
MAIN_PROJ = ATTN_WIDTH + 2 * KV_WIDTH + SSD_INNER + CONV_CH
D_FF = 5632
FFN_CONV = 3
EPS = 1e-6

LANES = 128
SUBLANES = 8
HALF = LANES // 2
NEG = -1e30
LOG2E = 1.4426950408889634
VMEM_LIMIT = 56 * 1024 * 1024

Q_HEAD = 1024
REST_PROJ = MAIN_PROJ - Q_HEAD
XBC_OFF = 0
Z_OFF = XBC_OFF + CONV_CH
Q_REST_OFF = Z_OFF + SSD_INNER
K_OFF = Q_REST_OFF + (ATTN_WIDTH - Q_HEAD)
V_OFF = K_OFF + KV_WIDTH


def _params(n_axes, flags=None):
    return pltpu.CompilerParams(dimension_semantics=("arbitrary",) * n_axes,
                                vmem_limit_bytes=VMEM_LIMIT, flags=flags)


def _sigmoid(x):
    return 1.0 / (1.0 + jnp.exp2(x * -LOG2E))


def _softplus(x):
    return jnp.maximum(x, 0.0) + jnp.log1p(jnp.exp(-jnp.abs(x)))


def _cast_rows(src_ref, dst_ref, rows=256):
    k = src_ref.shape[0]
    for r in range(0, k, rows):
        dst_ref[r:r + rows, :] = src_ref[r:r + rows, :].astype(BF16)


def _delay_rows(blk, sh):
    n, w = blk.shape[0] - SUBLANES, blk.shape[1]
    g = blk.reshape(n // SUBLANES + 1, SUBLANES, w)
    rot = pltpu.roll(g, sh, 1)
    row = lax.broadcasted_iota(jnp.int32, (SUBLANES, w), 0)
    out = jnp.where(row < sh, rot[:-1], rot[1:])
    return out.reshape(n, w)


def _rmsnorm_rows(x, g):
    ms = jnp.mean(x * x, axis=-1, keepdims=True)
    return x * lax.rsqrt(ms + EPS) * g


def _in_proj_head_kernel(x_ref, g_ref, wdtt_ref, wt_ref, xn_ref, dtt_ref, o_ref, wbf_ref):
    @pl.when(pl.program_id(0) == 0)
    def _():
        _cast_rows(wt_ref, wbf_ref)

    xn = _rmsnorm_rows(x_ref[...], g_ref[...]).astype(BF16)
    xn_ref[...] = xn
    dtt_ref[...] = lax.dot_general(wdtt_ref[...], xn, (((1,), (1,)), ((), ())),
                                   preferred_element_type=F32)
    o_ref[...] = lax.dot_general(xn, wbf_ref[...], (((1,), (1,)), ((), ())),
                                 preferred_element_type=F32).astype(o_ref.dtype)


def _in_proj_kernel(a_ref, wt_ref, o_ref, wbf_ref):
    @pl.when(pl.program_id(1) == 0)
    def _():
        _cast_rows(wt_ref, wbf_ref)

    o_ref[...] = lax.dot_general(a_ref[...], wbf_ref[...], (((1,), (1,)), ((), ())),
                                 preferred_element_type=F32).astype(o_ref.dtype)


def _in_proj(x2, g, wdtt, wt, tm, tm_rest, batch, seq):
    t, k = x2.shape
    tn = Q_HEAD
    assert seq % tm == 0
    per_seq = seq // tm

    xn, dtt, q_head = pl.pallas_call(
        _in_proj_head_kernel,
        grid=(t // tm,),
        in_specs=[pl.BlockSpec((tm, k), lambda m: (m, 0)),
                  pl.BlockSpec((1, k), lambda m: (0, 0)),
                  pl.BlockSpec((LANES, k), lambda m: (0, 0)),
                  pl.BlockSpec((tn, k), lambda m: (0, 0))],
        out_specs=[pl.BlockSpec((tm, k), lambda m: (m, 0)),
                   pl.BlockSpec((None, LANES, tm), lambda m: (m // per_seq, 0, m % per_seq)),
                   pl.BlockSpec((tm, tn), lambda m: (m, 0))],
        out_shape=[jax.ShapeDtypeStruct((t, k), BF16),
                   jax.ShapeDtypeStruct((batch, LANES, seq), F32),
                   jax.ShapeDtypeStruct((t, tn), BF16)],
        scratch_shapes=[pltpu.VMEM((tn, k), BF16)],
        compiler_params=_params(1),
        name="norm_in_proj_head",
    )(x2, g, wdtt, wt)

    def wblock(j):
        return jnp.where(j < 4, j + 5, jnp.where(j < 6, j - 1, j - 5))

    rest = pl.pallas_call(
        _in_proj_kernel,
        grid=(REST_PROJ // tn, t // tm_rest),
        in_specs=[pl.BlockSpec((tm_rest, k), lambda j, m: (m, 0)),
                  pl.BlockSpec((tn, k), lambda j, m: (wblock(j), 0))],
        out_specs=pl.BlockSpec((tm_rest, tn), lambda j, m: (m, j)),
        out_shape=jax.ShapeDtypeStruct((t, REST_PROJ), BF16),
        scratch_shapes=[pltpu.VMEM((tn, k), BF16)],
        compiler_params=_params(2),
        name="in_proj",
    )(xn, wt)
    return q_head, rest, dtt


def _rope_tables(seq):
    half = ROT_DIM // 2
    inv = 1.0 / (ROPE_THETA ** (np.arange(0, ROT_DIM, 2, dtype=np.float64) / ROT_DIM))
    ang = np.arange(seq, dtype=np.float64)[:, None] * inv[None, :]
    cos, sin = np.cos(ang), np.sin(ang)
    d = np.arange(LANES) % HEAD_DIM
    idx = d % half
    c = np.where((d < ROT_DIM)[None, :], cos[:, idx], 1.0)
    s1 = np.where((d < half)[None, :], -sin[:, idx], 0.0)
    s2 = np.where(((d >= half) & (d < ROT_DIM))[None, :], sin[:, idx], 0.0)
    return jnp.asarray(np.stack([c, s1, s2]).astype(np.float32))


def _rope(x, c, s1, s2):
    half = ROT_DIM // 2
    return x * c + pltpu.roll(x, LANES - half, 1) * s1 + pltpu.roll(x, half, 1) * s2


def _attn_kernel(sinks_ref, qa_ref, qb_ref, kp_ref, kc_ref, vp_ref, vc_ref,
                 tc_ref, tp_ref, g_ref, wo_ref, o_ref, wob_ref,
                 lhs_ref, kb_ref, vb_ref, s_ref, p_ref, o2_ref, m_ref, tq_ref, bias_ref, acc_ref, *, qb):
    blk = ATTN_BLOCK
    n = pl.program_id(1)
    lo = lax.broadcasted_iota(jnp.int32, (blk, LANES), 1) < HALF

    wob_ref[...] = wo_ref[...].astype(BF16)

    scale = HEAD_DIM ** -0.5 * LOG2E
    for i in range(3):
        tq_ref[i] = tc_ref[i] * scale
    qi = lax.broadcasted_iota(jnp.int32, (blk, 2 * blk), 0)
    kj = lax.broadcasted_iota(jnp.int32, (blk, 2 * blk), 1)
    rel = qi + blk - kj
    band = (rel >= 0) & (rel < WINDOW)
    kmin = jnp.where(n > 0, 0, blk)
    bias_ref[0] = jnp.where(band & (kj >= kmin), 0.0, NEG)
    bias_ref[1] = jnp.where(band, 0.0, NEG)

    for j in range(qb):
        rows = slice(j * blk, (j + 1) * blk)
        for col in range(ATTN_WIDTH // LANES):
            h, jj = col // 2, col % 2
            q_ref, qc = (qa_ref, col) if col < Q_HEAD // LANES else (qb_ref, col - Q_HEAD // LANES)
            q2 = _rope(q_ref[rows, qc * LANES:(qc + 1) * LANES].astype(F32),
                       tq_ref[0, rows, :], tq_ref[1, rows, :], tq_ref[2, rows, :])
            lhs_ref[j, h, (2 * jj) * blk:(2 * jj + 1) * blk, :] = jnp.where(lo, q2, 0.0).astype(BF16)
            lhs_ref[j, h, (2 * jj + 1) * blk:(2 * jj + 2) * blk, :] = jnp.where(lo, 0.0, q2).astype(BF16)
    nk = (qb + 1) * blk
    lok = lax.broadcasted_iota(jnp.int32, (nk, LANES), 1) < HALF
    ones = jnp.ones((nk, LANES), BF16)
    for i in range(N_KV_HEADS // 2):
        cols = slice(i * LANES, (i + 1) * LANES)
        kcat = jnp.concatenate([_rope(kp_ref[:, cols].astype(F32), tp_ref[0], tp_ref[1], tp_ref[2]),
                                _rope(kc_ref[:, cols].astype(F32), tc_ref[0], tc_ref[1], tc_ref[2])], axis=0)
        vcat = jnp.concatenate([vp_ref[:, cols], vc_ref[:, cols]], axis=0).astype(F32)
        kswp = pltpu.roll(kcat, HALF, 1)
        vswp = pltpu.roll(vcat, HALF, 1)
        kb_ref[2 * i] = jnp.where(lok, kcat, kswp).astype(BF16)
        kb_ref[2 * i + 1] = jnp.where(lok, kswp, kcat).astype(BF16)
        vb_ref[2 * i, :, 0:LANES] = jnp.where(lok, vcat, vswp).astype(BF16)
        vb_ref[2 * i + 1, :, 0:LANES] = jnp.where(lok, vswp, vcat).astype(BF16)
        vb_ref[2 * i, :, LANES:2 * LANES] = ones
        vb_ref[2 * i + 1, :, LANES:2 * LANES] = ones

    def keys(j):
        return slice(j * blk, (j + 2) * blk)

    def scores(j, h):
        s_ref[j, h % 2] = lax.dot_general(lhs_ref[j, h], kb_ref[h, keys(j), :], (((1,), (1,)), ((), ())),
                                          preferred_element_type=F32)

    def row_max(j, h):
        slot = h % 2
        for r in range(Q_PER_KV):
            rows = slice(r * blk, (r + 1) * blk)
            sink = sinks_ref[Q_PER_KV * h + r] * LOG2E
            sr = s_ref[j, slot, rows, :] + bias_ref[min(j, 1)]
            s_ref[j, slot, rows, :] = sr
            m = jnp.maximum(jnp.max(sr, axis=-1, keepdims=True), sink)
            m_ref[j, slot, rows, :] = jnp.broadcast_to(m, (blk, LANES))

    def probs(j, h):
        slot = h % 2
        for r in range(Q_PER_KV):
            rows = slice(r * blk, (r + 1) * blk)
            m = m_ref[j, slot, rows, :]
            for half in range(2):
                ln = slice(half * LANES, (half + 1) * LANES)
                p_ref[j, slot, rows, ln] = jnp.exp2(s_ref[j, slot, rows, ln] - m).astype(BF16)

    def weighted(j, h):
        o2_ref[j, h % 2] = jnp.dot(p_ref[j, h % 2], vb_ref[h, keys(j), :],
                                   preferred_element_type=F32)

    def finish(j, h):
        slot = h % 2
        outs = []
        for r in range(Q_PER_KV):
            rows = slice(r * blk, (r + 1) * blk)
            sink = sinks_ref[Q_PER_KV * h + r] * LOG2E
            den = o2_ref[j, slot, rows, LANES:2 * LANES] + jnp.exp2(sink - m_ref[j, slot, rows, :])
            outs.append(o2_ref[j, slot, rows, 0:LANES] * (1.0 / den))
        for jj in range(2):
            col = 2 * h + jj
            acc_ref[j * blk:(j + 1) * blk, col * LANES:(col + 1) * LANES] = jnp.where(
                lo, outs[2 * jj], outs[2 * jj + 1])

    for j in range(qb):
        scores(j, 0)
    for h in range(N_KV_HEADS + 1):
        for stage in (scores, row_max, probs, weighted, finish):
            hh = {scores: h + 1, finish: h - 1}.get(stage, h)
            if 0 <= hh < N_KV_HEADS:
                for j in range(qb):
                    stage(j, hh)

    o_ref[...] = _rmsnorm_rows(acc_ref[...], g_ref[...]).astype(o_ref.dtype)


def _attention(q_head, proj, sinks, g, tables, w_out, batch, seq, qb):
    t = proj.shape[0]
    blk = ATTN_BLOCK
    assert seq % (qb * blk) == 0
    nb = seq // blk
    ns = nb // qb
    ko, no = w_out.shape
    slab = ko // (batch * ns)
    assert slab * batch * ns == ko and slab % 16 == 0
    kblk = K_OFF // KV_WIDTH
    vblk = V_OFF // KV_WIDTH

    def cur(b, n):
        return (b * ns + n, 0)

    def prev_rows(b, n):
        return jnp.maximum(b * nb + n * qb - 1, 0)

    return pl.pallas_call(
        functools.partial(_attn_kernel, qb=qb),
        grid=(batch, ns),
        in_specs=[pl.BlockSpec(memory_space=pltpu.SMEM),
                  pl.BlockSpec((qb * blk, Q_HEAD), cur),
                  pl.BlockSpec((qb * blk, ATTN_WIDTH - Q_HEAD),
                               lambda b, n: (b * ns + n, Q_REST_OFF // (ATTN_WIDTH - Q_HEAD))),
                  pl.BlockSpec((blk, KV_WIDTH), lambda b, n: (prev_rows(b, n), kblk)),
                  pl.BlockSpec((qb * blk, KV_WIDTH), lambda b, n: (b * ns + n, kblk)),
                  pl.BlockSpec((blk, KV_WIDTH), lambda b, n: (prev_rows(b, n), vblk)),
                  pl.BlockSpec((qb * blk, KV_WIDTH), lambda b, n: (b * ns + n, vblk)),
                  pl.BlockSpec((3, qb * blk, LANES), lambda b, n: (0, n, 0)),
                  pl.BlockSpec((3, blk, LANES), lambda b, n: (0, jnp.maximum(n * qb - 1, 0), 0)),
                  pl.BlockSpec((1, ATTN_WIDTH), lambda b, n: (0, 0)),
                  pl.BlockSpec((slab, no), cur)],
        out_specs=[pl.BlockSpec((qb * blk, ATTN_WIDTH), cur),
                   pl.BlockSpec((slab, no), cur)],
        out_shape=[jax.ShapeDtypeStruct((t, ATTN_WIDTH), BF16),
                   jax.ShapeDtypeStruct((ko, no), BF16)],
        scratch_shapes=[pltpu.VMEM((qb, N_KV_HEADS, Q_PER_KV * blk, LANES), BF16),
                        pltpu.VMEM((N_KV_HEADS, (qb + 1) * blk, LANES), BF16),
                        pltpu.VMEM((N_KV_HEADS, (qb + 1) * blk, 2 * LANES), BF16),
                        pltpu.VMEM((qb, 2, Q_PER_KV * blk, 2 * blk), F32),
                        pltpu.VMEM((qb, 2, Q_PER_KV * blk, 2 * blk), BF16),
                        pltpu.VMEM((qb, 2, Q_PER_KV * blk, 2 * LANES), F32),
                        pltpu.VMEM((qb, 2, Q_PER_KV * blk, LANES), F32),
                        pltpu.VMEM((3, qb * blk, LANES), F32),
                        pltpu.VMEM((2, blk, 2 * blk), F32),
                        pltpu.VMEM((qb * blk, ATTN_WIDTH), F32)],
        compiler_params=_params(2),
        name="swa_attention",
    )(sinks, q_head, proj, proj, proj, proj, proj, tables, tables, g, w_out)


def _split3(x):
    h = x.astype(BF16)
    r = x - h.astype(F32)
    m = r.astype(BF16)
    l = (r - m.astype(F32)).astype(BF16)
    return h, m, l


def _dot3_lhs(x, w):
    h, m, l = _split3(x)
    d = lambda a: jnp.dot(a, w, preferred_element_type=F32)
    return (d(l) + d(m)) + d(h)


def _ssd_kernel(xbc_ref, z_ref, dtt_ref, cw_ref, cb_ref, bcol_ref, alcol_ref, e_ref, dfull_ref, gn_ref, o_ref,
                ext_ref, xact_ref, state_ref, xdt_ref, xdec_ref, y_ref, exp_ref, acs_ref, acst_ref, split_ref,
                *, nbat):
    L = SSD_CHUNK
    c = pl.program_id(0)
    cw_chunk = 512
    gw = SSD_HEAD_DIM * (SSD_HEADS // SSD_GROUPS)
    ri = lax.broadcasted_iota(jnp.int32, (L, L), 0)
    ci = lax.broadcasted_iota(jnp.int32, (L, L), 1)
    causal = ri >= ci
    lo = ci < HALF

    @pl.when(c == 0)
    def _():
        ext_ref[:, 0:SUBLANES, :] = jnp.zeros((nbat, SUBLANES, CONV_CH), F32)
        state_ref[...] = jnp.zeros_like(state_ref)

    def load_x(b):
        ext_ref[b, SUBLANES:SUBLANES + L, :] = xbc_ref[b].astype(F32)

    def conv(b, j):
        cs = slice(j * cw_chunk, (j + 1) * cw_chunk)
        blk = ext_ref[b, :, cs]
        acc = cb_ref[:, cs]
        for k in range(SSD_CONV - 1):
            acc = acc + _delay_rows(blk, SSD_CONV - 1 - k) * cw_ref[k:k + 1, cs]
        acc = acc + blk[SUBLANES:] * cw_ref[SSD_CONV - 1:SSD_CONV, cs]
        xact_ref[b, :, cs] = acc * _sigmoid(acc)

    def keep_tail(b):
        ext_ref[b, 0:SUBLANES, :] = ext_ref[b, L:L + SUBLANES, :]

    def factors(b):
        nh = SSD_HEADS
        dtt = _softplus(dtt_ref[b, 0:nh, :] + bcol_ref[0:nh, :])
        dat = dtt * (-jnp.exp(alcol_ref[0:nh, :]))
        tri_u = jnp.where(ri <= ci, 1.0, 0.0).astype(BF16)
        acst = _dot3_lhs(dat, tri_u) * LOG2E
        acst_ref[b, 0:nh, :] = acst
        pad = jnp.zeros((LANES - nh, L), F32)
        a_cs = jnp.concatenate([acst, pad], axis=0).T
        dt = jnp.concatenate([dtt, pad], axis=0).T
        acs_ref[b] = a_cs
        a_last = a_cs[L - 1:L, :]
        stack = jnp.concatenate([dt, jnp.exp2(a_last - a_cs), jnp.exp2(a_cs),
                                 jnp.broadcast_to(jnp.exp2(a_last), (SUBLANES, LANES))], axis=0)
        hi = stack.astype(BF16)
        split_ref[b, :, 0:LANES] = hi
        split_ref[b, :, LANES:2 * LANES] = (stack - hi.astype(F32)).astype(BF16)

    def expand(b, j):
        cs = slice(j * cw_chunk, (j + 1) * cw_chunk)
        exp_ref[b, :, cs] = jnp.dot(split_ref[b], e_ref[:, cs], preferred_element_type=F32)

    def scale_x(b):
        for j in range(SSD_INNER // cw_chunk):
            cs = slice(j * cw_chunk, (j + 1) * cw_chunk)
            xdt = xact_ref[b, :, cs] * exp_ref[b, 0:L, cs]
            xdt_ref[b, :, cs] = xdt.astype(BF16)
            xdec_ref[b, :, cs] = (xdt * exp_ref[b, L:2 * L, cs]).astype(BF16)

    def group(b, g):
        bg = xact_ref[b, :, SSD_INNER + g * SSD_STATE:SSD_INNER + (g + 1) * SSD_STATE]
        cg = xact_ref[b, :, SSD_INNER + BC_WIDTH + g * SSD_STATE:SSD_INNER + BC_WIDTH + (g + 1) * SSD_STATE]
        bb = bg.astype(BF16)
        cbf = cg.astype(BF16)
        cbm = lax.dot_general(cbf, bb, (((1,), (1,)), ((), ())), preferred_element_type=F32)
        gs = slice(g * gw, (g + 1) * gw)
        prev = state_ref[b, g]
        yoff = jnp.dot(cbf, prev.astype(BF16), preferred_element_type=F32) * exp_ref[b, 2 * L:3 * L, gs]
        btb = bg.T.astype(BF16)
        state_ref[b, g] = prev * exp_ref[b, 3 * L:3 * L + 1, gs] + jnp.dot(
            btb, xdec_ref[b, :, gs], preferred_element_type=F32)
        for e in range(2):
            pc = slice((2 * g + e) * LANES, (2 * g + e + 1) * LANES)
            xpair = xdt_ref[b, :, pc]
            yd = []
            for r in range(2):
                h = 4 * g + 2 * e + r
                seg = acs_ref[b, :, h:h + 1] - acst_ref[b, h:h + 1, :]
                lm = jnp.exp2(jnp.where(causal, seg, NEG))
                yd.append(jnp.dot((cbm * lm).astype(BF16), xpair, preferred_element_type=F32))
            y_ref[b, :, pc] = (jnp.where(lo, yd[0], yd[1]) + yoff[:, e * LANES:(e + 1) * LANES]
                               + dfull_ref[:, pc] * xact_ref[b, :, pc])

    def gate_norm(b, g):
        gs = slice(g * gw, (g + 1) * gw)
        zz = z_ref[b, :, gs].astype(F32)
        yg = y_ref[b, :, gs] * (zz * _sigmoid(zz))
        ms = jnp.mean(yg * yg, axis=-1, keepdims=True)
        o_ref[b, :, gs] = (yg * lax.rsqrt(ms + EPS) * gn_ref[:, gs]).astype(o_ref.dtype)

    for b in range(nbat):
        load_x(b)
    pieces = [functools.partial(factors, b) for b in range(nbat)]
    pieces += [functools.partial(expand, b, j) for b in range(nbat) for j in range(SSD_INNER // cw_chunk)]
    chunks = [functools.partial(conv, b, j) for b in range(nbat) for j in range(CONV_CH // cw_chunk)]
    for idx in range(max(len(pieces), len(chunks))):
        if idx < len(pieces):
            pieces[idx]()
        if idx < len(chunks):
            chunks[idx]()
    for b in range(nbat):
        keep_tail(b)
        scale_x(b)
    for stage in (group, gate_norm):
        for g in range(SSD_GROUPS):
            for b in range(nbat):
                stage(b, g)


def _ssd(proj, dt_rawt, cw, cb, bcol, alcol, emat, dfull, gn, batch, seq):
    L = SSD_CHUNK
    nc = seq // L
    proj3 = proj.reshape(batch, seq, REST_PROJ)
    full = lambda shape: pl.BlockSpec(shape, lambda c: (0, 0))
    out = pl.pallas_call(
        functools.partial(_ssd_kernel, nbat=batch),
        grid=(nc,),
        in_specs=[pl.BlockSpec((batch, L, CONV_CH), lambda c: (0, c, XBC_OFF // CONV_CH)),
                  pl.BlockSpec((batch, L, SSD_INNER), lambda c: (0, c, Z_OFF // SSD_INNER)),
                  pl.BlockSpec((batch, LANES, L), lambda c: (0, 0, c)),
                  full((SSD_CONV, CONV_CH)), full((1, CONV_CH)),
                  full((LANES, 1)), full((LANES, 1)),
                  full((2 * LANES, SSD_INNER)), full((1, SSD_INNER)), full((1, SSD_INNER))],
        out_specs=pl.BlockSpec((batch, L, SSD_INNER), lambda c: (0, c, 0)),
        out_shape=jax.ShapeDtypeStruct((batch, seq, SSD_INNER), BF16),
        scratch_shapes=[pltpu.VMEM((batch, L + SUBLANES, CONV_CH), F32),
                        pltpu.VMEM((batch, L, CONV_CH), F32),
                        pltpu.VMEM((batch, SSD_GROUPS, SSD_STATE, SSD_INNER // SSD_GROUPS), F32),
                        pltpu.VMEM((batch, L, SSD_INNER), BF16),
                        pltpu.VMEM((batch, L, SSD_INNER), BF16),
                        pltpu.VMEM((batch, L, SSD_INNER), F32),
                        pltpu.VMEM((batch, 3 * L + SUBLANES, SSD_INNER), F32),
                        pltpu.VMEM((batch, L, LANES), F32),
                        pltpu.VMEM((batch, LANES, L), F32),
                        pltpu.VMEM((batch, 3 * L + SUBLANES, 2 * LANES), BF16)],
        compiler_params=_params(1),
        name="ssd_scan",
    )(proj3, proj3, dt_rawt, cw, cb, bcol, alcol, emat, dfull, gn)
    return out.reshape(batch * seq, SSD_INNER)


def _out_proj_kernel(a1_ref, a2_ref, w1_ref, w2_ref, r_ref, g_ref, h_ref, hn_ref):
    acc = jnp.dot(a1_ref[...], w1_ref[...], preferred_element_type=F32)
    acc = acc + jnp.dot(a2_ref[...], w2_ref[...], preferred_element_type=F32)
    h = r_ref[...] + acc
    h_ref[...] = h
    hn_ref[...] = _rmsnorm_rows(h, g_ref[...]).astype(hn_ref.dtype)


def _out_proj(a1, a2, w_bf16, res, g, tm):
    t, k = a1.shape
    n = w_bf16.shape[1]
    row = lambda shape: pl.BlockSpec(shape, lambda m: (m, 0))
    wspec = lambda i: pl.BlockSpec((k, n), lambda m: (i, 0), pipeline_mode=pl.Buffered(1))
    return pl.pallas_call(
        _out_proj_kernel,
        grid=(t // tm,),
        in_specs=[row((tm, k)), row((tm, k)), wspec(0), wspec(1), row((tm, n)),
                  pl.BlockSpec((1, n), lambda m: (0, 0))],
        out_specs=[row((tm, n)), row((tm, n))],
        out_shape=[jax.ShapeDtypeStruct((t, n), F32), jax.ShapeDtypeStruct((t, n), BF16)],
        compiler_params=_params(1),
        name="out_proj_norm",
    )(a1, a2, w_bf16, w_bf16, res, g)


def _up_kernel(a_ref, wg_ref, wv_ref, cwg_ref, cwv_ref, cbg_ref, cbv_ref, wd_ref, o_ref, wdb_ref,
               wgb_ref, wvb_ref, carry_ref, *, tm, seq):
    m = pl.program_id(1)
    wdb_ref[...] = wd_ref[...].astype(BF16)

    @pl.when(m == 0)
    def _():
        _cast_rows(wg_ref, wgb_ref)
        _cast_rows(wv_ref, wvb_ref)

    a = a_ref[...]
    seq_start = (m * tm) % seq == 0
    outs = []
    for idx, (wb_ref, cw_ref, cb_ref) in enumerate(((wgb_ref, cwg_ref, cbg_ref),
                                                    (wvb_ref, cwv_ref, cbv_ref))):
        u = jnp.dot(a, wb_ref[...], preferred_element_type=F32)
        above = jnp.where(seq_start, 0.0, carry_ref[idx])
        blk = jnp.concatenate([above, u], axis=0)
        cw = cw_ref[...]
        conv = cb_ref[...]
        for k in range(FFN_CONV - 1):
            conv = conv + _delay_rows(blk, FFN_CONV - 1 - k) * cw[k:k + 1, :]
        conv = conv + u * cw[FFN_CONV - 1:FFN_CONV, :]
        carry_ref[idx] = u[tm - SUBLANES:tm]
        outs.append(conv)
    gate, val = outs
    o_ref[...] = ((gate * _sigmoid(gate)) * val).astype(o_ref.dtype)


def _up_proj(a, w, cw, cb, w_down, tm, tn, seq):
    t, k = a.shape
    assert seq % tm == 0 and D_FF % tn == 0
    nb = D_FF // tn
    n_m = t // tm
    kd, nd = w_down.shape
    slab = kd // (nb * n_m)
    assert slab * nb * n_m == kd and slab % 16 == 0
    return pl.pallas_call(
        functools.partial(_up_kernel, tm=tm, seq=seq),
        grid=(nb, t // tm),
        in_specs=[pl.BlockSpec((tm, k), lambda j, m: (m, 0)),
                  pl.BlockSpec((k, tn), lambda j, m: (0, j)),
                  pl.BlockSpec((k, tn), lambda j, m: (0, j + nb)),
                  pl.BlockSpec((FFN_CONV, tn), lambda j, m: (0, j)),
                  pl.BlockSpec((FFN_CONV, tn), lambda j, m: (0, j + nb)),
                  pl.BlockSpec((1, tn), lambda j, m: (0, j)),
                  pl.BlockSpec((1, tn), lambda j, m: (0, j + nb)),
                  pl.BlockSpec((slab, nd), lambda j, m: (j * n_m + m, 0))],
        out_specs=[pl.BlockSpec((tm, tn), lambda j, m: (m, j)),
                   pl.BlockSpec((slab, nd), lambda j, m: (j * n_m + m, 0))],
        out_shape=[jax.ShapeDtypeStruct((t, D_FF), BF16),
                   jax.ShapeDtypeStruct((kd, nd), BF16)],
        scratch_shapes=[pltpu.VMEM((k, tn), BF16), pltpu.VMEM((k, tn), BF16),
                        pltpu.VMEM((2, SUBLANES, tn), F32)],
        compiler_params=_params(2),
        name="up_proj_conv_swiglu",
    )(a, w, w, cw, cw, cb, cb, w_down)


def _down_kernel(a_ref, w_ref, r_ref, g_ref, o_ref, *, final_norm):
    h = r_ref[...] + jnp.dot(a_ref[...], w_ref[...], preferred_element_type=F32)
    o_ref[...] = _rmsnorm_rows(h, g_ref[...]) if final_norm else h


def _down_proj(a, w_bf16, res, g, tm, final_norm):
    t, k = a.shape
    n = w_bf16.shape[1]
    row = lambda shape: pl.BlockSpec(shape, lambda m: (m, 0))
    return pl.pallas_call(
        functools.partial(_down_kernel, final_norm=final_norm),
        grid=(t // tm,),
        in_specs=[row((tm, k)),
                  pl.BlockSpec((k, n), lambda m: (0, 0), pipeline_mode=pl.Buffered(1)),
                  row((tm, n)),
                  pl.BlockSpec((1, n), lambda m: (0, 0))],
        out_specs=row((tm, n)),
        out_shape=jax.ShapeDtypeStruct((t, n), F32),
        compiler_params=_params(1),
        name="down_proj_norm" if final_norm else "down_proj",
    )(a, w_bf16, res, g)


def _pad_lanes(v):
    return jnp.pad(v.astype(F32), (0, LANES - v.shape[0]))


def _mixer_layer(h, batch, seq, norm_mix, w_in, sinks, attn_out_norm, ssd_conv_w, ssd_conv_b, dt_bias,
                 a_log, ssd_d, ssd_norm, w_out, norm_ffn, tables, emat):
    w_in_t = jnp.swapaxes(w_in, 0, 1)
    wdt_t = jnp.pad(w_in_t[MAIN_PROJ:], ((0, LANES - SSD_HEADS), (0, 0))).astype(BF16)
    q_head, proj, dt_rawt = _in_proj(h, norm_mix.reshape(1, -1), wdt_t, w_in_t, 1024, 2048, batch, seq)
    attn, w_out_bf16 = _attention(q_head, proj, sinks.astype(F32), attn_out_norm.reshape(1, -1), tables, w_out,
                                  batch, seq, qb=4)
    bias = _pad_lanes(dt_bias)
    alog = _pad_lanes(a_log)
    y = _ssd(proj, dt_rawt, ssd_conv_w, ssd_conv_b.reshape(1, -1), bias.reshape(-1, 1), alog.reshape(-1, 1),
             emat, jnp.repeat(ssd_d.astype(F32), SSD_HEAD_DIM).reshape(1, -1),
             ssd_norm.reshape(1, -1), batch, seq)
    return _out_proj(attn, y, w_out_bf16, h, norm_ffn.reshape(1, -1), tm=512)


def kernel(x, norm_mix, w_in, sinks, attn_out_norm, ssd_conv_w, ssd_conv_b, dt_bias, a_log, ssd_d, ssd_norm,
           w_out, norm_ffn, w_up, ffn_conv_w, ffn_conv_b, w_down, norm_final):
    batch, seq, d = x.shape
    h = x.reshape(batch * seq, d)
    tables = _rope_tables(seq)
    head_of_channel = np.arange(SSD_INNER) // SSD_HEAD_DIM
    emat = np.arange(LANES)[:, None] == head_of_channel[None, :]
    emat = jnp.asarray(np.concatenate([emat, emat], axis=0), dtype=BF16)
    for l in range(norm_mix.shape[0]):
        h, hn = _mixer_layer(h, batch, seq, norm_mix[l], w_in[l], sinks[l], attn_out_norm[l], ssd_conv_w[l],
                             ssd_conv_b[l], dt_bias[l], a_log[l], ssd_d[l], ssd_norm[l], w_out[l], norm_ffn[l],
                             tables, emat)
        act, w_down_bf16 = _up_proj(hn, w_up[l], ffn_conv_w[l], ffn_conv_b[l].reshape(1, -1), w_down[l],
                                    tm=1024, tn=512, seq=seq)
        h = _down_proj(act, w_down_bf16, h, norm_final.reshape(1, -1), tm=512,
                       final_norm=(l == norm_mix.shape[0] - 1))
    return h.reshape(batch, seq, d)
```

```python
import functools

import numpy as np
import jax
import jax.numpy as jnp
from jax import lax
from jax.experimental import pallas as pl
from jax.experimental.pallas import tpu as pltpu

F32 = jnp.float32
BF16 = jnp.bfloat16

D_MODEL = 2048
N_Q_HEADS = 32
N_KV_HEADS = 8
HEAD_DIM = 64
Q_PER_KV = N_Q_HEADS // N_KV_HEADS
WINDOW = 128
ATTN_BLOCK = 128
ROT_DIM = HEAD_DIM // 4
ROPE_THETA = 500000.0
SSD_HEADS = 32
SSD_HEAD_DIM = 64
SSD_INNER = SSD_HEADS * SSD_HEAD_DIM
SSD_GROUPS = 8
SSD_STATE = 128
SSD_CONV = 4
SSD_CHUNK = 128
ATTN_WIDTH = N_Q_HEADS * HEAD_DIM
KV_WIDTH = N_KV_HEADS * HEAD_DIM
BC_WIDTH = SSD_GROUPS * SSD_STATE
CONV_CH = SSD_INNER + 2 * BC_WIDTH
MAIN_PROJ = ATTN_WIDTH + 2 * KV_WIDTH + SSD_INNER + CONV_CH
D_FF = 5632
FFN_CONV = 3
EPS = 1e-6

LANES = 128
SUBLANES = 8
HALF = LANES // 2
NEG = -1e30
LOG2E = 1.4426950408889634
VMEM_LIMIT = 56 * 1024 * 1024

Q_HEAD = 1024
REST_PROJ = MAIN_PROJ - Q_HEAD - CONV_CH
Z_OFF = 0
Q_REST_OFF = Z_OFF + SSD_INNER
K_OFF = Q_REST_OFF + (ATTN_WIDTH - Q_HEAD)
V_OFF = K_OFF + KV_WIDTH


def _params(n_axes, flags=None):
    return pltpu.CompilerParams(dimension_semantics=("arbitrary",) * n_axes,
                                vmem_limit_bytes=VMEM_LIMIT, flags=flags)


def _sigmoid(x):
    return 1.0 / (1.0 + jnp.exp2(x * -LOG2E))


def _softplus(x):
    return jnp.maximum(x, 0.0) + jnp.log1p(jnp.exp(-jnp.abs(x)))


def _cast_rows(src_ref, dst_ref, rows=256):
    k = src_ref.shape[0]
    for r in range(0, k, rows):
        dst_ref[r:r + rows, :] = src_ref[r:r + rows, :].astype(BF16)


def _delay_rows(blk, sh):
    n, w = blk.shape[0] - SUBLANES, blk.shape[1]
    g = blk.reshape(n // SUBLANES + 1, SUBLANES, w)
    rot = pltpu.roll(g, sh, 1)
    row = lax.broadcasted_iota(jnp.int32, (SUBLANES, w), 0)
    out = jnp.where(row < sh, rot[:-1], rot[1:])
    return out.reshape(n, w)


def _rmsnorm_rows(x, g):
    ms = jnp.mean(x * x, axis=-1, keepdims=True)
    return x * lax.rsqrt(ms + EPS) * g


def _in_proj_head_kernel(x_ref, g_ref, wdtt_ref, wt_ref, xn_ref, dtt_ref, o_ref, wbf_ref):
    @pl.when(pl.program_id(0) == 0)
    def _():
        _cast_rows(wt_ref, wbf_ref)

    xn = _rmsnorm_rows(x_ref[...], g_ref[...]).astype(BF16)
    xn_ref[...] = xn
    dtt_ref[...] = lax.dot_general(wdtt_ref[...], xn, (((1,), (1,)), ((), ())),
                                   preferred_element_type=F32)
    o_ref[...] = lax.dot_general(xn, wbf_ref[...], (((1,), (1,)), ((), ())),
                                 preferred_element_type=F32).astype(o_ref.dtype)


def _in_proj_kernel(a_ref, wt_ref, o_ref, wbf_ref):
    @pl.when(pl.program_id(1) == 0)
    def _():
        _cast_rows(wt_ref, wbf_ref)

    o_ref[...] = lax.dot_general(a_ref[...], wbf_ref[...], (((1,), (1,)), ((), ())),
                                 preferred_element_type=F32).astype(o_ref.dtype)


def _in_proj_conv_kernel(a_ref, wt_ref, cw_ref, cb_ref, o_ref, wbf_ref, carry_ref, *, tm, seq):
    m = pl.program_id(1)

    @pl.when(m == 0)
    def _():
        _cast_rows(wt_ref, wbf_ref)

    u = lax.dot_general(a_ref[...], wbf_ref[...], (((1,), (1,)), ((), ())), preferred_element_type=F32)
    above = jnp.where((m * tm) % seq == 0, 0.0, carry_ref[...])
    blk = jnp.concatenate([above, u], axis=0)
    cw = cw_ref[...]
    acc = cb_ref[...]
    for k in range(SSD_CONV - 1):
        acc = acc + _delay_rows(blk, SSD_CONV - 1 - k) * cw[k:k + 1, :]
    acc = acc + u * cw[SSD_CONV - 1:SSD_CONV, :]
    carry_ref[...] = u[tm - SUBLANES:tm]
    o_ref[...] = (acc * _sigmoid(acc)).astype(o_ref.dtype)


def _in_proj(x2, g, wdtt, wt, cw, cb, tm, tm_rest, batch, seq):
    t, k = x2.shape
    tn = Q_HEAD
    assert seq % tm == 0
    per_seq = seq // tm

    xn, dtt, q_head = pl.pallas_call(
        _in_proj_head_kernel,
        grid=(t // tm,),
        in_specs=[pl.BlockSpec((tm, k), lambda m: (m, 0)),
                  pl.BlockSpec((1, k), lambda m: (0, 0)),
                  pl.BlockSpec((LANES, k), lambda m: (0, 0)),
                  pl.BlockSpec((tn, k), lambda m: (0, 0))],
        out_specs=[pl.BlockSpec((tm, k), lambda m: (m, 0)),
                   pl.BlockSpec((None, LANES, tm), lambda m: (m // per_seq, 0, m % per_seq)),
                   pl.BlockSpec((tm, tn), lambda m: (m, 0))],
        out_shape=[jax.ShapeDtypeStruct((t, k), BF16),
                   jax.ShapeDtypeStruct((batch, LANES, seq), F32),
                   jax.ShapeDtypeStruct((t, tn), BF16)],
        scratch_shapes=[pltpu.VMEM((tn, k), BF16)],
        compiler_params=_params(1),
        name="norm_in_proj_head",
    )(x2, g, wdtt, wt)

    xbc_first = (MAIN_PROJ - CONV_CH) // tn
    xbc_act = pl.pallas_call(
        functools.partial(_in_proj_conv_kernel, tm=tm, seq=seq),
        grid=(CONV_CH // tn, t // tm),
        in_specs=[pl.BlockSpec((tm, k), lambda j, m: (m, 0)),
                  pl.BlockSpec((tn, k), lambda j, m: (j + xbc_first, 0)),
                  pl.BlockSpec((SSD_CONV, tn), lambda j, m: (0, j)),
                  pl.BlockSpec((1, tn), lambda j, m: (0, j))],
        out_specs=pl.BlockSpec((tm, tn), lambda j, m: (m, j)),
        out_shape=jax.ShapeDtypeStruct((t, CONV_CH), BF16),
        scratch_shapes=[pltpu.VMEM((tn, k), BF16), pltpu.VMEM((SUBLANES, tn), F32)],
        compiler_params=_params(2),
        name="in_proj_conv",
    )(xn, wt, cw, cb)

    def wblock(j):
        return jnp.where(j < 2, j + 3, j - 1)

    rest = pl.pallas_call(
        _in_proj_kernel,
        grid=(REST_PROJ // tn, t // tm_rest),
        in_specs=[pl.BlockSpec((tm_rest, k), lambda j, m: (m, 0)),
                  pl.BlockSpec((tn, k), lambda j, m: (wblock(j), 0))],
        out_specs=pl.BlockSpec((tm_rest, tn), lambda j, m: (m, j)),
        out_shape=jax.ShapeDtypeStruct((t, REST_PROJ), BF16),
        scratch_shapes=[pltpu.VMEM((tn, k), BF16)],
        compiler_params=_params(2),
        name="in_proj",
    )(xn, wt)
    return q_head, xbc_act, rest, dtt


def _rope_tables(seq):
    half = ROT_DIM // 2
    inv = 1.0 / (ROPE_THETA ** (np.arange(0, ROT_DIM, 2, dtype=np.float64) / ROT_DIM))
    ang = np.arange(seq, dtype=np.float64)[:, None] * inv[None, :]
    cos, sin = np.cos(ang), np.sin(ang)
    d = np.arange(LANES) % HEAD_DIM
    idx = d % half
    c = np.where((d < ROT_DIM)[None, :], cos[:, idx], 1.0)
    s1 = np.where((d < half)[None, :], -sin[:, idx], 0.0)
    s2 = np.where(((d >= half) & (d < ROT_DIM))[None, :], sin[:, idx], 0.0)
    return jnp.asarray(np.stack([c, s1, s2]).astype(np.float32))


def _rope(x, c, s1, s2):
    half = ROT_DIM // 2
    return x * c + pltpu.roll(x, LANES - half, 1) * s1 + pltpu.roll(x, half, 1) * s2


def _attn_kernel(sinks_ref, qa_ref, qb_ref, kp_ref, kc_ref, vp_ref, vc_ref,
                 tc_ref, tp_ref, g_ref, wo_ref, o_ref, wob_ref,
                 lhs_ref, kb_ref, vb_ref, s_ref, p_ref, o2_ref, m_ref, tq_ref, bias_ref, acc_ref, *, qb):
    blk = ATTN_BLOCK
    n = pl.program_id(1)
    lo = lax.broadcasted_iota(jnp.int32, (blk, LANES), 1) < HALF

    wob_ref[...] = wo_ref[...].astype(BF16)

    scale = HEAD_DIM ** -0.5 * LOG2E
    for i in range(3):
        tq_ref[i] = tc_ref[i] * scale
    qi = lax.broadcasted_iota(jnp.int32, (blk, 2 * blk), 0)
    kj = lax.broadcasted_iota(jnp.int32, (blk, 2 * blk), 1)
    rel = qi + blk - kj
    band = (rel >= 0) & (rel < WINDOW)
    kmin = jnp.where(n > 0, 0, blk)
    bias_ref[0] = jnp.where(band & (kj >= kmin), 0.0, NEG)
    bias_ref[1] = jnp.where(band, 0.0, NEG)

    for j in range(qb):
        rows = slice(j * blk, (j + 1) * blk)
        for col in range(ATTN_WIDTH // LANES):
            h, jj = col // 2, col % 2
            q_ref, qc = (qa_ref, col) if col < Q_HEAD // LANES else (qb_ref, col - Q_HEAD // LANES)
            q2 = _rope(q_ref[rows, qc * LANES:(qc + 1) * LANES].astype(F32),
                       tq_ref[0, rows, :], tq_ref[1, rows, :], tq_ref[2, rows, :])
            lhs_ref[j, h, (2 * jj) * blk:(2 * jj + 1) * blk, :] = jnp.where(lo, q2, 0.0).astype(BF16)
            lhs_ref[j, h, (2 * jj + 1) * blk:(2 * jj + 2) * blk, :] = jnp.where(lo, 0.0, q2).astype(BF16)
    nk = (qb + 1) * blk
    lok = lax.broadcasted_iota(jnp.int32, (nk, LANES), 1) < HALF
    ones = jnp.ones((nk, LANES), BF16)
    for i in range(N_KV_HEADS // 2):
        cols = slice(i * LANES, (i + 1) * LANES)
        kcat = jnp.concatenate([_rope(kp_ref[:, cols].astype(F32), tp_ref[0], tp_ref[1], tp_ref[2]),
                                _rope(kc_ref[:, cols].astype(F32), tc_ref[0], tc_ref[1], tc_ref[2])], axis=0)
        vcat = jnp.concatenate([vp_ref[:, cols], vc_ref[:, cols]], axis=0).astype(F32)
        kswp = pltpu.roll(kcat, HALF, 1)
        vswp = pltpu.roll(vcat, HALF, 1)
        kb_ref[2 * i] = jnp.where(lok, kcat, kswp).astype(BF16)
        kb_ref[2 * i + 1] = jnp.where(lok, kswp, kcat).astype(BF16)
        vb_ref[2 * i, :, 0:LANES] = jnp.where(lok, vcat, vswp).astype(BF16)
        vb_ref[2 * i + 1, :, 0:LANES] = jnp.where(lok, vswp, vcat).astype(BF16)
        vb_ref[2 * i, :, LANES:2 * LANES] = ones
        vb_ref[2 * i + 1, :, LANES:2 * LANES] = ones

    def keys(j):
        return slice(j * blk, (j + 2) * blk)

    def scores(j, h):
        s_ref[j, h % 2] = lax.dot_general(lhs_ref[j, h], kb_ref[h, keys(j), :], (((1,), (1,)), ((), ())),
                                          preferred_element_type=F32)

    def row_max(j, h):
        slot = h % 2
        for r in range(Q_PER_KV):
            rows = slice(r * blk, (r + 1) * blk)
            sink = sinks_ref[Q_PER_KV * h + r] * LOG2E
            sr = s_ref[j, slot, rows, :] + bias_ref[min(j, 1)]
            s_ref[j, slot, rows, :] = sr
            m = jnp.maximum(jnp.max(sr, axis=-1, keepdims=True), sink)
            m_ref[j, slot, rows, :] = jnp.broadcast_to(m, (blk, LANES))

    def probs(j, h):
        slot = h % 2
        for r in range(Q_PER_KV):
            rows = slice(r * blk, (r + 1) * blk)
            m = m_ref[j, slot, rows, :]
            for half in range(2):
                ln = slice(half * LANES, (half + 1) * LANES)
                p_ref[j, slot, rows, ln] = jnp.exp2(s_ref[j, slot, rows, ln] - m).astype(BF16)

    def weighted(j, h):
        o2_ref[j, h % 2] = jnp.dot(p_ref[j, h % 2], vb_ref[h, keys(j), :],
                                   preferred_element_type=F32)

    def finish(j, h):
        slot = h % 2
        outs = []
        for r in range(Q_PER_KV):
            rows = slice(r * blk, (r + 1) * blk)
            sink = sinks_ref[Q_PER_KV * h + r] * LOG2E
            den = o2_ref[j, slot, rows, LANES:2 * LANES] + jnp.exp2(sink - m_ref[j, slot, rows, :])
            outs.append(o2_ref[j, slot, rows, 0:LANES] * (1.0 / den))
        for jj in range(2):
            col = 2 * h + jj
            acc_ref[j * blk:(j + 1) * blk, col * LANES:(col + 1) * LANES] = jnp.where(
                lo, outs[2 * jj], outs[2 * jj + 1])

    for j in range(qb):
        scores(j, 0)
    for h in range(N_KV_HEADS + 1):
        for stage in (scores, row_max, probs, weighted, finish):
            hh = {scores: h + 1, finish: h - 1}.get(stage, h)
            if 0 <= hh < N_KV_HEADS:
                for j in range(qb):
                    stage(j, hh)

    o_ref[...] = _rmsnorm_rows(acc_ref[...], g_ref[...]).astype(o_ref.dtype)


def _attention(q_head, proj, sinks, g, tables, w_out, batch, seq, qb):
    t = proj.shape[0]
    blk = ATTN_BLOCK
    assert seq % (qb * blk) == 0
    nb = seq // blk
    ns = nb // qb
    ko, no = w_out.shape
    slab = ko // (batch * ns)
    assert slab * batch * ns == ko and slab % 16 == 0
    kblk = K_OFF // KV_WIDTH
    vblk = V_OFF // KV_WIDTH

    def cur(b, n):
        return (b * ns + n, 0)

    def prev_rows(b, n):
        return jnp.maximum(b * nb + n * qb - 1, 0)

    return pl.pallas_call(
        functools.partial(_attn_kernel, qb=qb),
        grid=(batch, ns),
        in_specs=[pl.BlockSpec(memory_space=pltpu.SMEM),
                  pl.BlockSpec((qb * blk, Q_HEAD), cur),
                  pl.BlockSpec((qb * blk, ATTN_WIDTH - Q_HEAD),
                               lambda b, n: (b * ns + n, Q_REST_OFF // (ATTN_WIDTH - Q_HEAD))),
                  pl.BlockSpec((blk, KV_WIDTH), lambda b, n: (prev_rows(b, n), kblk)),
                  pl.BlockSpec((qb * blk, KV_WIDTH), lambda b, n: (b * ns + n, kblk)),
                  pl.BlockSpec((blk, KV_WIDTH), lambda b, n: (prev_rows(b, n), vblk)),
                  pl.BlockSpec((qb * blk, KV_WIDTH), lambda b, n: (b * ns + n, vblk)),
                  pl.BlockSpec((3, qb * blk, LANES), lambda b, n: (0, n, 0)),
                  pl.BlockSpec((3, blk, LANES), lambda b, n: (0, jnp.maximum(n * qb - 1, 0), 0)),
                  pl.BlockSpec((1, ATTN_WIDTH), lambda b, n: (0, 0)),
                  pl.BlockSpec((slab, no), cur)],
        out_specs=[pl.BlockSpec((qb * blk, ATTN_WIDTH), cur),
                   pl.BlockSpec((slab, no), cur)],
        out_shape=[jax.ShapeDtypeStruct((t, ATTN_WIDTH), BF16),
                   jax.ShapeDtypeStruct((ko, no), BF16)],
        scratch_shapes=[pltpu.VMEM((qb, N_KV_HEADS, Q_PER_KV * blk, LANES), BF16),
                        pltpu.VMEM((N_KV_HEADS, (qb + 1) * blk, LANES), BF16),
                        pltpu.VMEM((N_KV_HEADS, (qb + 1) * blk, 2 * LANES), BF16),
                        pltpu.VMEM((qb, 2, Q_PER_KV * blk, 2 * blk), F32),
                        pltpu.VMEM((qb, 2, Q_PER_KV * blk, 2 * blk), BF16),
                        pltpu.VMEM((qb, 2, Q_PER_KV * blk, 2 * LANES), F32),
                        pltpu.VMEM((qb, 2, Q_PER_KV * blk, LANES), F32),
                        pltpu.VMEM((3, qb * blk, LANES), F32),
                        pltpu.VMEM((2, blk, 2 * blk), F32),
                        pltpu.VMEM((qb * blk, ATTN_WIDTH), F32)],
        compiler_params=_params(2),
        name="swa_attention",
    )(sinks, q_head, proj, proj, proj, proj, proj, tables, tables, g, w_out)


def _split3(x):
    h = x.astype(BF16)
    r = x - h.astype(F32)
    m = r.astype(BF16)
    l = (r - m.astype(F32)).astype(BF16)
    return h, m, l


def _dot3_lhs(x, w):
    h, m, l = _split3(x)
    d = lambda a: jnp.dot(a, w, preferred_element_type=F32)
    return (d(l) + d(m)) + d(h)


def _ssd_kernel(xbc_ref, z_ref, dtt_ref, bcol_ref, alcol_ref, e_ref, dfull_ref, gn_ref, o_ref,
                xact_ref, state_ref, xdt_ref, xdec_ref, y_ref, exp_ref, acs_ref, acst_ref, split_ref,
                *, nbat):
    L = SSD_CHUNK
    c = pl.program_id(0)
    cw_chunk = 512
    gw = SSD_HEAD_DIM * (SSD_HEADS // SSD_GROUPS)
    ri = lax.broadcasted_iota(jnp.int32, (L, L), 0)
    ci = lax.broadcasted_iota(jnp.int32, (L, L), 1)
    causal = ri >= ci
    lo = ci < HALF

    @pl.when(c == 0)
    def _():
        state_ref[...] = jnp.zeros_like(state_ref)

    def load_x(b, j):
        cs = slice(j * cw_chunk, (j + 1) * cw_chunk)
        xact_ref[b, :, cs] = xbc_ref[b, :, cs].astype(F32)

    def factors(b):
        nh = SSD_HEADS
        dtt = _softplus(dtt_ref[b, 0:nh, :] + bcol_ref[0:nh, :])
        dat = dtt * (-jnp.exp(alcol_ref[0:nh, :]))
        tri_u = jnp.where(ri <= ci, 1.0, 0.0).astype(BF16)
        acst = _dot3_lhs(dat, tri_u) * LOG2E
        acst_ref[b, 0:nh, :] = acst
        pad = jnp.zeros((LANES - nh, L), F32)
        a_cs = jnp.concatenate([acst, pad], axis=0).T
        dt = jnp.concatenate([dtt, pad], axis=0).T
        acs_ref[b] = a_cs
        a_last = a_cs[L - 1:L, :]
        stack = jnp.concatenate([dt, jnp.exp2(a_last - a_cs), jnp.exp2(a_cs),
                                 jnp.broadcast_to(jnp.exp2(a_last), (SUBLANES, LANES))], axis=0)
        hi = stack.astype(BF16)
        split_ref[b, :, 0:LANES] = hi
        split_ref[b, :, LANES:2 * LANES] = (stack - hi.astype(F32)).astype(BF16)

    def expand(b, j):
        cs = slice(j * cw_chunk, (j + 1) * cw_chunk)
        exp_ref[b, :, cs] = jnp.dot(split_ref[b], e_ref[:, cs], preferred_element_type=F32)

    def scale_x(b):
        for j in range(SSD_INNER // cw_chunk):
            cs = slice(j * cw_chunk, (j + 1) * cw_chunk)
            xdt = xact_ref[b, :, cs] * exp_ref[b, 0:L, cs]
            xdt_ref[b, :, cs] = xdt.astype(BF16)
            xdec_ref[b, :, cs] = (xdt * exp_ref[b, L:2 * L, cs]).astype(BF16)

    def group(b, g):
        bg = xact_ref[b, :, SSD_INNER + g * SSD_STATE:SSD_INNER + (g + 1) * SSD_STATE]
        cg = xact_ref[b, :, SSD_INNER + BC_WIDTH + g * SSD_STATE:SSD_INNER + BC_WIDTH + (g + 1) * SSD_STATE]
        bb = bg.astype(BF16)
        cbf = cg.astype(BF16)
        cbm = lax.dot_general(cbf, bb, (((1,), (1,)), ((), ())), preferred_element_type=F32)
        gs = slice(g * gw, (g + 1) * gw)
        prev = state_ref[b, g]
        yoff = jnp.dot(cbf, prev.astype(BF16), preferred_element_type=F32) * exp_ref[b, 2 * L:3 * L, gs]
        btb = bg.T.astype(BF16)
        state_ref[b, g] = prev * exp_ref[b, 3 * L:3 * L + 1, gs] + jnp.dot(
            btb, xdec_ref[b, :, gs], preferred_element_type=F32)
        for e in range(2):
            pc = slice((2 * g + e) * LANES, (2 * g + e + 1) * LANES)
            xpair = xdt_ref[b, :, pc]
            yd = []
            for r in range(2):
                h = 4 * g + 2 * e + r
                seg = acs_ref[b, :, h:h + 1] - acst_ref[b, h:h + 1, :]
                lm = jnp.exp2(jnp.where(causal, seg, NEG))
                yd.append(jnp.dot((cbm * lm).astype(BF16), xpair, preferred_element_type=F32))
            y_ref[b, :, pc] = (jnp.where(lo, yd[0], yd[1]) + yoff[:, e * LANES:(e + 1) * LANES]
                               + dfull_ref[:, pc] * xact_ref[b, :, pc])

    def gate_norm(b, g):
        gs = slice(g * gw, (g + 1) * gw)
        zz = z_ref[b, :, gs].astype(F32)
        yg = y_ref[b, :, gs] * (zz * _sigmoid(zz))
        ms = jnp.mean(yg * yg, axis=-1, keepdims=True)
        o_ref[b, :, gs] = (yg * lax.rsqrt(ms + EPS) * gn_ref[:, gs]).astype(o_ref.dtype)

    pieces = [functools.partial(factors, b) for b in range(nbat)]
    pieces += [functools.partial(expand, b, j) for b in range(nbat) for j in range(SSD_INNER // cw_chunk)]
    chunks = [functools.partial(load_x, b, j) for b in range(nbat) for j in range(CONV_CH // cw_chunk)]
    for idx in range(max(len(pieces), len(chunks))):
        if idx < len(pieces):
            pieces[idx]()
        if idx < len(chunks):
            chunks[idx]()
    for b in range(nbat):
        scale_x(b)
    for stage in (group, gate_norm):
        for g in range(SSD_GROUPS):
            for b in range(nbat):
                stage(b, g)


def _ssd(xbc_act, proj, dt_rawt, bcol, alcol, emat, dfull, gn, batch, seq):
    L = SSD_CHUNK
    nc = seq // L
    proj3 = proj.reshape(batch, seq, REST_PROJ)
    xbc3 = xbc_act.reshape(batch, seq, CONV_CH)
    full = lambda shape: pl.BlockSpec(shape, lambda c: (0, 0))
    out = pl.pallas_call(
        functools.partial(_ssd_kernel, nbat=batch),
        grid=(nc,),
        in_specs=[pl.BlockSpec((batch, L, CONV_CH), lambda c: (0, c, 0)),
                  pl.BlockSpec((batch, L, SSD_INNER), lambda c: (0, c, Z_OFF // SSD_INNER)),
                  pl.BlockSpec((batch, LANES, L), lambda c: (0, 0, c)),
                  full((LANES, 1)), full((LANES, 1)),
                  full((2 * LANES, SSD_INNER)), full((1, SSD_INNER)), full((1, SSD_INNER))],
        out_specs=pl.BlockSpec((batch, L, SSD_INNER), lambda c: (0, c, 0)),
        out_shape=jax.ShapeDtypeStruct((batch, seq, SSD_INNER), BF16),
        scratch_shapes=[pltpu.VMEM((batch, L, CONV_CH), F32),
                        pltpu.VMEM((batch, SSD_GROUPS, SSD_STATE, SSD_INNER // SSD_GROUPS), F32),
                        pltpu.VMEM((batch, L, SSD_INNER), BF16),
                        pltpu.VMEM((batch, L, SSD_INNER), BF16),
                        pltpu.VMEM((batch, L, SSD_INNER), F32),
                        pltpu.VMEM((batch, 3 * L + SUBLANES, SSD_INNER), F32),
                        pltpu.VMEM((batch, L, LANES), F32),
                        pltpu.VMEM((batch, LANES, L), F32),
                        pltpu.VMEM((batch, 3 * L + SUBLANES, 2 * LANES), BF16)],
        compiler_params=_params(1),
        name="ssd_scan",
    )(xbc3, proj3, dt_rawt, bcol, alcol, emat, dfull, gn)
    return out.reshape(batch * seq, SSD_INNER)


def _out_proj_kernel(a1_ref, a2_ref, w1_ref, w2_ref, r_ref, g_ref, h_ref, hn_ref):
    acc = jnp.dot(a1_ref[...], w1_ref[...], preferred_element_type=F32)
    acc = acc + jnp.dot(a2_ref[...], w2_ref[...], preferred_element_type=F32)
    h = r_ref[...] + acc
    h_ref[...] = h
    hn_ref[...] = _rmsnorm_rows(h, g_ref[...]).astype(hn_ref.dtype)


def _out_proj(a1, a2, w_bf16, res, g, tm):
    t, k = a1.shape
    n = w_bf16.shape[1]
    row = lambda shape: pl.BlockSpec(shape, lambda m: (m, 0))
    wspec = lambda i: pl.BlockSpec((k, n), lambda m: (i, 0), pipeline_mode=pl.Buffered(1))
    return pl.pallas_call(
        _out_proj_kernel,
        grid=(t // tm,),
        in_specs=[row((tm, k)), row((tm, k)), wspec(0), wspec(1), row((tm, n)),
                  pl.BlockSpec((1, n), lambda m: (0, 0))],
        out_specs=[row((tm, n)), row((tm, n))],
        out_shape=[jax.ShapeDtypeStruct((t, n), F32), jax.ShapeDtypeStruct((t, n), BF16)],
        compiler_params=_params(1),
        name="out_proj_norm",
    )(a1, a2, w_bf16, w_bf16, res, g)


def _up_kernel(a_ref, wg_ref, wv_ref, cwg_ref, cwv_ref, cbg_ref, cbv_ref, wd_ref, o_ref, wdb_ref,
               wgb_ref, wvb_ref, carry_ref, *, tm, seq):
    m = pl.program_id(1)
    wdb_ref[...] = wd_ref[...].astype(BF16)

    @pl.when(m == 0)
    def _():
        _cast_rows(wg_ref, wgb_ref)
        _cast_rows(wv_ref, wvb_ref)

    a = a_ref[...]
    seq_start = (m * tm) % seq == 0
    outs = []
    for idx, (wb_ref, cw_ref, cb_ref) in enumerate(((wgb_ref, cwg_ref, cbg_ref),
                                                    (wvb_ref, cwv_ref, cbv_ref))):
        u = jnp.dot(a, wb_ref[...], preferred_element_type=F32)
        above = jnp.where(seq_start, 0.0, carry_ref[idx])
        blk = jnp.concatenate([above, u], axis=0)
        cw = cw_ref[...]
        conv = cb_ref[...]
        for k in range(FFN_CONV - 1):
            conv = conv + _delay_rows(blk, FFN_CONV - 1 - k) * cw[k:k + 1, :]
        conv = conv + u * cw[FFN_CONV - 1:FFN_CONV, :]
        carry_ref[idx] = u[tm - SUBLANES:tm]
        outs.append(conv)
    gate, val = outs
    o_ref[...] = ((gate * _sigmoid(gate)) * val).astype(o_ref.dtype)


def _up_proj(a, w, cw, cb, w_down, tm, tn, seq):
    t, k = a.shape
    assert seq % tm == 0 and D_FF % tn == 0
    nb = D_FF // tn
    n_m = t // tm
    kd, nd = w_down.shape
    slab = kd // (nb * n_m)
    assert slab * nb * n_m == kd and slab % 16 == 0
    return pl.pallas_call(
        functools.partial(_up_kernel, tm=tm, seq=seq),
        grid=(nb, t // tm),
        in_specs=[pl.BlockSpec((tm, k), lambda j, m: (m, 0)),
                  pl.BlockSpec((k, tn), lambda j, m: (0, j)),
                  pl.BlockSpec((k, tn), lambda j, m: (0, j + nb)),
                  pl.BlockSpec((FFN_CONV, tn), lambda j, m: (0, j)),
                  pl.BlockSpec((FFN_CONV, tn), lambda j, m: (0, j + nb)),
                  pl.BlockSpec((1, tn), lambda j, m: (0, j)),
                  pl.BlockSpec((1, tn), lambda j, m: (0, j + nb)),
                  pl.BlockSpec((slab, nd), lambda j, m: (j * n_m + m, 0))],
        out_specs=[pl.BlockSpec((tm, tn), lambda j, m: (m, j)),
                   pl.BlockSpec((slab, nd), lambda j, m: (j * n_m + m, 0))],
        out_shape=[jax.ShapeDtypeStruct((t, D_FF), BF16),
                   jax.ShapeDtypeStruct((kd, nd), BF16)],
        scratch_shapes=[pltpu.VMEM((k, tn), BF16), pltpu.VMEM((k, tn), BF16),
                        pltpu.VMEM((2, SUBLANES, tn), F32)],
        compiler_params=_params(2),
        name="up_proj_conv_swiglu",
    )(a, w, w, cw, cw, cb, cb, w_down)


def _down_kernel(a_ref, w_ref, r_ref, g_ref, o_ref, *, final_norm):
    h = r_ref[...] + jnp.dot(a_ref[...], w_ref[...], preferred_element_type=F32)
    o_ref[...] = _rmsnorm_rows(h, g_ref[...]) if final_norm else h


def _down_proj(a, w_bf16, res, g, tm, final_norm):
    t, k = a.shape
    n = w_bf16.shape[1]
    row = lambda shape: pl.BlockSpec(shape, lambda m: (m, 0))
    return pl.pallas_call(
        functools.partial(_down_kernel, final_norm=final_norm),
        grid=(t // tm,),
        in_specs=[row((tm, k)),
                  pl.BlockSpec((k, n), lambda m: (0, 0), pipeline_mode=pl.Buffered(1)),
                  row((tm, n)),
                  pl.BlockSpec((1, n), lambda m: (0, 0))],
        out_specs=row((tm, n)),
        out_shape=jax.ShapeDtypeStruct((t, n), F32),
        compiler_params=_params(1),
        name="down_proj_norm" if final_norm else "down_proj",
    )(a, w_bf16, res, g)


def _pad_lanes(v):
    return jnp.pad(v.astype(F32), (0, LANES - v.shape[0]))


def _mixer_layer(h, batch, seq, norm_mix, w_in, sinks, attn_out_norm, ssd_conv_w, ssd_conv_b, dt_bias,
                 a_log, ssd_d, ssd_norm, w_out, norm_ffn, tables, emat):
    w_in_t = jnp.swapaxes(w_in, 0, 1)
    wdt_t = jnp.pad(w_in_t[MAIN_PROJ:], ((0, LANES - SSD_HEADS), (0, 0))).astype(BF16)
    q_head, xbc_act, proj, dt_rawt = _in_proj(h, norm_mix.reshape(1, -1), wdt_t, w_in_t, ssd_conv_w,
                                              ssd_conv_b.reshape(1, -1), 1024, 2048, batch, seq)
    attn, w_out_bf16 = _attention(q_head, proj, sinks.astype(F32), attn_out_norm.reshape(1, -1), tables, w_out,
                                  batch, seq, qb=4)
    bias = _pad_lanes(dt_bias)
    alog = _pad_lanes(a_log)
    y = _ssd(xbc_act, proj, dt_rawt, bias.reshape(-1, 1), alog.reshape(-1, 1),
             emat, jnp.repeat(ssd_d.astype(F32), SSD_HEAD_DIM).reshape(1, -1),
             ssd_norm.reshape(1, -1), batch, seq)
    return _out_proj(attn, y, w_out_bf16, h, norm_ffn.reshape(1, -1), tm=512)


def kernel(x, norm_mix, w_in, sinks, attn_out_norm, ssd_conv_w, ssd_conv_b, dt_bias, a_log, ssd_d, ssd_norm,
           w_out, norm_ffn, w_up, ffn_conv_w, ffn_conv_b, w_down, norm_final):
    batch, seq, d = x.shape
    h = x.reshape(batch * seq, d)
    tables = _rope_tables(seq)
    head_of_channel = np.arange(SSD_INNER) // SSD_HEAD_DIM
    emat = np.arange(LANES)[:, None] == head_of_channel[None, :]
    emat = jnp.asarray(np.concatenate([emat, emat], axis=0), dtype=BF16)
    for l in range(norm_mix.shape[0]):
        h, hn = _mixer_layer(h, batch, seq, norm_mix[l], w_in[l], sinks[l], attn_out_norm[l], ssd_conv_w[l],
                             ssd_conv_b[l], dt_bias[l], a_log[l], ssd_d[l], ssd_norm[l], w_out[l], norm_ffn[l],
                             tables, emat)
        act, w_down_bf16 = _up_proj(hn, w_up[l], ffn_conv_w[l], ffn_conv_b[l].reshape(1, -1), w_down[l],
                                    tm=1024, tn=512, seq=seq)
        h = _down_proj(act, w_down_bf16, h, norm_final.reshape(1, -1), tm=512,
                       final_norm=(l == norm_mix.shape[0] - 1))
    return h.reshape(batch, seq, d)
```

```python
import functools

import numpy as np
import jax
import jax.numpy as jnp
from jax import lax
from jax.experimental import pallas as pl
from jax.experimental.pallas import tpu as pltpu

F32 = jnp.float32
BF16 = jnp.bfloat16

D_MODEL = 2048
N_Q_HEADS = 32
N_KV_HEADS = 8
HEAD_DIM = 64
Q_PER_KV = N_Q_HEADS // N_KV_HEADS
WINDOW = 128
ATTN_BLOCK = 128
ROT_DIM = HEAD_DIM // 4
ROPE_THETA = 500000.0
SSD_HEADS = 32
SSD_HEAD_DIM = 64
SSD_INNER = SSD_HEADS * SSD_HEAD_DIM
SSD_GROUPS = 8
SSD_STATE = 128
SSD_CONV = 4
SSD_CHUNK = 128
ATTN_WIDTH = N_Q_HEADS * HEAD_DIM
KV_WIDTH = N_KV_HEADS * HEAD_DIM
BC_WIDTH = SSD_GROUPS * SSD_STATE
CONV_CH = SSD_INNER + 2 * BC_WIDTH
MAIN_PROJ = ATTN_WIDTH + 2 * KV_WIDTH + SSD_INNER + CONV_CH
D_FF = 5632
FFN_CONV = 3
EPS = 1e-6

LANES = 128
SUBLANES = 8
HALF = LANES // 2
NEG = -1e30
LOG2E = 1.4426950408889634
VMEM_LIMIT = 56 * 1024 * 1024

TM_IN_HEAD = 1024
TM_IN_REST = 2048
ATTN_QB = 4
TM_OUT = 512
TM_UP, TN_UP = 1024, 512
TM_DOWN = 512

Q_HEAD = 1024
REST_PROJ = MAIN_PROJ - Q_HEAD
XBC_OFF = 0
Z_OFF = XBC_OFF + CONV_CH
Q_REST_OFF = Z_OFF + SSD_INNER
K_OFF = Q_REST_OFF + (ATTN_WIDTH - Q_HEAD)
V_OFF = K_OFF + KV_WIDTH


def _params(n_axes):
    return pltpu.CompilerParams(dimension_semantics=("arbitrary",) * n_axes,
                                vmem_limit_bytes=VMEM_LIMIT)


def _sigmoid(x):
    return 1.0 / (1.0 + jnp.exp2(x * -LOG2E))


def _softplus(x):
    return jnp.maximum(x, 0.0) + jnp.log1p(jnp.exp(-jnp.abs(x)))


def _cast_rows(src_ref, dst_ref, rows=256):
    k = src_ref.shape[0]
    for r in range(0, k, rows):
        dst_ref[r:r + rows, :] = src_ref[r:r + rows, :].astype(BF16)


def _delay_rows(blk, sh):
    n, w = blk.shape[0] - SUBLANES, blk.shape[1]
    g = blk.reshape(n // SUBLANES + 1, SUBLANES, w)
    rot = pltpu.roll(g, sh, 1)
    row = lax.broadcasted_iota(jnp.int32, (SUBLANES, w), 0)
    out = jnp.where(row < sh, rot[:-1], rot[1:])
    return out.reshape(n, w)


def _rmsnorm_rows(x, g):
    ms = jnp.mean(x * x, axis=-1, keepdims=True)
    return x * lax.rsqrt(ms + EPS) * g


def _in_proj_head_kernel(x_ref, g_ref, wdtt_ref, wt_ref, xn_ref, dtt_ref, o_ref, wbf_ref):
    @pl.when(pl.program_id(0) == 0)
    def _():
        _cast_rows(wt_ref, wbf_ref)

    xn = _rmsnorm_rows(x_ref[...], g_ref[...]).astype(BF16)
    xn_ref[...] = xn
    dtt_ref[...] = lax.dot_general(wdtt_ref[...], xn, (((1,), (1,)), ((), ())),
                                   preferred_element_type=F32)
    o_ref[...] = lax.dot_general(xn, wbf_ref[...], (((1,), (1,)), ((), ())),
                                 preferred_element_type=F32).astype(o_ref.dtype)


def _in_proj_kernel(a_ref, wt_ref, o_ref, wbf_ref):
    @pl.when(pl.program_id(1) == 0)
    def _():
        _cast_rows(wt_ref, wbf_ref)

    o_ref[...] = lax.dot_general(a_ref[...], wbf_ref[...], (((1,), (1,)), ((), ())),
                                 preferred_element_type=F32).astype(o_ref.dtype)


def _in_proj(x2, g, wdtt, wt, tm, tm_rest, batch, seq):
    t, k = x2.shape
    tn = Q_HEAD
    assert seq % tm == 0
    per_seq = seq // tm

    xn, dtt, q_head = pl.pallas_call(
        _in_proj_head_kernel,
        grid=(t // tm,),
        in_specs=[pl.BlockSpec((tm, k), lambda m: (m, 0)),
                  pl.BlockSpec((1, k), lambda m: (0, 0)),
                  pl.BlockSpec((SSD_HEADS, k), lambda m: (0, 0)),
                  pl.BlockSpec((tn, k), lambda m: (0, 0))],
        out_specs=[pl.BlockSpec((tm, k), lambda m: (m, 0)),
                   pl.BlockSpec((None, SSD_HEADS, tm), lambda m: (m // per_seq, 0, m % per_seq)),
                   pl.BlockSpec((tm, tn), lambda m: (m, 0))],
        out_shape=[jax.ShapeDtypeStruct((t, k), BF16),
                   jax.ShapeDtypeStruct((batch, SSD_HEADS, seq), F32),
                   jax.ShapeDtypeStruct((t, tn), BF16)],
        scratch_shapes=[pltpu.VMEM((tn, k), BF16)],
        compiler_params=_params(1),
        name="norm_in_proj_head",
    )(x2, g, wdtt, wt)

    def wblock(j):
        return jnp.where(j < 4, j + 5, jnp.where(j < 6, j - 1, j - 5))

    rest = pl.pallas_call(
        _in_proj_kernel,
        grid=(REST_PROJ // tn, t // tm_rest),
        in_specs=[pl.BlockSpec((tm_rest, k), lambda j, m: (m, 0)),
                  pl.BlockSpec((tn, k), lambda j, m: (wblock(j), 0))],
        out_specs=pl.BlockSpec((tm_rest, tn), lambda j, m: (m, j)),
        out_shape=jax.ShapeDtypeStruct((t, REST_PROJ), BF16),
        scratch_shapes=[pltpu.VMEM((tn, k), BF16)],
        compiler_params=_params(2),
        name="in_proj",
    )(xn, wt)
    return q_head, rest, dtt


def _rope_tables(seq):
    half = ROT_DIM // 2
    inv = 1.0 / (ROPE_THETA ** (np.arange(0, ROT_DIM, 2, dtype=np.float64) / ROT_DIM))
    ang = np.arange(seq, dtype=np.float64)[:, None] * inv[None, :]
    cos, sin = np.cos(ang), np.sin(ang)
    d = np.arange(LANES) % HEAD_DIM
    idx = d % half
    c = np.where((d < ROT_DIM)[None, :], cos[:, idx], 1.0)
    s1 = np.where((d < half)[None, :], -sin[:, idx], 0.0)
    s2 = np.where(((d >= half) & (d < ROT_DIM))[None, :], sin[:, idx], 0.0)
    return jnp.asarray(np.stack([c, s1, s2]).astype(np.float32))


def _rope(x, c, s1, s2):
    half = ROT_DIM // 2
    return x * c + pltpu.roll(x, LANES - half, 1) * s1 + pltpu.roll(x, half, 1) * s2


def _attn_kernel(sinks_ref, qa_ref, qb_ref, kp_ref, kc_ref, vp_ref, vc_ref,
                 tc_ref, tp_ref, g_ref, wo_ref, o_ref, wob_ref,
                 lhs_ref, kb_ref, vb_ref, s_ref, p_ref, o2_ref, m_ref, tq_ref, bias_ref, acc_ref, *, qb):
    blk = ATTN_BLOCK
    n = pl.program_id(1)
    lo = lax.broadcasted_iota(jnp.int32, (blk, LANES), 1) < HALF

    wob_ref[...] = wo_ref[...].astype(BF16)

    scale = HEAD_DIM ** -0.5 * LOG2E
    for i in range(3):
        tq_ref[i] = tc_ref[i] * scale
    qi = lax.broadcasted_iota(jnp.int32, (blk, 2 * blk), 0)
    kj = lax.broadcasted_iota(jnp.int32, (blk, 2 * blk), 1)
    rel = qi + blk - kj
    band = (rel >= 0) & (rel < WINDOW)
    kmin = jnp.where(n > 0, 0, blk)
    bias_ref[0] = jnp.where(band & (kj >= kmin), 0.0, NEG)
    bias_ref[1] = jnp.where(band, 0.0, NEG)

    for j in range(qb):
        rows = slice(j * blk, (j + 1) * blk)
        for col in range(ATTN_WIDTH // LANES):
            h, jj = col // 2, col % 2
            q_ref, qc = (qa_ref, col) if col < Q_HEAD // LANES else (qb_ref, col - Q_HEAD // LANES)
            q2 = _rope(q_ref[rows, qc * LANES:(qc + 1) * LANES].astype(F32),
                       tq_ref[0, rows, :], tq_ref[1, rows, :], tq_ref[2, rows, :])
            lhs_ref[j, h, (2 * jj) * blk:(2 * jj + 1) * blk, :] = jnp.where(lo, q2, 0.0).astype(BF16)
            lhs_ref[j, h, (2 * jj + 1) * blk:(2 * jj + 2) * blk, :] = jnp.where(lo, 0.0, q2).astype(BF16)
    nk = (qb + 1) * blk
    lok = lax.broadcasted_iota(jnp.int32, (nk, LANES), 1) < HALF
    ones = jnp.ones((nk, LANES), BF16)
    for i in range(N_KV_HEADS // 2):
        cols = slice(i * LANES, (i + 1) * LANES)
        kcat = jnp.concatenate([_rope(kp_ref[:, cols].astype(F32), tp_ref[0], tp_ref[1], tp_ref[2]),
                                _rope(kc_ref[:, cols].astype(F32), tc_ref[0], tc_ref[1], tc_ref[2])], axis=0)
        vcat = jnp.concatenate([vp_ref[:, cols], vc_ref[:, cols]], axis=0).astype(F32)
        kswp = pltpu.roll(kcat, HALF, 1)
        vswp = pltpu.roll(vcat, HALF, 1)
        kb_ref[2 * i] = jnp.where(lok, kcat, kswp).astype(BF16)
        kb_ref[2 * i + 1] = jnp.where(lok, kswp, kcat).astype(BF16)
        vb_ref[2 * i, :, 0:LANES] = jnp.where(lok, vcat, vswp).astype(BF16)
        vb_ref[2 * i + 1, :, 0:LANES] = jnp.where(lok, vswp, vcat).astype(BF16)
        vb_ref[2 * i, :, LANES:2 * LANES] = ones
        vb_ref[2 * i + 1, :, LANES:2 * LANES] = ones

    def keys(j):
        return slice(j * blk, (j + 2) * blk)

    def scores(j, h):
        s_ref[j, h % 2] = lax.dot_general(lhs_ref[j, h], kb_ref[h, keys(j), :], (((1,), (1,)), ((), ())),
                                          preferred_element_type=F32)

    def row_max(j, h):
        slot = h % 2
        for r in range(Q_PER_KV):
            rows = slice(r * blk, (r + 1) * blk)
            sink = sinks_ref[Q_PER_KV * h + r] * LOG2E
            sr = s_ref[j, slot, rows, :] + bias_ref[min(j, 1)]
            s_ref[j, slot, rows, :] = sr
            m = jnp.maximum(jnp.max(sr, axis=-1, keepdims=True), sink)
            m_ref[j, slot, rows, :] = jnp.broadcast_to(m, (blk, LANES))

    def probs(j, h):
        slot = h % 2
        for r in range(Q_PER_KV):
            rows = slice(r * blk, (r + 1) * blk)
            m = m_ref[j, slot, rows, :]
            for half in range(2):
                ln = slice(half * LANES, (half + 1) * LANES)
                p_ref[j, slot, rows, ln] = jnp.exp2(s_ref[j, slot, rows, ln] - m).astype(BF16)

    def weighted(j, h):
        o2_ref[j, h % 2] = jnp.dot(p_ref[j, h % 2], vb_ref[h, keys(j), :],
                                   preferred_element_type=F32)

    def finish(j, h):
        slot = h % 2
        outs = []
        for r in range(Q_PER_KV):
            rows = slice(r * blk, (r + 1) * blk)
            sink = sinks_ref[Q_PER_KV * h + r] * LOG2E
            den = o2_ref[j, slot, rows, LANES:2 * LANES] + jnp.exp2(sink - m_ref[j, slot, rows, :])
            outs.append(o2_ref[j, slot, rows, 0:LANES] * (1.0 / den))
        for jj in range(2):
            col = 2 * h + jj
            acc_ref[j * blk:(j + 1) * blk, col * LANES:(col + 1) * LANES] = jnp.where(
                lo, outs[2 * jj], outs[2 * jj + 1])

    for j in range(qb):
        scores(j, 0)
    for h in range(N_KV_HEADS + 1):
        for stage in (scores, row_max, probs, weighted, finish):
            hh = {scores: h + 1, finish: h - 1}.get(stage, h)
            if 0 <= hh < N_KV_HEADS:
                for j in range(qb):
                    stage(j, hh)

    o_ref[...] = _rmsnorm_rows(acc_ref[...], g_ref[...]).astype(o_ref.dtype)


def _attention(q_head, proj, sinks, g, tables, w_out, batch, seq, qb):
    t = proj.shape[0]
    blk = ATTN_BLOCK
    assert seq % (qb * blk) == 0
    nb = seq // blk
    ns = nb // qb
    ko, no = w_out.shape
    slab = ko // (batch * ns)
    assert slab * batch * ns == ko and slab % 16 == 0
    kblk = K_OFF // KV_WIDTH
    vblk = V_OFF // KV_WIDTH

    def cur(b, n):
        return (b * ns + n, 0)

    def prev_rows(b, n):
        return jnp.maximum(b * nb + n * qb - 1, 0)

    return pl.pallas_call(
        functools.partial(_attn_kernel, qb=qb),
        grid=(batch, ns),
        in_specs=[pl.BlockSpec(memory_space=pltpu.SMEM),
                  pl.BlockSpec((qb * blk, Q_HEAD), cur),
                  pl.BlockSpec((qb * blk, ATTN_WIDTH - Q_HEAD),
                               lambda b, n: (b * ns + n, Q_REST_OFF // (ATTN_WIDTH - Q_HEAD))),
                  pl.BlockSpec((blk, KV_WIDTH), lambda b, n: (prev_rows(b, n), kblk)),
                  pl.BlockSpec((qb * blk, KV_WIDTH), lambda b, n: (b * ns + n, kblk)),
                  pl.BlockSpec((blk, KV_WIDTH), lambda b, n: (prev_rows(b, n), vblk)),
                  pl.BlockSpec((qb * blk, KV_WIDTH), lambda b, n: (b * ns + n, vblk)),
                  pl.BlockSpec((3, qb * blk, LANES), lambda b, n: (0, n, 0)),
                  pl.BlockSpec((3, blk, LANES), lambda b, n: (0, jnp.maximum(n * qb - 1, 0), 0)),
                  pl.BlockSpec((1, ATTN_WIDTH), lambda b, n: (0, 0)),
                  pl.BlockSpec((slab, no), cur)],
        out_specs=[pl.BlockSpec((qb * blk, ATTN_WIDTH), cur),
                   pl.BlockSpec((slab, no), cur)],
        out_shape=[jax.ShapeDtypeStruct((t, ATTN_WIDTH), BF16),
                   jax.ShapeDtypeStruct((ko, no), BF16)],
        scratch_shapes=[pltpu.VMEM((qb, N_KV_HEADS, Q_PER_KV * blk, LANES), BF16),
                        pltpu.VMEM((N_KV_HEADS, (qb + 1) * blk, LANES), BF16),
                        pltpu.VMEM((N_KV_HEADS, (qb + 1) * blk, 2 * LANES), BF16),
                        pltpu.VMEM((qb, 2, Q_PER_KV * blk, 2 * blk), F32),
                        pltpu.VMEM((qb, 2, Q_PER_KV * blk, 2 * blk), BF16),
                        pltpu.VMEM((qb, 2, Q_PER_KV * blk, 2 * LANES), F32),
                        pltpu.VMEM((qb, 2, Q_PER_KV * blk, LANES), F32),
                        pltpu.VMEM((3, qb * blk, LANES), F32),
                        pltpu.VMEM((2, blk, 2 * blk), F32),
                        pltpu.VMEM((qb * blk, ATTN_WIDTH), F32)],
        compiler_params=_params(2),
        name="swa_attention",
    )(sinks, q_head, proj, proj, proj, proj, proj, tables, tables, g, w_out)


def _split3(x):
    h = x.astype(BF16)
    r = x - h.astype(F32)
    m = r.astype(BF16)
    l = (r - m.astype(F32)).astype(BF16)
    return h, m, l


def _dot3_lhs(x, w):
    h, m, l = _split3(x)
    d = lambda a: jnp.dot(a, w, preferred_element_type=F32)
    return (d(l) + d(m)) + d(h)


def _ssd_kernel(xbc_ref, z_ref, dtt_ref, cw_ref, cb_ref, bcol_ref, alcol_ref, e_ref, dfull_ref, gn_ref, o_ref,
                ext_ref, xact_ref, state_ref, xdt_ref, xdec_ref, y_ref, exp_ref, acs_ref, acst_ref, split_ref,
                *, nbat):
    L = SSD_CHUNK
    c = pl.program_id(0)
    cw_chunk = 512
    gw = SSD_HEAD_DIM * (SSD_HEADS // SSD_GROUPS)
    ri = lax.broadcasted_iota(jnp.int32, (L, L), 0)
    ci = lax.broadcasted_iota(jnp.int32, (L, L), 1)
    causal = ri >= ci
    lo = ci < HALF

    @pl.when(c == 0)
    def _():
        ext_ref[:, 0:SUBLANES, :] = jnp.zeros((nbat, SUBLANES, CONV_CH), F32)
        state_ref[...] = jnp.zeros_like(state_ref)

    def load_x(b):
        ext_ref[b, SUBLANES:SUBLANES + L, :] = xbc_ref[b].astype(F32)

    def conv(b, j):
        cs = slice(j * cw_chunk, (j + 1) * cw_chunk)
        blk = ext_ref[b, :, cs]
        acc = cb_ref[:, cs]
        for k in range(SSD_CONV - 1):
            acc = acc + _delay_rows(blk, SSD_CONV - 1 - k) * cw_ref[k:k + 1, cs]
        acc = acc + blk[SUBLANES:] * cw_ref[SSD_CONV - 1:SSD_CONV, cs]
        xact_ref[b, :, cs] = acc * _sigmoid(acc)

    def keep_tail(b):
        ext_ref[b, 0:SUBLANES, :] = ext_ref[b, L:L + SUBLANES, :]

    def factors(b):
        nh = SSD_HEADS
        dtt = _softplus(dtt_ref[b] + bcol_ref[...])
        dat = dtt * (-jnp.exp(alcol_ref[...]))
        tri_u = jnp.where(ri <= ci, 1.0, 0.0).astype(BF16)
        acst = _dot3_lhs(dat, tri_u) * LOG2E
        acst_ref[b, 0:nh, :] = acst
        pad = jnp.zeros((LANES - nh, L), F32)
        a_cs = jnp.concatenate([acst, pad], axis=0).T
        dt = jnp.concatenate([dtt, pad], axis=0).T
        acs_ref[b] = a_cs
        a_last = a_cs[L - 1:L, :]
        stack = jnp.concatenate([dt, jnp.exp2(a_last - a_cs), jnp.exp2(a_cs),
                                 jnp.broadcast_to(jnp.exp2(a_last), (SUBLANES, LANES))], axis=0)
        hi = stack.astype(BF16)
        split_ref[b, :, 0:LANES] = hi
        split_ref[b, :, LANES:2 * LANES] = (stack - hi.astype(F32)).astype(BF16)

    def expand(b, j):
        cs = slice(j * cw_chunk, (j + 1) * cw_chunk)
        exp_ref[b, :, cs] = jnp.dot(split_ref[b], e_ref[:, cs], preferred_element_type=F32)

    def scale_x(b):
        for j in range(SSD_INNER // cw_chunk):
            cs = slice(j * cw_chunk, (j + 1) * cw_chunk)
            xdt = xact_ref[b, :, cs] * exp_ref[b, 0:L, cs]
            xdt_ref[b, :, cs] = xdt.astype(BF16)
            xdec_ref[b, :, cs] = (xdt * exp_ref[b, L:2 * L, cs]).astype(BF16)

    def group(b, g):
        bg = xact_ref[b, :, SSD_INNER + g * SSD_STATE:SSD_INNER + (g + 1) * SSD_STATE]
        cg = xact_ref[b, :, SSD_INNER + BC_WIDTH + g * SSD_STATE:SSD_INNER + BC_WIDTH + (g + 1) * SSD_STATE]
        bb = bg.astype(BF16)
        cbf = cg.astype(BF16)
        cbm = lax.dot_general(cbf, bb, (((1,), (1,)), ((), ())), preferred_element_type=F32)
        gs = slice(g * gw, (g + 1) * gw)
        prev = state_ref[b, g]
        yoff = jnp.dot(cbf, prev.astype(BF16), preferred_element_type=F32) * exp_ref[b, 2 * L:3 * L, gs]
        btb = bg.T.astype(BF16)
        state_ref[b, g] = prev * exp_ref[b, 3 * L:3 * L + 1, gs] + jnp.dot(
            btb, xdec_ref[b, :, gs], preferred_element_type=F32)
        for e in range(2):
            pc = slice((2 * g + e) * LANES, (2 * g + e + 1) * LANES)
            xpair = xdt_ref[b, :, pc]
            yd = []
            for r in range(2):
                h = 4 * g + 2 * e + r
                seg = acs_ref[b, :, h:h + 1] - acst_ref[b, h:h + 1, :]
                lm = jnp.exp2(jnp.where(causal, seg, NEG))
                yd.append(jnp.dot((cbm * lm).astype(BF16), xpair, preferred_element_type=F32))
            y_ref[b, :, pc] = (jnp.where(lo, yd[0], yd[1]) + yoff[:, e * LANES:(e + 1) * LANES]
                               + dfull_ref[:, pc] * xact_ref[b, :, pc])

    def gate_norm(b, g):
        gs = slice(g * gw, (g + 1) * gw)
        zz = z_ref[b, :, gs].astype(F32)
        yg = y_ref[b, :, gs] * (zz * _sigmoid(zz))
        ms = jnp.mean(yg * yg, axis=-1, keepdims=True)
        o_ref[b, :, gs] = (yg * lax.rsqrt(ms + EPS) * gn_ref[:, gs]).astype(o_ref.dtype)

    for b in range(nbat):
        load_x(b)
    pieces = [functools.partial(factors, b) for b in range(nbat)]
    pieces += [functools.partial(expand, b, j) for b in range(nbat) for j in range(SSD_INNER // cw_chunk)]
    chunks = [functools.partial(conv, b, j) for b in range(nbat) for j in range(CONV_CH // cw_chunk)]
    for idx in range(max(len(pieces), len(chunks))):
        if idx < len(pieces):
            pieces[idx]()
        if idx < len(chunks):
            chunks[idx]()
    for b in range(nbat):
        keep_tail(b)
        scale_x(b)
    for stage in (group, gate_norm):
        for g in range(SSD_GROUPS):
            for b in range(nbat):
                stage(b, g)


def _ssd(proj, dt_rawt, cw, cb, bcol, alcol, emat, dfull, gn, batch, seq):
    L = SSD_CHUNK
    nc = seq // L
    proj3 = proj.reshape(batch, seq, REST_PROJ)
    full = lambda shape: pl.BlockSpec(shape, lambda c: (0, 0))
    out = pl.pallas_call(
        functools.partial(_ssd_kernel, nbat=batch),
        grid=(nc,),
        in_specs=[pl.BlockSpec((batch, L, CONV_CH), lambda c: (0, c, XBC_OFF // CONV_CH)),
                  pl.BlockSpec((batch, L, SSD_INNER), lambda c: (0, c, Z_OFF // SSD_INNER)),
                  pl.BlockSpec((batch, SSD_HEADS, L), lambda c: (0, 0, c)),
                  full((SSD_CONV, CONV_CH)), full((1, CONV_CH)),
                  full((SSD_HEADS, 1)), full((SSD_HEADS, 1)),
                  full((2 * LANES, SSD_INNER)), full((1, SSD_INNER)), full((1, SSD_INNER))],
        out_specs=pl.BlockSpec((batch, L, SSD_INNER), lambda c: (0, c, 0)),
        out_shape=jax.ShapeDtypeStruct((batch, seq, SSD_INNER), BF16),
        scratch_shapes=[pltpu.VMEM((batch, L + SUBLANES, CONV_CH), F32),
                        pltpu.VMEM((batch, L, CONV_CH), F32),
                        pltpu.VMEM((batch, SSD_GROUPS, SSD_STATE, SSD_INNER // SSD_GROUPS), F32),
                        pltpu.VMEM((batch, L, SSD_INNER), BF16),
                        pltpu.VMEM((batch, L, SSD_INNER), BF16),
                        pltpu.VMEM((batch, L, SSD_INNER), F32),
                        pltpu.VMEM((batch, 3 * L + SUBLANES, SSD_INNER), F32),
                        pltpu.VMEM((batch, L, LANES), F32),
                        pltpu.VMEM((batch, LANES, L), F32),
                        pltpu.VMEM((batch, 3 * L + SUBLANES, 2 * LANES), BF16)],
        compiler_params=_params(1),
        name="ssd_scan",
    )(proj3, proj3, dt_rawt, cw, cb, bcol, alcol, emat, dfull, gn)
    return out.reshape(batch * seq, SSD_INNER)


def _out_proj_kernel(a1_ref, a2_ref, w1_ref, w2_ref, r_ref, g_ref, h_ref, hn_ref):
    acc = jnp.dot(a1_ref[...], w1_ref[...], preferred_element_type=F32)
    acc = acc + jnp.dot(a2_ref[...], w2_ref[...], preferred_element_type=F32)
    h = r_ref[...] + acc
    h_ref[...] = h
    hn_ref[...] = _rmsnorm_rows(h, g_ref[...]).astype(hn_ref.dtype)


def _out_proj(a1, a2, w_bf16, res, g, tm):
    t, k = a1.shape
    n = w_bf16.shape[1]
    row = lambda shape: pl.BlockSpec(shape, lambda m: (m, 0))
    wspec = lambda i: pl.BlockSpec((k, n), lambda m: (i, 0), pipeline_mode=pl.Buffered(1))
    return pl.pallas_call(
        _out_proj_kernel,
        grid=(t // tm,),
        in_specs=[row((tm, k)), row((tm, k)), wspec(0), wspec(1), row((tm, n)),
                  pl.BlockSpec((1, n), lambda m: (0, 0))],
        out_specs=[row((tm, n)), row((tm, n))],
        out_shape=[jax.ShapeDtypeStruct((t, n), F32), jax.ShapeDtypeStruct((t, n), BF16)],
        compiler_params=_params(1),
        name="out_proj_norm",
    )(a1, a2, w_bf16, w_bf16, res, g)


def _up_kernel(a_ref, wg_ref, wv_ref, cwg_ref, cwv_ref, cbg_ref, cbv_ref, wd_ref, o_ref, wdb_ref,
               wgb_ref, wvb_ref, carry_ref, *, tm, seq):
    m = pl.program_id(1)
    wdb_ref[...] = wd_ref[...].astype(BF16)

    @pl.when(m == 0)
    def _():
        _cast_rows(wg_ref, wgb_ref)
        _cast_rows(wv_ref, wvb_ref)

    a = a_ref[...]
    seq_start = (m * tm) % seq == 0
    outs = []
    for idx, (wb_ref, cw_ref, cb_ref) in enumerate(((wgb_ref, cwg_ref, cbg_ref),
                                                    (wvb_ref, cwv_ref, cbv_ref))):
        u = jnp.dot(a, wb_ref[...], preferred_element_type=F32)
        above = jnp.where(seq_start, 0.0, carry_ref[idx])
        blk = jnp.concatenate([above, u], axis=0)
        cw = cw_ref[...]
        conv = cb_ref[...]
        for k in range(FFN_CONV - 1):
            conv = conv + _delay_rows(blk, FFN_CONV - 1 - k) * cw[k:k + 1, :]
        conv = conv + u * cw[FFN_CONV - 1:FFN_CONV, :]
        carry_ref[idx] = u[tm - SUBLANES:tm]
        outs.append(conv)
    gate, val = outs
    o_ref[...] = ((gate * _sigmoid(gate)) * val).astype(o_ref.dtype)


def _up_proj(a, w, cw, cb, w_down, tm, tn, seq):
    t, k = a.shape
    assert seq % tm == 0 and D_FF % tn == 0
    nb = D_FF // tn
    n_m = t // tm
    kd, nd = w_down.shape
    slab = kd // (nb * n_m)
    assert slab * nb * n_m == kd and slab % 16 == 0
    return pl.pallas_call(
        functools.partial(_up_kernel, tm=tm, seq=seq),
        grid=(nb, t // tm),
        in_specs=[pl.BlockSpec((tm, k), lambda j, m: (m, 0)),
                  pl.BlockSpec((k, tn), lambda j, m: (0, j)),
                  pl.BlockSpec((k, tn), lambda j, m: (0, j + nb)),
                  pl.BlockSpec((FFN_CONV, tn), lambda j, m: (0, j)),
                  pl.BlockSpec((FFN_CONV, tn), lambda j, m: (0, j + nb)),
                  pl.BlockSpec((1, tn), lambda j, m: (0, j)),
                  pl.BlockSpec((1, tn), lambda j, m: (0, j + nb)),
                  pl.BlockSpec((slab, nd), lambda j, m: (j * n_m + m, 0))],
        out_specs=[pl.BlockSpec((tm, tn), lambda j, m: (m, j)),
                   pl.BlockSpec((slab, nd), lambda j, m: (j * n_m + m, 0))],
        out_shape=[jax.ShapeDtypeStruct((t, D_FF), BF16),
                   jax.ShapeDtypeStruct((kd, nd), BF16)],
        scratch_shapes=[pltpu.VMEM((k, tn), BF16), pltpu.VMEM((k, tn), BF16),
                        pltpu.VMEM((2, SUBLANES, tn), F32)],
        compiler_params=_params(2),
        name="up_proj_conv_swiglu",
    )(a, w, w, cw, cw, cb, cb, w_down)


def _down_kernel(a_ref, w_ref, r_ref, g_ref, o_ref, *, final_norm):
    h = r_ref[...] + jnp.dot(a_ref[...], w_ref[...], preferred_element_type=F32)
    o_ref[...] = _rmsnorm_rows(h, g_ref[...]) if final_norm else h


def _down_proj(a, w_bf16, res, g, tm, final_norm):
    t, k = a.shape
    n = w_bf16.shape[1]
    row = lambda shape: pl.BlockSpec(shape, lambda m: (m, 0))
    return pl.pallas_call(
        functools.partial(_down_kernel, final_norm=final_norm),
        grid=(t // tm,),
        in_specs=[row((tm, k)),
                  pl.BlockSpec((k, n), lambda m: (0, 0), pipeline_mode=pl.Buffered(1)),
                  row((tm, n)),
                  pl.BlockSpec((1, n), lambda m: (0, 0))],
        out_specs=row((tm, n)),
        out_shape=jax.ShapeDtypeStruct((t, n), F32),
        compiler_params=_params(1),
        name="down_proj_norm" if final_norm else "down_proj",
    )(a, w_bf16, res, g)


def _mixer_layer(h, batch, seq, norm_mix, w_in, sinks, attn_out_norm, ssd_conv_w, ssd_conv_b, dt_bias,
                 a_log, ssd_d, ssd_norm, w_out, norm_ffn, tables, emat):
    w_in_t = jnp.swapaxes(w_in, 0, 1)
    wdt_t = w_in_t[MAIN_PROJ:].astype(BF16)
    q_head, proj, dt_rawt = _in_proj(h, norm_mix.reshape(1, -1), wdt_t, w_in_t, TM_IN_HEAD, TM_IN_REST,
                                     batch, seq)
    attn, w_out_bf16 = _attention(q_head, proj, sinks.astype(F32), attn_out_norm.reshape(1, -1), tables, w_out,
                                  batch, seq, qb=ATTN_QB)
    y = _ssd(proj, dt_rawt, ssd_conv_w, ssd_conv_b.reshape(1, -1),
             dt_bias.astype(F32).reshape(-1, 1), a_log.astype(F32).reshape(-1, 1),
             emat, jnp.repeat(ssd_d.astype(F32), SSD_HEAD_DIM).reshape(1, -1),
             ssd_norm.reshape(1, -1), batch, seq)
    return _out_proj(attn, y, w_out_bf16, h, norm_ffn.reshape(1, -1), tm=TM_OUT)


def kernel(x, norm_mix, w_in, sinks, attn_out_norm, ssd_conv_w, ssd_conv_b, dt_bias, a_log, ssd_d, ssd_norm,
           w_out, norm_ffn, w_up, ffn_conv_w, ffn_conv_b, w_down, norm_final):
    batch, seq, d = x.shape
    h = x.reshape(batch * seq, d)
    tables = _rope_tables(seq)
    head_of_channel = np.arange(SSD_INNER) // SSD_HEAD_DIM
    emat = np.arange(LANES)[:, None] == head_of_channel[None, :]
    emat = jnp.asarray(np.concatenate([emat, emat], axis=0), dtype=BF16)
    for l in range(norm_mix.shape[0]):
        h, hn = _mixer_layer(h, batch, seq, norm_mix[l], w_in[l], sinks[l], attn_out_norm[l], ssd_conv_w[l],
                             ssd_conv_b[l], dt_bias[l], a_log[l], ssd_d[l], ssd_norm[l], w_out[l], norm_ffn[l],
                             tables, emat)
        act, w_down_bf16 = _up_proj(hn, w_up[l], ffn_conv_w[l], ffn_conv_b[l].reshape(1, -1), w_down[l],
                                    tm=TM_UP, tn=TN_UP, seq=seq)
        h = _down_proj(act, w_down_bf16, h, norm_final.reshape(1, -1), tm=TM_DOWN,
                       final_norm=(l == norm_mix.shape[0] - 1))
    return h.reshape(batch, seq, d)
```

```python
import functools

import numpy as np
import jax
import jax.numpy as jnp
from jax import lax
from jax.experimental import pallas as pl
from jax.experimental.pallas import tpu as pltpu

F32 = jnp.float32
BF16 = jnp.bfloat16

D_MODEL = 2048
N_Q_HEADS = 32
N_KV_HEADS = 8
HEAD_DIM = 64
Q_PER_KV = N_Q_HEADS // N_KV_HEADS
WINDOW = 128
ATTN_BLOCK = 128
ROT_DIM = HEAD_DIM // 4
ROPE_THETA = 500000.0
SSD_HEADS = 32
SSD_HEAD_DIM = 64
SSD_INNER = SSD_HEADS * SSD_HEAD_DIM
SSD_GROUPS = 8
SSD_STATE = 128
SSD_CONV = 4
SSD_CHUNK = 128
ATTN_WIDTH = N_Q_HEADS * HEAD_DIM
KV_WIDTH = N_KV_HEADS * HEAD_DIM
BC_WIDTH = SSD_GROUPS * SSD_STATE
CONV_CH = SSD_INNER + 2 * BC_WIDTH
MAIN_PROJ = ATTN_WIDTH + 2 * KV_WIDTH + SSD_INNER + CONV_CH
D_FF = 5632
FFN_CONV = 3
EPS = 1e-6

LANES = 128
SUBLANES = 8
HALF = LANES // 2
NEG = -1e30
LOG2E = 1.4426950408889634
VMEM_LIMIT = 56 * 1024 * 1024

TM_IN_HEAD = 1024
TM_IN_REST = 2048
ATTN_QB = 4
TM_OUT = 512
TM_UP, TN_UP = 1024, 512
TM_DOWN = 512

Q_HEAD = 1024
REST_PROJ = MAIN_PROJ - Q_HEAD
XBC_OFF = 0
Z_OFF = XBC_OFF + CONV_CH
Q_REST_OFF = Z_OFF + SSD_INNER
K_OFF = Q_REST_OFF + (ATTN_WIDTH - Q_HEAD)
V_OFF = K_OFF + KV_WIDTH


def _params(n_axes):
    return pltpu.CompilerParams(dimension_semantics=("arbitrary",) * n_axes,
                                vmem_limit_bytes=VMEM_LIMIT)


def _sigmoid(x):
    return 1.0 / (1.0 + jnp.exp2(x * -LOG2E))


def _softplus(x):
    return jnp.maximum(x, 0.0) + jnp.log1p(jnp.exp(-jnp.abs(x)))


def _cast_rows(src_ref, dst_ref, rows=256):
    k = src_ref.shape[0]
    for r in range(0, k, rows):
        dst_ref[r:r + rows, :] = src_ref[r:r + rows, :].astype(BF16)


def _delay_rows(blk, sh):
    n, w = blk.shape[0] - SUBLANES, blk.shape[1]
    g = blk.reshape(n // SUBLANES + 1, SUBLANES, w)
    rot = pltpu.roll(g, sh, 1)
    row = lax.broadcasted_iota(jnp.int32, (SUBLANES, w), 0)
    out = jnp.where(row < sh, rot[:-1], rot[1:])
    return out.reshape(n, w)


def _rmsnorm_rows(x, g):
    ms = jnp.mean(x * x, axis=-1, keepdims=True)
    return x * lax.rsqrt(ms + EPS) * g


def _in_proj_head_kernel(x_ref, g_ref, wdtt_ref, wt_ref, xn_ref, dtt_ref, o_ref, wbf_ref):
    @pl.when(pl.program_id(0) == 0)
    def _():
        _cast_rows(wt_ref, wbf_ref)

    xn = _rmsnorm_rows(x_ref[...], g_ref[...]).astype(BF16)
    xn_ref[...] = xn
    dtt_ref[...] = lax.dot_general(wdtt_ref[...], xn, (((1,), (1,)), ((), ())),
                                   preferred_element_type=F32)
    o_ref[...] = lax.dot_general(xn, wbf_ref[...], (((1,), (1,)), ((), ())),
                                 preferred_element_type=F32).astype(o_ref.dtype)


def _in_proj_kernel(a_ref, wt_ref, o_ref, wbf_ref):
    @pl.when(pl.program_id(1) == 0)
    def _():
        _cast_rows(wt_ref, wbf_ref)

    o_ref[...] = lax.dot_general(a_ref[...], wbf_ref[...], (((1,), (1,)), ((), ())),
                                 preferred_element_type=F32).astype(o_ref.dtype)


def _in_proj(x2, g, wdtt, wt, tm, tm_rest, batch, seq):
    t, k = x2.shape
    tn = Q_HEAD
    assert seq % tm == 0
    per_seq = seq // tm

    xn, dtt, q_head = pl.pallas_call(
        _in_proj_head_kernel,
        grid=(t // tm,),
        in_specs=[pl.BlockSpec((tm, k), lambda m: (m, 0)),
                  pl.BlockSpec((1, k), lambda m: (0, 0)),
                  pl.BlockSpec((SSD_HEADS, k), lambda m: (0, 0)),
                  pl.BlockSpec((tn, k), lambda m: (0, 0))],
        out_specs=[pl.BlockSpec((tm, k), lambda m: (m, 0)),
                   pl.BlockSpec((None, SSD_HEADS, tm), lambda m: (m // per_seq, 0, m % per_seq)),
                   pl.BlockSpec((tm, tn), lambda m: (m, 0))],
        out_shape=[jax.ShapeDtypeStruct((t, k), BF16),
                   jax.ShapeDtypeStruct((batch, SSD_HEADS, seq), F32),
                   jax.ShapeDtypeStruct((t, tn), BF16)],
        scratch_shapes=[pltpu.VMEM((tn, k), BF16)],
        compiler_params=_params(1),
        name="norm_in_proj_head",
    )(x2, g, wdtt, wt)

    def wblock(j):
        return jnp.where(j < 4, j + 5, jnp.where(j < 6, j - 1, j - 5))

    rest = pl.pallas_call(
        _in_proj_kernel,
        grid=(REST_PROJ // tn, t // tm_rest),
        in_specs=[pl.BlockSpec((tm_rest, k), lambda j, m: (m, 0)),
                  pl.BlockSpec((tn, k), lambda j, m: (wblock(j), 0))],
        out_specs=pl.BlockSpec((tm_rest, tn), lambda j, m: (m, j)),
        out_shape=jax.ShapeDtypeStruct((t, REST_PROJ), BF16),
        scratch_shapes=[pltpu.VMEM((tn, k), BF16)],
        compiler_params=_params(2),
        name="in_proj",
    )(xn, wt)
    return q_head, rest, dtt


def _rope_tables(seq):
    half = ROT_DIM // 2
    inv = 1.0 / (ROPE_THETA ** (np.arange(0, ROT_DIM, 2, dtype=np.float64) / ROT_DIM))
    ang = np.arange(seq, dtype=np.float64)[:, None] * inv[None, :]
    cos, sin = np.cos(ang), np.sin(ang)
    d = np.arange(LANES) % HEAD_DIM
    idx = d % half
    c = np.where((d < ROT_DIM)[None, :], cos[:, idx], 1.0)
    s1 = np.where((d < half)[None, :], -sin[:, idx], 0.0)
    s2 = np.where(((d >= half) & (d < ROT_DIM))[None, :], sin[:, idx], 0.0)
    return jnp.asarray(np.stack([c, s1, s2]).astype(np.float32))


def _rope(x, c, s1, s2):
    half = ROT_DIM // 2
    return x * c + pltpu.roll(x, LANES - half, 1) * s1 + pltpu.roll(x, half, 1) * s2


def _attn_kernel(sinks_ref, qa_ref, qb_ref, kp_ref, kc_ref, vp_ref, vc_ref,
                 tc_ref, tp_ref, g_ref, wo_ref, o_ref, wob_ref,
                 lhs_ref, kb_ref, vb_ref, s_ref, p_ref, o2_ref, m_ref, tq_ref, bias_ref, acc_ref, *, qb):
    blk = ATTN_BLOCK
    n = pl.program_id(1)
    lo = lax.broadcasted_iota(jnp.int32, (blk, LANES), 1) < HALF

    wob_ref[...] = wo_ref[...].astype(BF16)

    scale = HEAD_DIM ** -0.5 * LOG2E
    for i in range(3):
        tq_ref[i] = tc_ref[i] * scale
    qi = lax.broadcasted_iota(jnp.int32, (blk, 2 * blk), 0)
    kj = lax.broadcasted_iota(jnp.int32, (blk, 2 * blk), 1)
    rel = qi + blk - kj
    band = (rel >= 0) & (rel < WINDOW)
    kmin = jnp.where(n > 0, 0, blk)
    bias_ref[0] = jnp.where(band & (kj >= kmin), 0.0, NEG)
    bias_ref[1] = jnp.where(band, 0.0, NEG)

    for j in range(qb):
        rows = slice(j * blk, (j + 1) * blk)
        for col in range(ATTN_WIDTH // LANES):
            h, jj = col // 2, col % 2
            q_ref, qc = (qa_ref, col) if col < Q_HEAD // LANES else (qb_ref, col - Q_HEAD // LANES)
            q2 = _rope(q_ref[rows, qc * LANES:(qc + 1) * LANES].astype(F32),
                       tq_ref[0, rows, :], tq_ref[1, rows, :], tq_ref[2, rows, :])
            lhs_ref[j, h, (2 * jj) * blk:(2 * jj + 1) * blk, :] = jnp.where(lo, q2, 0.0).astype(BF16)
            lhs_ref[j, h, (2 * jj + 1) * blk:(2 * jj + 2) * blk, :] = jnp.where(lo, 0.0, q2).astype(BF16)
    nk = (qb + 1) * blk
    lok = lax.broadcasted_iota(jnp.int32, (nk, LANES), 1) < HALF
    ones = jnp.ones((nk, LANES), BF16)
    for i in range(N_KV_HEADS // 2):
        cols = slice(i * LANES, (i + 1) * LANES)
        kcat = jnp.concatenate([_rope(kp_ref[:, cols].astype(F32), tp_ref[0], tp_ref[1], tp_ref[2]),
                                _rope(kc_ref[:, cols].astype(F32), tc_ref[0], tc_ref[1], tc_ref[2])], axis=0)
        vcat = jnp.concatenate([vp_ref[:, cols], vc_ref[:, cols]], axis=0).astype(F32)
        kswp = pltpu.roll(kcat, HALF, 1)
        vswp = pltpu.roll(vcat, HALF, 1)
        kb_ref[2 * i] = jnp.where(lok, kcat, kswp).astype(BF16)
        kb_ref[2 * i + 1] = jnp.where(lok, kswp, kcat).astype(BF16)
        vb_ref[2 * i, :, 0:LANES] = jnp.where(lok, vcat, vswp).astype(BF16)
        vb_ref[2 * i + 1, :, 0:LANES] = jnp.where(lok, vswp, vcat).astype(BF16)
        vb_ref[2 * i, :, LANES:2 * LANES] = ones
        vb_ref[2 * i + 1, :, LANES:2 * LANES] = ones

    def keys(j):
        return slice(j * blk, (j + 2) * blk)

    def scores(j, h):
        s_ref[j, h % 2] = lax.dot_general(lhs_ref[j, h], kb_ref[h, keys(j), :], (((1,), (1,)), ((), ())),
                                          preferred_element_type=F32)

    def row_max(j, h):
        slot = h % 2
        for r in range(Q_PER_KV):
            rows = slice(r * blk, (r + 1) * blk)
            sink = sinks_ref[Q_PER_KV * h + r] * LOG2E
            sr = s_ref[j, slot, rows, :] + bias_ref[min(j, 1)]
            s_ref[j, slot, rows, :] = sr
            m = jnp.maximum(jnp.max(sr, axis=-1, keepdims=True), sink)
            m_ref[j, slot, rows, :] = jnp.broadcast_to(m, (blk, LANES))

    def probs(j, h):
        slot = h % 2
        for r in range(Q_PER_KV):
            rows = slice(r * blk, (r + 1) * blk)
            m = m_ref[j, slot, rows, :]
            for half in range(2):
                ln = slice(half * LANES, (half + 1) * LANES)
                p_ref[j, slot, rows, ln] = jnp.exp2(s_ref[j, slot, rows, ln] - m).astype(BF16)

    def weighted(j, h):
        o2_ref[j, h % 2] = jnp.dot(p_ref[j, h % 2], vb_ref[h, keys(j), :],
                                   preferred_element_type=F32)

    def finish(j, h):
        slot = h % 2
        outs = []
        for r in range(Q_PER_KV):
            rows = slice(r * blk, (r + 1) * blk)
            sink = sinks_ref[Q_PER_KV * h + r] * LOG2E
            den = o2_ref[j, slot, rows, LANES:2 * LANES] + jnp.exp2(sink - m_ref[j, slot, rows, :])
            outs.append(o2_ref[j, slot, rows, 0:LANES] * (1.0 / den))
        for jj in range(2):
            col = 2 * h + jj
            acc_ref[j * blk:(j + 1) * blk, col * LANES:(col + 1) * LANES] = jnp.where(
                lo, outs[2 * jj], outs[2 * jj + 1])

    for j in range(qb):
        scores(j, 0)
    for h in range(N_KV_HEADS + 1):
        for stage in (scores, row_max, probs, weighted, finish):
            hh = {scores: h + 1, finish: h - 1}.get(stage, h)
            if 0 <= hh < N_KV_HEADS:
                for j in range(qb):
                    stage(j, hh)

    o_ref[...] = _rmsnorm_rows(acc_ref[...], g_ref[...]).astype(o_ref.dtype)


def _attention(q_head, proj, sinks, g, tables, w_out, batch, seq, qb):
    t = proj.shape[0]
    blk = ATTN_BLOCK
    assert seq % (qb * blk) == 0
    nb = seq // blk
    ns = nb // qb
    ko, no = w_out.shape
    slab = ko // (batch * ns)
    assert slab * batch * ns == ko and slab % 16 == 0
    kblk = K_OFF // KV_WIDTH
    vblk = V_OFF // KV_WIDTH

    def cur(b, n):
        return (b * ns + n, 0)

    def prev_rows(b, n):
        return jnp.maximum(b * nb + n * qb - 1, 0)

    return pl.pallas_call(
        functools.partial(_attn_kernel, qb=qb),
        grid=(batch, ns),
        in_specs=[pl.BlockSpec(memory_space=pltpu.SMEM),
                  pl.BlockSpec((qb * blk, Q_HEAD), cur),
                  pl.BlockSpec((qb * blk, ATTN_WIDTH - Q_HEAD),
                               lambda b, n: (b * ns + n, Q_REST_OFF // (ATTN_WIDTH - Q_HEAD))),
                  pl.BlockSpec((blk, KV_WIDTH), lambda b, n: (prev_rows(b, n), kblk)),
                  pl.BlockSpec((qb * blk, KV_WIDTH), lambda b, n: (b * ns + n, kblk)),
                  pl.BlockSpec((blk, KV_WIDTH), lambda b, n: (prev_rows(b, n), vblk)),
                  pl.BlockSpec((qb * blk, KV_WIDTH), lambda b, n: (b * ns + n, vblk)),
                  pl.BlockSpec((3, qb * blk, LANES), lambda b, n: (0, n, 0)),
                  pl.BlockSpec((3, blk, LANES), lambda b, n: (0, jnp.maximum(n * qb - 1, 0), 0)),
                  pl.BlockSpec((1, ATTN_WIDTH), lambda b, n: (0, 0)),
                  pl.BlockSpec((slab, no), cur)],
        out_specs=[pl.BlockSpec((qb * blk, ATTN_WIDTH), cur),
                   pl.BlockSpec((slab, no), cur)],
        out_shape=[jax.ShapeDtypeStruct((t, ATTN_WIDTH), BF16),
                   jax.ShapeDtypeStruct((ko, no), BF16)],
        scratch_shapes=[pltpu.VMEM((qb, N_KV_HEADS, Q_PER_KV * blk, LANES), BF16),
                        pltpu.VMEM((N_KV_HEADS, (qb + 1) * blk, LANES), BF16),
                        pltpu.VMEM((N_KV_HEADS, (qb + 1) * blk, 2 * LANES), BF16),
                        pltpu.VMEM((qb, 2, Q_PER_KV * blk, 2 * blk), F32),
                        pltpu.VMEM((qb, 2, Q_PER_KV * blk, 2 * blk), BF16),
                        pltpu.VMEM((qb, 2, Q_PER_KV * blk, 2 * LANES), F32),
                        pltpu.VMEM((qb, 2, Q_PER_KV * blk, LANES), F32),
                        pltpu.VMEM((3, qb * blk, LANES), F32),
                        pltpu.VMEM((2, blk, 2 * blk), F32),
                        pltpu.VMEM((qb * blk, ATTN_WIDTH), F32)],
        compiler_params=_params(2),
        name="swa_attention",
    )(sinks, q_head, proj, proj, proj, proj, proj, tables, tables, g, w_out)


def _split3(x):
    h = x.astype(BF16)
    r = x - h.astype(F32)
    m = r.astype(BF16)
    l = (r - m.astype(F32)).astype(BF16)
    return h, m, l


def _dot3_lhs(x, w):
    h, m, l = _split3(x)
    d = lambda a: jnp.dot(a, w, preferred_element_type=F32)
    return (d(l) + d(m)) + d(h)


def _ssd_kernel(xbc_ref, z_ref, dtt_ref, cw_ref, cb_ref, bcol_ref, alcol_ref, e_ref, dfull_ref, gn_ref, o_ref,
                ext_ref, xact_ref, state_ref, xdt_ref, xdec_ref, y_ref, exp_ref, acs_ref, acst_ref, split_ref,
                *, nbat):
    L = SSD_CHUNK
    c = pl.program_id(0)
    cw_chunk = 512
    gw = SSD_HEAD_DIM * (SSD_HEADS // SSD_GROUPS)
    ri = lax.broadcasted_iota(jnp.int32, (L, L), 0)
    ci = lax.broadcasted_iota(jnp.int32, (L, L), 1)
    causal = ri >= ci
    lo = ci < HALF

    @pl.when(c == 0)
    def _():
        ext_ref[:, 0:SUBLANES, :] = jnp.zeros((nbat, SUBLANES, CONV_CH), F32)
        state_ref[...] = jnp.zeros_like(state_ref)

    def load_x(b):
        ext_ref[b, SUBLANES:SUBLANES + L, :] = xbc_ref[b].astype(F32)

    def conv(b, j):
        cs = slice(j * cw_chunk, (j + 1) * cw_chunk)
        blk = ext_ref[b, :, cs]
        acc = cb_ref[:, cs]
        for k in range(SSD_CONV - 1):
            acc = acc + _delay_rows(blk, SSD_CONV - 1 - k) * cw_ref[k:k + 1, cs]
        acc = acc + blk[SUBLANES:] * cw_ref[SSD_CONV - 1:SSD_CONV, cs]
        xact_ref[b, :, cs] = acc * _sigmoid(acc)

    def keep_tail(b):
        ext_ref[b, 0:SUBLANES, :] = ext_ref[b, L:L + SUBLANES, :]

    def factors(b):
        nh = SSD_HEADS
        dtt = _softplus(dtt_ref[b] + bcol_ref[...])
        dat = dtt * (-jnp.exp(alcol_ref[...]))
        tri_u = jnp.where(ri <= ci, 1.0, 0.0).astype(BF16)
        acst = _dot3_lhs(dat, tri_u) * LOG2E
        acst_ref[b, 0:nh, :] = acst
        pad = jnp.zeros((LANES - nh, L), F32)
        a_cs = jnp.concatenate([acst, pad], axis=0).T
        dt = jnp.concatenate([dtt, pad], axis=0).T
        acs_ref[b] = a_cs
        a_last = a_cs[L - 1:L, :]
        stack = jnp.concatenate([dt, jnp.exp2(a_last - a_cs), jnp.exp2(a_cs),
                                 jnp.broadcast_to(jnp.exp2(a_last), (SUBLANES, LANES))], axis=0)
        hi = stack.astype(BF16)
        split_ref[b, :, 0:LANES] = hi
        split_ref[b, :, LANES:2 * LANES] = (stack - hi.astype(F32)).astype(BF16)

    def expand(b, j):
        cs = slice(j * cw_chunk, (j + 1) * cw_chunk)
        exp_ref[b, :, cs] = jnp.dot(split_ref[b], e_ref[:, cs], preferred_element_type=F32)

    def scale_x(b):
        for j in range(SSD_INNER // cw_chunk):
            cs = slice(j * cw_chunk, (j + 1) * cw_chunk)
            xdt = xact_ref[b, :, cs] * exp_ref[b, 0:L, cs]
            xdt_ref[b, :, cs] = xdt.astype(BF16)
            xdec_ref[b, :, cs] = (xdt * exp_ref[b, L:2 * L, cs]).astype(BF16)

    def group(b, g):
        bg = xact_ref[b, :, SSD_INNER + g * SSD_STATE:SSD_INNER + (g + 1) * SSD_STATE]
        cg = xact_ref[b, :, SSD_INNER + BC_WIDTH + g * SSD_STATE:SSD_INNER + BC_WIDTH + (g + 1) * SSD_STATE]
        bb = bg.astype(BF16)
        cbf = cg.astype(BF16)
        cbm = lax.dot_general(cbf, bb, (((1,), (1,)), ((), ())), preferred_element_type=F32)
        gs = slice(g * gw, (g + 1) * gw)
        prev = state_ref[b, g]
        yoff = jnp.dot(cbf, prev.astype(BF16), preferred_element_type=F32) * exp_ref[b, 2 * L:3 * L, gs]
        state_ref[b, g] = prev * exp_ref[b, 3 * L:3 * L + 1, gs] + lax.dot_general(
            bb, xdec_ref[b, :, gs], (((0,), (0,)), ((), ())), preferred_element_type=F32)
        for e in range(2):
            pc = slice((2 * g + e) * LANES, (2 * g + e + 1) * LANES)
            xpair = xdt_ref[b, :, pc]
            yd = []
            for r in range(2):
                h = 4 * g + 2 * e + r
                seg = acs_ref[b, :, h:h + 1] - acst_ref[b, h:h + 1, :]
                lm = jnp.exp2(jnp.where(causal, seg, NEG))
                yd.append(jnp.dot((cbm * lm).astype(BF16), xpair, preferred_element_type=F32))
            y_ref[b, :, pc] = (jnp.where(lo, yd[0], yd[1]) + yoff[:, e * LANES:(e + 1) * LANES]
                               + dfull_ref[:, pc] * xact_ref[b, :, pc])

    def gate_norm(b, g):
        gs = slice(g * gw, (g + 1) * gw)
        zz = z_ref[b, :, gs].astype(F32)
        yg = y_ref[b, :, gs] * (zz * _sigmoid(zz))
        ms = jnp.mean(yg * yg, axis=-1, keepdims=True)
        o_ref[b, :, gs] = (yg * lax.rsqrt(ms + EPS) * gn_ref[:, gs]).astype(o_ref.dtype)

    for b in range(nbat):
        load_x(b)
    pieces = [functools.partial(factors, b) for b in range(nbat)]
    pieces += [functools.partial(expand, b, j) for b in range(nbat) for j in range(SSD_INNER // cw_chunk)]
    chunks = [functools.partial(conv, b, j) for b in range(nbat) for j in range(CONV_CH // cw_chunk)]
    for idx in range(max(len(pieces), len(chunks))):
        if idx < len(pieces):
            pieces[idx]()
        if idx < len(chunks):
            chunks[idx]()
    for b in range(nbat):
        keep_tail(b)
        scale_x(b)
    for g in range(SSD_GROUPS + 1):
        for b in range(nbat):
            if g < SSD_GROUPS:
                group(b, g)
            if g > 0:
                gate_norm(b, g - 1)


def _ssd(proj, dt_rawt, cw, cb, bcol, alcol, emat, dfull, gn, batch, seq):
    L = SSD_CHUNK
    nc = seq // L
    proj3 = proj.reshape(batch, seq, REST_PROJ)
    full = lambda shape: pl.BlockSpec(shape, lambda c: (0, 0))
    out = pl.pallas_call(
        functools.partial(_ssd_kernel, nbat=batch),
        grid=(nc,),
        in_specs=[pl.BlockSpec((batch, L, CONV_CH), lambda c: (0, c, XBC_OFF // CONV_CH)),
                  pl.BlockSpec((batch, L, SSD_INNER), lambda c: (0, c, Z_OFF // SSD_INNER)),
                  pl.BlockSpec((batch, SSD_HEADS, L), lambda c: (0, 0, c)),
                  full((SSD_CONV, CONV_CH)), full((1, CONV_CH)),
                  full((SSD_HEADS, 1)), full((SSD_HEADS, 1)),
                  full((2 * LANES, SSD_INNER)), full((1, SSD_INNER)), full((1, SSD_INNER))],
        out_specs=pl.BlockSpec((batch, L, SSD_INNER), lambda c: (0, c, 0)),
        out_shape=jax.ShapeDtypeStruct((batch, seq, SSD_INNER), BF16),
        scratch_shapes=[pltpu.VMEM((batch, L + SUBLANES, CONV_CH), F32),
                        pltpu.VMEM((batch, L, CONV_CH), F32),
                        pltpu.VMEM((batch, SSD_GROUPS, SSD_STATE, SSD_INNER // SSD_GROUPS), F32),
                        pltpu.VMEM((batch, L, SSD_INNER), BF16),
                        pltpu.VMEM((batch, L, SSD_INNER), BF16),
                        pltpu.VMEM((batch, L, SSD_INNER), F32),
                        pltpu.VMEM((batch, 3 * L + SUBLANES, SSD_INNER), F32),
                        pltpu.VMEM((batch, L, LANES), F32),
                        pltpu.VMEM((batch, LANES, L), F32),
                        pltpu.VMEM((batch, 3 * L + SUBLANES, 2 * LANES), BF16)],
        compiler_params=_params(1),
        name="ssd_scan",
    )(proj3, proj3, dt_rawt, cw, cb, bcol, alcol, emat, dfull, gn)
    return out.reshape(batch * seq, SSD_INNER)


def _out_proj_kernel(a1_ref, a2_ref, w1_ref, w2_ref, r_ref, g_ref, h_ref, hn_ref):
    acc = jnp.dot(a1_ref[...], w1_ref[...], preferred_element_type=F32)
    acc = acc + jnp.dot(a2_ref[...], w2_ref[...], preferred_element_type=F32)
    h = r_ref[...] + acc
    h_ref[...] = h
    hn_ref[...] = _rmsnorm_rows(h, g_ref[...]).astype(hn_ref.dtype)


def _out_proj(a1, a2, w_bf16, res, g, tm):
    t, k = a1.shape
    n = w_bf16.shape[1]
    row = lambda shape: pl.BlockSpec(shape, lambda m: (m, 0))
    wspec = lambda i: pl.BlockSpec((k, n), lambda m: (i, 0), pipeline_mode=pl.Buffered(1))
    return pl.pallas_call(
        _out_proj_kernel,
        grid=(t // tm,),
        in_specs=[row((tm, k)), row((tm, k)), wspec(0), wspec(1), row((tm, n)),
                  pl.BlockSpec((1, n), lambda m: (0, 0))],
        out_specs=[row((tm, n)), row((tm, n))],
        out_shape=[jax.ShapeDtypeStruct((t, n), F32), jax.ShapeDtypeStruct((t, n), BF16)],
        compiler_params=_params(1),
        name="out_proj_norm",
    )(a1, a2, w_bf16, w_bf16, res, g)


def _up_kernel(a_ref, wg_ref, wv_ref, cwg_ref, cwv_ref, cbg_ref, cbv_ref, wd_ref, o_ref, wdb_ref,
               wgb_ref, wvb_ref, carry_ref, *, tm, seq):
    m = pl.program_id(1)
    wdb_ref[...] = wd_ref[...].astype(BF16)

    @pl.when(m == 0)
    def _():
        _cast_rows(wg_ref, wgb_ref)
        _cast_rows(wv_ref, wvb_ref)

    a = a_ref[...]
    seq_start = (m * tm) % seq == 0
    outs = []
    for idx, (wb_ref, cw_ref, cb_ref) in enumerate(((wgb_ref, cwg_ref, cbg_ref),
                                                    (wvb_ref, cwv_ref, cbv_ref))):
        u = jnp.dot(a, wb_ref[...], preferred_element_type=F32)
        above = jnp.where(seq_start, 0.0, carry_ref[idx])
        blk = jnp.concatenate([above, u], axis=0)
        cw = cw_ref[...]
        conv = cb_ref[...]
        for k in range(FFN_CONV - 1):
            conv = conv + _delay_rows(blk, FFN_CONV - 1 - k) * cw[k:k + 1, :]
        conv = conv + u * cw[FFN_CONV - 1:FFN_CONV, :]
        carry_ref[idx] = u[tm - SUBLANES:tm]
        outs.append(conv)
    gate, val = outs
    o_ref[...] = ((gate * _sigmoid(gate)) * val).astype(o_ref.dtype)


def _up_proj(a, w, cw, cb, w_down, tm, tn, seq):
    t, k = a.shape
    assert seq % tm == 0 and D_FF % tn == 0
    nb = D_FF // tn
    n_m = t // tm
    kd, nd = w_down.shape
    slab = kd // (nb * n_m)
    assert slab * nb * n_m == kd and slab % 16 == 0
    return pl.pallas_call(
        functools.partial(_up_kernel, tm=tm, seq=seq),
        grid=(nb, t // tm),
        in_specs=[pl.BlockSpec((tm, k), lambda j, m: (m, 0)),
                  pl.BlockSpec((k, tn), lambda j, m: (0, j)),
                  pl.BlockSpec((k, tn), lambda j, m: (0, j + nb)),
                  pl.BlockSpec((FFN_CONV, tn), lambda j, m: (0, j)),
                  pl.BlockSpec((FFN_CONV, tn), lambda j, m: (0, j + nb)),
                  pl.BlockSpec((1, tn), lambda j, m: (0, j)),
                  pl.BlockSpec((1, tn), lambda j, m: (0, j + nb)),
                  pl.BlockSpec((slab, nd), lambda j, m: (j * n_m + m, 0))],
        out_specs=[pl.BlockSpec((tm, tn), lambda j, m: (m, j)),
                   pl.BlockSpec((slab, nd), lambda j, m: (j * n_m + m, 0))],
        out_shape=[jax.ShapeDtypeStruct((t, D_FF), BF16),
                   jax.ShapeDtypeStruct((kd, nd), BF16)],
        scratch_shapes=[pltpu.VMEM((k, tn), BF16), pltpu.VMEM((k, tn), BF16),
                        pltpu.VMEM((2, SUBLANES, tn), F32)],
        compiler_params=_params(2),
        name="up_proj_conv_swiglu",
    )(a, w, w, cw, cw, cb, cb, w_down)


def _down_kernel(a_ref, w_ref, r_ref, g_ref, o_ref, *, final_norm):
    h = r_ref[...] + jnp.dot(a_ref[...], w_ref[...], preferred_element_type=F32)
    o_ref[...] = _rmsnorm_rows(h, g_ref[...]) if final_norm else h


def _down_proj(a, w_bf16, res, g, tm, final_norm):
    t, k = a.shape
    n = w_bf16.shape[1]
    row = lambda shape: pl.BlockSpec(shape, lambda m: (m, 0))
    return pl.pallas_call(
        functools.partial(_down_kernel, final_norm=final_norm),
        grid=(t // tm,),
        in_specs=[row((tm, k)),
                  pl.BlockSpec((k, n), lambda m: (0, 0), pipeline_mode=pl.Buffered(1)),
                  row((tm, n)),
                  pl.BlockSpec((1, n), lambda m: (0, 0))],
        out_specs=row((tm, n)),
        out_shape=jax.ShapeDtypeStruct((t, n), F32),
        compiler_params=_params(1),
        name="down_proj_norm" if final_norm else "down_proj",
    )(a, w_bf16, res, g)


def _mixer_layer(h, batch, seq, norm_mix, w_in, sinks, attn_out_norm, ssd_conv_w, ssd_conv_b, dt_bias,
                 a_log, ssd_d, ssd_norm, w_out, norm_ffn, tables, emat):
    w_in_t = jnp.swapaxes(w_in, 0, 1)
    wdt_t = w_in_t[MAIN_PROJ:].astype(BF16)
    q_head, proj, dt_rawt = _in_proj(h, norm_mix.reshape(1, -1), wdt_t, w_in_t, TM_IN_HEAD, TM_IN_REST,
                                     batch, seq)
    attn, w_out_bf16 = _attention(q_head, proj, sinks.astype(F32), attn_out_norm.reshape(1, -1), tables, w_out,
                                  batch, seq, qb=ATTN_QB)
    y = _ssd(proj, dt_rawt, ssd_conv_w, ssd_conv_b.reshape(1, -1),
             dt_bias.astype(F32).reshape(-1, 1), a_log.astype(F32).reshape(-1, 1),
             emat, jnp.repeat(ssd_d.astype(F32), SSD_HEAD_DIM).reshape(1, -1),
             ssd_norm.reshape(1, -1), batch, seq)
    return _out_proj(attn, y, w_out_bf16, h, norm_ffn.reshape(1, -1), tm=TM_OUT)


def kernel(x, norm_mix, w_in, sinks, attn_out_norm, ssd_conv_w, ssd_conv_b, dt_bias, a_log, ssd_d, ssd_norm,
           w_out, norm_ffn, w_up, ffn_conv_w, ffn_conv_b, w_down, norm_final):
    batch, seq, d = x.shape
    h = x.reshape(batch * seq, d)
    tables = _rope_tables(seq)
    head_of_channel = np.arange(SSD_INNER) // SSD_HEAD_DIM
    emat = np.arange(LANES)[:, None] == head_of_channel[None, :]
    emat = jnp.asarray(np.concatenate([emat, emat], axis=0), dtype=BF16)
    for l in range(norm_mix.shape[0]):
        h, hn = _mixer_layer(h, batch, seq, norm_mix[l], w_in[l], sinks[l], attn_out_norm[l], ssd_conv_w[l],
                             ssd_conv_b[l], dt_bias[l], a_log[l], ssd_d[l], ssd_norm[l], w_out[l], norm_ffn[l],
                             tables, emat)
        act, w_down_bf16 = _up_proj(hn, w_up[l], ffn_conv_w[l], ffn_conv_b[l].reshape(1, -1), w_down[l],
                                    tm=TM_UP, tn=TN_UP, seq=seq)
        h = _down_proj(act, w_down_bf16, h, norm_final.reshape(1, -1), tm=TM_DOWN,
                       final_norm=(l == norm_mix.shape[0] - 1))
    return h.reshape(batch, seq, d)
```
